```python
import math
import jax, jax.numpy as jnp
from jax import lax
import numpy as np

D_MODEL = 1024
BATCH = 2
SEQ = 8192
DEPTH = 4

HEAD_DIM = 64
BLOCK = 128
DIFF_HEADS = D_MODEL // 256
DIFF_V_DIM = 2 * HEAD_DIM
SWA_Q_HEADS = D_MODEL // 128
SWA_KV_HEADS = max(1, SWA_Q_HEADS // 4)
SWA_GROUP = SWA_Q_HEADS // SWA_KV_HEADS
WINDOW = 128
A_Q = DIFF_HEADS * 2 * HEAD_DIM
A_K = DIFF_HEADS * 2 * HEAD_DIM
A_V = DIFF_HEADS * DIFF_V_DIM
B_Q = SWA_Q_HEADS * HEAD_DIM
B_KV = SWA_KV_HEADS * HEAD_DIM
IN_ATT = A_Q + A_K + A_V + B_Q + 2 * B_KV
MIX_WIDTH = A_V + B_Q
CONV_WIDTH = 31
D_FF = 2816
N_EXPERTS = 8
TOP_K = 2
D_FF_EXPERT = 3584
MOE_CHUNK = 256
RMS_EPS = 1e-6
LN_EPS = 1e-5
N_ATT_LAYERS = (DEPTH + 1) // 2
N_CONV_LAYERS = DEPTH // 2

kernel_name = "hybrid_diffattn_swa_conformer_moe"


def rms_norm(x, g, eps=RMS_EPS):
    xf = x.astype(jnp.float32)
    y = xf * lax.rsqrt(jnp.mean(xf * xf, axis=-1, keepdims=True) + eps)
    return (y * g.astype(jnp.float32)).astype(x.dtype)


def layer_norm(x, g, b, eps=LN_EPS):
    xf = x.astype(jnp.float32)
    mu = jnp.mean(xf, axis=-1, keepdims=True)
    xc = xf - mu
    var = jnp.mean(xc * xc, axis=-1, keepdims=True)
    y = xc * lax.rsqrt(var + eps) * g.astype(jnp.float32) + b.astype(jnp.float32)
    return y.astype(x.dtype)


def swiglu(h, w_gate, w_up, w_down):
    return (jax.nn.silu(h @ w_gate) * (h @ w_up)) @ w_down


def diff_lambda_init(layer):
    return 0.8 - 0.6 * math.exp(-0.3 * layer)


def differential_attention(q, k, v, lam, subln_g, lam_init):
    bsz, seq, heads = q.shape[0], q.shape[1], q.shape[2]
    nb = seq // BLOCK
    scale = HEAD_DIM ** -0.5
    kt = jnp.transpose(k, (0, 3, 2, 1, 4))
    vt = jnp.transpose(v, (0, 2, 1, 3))
    qb = q.reshape(bsz, nb, BLOCK, heads, 2, HEAD_DIM).transpose(1, 0, 4, 3, 2, 5)
    k_pos = jnp.arange(seq)

    def one_block(args):
        qblk, n = args
        s = jnp.einsum('bchqd,bchkd->bchqk', qblk, kt).astype(jnp.float32) * scale
        q_pos = n * BLOCK + jnp.arange(BLOCK)
        causal = k_pos[None, :] <= q_pos[:, None]
        p = jax.nn.softmax(jnp.where(causal, s, -jnp.inf), axis=-1)
        attn = p[:, 0] - lam * p[:, 1]
        return jnp.einsum('bhqk,bhkv->bhqv', attn.astype(vt.dtype), vt)

    o = lax.map(one_block, (qb, jnp.arange(nb)))
    o = o.transpose(1, 0, 3, 2, 4).reshape(bsz, seq, heads, DIFF_V_DIM)
    o = rms_norm(o, subln_g) * (1.0 - lam_init)
    return o.reshape(bsz, seq, heads * DIFF_V_DIM)


def sliding_window_attention(q, k, v, sinks):
    bsz, seq = q.shape[0], q.shape[1]
    nb = seq // BLOCK
    scale = HEAD_DIM ** -0.5
    qb = q.reshape(bsz, nb, BLOCK, SWA_KV_HEADS, SWA_GROUP, HEAD_DIM)

    def with_prev(t):
        tb = t.reshape(bsz, nb, BLOCK, SWA_KV_HEADS, HEAD_DIM)
        prev = jnp.concatenate([jnp.zeros_like(tb[:, :1]), tb[:, :-1]], axis=1)
        return jnp.concatenate([prev, tb], axis=2)

    kw, vw = with_prev(k), with_prev(v)
    s = jnp.einsum('bnqhgd,bnkhd->bnhgqk', qb, kw).astype(jnp.float32) * scale
    i = jnp.arange(BLOCK)[:, None]
    j = jnp.arange(2 * BLOCK)[None, :]
    rel = BLOCK + i - j
    band = (rel >= 0) & (rel < WINDOW)
    key_pos = jnp.arange(nb)[:, None] * BLOCK - BLOCK + jnp.arange(2 * BLOCK)[None, :]
    valid = band[None] & (key_pos >= 0)[:, None, :]
    s = jnp.where(valid[None, :, None, None], s, -jnp.inf)
    sink = sinks.astype(jnp.float32)[None, None, :, :, None, None]
    m = jnp.maximum(jnp.max(s, axis=-1, keepdims=True), sink)
    e = jnp.exp(s - m)
    p = e / (jnp.sum(e, axis=-1, keepdims=True) + jnp.exp(sink - m))
    o = jnp.einsum('bnhgqk,bnkhd->bnqhgd', p.astype(vw.dtype), vw)
    return o.reshape(bsz, seq, SWA_Q_HEADS * HEAD_DIM)


def attention_mixer(x, norm_g, w_in, lam_params, subln_g, sinks, w_out, layer):
    bsz, seq = x.shape[0], x.shape[1]
    h = rms_norm(x, norm_g)
    proj = h @ w_in
    cuts = [A_Q, A_Q + A_K, A_Q + A_K + A_V, A_Q + A_K + A_V + B_Q,
            A_Q + A_K + A_V + B_Q + B_KV]
    qa, ka, va, qb, kb, vb = jnp.split(proj, cuts, axis=-1)
    lam_init = diff_lambda_init(layer)
    lp = lam_params.astype(jnp.float32)
    lam = jnp.exp(jnp.sum(lp[0] * lp[1])) - jnp.exp(jnp.sum(lp[2] * lp[3])) + lam_init
    oa = differential_attention(
        qa.reshape(bsz, seq, DIFF_HEADS, 2, HEAD_DIM),
        ka.reshape(bsz, seq, DIFF_HEADS, 2, HEAD_DIM),
        va.reshape(bsz, seq, DIFF_HEADS, DIFF_V_DIM),
        lam, subln_g, lam_init)
    ob = sliding_window_attention(
        qb.reshape(bsz, seq, SWA_KV_HEADS, SWA_GROUP, HEAD_DIM),
        kb.reshape(bsz, seq, SWA_KV_HEADS, HEAD_DIM),
        vb.reshape(bsz, seq, SWA_KV_HEADS, HEAD_DIM),
        sinks.reshape(SWA_KV_HEADS, SWA_GROUP))
    return jnp.concatenate([oa, ob], axis=-1) @ w_out


def conformer_conv(x, norm_g, w_pw1, b_pw1, w_dw, b_dw, ln_g, ln_b, w_pw2, b_pw2):
    h = rms_norm(x, norm_g)
    a, gate = jnp.split(h @ w_pw1 + b_pw1, 2, axis=-1)
    u = a * jax.nn.sigmoid(gate)
    u = lax.conv_general_dilated(
        u, w_dw[:, None, :], window_strides=(1,),
        padding=((CONV_WIDTH - 1, 0),),
        dimension_numbers=('NWC', 'WIO', 'NWC'),
        feature_group_count=D_MODEL) + b_dw
    u = jax.nn.silu(layer_norm(u, ln_g, ln_b))
    return u @ w_pw2 + b_pw2


def moe_swiglu(x, norm_g, w_router, w_gate, w_up, w_down):
    bsz, seq, dm = x.shape
    h = rms_norm(x, norm_g).reshape(-1, dm)
    n_tok = h.shape[0]
    logits = (h @ w_router).astype(jnp.float32)
    top_logits, top_idx = lax.top_k(logits, TOP_K)
    top_w = jax.nn.softmax(top_logits, axis=-1)
    n_rows = n_tok * TOP_K
    flat_e = top_idx.reshape(-1)
    flat_tok = jnp.arange(n_rows) // TOP_K
    flat_w = top_w.reshape(-1)
    order = jnp.argsort(flat_e, stable=True)
    sorted_e = flat_e[order]
    counts = jnp.bincount(flat_e, length=N_EXPERTS)
    padded = (counts + MOE_CHUNK - 1) // MOE_CHUNK * MOE_CHUNK
    starts = jnp.cumsum(counts) - counts
    padded_ends = jnp.cumsum(padded)
    padded_starts = padded_ends - padded
    dest = padded_starts[sorted_e] + (jnp.arange(n_rows) - starts[sorted_e])
    n_chunks = -(-n_rows // MOE_CHUNK) + N_EXPERTS
    n_pad = n_chunks * MOE_CHUNK
    row_tok = jnp.zeros((n_pad,), jnp.int32).at[dest].set(flat_tok[order].astype(jnp.int32))
    row_w = jnp.zeros((n_pad,), jnp.float32).at[dest].set(flat_w[order])
    chunk_e = jnp.minimum(
        jnp.searchsorted(padded_ends, jnp.arange(n_chunks) * MOE_CHUNK, side='right'),
        N_EXPERTS - 1)
    xs = h[row_tok].reshape(n_chunks, MOE_CHUNK, dm)

    def expert_block(args):
        xc, e = args
        return swiglu(xc, w_gate[e], w_up[e], w_down[e])

    ys = lax.map(expert_block, (xs, chunk_e)).reshape(n_pad, dm)
    y = jnp.zeros((n_tok, dm), h.dtype).at[row_tok].add(ys * row_w[:, None].astype(ys.dtype))
    return y.reshape(bsz, seq, dm)


def setup_inputs(seed: int = 0) -> dict:
    key = jax.random.key(seed)
    ks = iter(jax.random.split(key, 32))

    def nrm(shape, scale):
        return jax.random.normal(next(ks), shape, jnp.float32) * scale

    def gain(shape):
        return 1.0 + nrm(shape, 0.02)

    na, nc = N_ATT_LAYERS, N_CONV_LAYERS
    return {
        "x": nrm((BATCH, SEQ, D_MODEL), 1.0),
        "attn_norm_g": gain((na, D_MODEL)),
        "w_in_att": nrm((na, D_MODEL, IN_ATT), D_MODEL ** -0.5),
        "diff_lambda": nrm((na, 4, HEAD_DIM), 0.1),
        "diff_subln_g": gain((na, DIFF_V_DIM)),
        "attn_sinks": nrm((na, SWA_Q_HEADS), 1.0),
        "w_out_att": nrm((na, MIX_WIDTH, D_MODEL), MIX_WIDTH ** -0.5),
        "ffn_norm_g": gain((na, D_MODEL)),
        "w_ffn_gate": nrm((na, D_MODEL, D_FF), D_MODEL ** -0.5),
        "w_ffn_up": nrm((na, D_MODEL, D_FF), D_MODEL ** -0.5),
        "w_ffn_down": nrm((na, D_FF, D_MODEL), D_FF ** -0.5),
        "conv_norm_g": gain((nc, D_MODEL)),
        "w_pw1": nrm((nc, D_MODEL, 2 * D_MODEL), D_MODEL ** -0.5),
        "b_pw1": nrm((nc, 2 * D_MODEL), 0.02),
        "w_dw": nrm((nc, CONV_WIDTH, D_MODEL), CONV_WIDTH ** -0.5),
        "b_dw": nrm((nc, D_MODEL), 0.02),
        "conv_ln_g": gain((nc, D_MODEL)),
        "conv_ln_b": nrm((nc, D_MODEL), 0.02),
        "w_pw2": nrm((nc, D_MODEL, D_MODEL), D_MODEL ** -0.5),
        "b_pw2": nrm((nc, D_MODEL), 0.02),
        "moe_norm_g": gain((nc, D_MODEL)),
        "w_router": nrm((nc, D_MODEL, N_EXPERTS), D_MODEL ** -0.5),
        "w_exp_gate": nrm((nc, N_EXPERTS, D_MODEL, D_FF_EXPERT), D_MODEL ** -0.5),
        "w_exp_up": nrm((nc, N_EXPERTS, D_MODEL, D_FF_EXPERT), D_MODEL ** -0.5),
        "w_exp_down": nrm((nc, N_EXPERTS, D_FF_EXPERT, D_MODEL), D_FF_EXPERT ** -0.5),
        "final_norm_g": gain((D_MODEL,)),
    }


def reference(x, attn_norm_g, w_in_att, diff_lambda, diff_subln_g, attn_sinks, w_out_att,
              ffn_norm_g, w_ffn_gate, w_ffn_up, w_ffn_down,
              conv_norm_g, w_pw1, b_pw1, w_dw, b_dw, conv_ln_g, conv_ln_b, w_pw2, b_pw2,
              moe_norm_g, w_router, w_exp_gate, w_exp_up, w_exp_down, final_norm_g):
    for layer in range(DEPTH):
        i = layer // 2
        if layer % 2 == 0:
            x = x + attention_mixer(x, attn_norm_g[i], w_in_att[i], diff_lambda[i],
                                    diff_subln_g[i], attn_sinks[i], w_out_att[i], layer)
            x = x + swiglu(rms_norm(x, ffn_norm_g[i]), w_ffn_gate[i], w_ffn_up[i], w_ffn_down[i])
        else:
            x = x + conformer_conv(x, conv_norm_g[i], w_pw1[i], b_pw1[i], w_dw[i], b_dw[i],
                                   conv_ln_g[i], conv_ln_b[i], w_pw2[i], b_pw2[i])
            x = x + moe_swiglu(x, moe_norm_g[i], w_router[i], w_exp_gate[i],
                               w_exp_up[i], w_exp_down[i])
    return rms_norm(x, final_norm_g)
```

```python
import functools
import math

import jax
import jax.numpy as jnp
from jax import lax
from jax.experimental import pallas as pl
from jax.experimental.pallas import tpu as pltpu

BF16 = jnp.bfloat16
F32 = jnp.float32
I32 = jnp.int32

RMS_EPS = 1e-6
LN_EPS = 1e-5
HEAD_DIM = 64
ATTN_BLOCK = 128
CONV_WIDTH = 31
TOP_K = 2
LANES = 128
CONV_HIST = 32
V7X_VMEM_LIMIT = 56 * 1024 * 1024


def _params(n_axes):
    return pltpu.CompilerParams(dimension_semantics=("arbitrary",) * n_axes,
                                vmem_limit_bytes=V7X_VMEM_LIMIT)


def _rms(x, g):
    return x * lax.rsqrt(jnp.mean(x * x, axis=-1, keepdims=True) + RMS_EPS) * g


def _pick(n, pref):
    t = min(n, pref)
    while n % t:
        t -= LANES if t > LANES else 8
    return t


def _rms_matmul_body(x_ref, g_ref, w_ref, o_ref):
    h = _rms(x_ref[...], g_ref[...]).astype(BF16)
    o_ref[...] = jnp.dot(h, w_ref[...], preferred_element_type=F32).astype(o_ref.dtype)


def rms_matmul(x, g, w, tm=512):
    t, d = x.shape
    n = w.shape[1]
    tm = _pick(t, tm)
    return pl.pallas_call(
        _rms_matmul_body,
        grid=(t // tm,),
        in_specs=[pl.BlockSpec((tm, d), lambda i: (i, 0)),
                  pl.BlockSpec((1, d), lambda i: (0, 0)),
                  pl.BlockSpec((d, n), lambda i: (0, 0))],
        out_specs=pl.BlockSpec((tm, n), lambda i: (i, 0)),
        out_shape=jax.ShapeDtypeStruct((t, n), BF16),
        compiler_params=_params(1),
        name="rms_inproj",
    )(x, g.reshape(1, d), w)


def _diff_attn_body(q_ref, k_ref, v_ref, lam_ref, g_ref, o_ref, m_ref, l_ref, acc_ref,
                    *, blk, lam_init):
    qi = pl.program_id(2)
    lane = lax.broadcasted_iota(I32, (blk, 2 * HEAD_DIM), 1)
    q = (q_ref[0].astype(F32) * (HEAD_DIM ** -0.5)).astype(BF16)
    zero = jnp.zeros_like(q)
    q_maps = (jnp.where(lane < HEAD_DIM, q, zero), jnp.where(lane >= HEAD_DIM, q, zero))

    m_ref[...] = jnp.full(m_ref.shape, -jnp.inf, F32)
    l_ref[...] = jnp.zeros(l_ref.shape, F32)
    acc_ref[...] = jnp.zeros(acc_ref.shape, F32)

    def block(ki, masked):
        start = pl.multiple_of(ki * blk, blk)
        k = k_ref[0, pl.ds(start, blk), :]
        v = v_ref[0, pl.ds(start, blk), :]
        for c in range(2):
            s = lax.dot_general(q_maps[c], k, (((1,), (1,)), ((), ())), preferred_element_type=F32)
            if masked:
                row = lax.broadcasted_iota(I32, (blk, blk), 0)
                col = lax.broadcasted_iota(I32, (blk, blk), 1)
                s = jnp.where(col <= row, s, -jnp.inf)
            m_prev = m_ref[c]
            m_new = jnp.maximum(m_prev, jnp.max(s, axis=-1, keepdims=True))
            alpha = jnp.exp(m_prev - m_new)
            p = jnp.exp(s - m_new)
            l_ref[c] = alpha * l_ref[c] + jnp.sum(p, axis=-1, keepdims=True)
            acc_ref[c] = alpha * acc_ref[c] + jnp.dot(p.astype(BF16), v, preferred_element_type=F32)
            m_ref[c] = m_new

    def body(ki, carry):
        block(ki, False)
        return carry

    lax.fori_loop(0, qi, body, 0)
    block(qi, True)

    lp = lam_ref[...]
    lam = (jnp.exp(jnp.sum(lp[0:1] * lp[1:2], axis=-1, keepdims=True))
           - jnp.exp(jnp.sum(lp[2:3] * lp[3:4], axis=-1, keepdims=True)) + lam_init)
    o = acc_ref[0] / l_ref[0] - lam * (acc_ref[1] / l_ref[1])
    o = _rms(o, g_ref[...]) * (1.0 - lam_init)
    o_ref[0] = o.astype(o_ref.dtype)


def diff_attention(proj, lam_params, subln_g, n_heads, lam_init, blk=512):
    b, s, _ = proj.shape
    blk = _pick(s, blk)
    hw = 2 * HEAD_DIM
    kernel = functools.partial(_diff_attn_body, blk=blk, lam_init=lam_init)
    return pl.pallas_call(
        kernel,
        grid=(b, n_heads, s // blk),
        in_specs=[pl.BlockSpec((1, blk, hw), lambda bi, h, qi: (bi, qi, h)),
                  pl.BlockSpec((1, s, hw), lambda bi, h, qi: (bi, 0, n_heads + h)),
                  pl.BlockSpec((1, s, hw), lambda bi, h, qi: (bi, 0, 2 * n_heads + h)),
                  pl.BlockSpec((4, HEAD_DIM), lambda bi, h, qi: (0, 0)),
                  pl.BlockSpec((1, hw), lambda bi, h, qi: (0, 0))],
        out_specs=pl.BlockSpec((1, blk, hw), lambda bi, h, qi: (bi, qi, h)),
        out_shape=jax.ShapeDtypeStruct((b, s, n_heads * hw), BF16),
        scratch_shapes=[pltpu.VMEM((2, blk, 1), F32), pltpu.VMEM((2, blk, 1), F32),
                        pltpu.VMEM((2, blk, hw), F32)],
        compiler_params=_params(3),
        name="diff_attention",
    )(proj, proj, proj, lam_params, subln_g.reshape(1, hw))


def _swa_body(sink_ref, q_ref, kc_ref, kp_ref, vc_ref, vp_ref, o_ref, *, tq, kv_heads, group):
    t = pl.program_id(1)
    w = ATTN_BLOCK
    row = lax.broadcasted_iota(I32, (group * w, 2 * w), 0) & (w - 1)
    col = lax.broadcasted_iota(I32, (group * w, 2 * w), 1)
    band = (col > row) & (col <= row + w)
    band_first = band & ((col >= w) | (t > 0))
    for j in range(tq // w):
        rows = slice(j * w, (j + 1) * w)
        if j == 0:
            k2 = jnp.concatenate([kp_ref[0], kc_ref[0, rows, :]], axis=0)
            v2 = jnp.concatenate([vp_ref[0], vc_ref[0, rows, :]], axis=0)
            mask = band_first
        else:
            k2 = kc_ref[0, (j - 1) * w:(j + 1) * w, :]
            v2 = vc_ref[0, (j - 1) * w:(j + 1) * w, :]
            mask = band
        outs = []
        for g in range(kv_heads):
            kk = k2[:, g * HEAD_DIM:(g + 1) * HEAD_DIM]
            vv = v2[:, g * HEAD_DIM:(g + 1) * HEAD_DIM]
            heads = [g * group + i for i in range(group)]
            qs = jnp.concatenate(
                [q_ref[0, rows, h * HEAD_DIM:(h + 1) * HEAD_DIM] for h in heads], axis=0)
            sink = jnp.concatenate([jnp.full((w, 1), sink_ref[h], F32) for h in heads], axis=0)
            s = lax.dot_general(qs, kk, (((1,), (1,)), ((), ())),
                                preferred_element_type=F32) * (HEAD_DIM ** -0.5)
            s = jnp.where(mask, s, -jnp.inf)
            m = jnp.maximum(jnp.max(s, axis=-1, keepdims=True), sink)
            e = jnp.exp(s - m)
            denom = jnp.sum(e, axis=-1, keepdims=True) + jnp.exp(sink - m)
            o = jnp.dot(e.astype(BF16), vv, preferred_element_type=F32) / denom
            outs.extend(o[i * w:(i + 1) * w, :] for i in range(group))
        o_ref[0, rows, :] = jnp.concatenate(outs, axis=1).astype(o_ref.dtype)


def sliding_window_attention(proj, sinks, q_col, k_col, v_col, kv_heads, group, tq=512):
    b, s, _ = proj.shape
    tq = _pick(s, tq)
    qw = kv_heads * group * HEAD_DIM
    kw = kv_heads * HEAD_DIM
    sub = tq // ATTN_BLOCK
    kernel = functools.partial(_swa_body, tq=tq, kv_heads=kv_heads, group=group)
    prev = lambda bi, t: (bi, jnp.maximum(t * sub - 1, 0), 0)
    return pl.pallas_call(
        kernel,
        grid=(b, s // tq),
        in_specs=[pl.BlockSpec(memory_space=pltpu.SMEM),
                  pl.BlockSpec((1, tq, qw), lambda bi, t: (bi, t, q_col // qw)),
                  pl.BlockSpec((1, tq, kw), lambda bi, t: (bi, t, k_col // kw)),
                  pl.BlockSpec((1, ATTN_BLOCK, kw), lambda bi, t: prev(bi, t)[:2] + (k_col // kw,)),
                  pl.BlockSpec((1, tq, kw), lambda bi, t: (bi, t, v_col // kw)),
                  pl.BlockSpec((1, ATTN_BLOCK, kw), lambda bi, t: prev(bi, t)[:2] + (v_col // kw,))],
        out_specs=pl.BlockSpec((1, tq, qw), lambda bi, t: (bi, t, 0)),
        out_shape=jax.ShapeDtypeStruct((b, s, qw), BF16),
        compiler_params=_params(2),
        name="swa_attention",
    )(sinks, proj, proj, proj, proj, proj)


def _outproj_body(x_ref, a_ref, b_ref, w_ref, o_ref):
    mix = jnp.concatenate([a_ref[...], b_ref[...]], axis=1)
    o_ref[...] = x_ref[...] + jnp.dot(mix, w_ref[...], preferred_element_type=F32)


def out_projection(x, oa, ob, w, tm=512):
    t, d = x.shape
    tm = _pick(t, tm)
    return pl.pallas_call(
        _outproj_body,
        grid=(t // tm,),
        in_specs=[pl.BlockSpec((tm, d), lambda i: (i, 0)),
                  pl.BlockSpec((tm, oa.shape[1]), lambda i: (i, 0)),
                  pl.BlockSpec((tm, ob.shape[1]), lambda i: (i, 0)),
                  pl.BlockSpec(w.shape, lambda i: (0, 0))],
        out_specs=pl.BlockSpec((tm, d), lambda i: (i, 0)),
        out_shape=jax.ShapeDtypeStruct((t, d), F32),
        compiler_params=_params(1),
        name="out_projection",
    )(x, oa, ob, w)


def _ffn_body(x_ref, g_ref, wg_ref, wu_ref, wd_ref, o_ref, h_ref, acc_ref):
    j = pl.program_id(1)

    @pl.when(j == 0)
    def _():
        h_ref[...] = _rms(x_ref[...], g_ref[...]).astype(BF16)
        acc_ref[...] = jnp.zeros(acc_ref.shape, F32)

    h = h_ref[...]
    a = jnp.dot(h, wg_ref[...], preferred_element_type=F32)
    u = jnp.dot(h, wu_ref[...], preferred_element_type=F32)
    act = (a * jax.nn.sigmoid(a) * u).astype(BF16)
    acc_ref[...] += jnp.dot(act, wd_ref[...], preferred_element_type=F32)

    @pl.when(j == pl.num_programs(1) - 1)
    def _():
        o_ref[...] = x_ref[...] + acc_ref[...]


def ffn_swiglu(x, g, wg, wu, wd, tm=512, tf=1408):
    t, d = x.shape
    f = wg.shape[1]
    tm = _pick(t, tm)
    tf = _pick(f, tf)
    return pl.pallas_call(
        _ffn_body,
        grid=(t // tm, f // tf),
        in_specs=[pl.BlockSpec((tm, d), lambda i, j: (i, 0)),
                  pl.BlockSpec((1, d), lambda i, j: (0, 0)),
                  pl.BlockSpec((d, tf), lambda i, j: (0, j)),
                  pl.BlockSpec((d, tf), lambda i, j: (0, j)),
                  pl.BlockSpec((tf, d), lambda i, j: (j, 0))],
        out_specs=pl.BlockSpec((tm, d), lambda i, j: (i, 0)),
        out_shape=jax.ShapeDtypeStruct((t, d), F32),
        scratch_shapes=[pltpu.VMEM((tm, d), BF16), pltpu.VMEM((tm, d), F32)],
        compiler_params=_params(2),
        name="ffn_swiglu",
    )(x, g.reshape(1, d), wg, wu, wd)


def _conv_body(x_ref, g_ref, w1_ref, b1_ref, wdw_ref, bdw_ref, lng_ref, lnb_ref, w2_ref, b2_ref,
               o_ref, u_ref, v_ref, *, tm, rows_per_chunk):
    d = x_ref.shape[-1]
    t = pl.program_id(1)

    @pl.when(t == 0)
    def _():
        u_ref[:, 0:CONV_HIST, :] = jnp.zeros((d // LANES, CONV_HIST, LANES), F32)

    h = _rms(x_ref[0], g_ref[...]).astype(BF16)
    z = jnp.dot(h, w1_ref[...], preferred_element_type=F32) + b1_ref[...]
    u = z[:, :d] * jax.nn.sigmoid(z[:, d:])
    n_slabs = d // LANES
    for c in range(n_slabs):
        u_ref[c, CONV_HIST:CONV_HIST + tm, :] = u[:, c * LANES:(c + 1) * LANES]

    first_tap = CONV_HIST - (CONV_WIDTH - 1)

    def chunk(i, carry):
        r0 = pl.multiple_of(i * rows_per_chunk, rows_per_chunk)
        cols = []
        for c in range(n_slabs):
            lanes = slice(c * LANES, (c + 1) * LANES)
            a = jnp.zeros((rows_per_chunk, LANES), F32) + bdw_ref[:, lanes]
            for j in range(CONV_WIDTH):
                taps = u_ref[c, pl.ds(r0 + first_tap + j, rows_per_chunk, stride=1), :]
                a = a + taps * wdw_ref[j:j + 1, lanes]
            cols.append(a)
        acc = jnp.concatenate(cols, axis=1)
        mu = jnp.mean(acc, axis=-1, keepdims=True)
        xc = acc - mu
        var = jnp.mean(xc * xc, axis=-1, keepdims=True)
        y = xc * lax.rsqrt(var + LN_EPS) * lng_ref[...] + lnb_ref[...]
        v_ref[pl.ds(r0, rows_per_chunk), :] = (y * jax.nn.sigmoid(y)).astype(BF16)
        return carry

    lax.fori_loop(0, tm // rows_per_chunk, chunk, 0)
    u_ref[:, 0:CONV_HIST, :] = u_ref[:, tm:tm + CONV_HIST, :]
    o_ref[0] = x_ref[0] + jnp.dot(v_ref[...], w2_ref[...], preferred_element_type=F32) + b2_ref[...]


def conformer_conv(x, g, w1, b1, wdw, bdw, lng, lnb, w2, b2, tm=512, rows_per_chunk=32):
    b, s, d = x.shape
    tm = _pick(s, tm)
    kernel = functools.partial(_conv_body, tm=tm, rows_per_chunk=rows_per_chunk)
    vec = lambda n: pl.BlockSpec((1, n), lambda bi, t: (0, 0))
    return pl.pallas_call(
        kernel,
        grid=(b, s // tm),
        in_specs=[pl.BlockSpec((1, tm, d), lambda bi, t: (bi, t, 0)),
                  vec(d),
                  pl.BlockSpec((d, 2 * d), lambda bi, t: (0, 0)),
                  vec(2 * d),
                  pl.BlockSpec((CONV_WIDTH, d), lambda bi, t: (0, 0)),
                  vec(d), vec(d), vec(d),
                  pl.BlockSpec((d, d), lambda bi, t: (0, 0)),
                  vec(d)],
        out_specs=pl.BlockSpec((1, tm, d), lambda bi, t: (bi, t, 0)),
        out_shape=jax.ShapeDtypeStruct((b, s, d), F32),
        scratch_shapes=[pltpu.VMEM((d // LANES, tm + CONV_HIST, LANES), F32), pltpu.VMEM((tm, d), BF16)],
        compiler_params=_params(2),
        name="conformer_conv",
    )(x, g.reshape(1, d), w1, b1.reshape(1, 2 * d), wdw, bdw.reshape(1, d), lng.reshape(1, d),
      lnb.reshape(1, d), w2, b2.reshape(1, d))


def _router_body(x_ref, g_ref, wr_ref, h_ref, meta_ref, *, n_experts):
    h = _rms(x_ref[...], g_ref[...])
    h_ref[...] = h
    logits = jnp.dot(h.astype(BF16), wr_ref[...], preferred_element_type=F32)
    lane = lax.broadcasted_iota(I32, logits.shape, 1).astype(F32)
    neg = jnp.full_like(logits, -jnp.inf)
    far = jnp.full_like(logits, float(LANES))
    l1 = jnp.where(lane < n_experts, logits, neg)
    m1 = jnp.max(l1, axis=-1, keepdims=True)
    i1 = jnp.min(jnp.where(l1 == m1, lane, far), axis=-1, keepdims=True)
    l2 = jnp.where(lane == i1, neg, l1)
    m2 = jnp.max(l2, axis=-1, keepdims=True)
    i2 = jnp.min(jnp.where(l2 == m2, lane, far), axis=-1, keepdims=True)
    e2 = jnp.exp(m2 - m1)
    w1 = 1.0 / (1.0 + e2)
    w2 = e2 / (1.0 + e2)
    zero = jnp.zeros_like(logits)
    meta_ref[...] = jnp.where(lane == 0, i1,
                    jnp.where(lane == 1, i2,
                    jnp.where(lane == 2, w1, jnp.where(lane == 3, w2, zero))))


def moe_router(x, g, wr, n_experts, tm=512):
    t, d = x.shape
    tm = _pick(t, tm)
    kernel = functools.partial(_router_body, n_experts=n_experts)
    return pl.pallas_call(
        kernel,
        grid=(t // tm,),
        in_specs=[pl.BlockSpec((tm, d), lambda i: (i, 0)),
                  pl.BlockSpec((1, d), lambda i: (0, 0)),
                  pl.BlockSpec((d, LANES), lambda i: (0, 0))],
        out_specs=[pl.BlockSpec((tm, d), lambda i: (i, 0)),
                   pl.BlockSpec((tm, LANES), lambda i: (i, 0))],
        out_shape=[jax.ShapeDtypeStruct((t, d), F32), jax.ShapeDtypeStruct((t, LANES), F32)],
        compiler_params=_params(1),
        name="moe_router",
    )(x, g.reshape(1, d), wr)


def _moe_body(te_ref, nv_ref, rt_cur_ref, rt_nxt_ref, h_hbm, wg_ref, wu_ref, wd_ref, y_ref,
              xg_ref, xb_ref, acc_ref, sem, *, rows):
    i = pl.program_id(0)
    j = pl.program_id(1)
    n_valid = nv_ref[0]

    def row_copy(rt_ref, r, slot):
        return pltpu.make_async_copy(h_hbm.at[pl.ds(rt_ref[0, 0, r], 1), :],
                                     xg_ref.at[slot, pl.ds(r, 1), :], sem.at[slot])

    def gather(rt_ref, slot):
        def body(r, carry):
            row_copy(rt_ref, r, slot).start()
            return carry
        lax.fori_loop(0, rows, body, 0)

    @pl.when((j == 0) & (i == 0) & (n_valid > 0))
    def _():
        gather(rt_cur_ref, 0)

    @pl.when((j == 0) & (i + 1 < n_valid))
    def _():
        gather(rt_nxt_ref, (i + 1) % 2)

    @pl.when((j == 0) & (i < n_valid))
    def _():
        slot = i % 2
        pltpu.make_async_copy(h_hbm.at[pl.ds(0, rows), :], xg_ref.at[slot], sem.at[slot]).wait()
        xb_ref[...] = xg_ref[slot].astype(BF16)
        acc_ref[...] = jnp.zeros(acc_ref.shape, F32)

    @pl.when(i < n_valid)
    def _():
        x = xb_ref[...]
        a = jnp.dot(x, wg_ref[0], preferred_element_type=F32)
        u = jnp.dot(x, wu_ref[0], preferred_element_type=F32)
        act = (a * jax.nn.sigmoid(a) * u).astype(BF16)
        acc_ref[...] += jnp.dot(act, wd_ref[0], preferred_element_type=F32)

    last = j == pl.num_programs(1) - 1

    @pl.when(last & (i < n_valid))
    def _():
        y_ref[...] = acc_ref[...]

    @pl.when(last & (i >= n_valid))
    def _():
        y_ref[...] = jnp.zeros(y_ref.shape, F32)


def moe_experts(h, row_tok, tile_expert, n_valid, wg, wu, wd, rows, tf=512):
    t, d = h.shape
    n_tiles = row_tok.shape[0]
    f = wg.shape[2]
    tf = _pick(f, tf)
    nf = f // tf

    def w_col(i, j, te, nv):
        return (te[i], 0, jnp.where(i < nv[0], j, nf - 1))

    def w_row(i, j, te, nv):
        return (te[i], jnp.where(i < nv[0], j, nf - 1), 0)

    smem_rows = lambda fn: pl.BlockSpec((1, 1, rows), fn, memory_space=pltpu.SMEM)
    grid_spec = pltpu.PrefetchScalarGridSpec(
        num_scalar_prefetch=2,
        grid=(n_tiles, nf),
        in_specs=[smem_rows(lambda i, j, te, nv: (i, 0, 0)),
                  smem_rows(lambda i, j, te, nv: (jnp.minimum(i + 1, n_tiles - 1), 0, 0)),
                  pl.BlockSpec(memory_space=pl.ANY),
                  pl.BlockSpec((1, d, tf), w_col),
                  pl.BlockSpec((1, d, tf), w_col),
                  pl.BlockSpec((1, tf, d), w_row)],
        out_specs=pl.BlockSpec((rows, d), lambda i, j, te, nv: (i, 0)),
        scratch_shapes=[pltpu.VMEM((2, rows, d), F32), pltpu.VMEM((rows, d), BF16),
                        pltpu.VMEM((rows, d), F32), pltpu.SemaphoreType.DMA((2,))],
    )
    return pl.pallas_call(
        functools.partial(_moe_body, rows=rows),
        grid_spec=grid_spec,
        out_shape=jax.ShapeDtypeStruct((n_tiles * rows, d), F32),
        compiler_params=_params(2),
        name="moe_experts",
    )(tile_expert, n_valid, row_tok, row_tok, h, wg, wu, wd)


def _combine_body(d_cur_ref, d_nxt_ref, x_ref, meta_ref, g_ref, y_hbm, o_ref, buf_ref, sem,
                  *, tm, final_norm):
    i = pl.program_id(0)
    n = pl.num_programs(0)

    def gather(d_ref, slot):
        def body(r, carry):
            pltpu.make_async_copy(y_hbm.at[pl.ds(d_ref[0, 0, r], 1), :],
                                  buf_ref.at[slot, pl.ds(r, 1), :], sem.at[slot]).start()
            return carry
        lax.fori_loop(0, TOP_K * tm, body, 0)

    @pl.when(i == 0)
    def _():
        gather(d_cur_ref, 0)

    @pl.when(i + 1 < n)
    def _():
        gather(d_nxt_ref, (i + 1) % 2)

    slot = i % 2
    pltpu.make_async_copy(y_hbm.at[pl.ds(0, TOP_K * tm), :], buf_ref.at[slot], sem.at[slot]).wait()
    meta = meta_ref[...]
    out = (x_ref[...] + meta[:, 2:3] * buf_ref[slot, 0:tm, :]
           + meta[:, 3:4] * buf_ref[slot, tm:2 * tm, :])
    if final_norm:
        out = _rms(out, g_ref[...])
    o_ref[...] = out


def moe_combine(x, meta, dest, y, g, final_norm, tm=256):
    t, d = x.shape
    n = t // tm
    smem = lambda fn: pl.BlockSpec((1, 1, TOP_K * tm), fn, memory_space=pltpu.SMEM)
    return pl.pallas_call(
        functools.partial(_combine_body, tm=tm, final_norm=final_norm),
        grid=(n,),
        in_specs=[smem(lambda i: (i, 0, 0)),
                  smem(lambda i: (jnp.minimum(i + 1, n - 1), 0, 0)),
                  pl.BlockSpec((tm, d), lambda i: (i, 0)),
                  pl.BlockSpec((tm, LANES), lambda i: (i, 0)),
                  pl.BlockSpec((1, d), lambda i: (0, 0)),
                  pl.BlockSpec(memory_space=pl.ANY)],
        out_specs=pl.BlockSpec((tm, d), lambda i: (i, 0)),
        out_shape=jax.ShapeDtypeStruct((t, d), F32),
        scratch_shapes=[pltpu.VMEM((2, TOP_K * tm, d), F32), pltpu.SemaphoreType.DMA((2,))],
        compiler_params=_params(1),
        name="moe_combine",
    )(dest, dest, x, meta, g.reshape(1, d), y)


def _route(meta, n_experts, rows, tm_combine):
    t = meta.shape[0]
    flat_e = meta[:, :TOP_K].astype(I32).reshape(-1)
    onehot = (flat_e[:, None] == jnp.arange(n_experts, dtype=I32)[None, :]).astype(I32)
    csum = jnp.cumsum(onehot, axis=0)
    rank = jnp.sum((csum - onehot) * onehot, axis=1)
    counts = csum[-1]
    padded = (counts + rows - 1) // rows * rows
    padded_ends = jnp.cumsum(padded)
    dest = (padded_ends - padded)[flat_e] + rank
    n_tiles = (t * TOP_K) // rows + n_experts
    tile_expert = jnp.minimum(
        jnp.searchsorted(padded_ends, jnp.arange(n_tiles, dtype=I32) * rows, side="right"),
        n_experts - 1).astype(I32)
    n_valid = (padded_ends[-1] // rows).astype(I32).reshape(1)
    row_tok = jnp.zeros((n_tiles * rows,), I32).at[dest].set(
        jnp.arange(t * TOP_K, dtype=I32) // TOP_K).reshape(n_tiles, 1, rows)
    dest_tiles = dest.reshape(t // tm_combine, tm_combine, TOP_K).transpose(0, 2, 1).reshape(
        t // tm_combine, 1, TOP_K * tm_combine)
    return row_tok, tile_expert, n_valid, dest_tiles


def moe_layer(x, g, wr, wg, wu, wd, final_g, final_norm, rows=512, tm_combine=256):
    t, d = x.shape
    n_experts = wr.shape[1]
    rows = min(rows, t)
    tm_combine = min(tm_combine, t)
    wr_pad = jnp.zeros((d, LANES), BF16).at[:, :n_experts].set(wr.astype(BF16))
    h, meta = moe_router(x, g, wr_pad, n_experts)
    row_tok, tile_expert, n_valid, dest_tiles = _route(meta, n_experts, rows, tm_combine)
    y = moe_experts(h, row_tok, tile_expert, n_valid, wg, wu, wd, rows)
    return moe_combine(x, meta, dest_tiles, y, final_g, final_norm, tm_combine)


def attention_layer(x, norm_g, w_in, lam_params, subln_g, sinks, w_out, layer):
    b, s, d = x.shape
    diff_heads = d // (4 * HEAD_DIM)
    swa_q_heads = d // (2 * HEAD_DIM)
    swa_kv_heads = max(1, swa_q_heads // 4)
    group = swa_q_heads // swa_kv_heads
    a_width = diff_heads * 2 * HEAD_DIM
    lam_init = 0.8 - 0.6 * math.exp(-0.3 * layer)
    x2 = x.reshape(b * s, d)
    proj = rms_matmul(x2, norm_g, w_in).reshape(b, s, -1)
    oa = diff_attention(proj, lam_params, subln_g, diff_heads, lam_init)
    q_col = 3 * a_width
    k_col = q_col + swa_q_heads * HEAD_DIM
    v_col = k_col + swa_kv_heads * HEAD_DIM
    ob = sliding_window_attention(proj, sinks, q_col, k_col, v_col, swa_kv_heads, group)
    x2 = out_projection(x2, oa.reshape(b * s, -1), ob.reshape(b * s, -1), w_out)
    return x2


def kernel(x, attn_norm_g, w_in_att, diff_lambda, diff_subln_g, attn_sinks, w_out_att, ffn_norm_g, w_ffn_gate, w_ffn_up, w_ffn_down, conv_norm_g, w_pw1, b_pw1, w_dw, b_dw, conv_ln_g, conv_ln_b, w_pw2, b_pw2, moe_norm_g, w_router, w_exp_gate, w_exp_up, w_exp_down, final_norm_g):
    b, s, d = x.shape
    depth = attn_norm_g.shape[0] + conv_norm_g.shape[0]
    assert depth % 2 == 0, "the final RMSNorm is fused into the last expert layer"
    bf = lambda w: w.astype(BF16)
    for layer in range(depth):
        i = layer // 2
        if layer % 2 == 0:
            x2 = attention_layer(x, attn_norm_g[i], bf(w_in_att[i]), diff_lambda[i], diff_subln_g[i],
                                 attn_sinks[i], bf(w_out_att[i]), layer)
            x2 = ffn_swiglu(x2, ffn_norm_g[i], bf(w_ffn_gate[i]), bf(w_ffn_up[i]), bf(w_ffn_down[i]))
            x = x2.reshape(b, s, d)
        else:
            x = conformer_conv(x, conv_norm_g[i], bf(w_pw1[i]), b_pw1[i], w_dw[i], b_dw[i],
                               conv_ln_g[i], conv_ln_b[i], bf(w_pw2[i]), b_pw2[i])
            x2 = moe_layer(x.reshape(b * s, d), moe_norm_g[i], w_router[i], bf(w_exp_gate[i]),
                           bf(w_exp_up[i]), bf(w_exp_down[i]), final_norm_g,
                           final_norm=(layer == depth - 1))
            x = x2.reshape(b, s, d)
    return x
```

```python
import functools
import math

import jax
import jax.numpy as jnp
from jax import lax
from jax.experimental import pallas as pl
from jax.experimental.pallas import tpu as pltpu

BF16 = jnp.bfloat16
F32 = jnp.float32
I32 = jnp.int32

RMS_EPS = 1e-6
LN_EPS = 1e-5
HEAD_DIM = 64
ATTN_BLOCK = 128
CONV_WIDTH = 31
TOP_K = 2
LANES = 128
CONV_HIST = 32
V7X_VMEM_LIMIT = 56 * 1024 * 1024


def _params(n_axes):
    return pltpu.CompilerParams(dimension_semantics=("arbitrary",) * n_axes,
                                vmem_limit_bytes=V7X_VMEM_LIMIT)


def _rms(x, g):
    return x * lax.rsqrt(jnp.mean(x * x, axis=-1, keepdims=True) + RMS_EPS) * g


def _pick(n, pref):
    t = min(n, pref)
    while n % t:
        t -= LANES if t > LANES else 8
    return t


def _rms_matmul_body(x_ref, g_ref, w_ref, o_ref):
    h = _rms(x_ref[...], g_ref[...]).astype(BF16)
    o_ref[...] = jnp.dot(h, w_ref[...], preferred_element_type=F32).astype(o_ref.dtype)


def rms_matmul(x, g, w, tm=512):
    t, d = x.shape
    n = w.shape[1]
    tm = _pick(t, tm)
    return pl.pallas_call(
        _rms_matmul_body,
        grid=(t // tm,),
        in_specs=[pl.BlockSpec((tm, d), lambda i: (i, 0)),
                  pl.BlockSpec((1, d), lambda i: (0, 0)),
                  pl.BlockSpec((d, n), lambda i: (0, 0))],
        out_specs=pl.BlockSpec((tm, n), lambda i: (i, 0)),
        out_shape=jax.ShapeDtypeStruct((t, n), BF16),
        compiler_params=_params(1),
        name="rms_inproj",
    )(x, g.reshape(1, d), w)


def _diff_attn_body(q_ref, k_ref, v_ref, lam_ref, g_ref, o_ref, q_ref2, m_ref, l_ref, acc_ref,
                    *, blk, lam_init):
    qi = pl.program_id(2)
    hw = 2 * HEAD_DIM
    lane = lax.broadcasted_iota(I32, (blk, hw), 1)
    q = (q_ref[0].astype(F32) * (HEAD_DIM ** -0.5)).astype(BF16)
    zero = jnp.zeros_like(q)
    q_ref2[...] = jnp.concatenate([jnp.where(lane < HEAD_DIM, q, zero),
                                   jnp.where(lane >= HEAD_DIM, q, zero)], axis=0)

    m_ref[...] = jnp.full(m_ref.shape, -jnp.inf, F32)
    l_ref[...] = jnp.zeros(l_ref.shape, F32)
    acc_ref[...] = jnp.zeros(acc_ref.shape, F32)
    n_lane_tiles = blk // LANES

    def block(ki, masked):
        start = pl.multiple_of(ki * blk, blk)
        k = k_ref[0, pl.ds(start, blk), :]
        v = v_ref[0, pl.ds(start, blk), :]
        s = lax.dot_general(q_ref2[...], k, (((1,), (1,)), ((), ())), preferred_element_type=F32)
        if masked:
            row = lax.broadcasted_iota(I32, (2 * blk, blk), 0)
            row = jnp.where(row >= blk, row - blk, row)
            col = lax.broadcasted_iota(I32, (2 * blk, blk), 1)
            s = jnp.where(col <= row, s, -jnp.inf)
        tiles = [s[:, c * LANES:(c + 1) * LANES] for c in range(n_lane_tiles)]
        m_prev = m_ref[...]
        m_new = jnp.maximum(m_prev, jnp.max(functools.reduce(jnp.maximum, tiles), axis=-1, keepdims=True))
        alpha = jnp.exp(m_prev - m_new)
        p_tiles = [jnp.exp(t - m_new) for t in tiles]
        l_ref[...] = alpha * l_ref[...] + functools.reduce(jnp.add, p_tiles)
        p = jnp.concatenate(p_tiles, axis=1).astype(BF16)
        acc_ref[...] = alpha * acc_ref[...] + jnp.dot(p, v, preferred_element_type=F32)
        m_ref[...] = m_new

    def body(ki, carry):
        block(ki, False)
        return carry

    lax.fori_loop(0, qi, body, 0)
    block(qi, True)

    lp = lam_ref[...]
    lam = (jnp.exp(jnp.sum(lp[0:1] * lp[1:2], axis=-1, keepdims=True))
           - jnp.exp(jnp.sum(lp[2:3] * lp[3:4], axis=-1, keepdims=True)) + lam_init)
    o = acc_ref[...] / jnp.sum(l_ref[...], axis=-1, keepdims=True)
    o = o[0:blk] - lam * o[blk:2 * blk]
    o = _rms(o, g_ref[...]) * (1.0 - lam_init)
    o_ref[0] = o.astype(o_ref.dtype)


def diff_attention(proj, lam_params, subln_g, n_heads, lam_init, blk=512):
    b, s, _ = proj.shape
    blk = _pick(s, blk)
    hw = 2 * HEAD_DIM
    kernel = functools.partial(_diff_attn_body, blk=blk, lam_init=lam_init)
    return pl.pallas_call(
        kernel,
        grid=(b, n_heads, s // blk),
        in_specs=[pl.BlockSpec((1, blk, hw), lambda bi, h, qi: (bi, qi, h)),
                  pl.BlockSpec((1, s, hw), lambda bi, h, qi: (bi, 0, n_heads + h)),
                  pl.BlockSpec((1, s, hw), lambda bi, h, qi: (bi, 0, 2 * n_heads + h)),
                  pl.BlockSpec((4, HEAD_DIM), lambda bi, h, qi: (0, 0)),
                  pl.BlockSpec((1, hw), lambda bi, h, qi: (0, 0))],
        out_specs=pl.BlockSpec((1, blk, hw), lambda bi, h, qi: (bi, qi, h)),
        out_shape=jax.ShapeDtypeStruct((b, s, n_heads * hw), BF16),
        scratch_shapes=[pltpu.VMEM((2 * blk, hw), BF16), pltpu.VMEM((2 * blk, LANES), F32),
                        pltpu.VMEM((2 * blk, LANES), F32), pltpu.VMEM((2 * blk, hw), F32)],
        compiler_params=_params(3),
        name="diff_attention",
    )(proj, proj, proj, lam_params, subln_g.reshape(1, hw))


def _swa_body(sink_ref, q_ref, kc_ref, kp_ref, vc_ref, vp_ref, o_ref, *, tq, kv_heads, group):
    t = pl.program_id(1)
    w = ATTN_BLOCK
    row = lax.broadcasted_iota(I32, (group * w, 2 * w), 0) & (w - 1)
    col = lax.broadcasted_iota(I32, (group * w, 2 * w), 1)
    band = (col > row) & (col <= row + w)
    band_first = band & ((col >= w) | (t > 0))
    for j in range(tq // w):
        rows = slice(j * w, (j + 1) * w)
        if j == 0:
            k2 = jnp.concatenate([kp_ref[0], kc_ref[0, rows, :]], axis=0)
            v2 = jnp.concatenate([vp_ref[0], vc_ref[0, rows, :]], axis=0)
            mask = band_first
        else:
            k2 = kc_ref[0, (j - 1) * w:(j + 1) * w, :]
            v2 = vc_ref[0, (j - 1) * w:(j + 1) * w, :]
            mask = band
        outs = []
        for g in range(kv_heads):
            kk = k2[:, g * HEAD_DIM:(g + 1) * HEAD_DIM]
            vv = v2[:, g * HEAD_DIM:(g + 1) * HEAD_DIM]
            heads = [g * group + i for i in range(group)]
            qs = jnp.concatenate(
                [q_ref[0, rows, h * HEAD_DIM:(h + 1) * HEAD_DIM] for h in heads], axis=0)
            sink = jnp.concatenate([jnp.full((w, 1), sink_ref[h], F32) for h in heads], axis=0)
            s = lax.dot_general(qs, kk, (((1,), (1,)), ((), ())),
                                preferred_element_type=F32) * (HEAD_DIM ** -0.5)
            s = jnp.where(mask, s, -jnp.inf)
            m = jnp.maximum(jnp.max(s, axis=-1, keepdims=True), sink)
            e = jnp.exp(s - m)
            denom = jnp.sum(e, axis=-1, keepdims=True) + jnp.exp(sink - m)
            o = jnp.dot(e.astype(BF16), vv, preferred_element_type=F32) / denom
            outs.extend(o[i * w:(i + 1) * w, :] for i in range(group))
        o_ref[0, rows, :] = jnp.concatenate(outs, axis=1).astype(o_ref.dtype)


def sliding_window_attention(proj, sinks, q_col, k_col, v_col, kv_heads, group, tq=512):
    b, s, _ = proj.shape
    tq = _pick(s, tq)
    qw = kv_heads * group * HEAD_DIM
    kw = kv_heads * HEAD_DIM
    sub = tq // ATTN_BLOCK
    kernel = functools.partial(_swa_body, tq=tq, kv_heads=kv_heads, group=group)
    prev = lambda bi, t: (bi, jnp.maximum(t * sub - 1, 0), 0)
    return pl.pallas_call(
        kernel,
        grid=(b, s // tq),
        in_specs=[pl.BlockSpec(memory_space=pltpu.SMEM),
                  pl.BlockSpec((1, tq, qw), lambda bi, t: (bi, t, q_col // qw)),
                  pl.BlockSpec((1, tq, kw), lambda bi, t: (bi, t, k_col // kw)),
                  pl.BlockSpec((1, ATTN_BLOCK, kw), lambda bi, t: prev(bi, t)[:2] + (k_col // kw,)),
                  pl.BlockSpec((1, tq, kw), lambda bi, t: (bi, t, v_col // kw)),
                  pl.BlockSpec((1, ATTN_BLOCK, kw), lambda bi, t: prev(bi, t)[:2] + (v_col // kw,))],
        out_specs=pl.BlockSpec((1, tq, qw), lambda bi, t: (bi, t, 0)),
        out_shape=jax.ShapeDtypeStruct((b, s, qw), BF16),
        compiler_params=_params(2),
        name="swa_attention",
    )(sinks, proj, proj, proj, proj, proj)


def _outproj_body(x_ref, a_ref, b_ref, w_ref, o_ref):
    mix = jnp.concatenate([a_ref[...], b_ref[...]], axis=1)
    o_ref[...] = x_ref[...] + jnp.dot(mix, w_ref[...], preferred_element_type=F32)


def out_projection(x, oa, ob, w, tm=512):
    t, d = x.shape
    tm = _pick(t, tm)
    return pl.pallas_call(
        _outproj_body,
        grid=(t // tm,),
        in_specs=[pl.BlockSpec((tm, d), lambda i: (i, 0)),
                  pl.BlockSpec((tm, oa.shape[1]), lambda i: (i, 0)),
                  pl.BlockSpec((tm, ob.shape[1]), lambda i: (i, 0)),
                  pl.BlockSpec(w.shape, lambda i: (0, 0))],
        out_specs=pl.BlockSpec((tm, d), lambda i: (i, 0)),
        out_shape=jax.ShapeDtypeStruct((t, d), F32),
        compiler_params=_params(1),
        name="out_projection",
    )(x, oa, ob, w)


def _ffn_body(x_ref, g_ref, wg_ref, wu_ref, wd_ref, o_ref, h_ref, acc_ref):
    j = pl.program_id(1)

    @pl.when(j == 0)
    def _():
        h_ref[...] = _rms(x_ref[...], g_ref[...]).astype(BF16)
        acc_ref[...] = jnp.zeros(acc_ref.shape, F32)

    h = h_ref[...]
    a = jnp.dot(h, wg_ref[...], preferred_element_type=F32)
    u = jnp.dot(h, wu_ref[...], preferred_element_type=F32)
    act = (a * jax.nn.sigmoid(a) * u).astype(BF16)
    acc_ref[...] += jnp.dot(act, wd_ref[...], preferred_element_type=F32)

    @pl.when(j == pl.num_programs(1) - 1)
    def _():
        o_ref[...] = x_ref[...] + acc_ref[...]


def ffn_swiglu(x, g, wg, wu, wd, tm=512, tf=1408):
    t, d = x.shape
    f = wg.shape[1]
    tm = _pick(t, tm)
    tf = _pick(f, tf)
    return pl.pallas_call(
        _ffn_body,
        grid=(t // tm, f // tf),
        in_specs=[pl.BlockSpec((tm, d), lambda i, j: (i, 0)),
                  pl.BlockSpec((1, d), lambda i, j: (0, 0)),
                  pl.BlockSpec((d, tf), lambda i, j: (0, j)),
                  pl.BlockSpec((d, tf), lambda i, j: (0, j)),
                  pl.BlockSpec((tf, d), lambda i, j: (j, 0))],
        out_specs=pl.BlockSpec((tm, d), lambda i, j: (i, 0)),
        out_shape=jax.ShapeDtypeStruct((t, d), F32),
        scratch_shapes=[pltpu.VMEM((tm, d), BF16), pltpu.VMEM((tm, d), F32)],
        compiler_params=_params(2),
        name="ffn_swiglu",
    )(x, g.reshape(1, d), wg, wu, wd)


def _conv_body(x_ref, g_ref, w1_ref, b1_ref, wdw_ref, bdw_ref, lng_ref, lnb_ref, w2_ref, b2_ref,
               o_ref, u_ref, v_ref, *, tm, rows_per_chunk):
    d = x_ref.shape[-1]
    t = pl.program_id(1)

    @pl.when(t == 0)
    def _():
        u_ref[:, 0:CONV_HIST, :] = jnp.zeros((d // LANES, CONV_HIST, LANES), F32)

    h = _rms(x_ref[0], g_ref[...]).astype(BF16)
    z = jnp.dot(h, w1_ref[...], preferred_element_type=F32) + b1_ref[...]
    u = z[:, :d] * jax.nn.sigmoid(z[:, d:])
    n_slabs = d // LANES
    for c in range(n_slabs):
        u_ref[c, CONV_HIST:CONV_HIST + tm, :] = u[:, c * LANES:(c + 1) * LANES]

    first_tap = CONV_HIST - (CONV_WIDTH - 1)

    def chunk(i, carry):
        r0 = pl.multiple_of(i * rows_per_chunk, rows_per_chunk)
        cols = []
        for c in range(n_slabs):
            lanes = slice(c * LANES, (c + 1) * LANES)
            a = jnp.zeros((rows_per_chunk, LANES), F32) + bdw_ref[:, lanes]
            for j in range(CONV_WIDTH):
                taps = u_ref[c, pl.ds(r0 + first_tap + j, rows_per_chunk, stride=1), :]
                a = a + taps * wdw_ref[j:j + 1, lanes]
            cols.append(a)
        acc = jnp.concatenate(cols, axis=1)
        mu = jnp.mean(acc, axis=-1, keepdims=True)
        xc = acc - mu
        var = jnp.mean(xc * xc, axis=-1, keepdims=True)
        y = xc * lax.rsqrt(var + LN_EPS) * lng_ref[...] + lnb_ref[...]
        v_ref[pl.ds(r0, rows_per_chunk), :] = (y * jax.nn.sigmoid(y)).astype(BF16)
        return carry

    lax.fori_loop(0, tm // rows_per_chunk, chunk, 0)
    u_ref[:, 0:CONV_HIST, :] = u_ref[:, tm:tm + CONV_HIST, :]
    o_ref[0] = x_ref[0] + jnp.dot(v_ref[...], w2_ref[...], preferred_element_type=F32) + b2_ref[...]


def conformer_conv(x, g, w1, b1, wdw, bdw, lng, lnb, w2, b2, tm=512, rows_per_chunk=32):
    b, s, d = x.shape
    tm = _pick(s, tm)
    kernel = functools.partial(_conv_body, tm=tm, rows_per_chunk=rows_per_chunk)
    vec = lambda n: pl.BlockSpec((1, n), lambda bi, t: (0, 0))
    return pl.pallas_call(
        kernel,
        grid=(b, s // tm),
        in_specs=[pl.BlockSpec((1, tm, d), lambda bi, t: (bi, t, 0)),
                  vec(d),
                  pl.BlockSpec((d, 2 * d), lambda bi, t: (0, 0)),
                  vec(2 * d),
                  pl.BlockSpec((CONV_WIDTH, d), lambda bi, t: (0, 0)),
                  vec(d), vec(d), vec(d),
                  pl.BlockSpec((d, d), lambda bi, t: (0, 0)),
                  vec(d)],
        out_specs=pl.BlockSpec((1, tm, d), lambda bi, t: (bi, t, 0)),
        out_shape=jax.ShapeDtypeStruct((b, s, d), F32),
        scratch_shapes=[pltpu.VMEM((d // LANES, tm + CONV_HIST, LANES), F32), pltpu.VMEM((tm, d), BF16)],
        compiler_params=_params(2),
        name="conformer_conv",
    )(x, g.reshape(1, d), w1, b1.reshape(1, 2 * d), wdw, bdw.reshape(1, d), lng.reshape(1, d),
      lnb.reshape(1, d), w2, b2.reshape(1, d))


def _router_body(x_ref, g_ref, wr_ref, h_ref, meta_ref, *, n_experts):
    h = _rms(x_ref[...], g_ref[...])
    h_ref[...] = h
    logits = jnp.dot(h.astype(BF16), wr_ref[...], preferred_element_type=F32)
    lane = lax.broadcasted_iota(I32, logits.shape, 1).astype(F32)
    neg = jnp.full_like(logits, -jnp.inf)
    far = jnp.full_like(logits, float(LANES))
    l1 = jnp.where(lane < n_experts, logits, neg)
    m1 = jnp.max(l1, axis=-1, keepdims=True)
    i1 = jnp.min(jnp.where(l1 == m1, lane, far), axis=-1, keepdims=True)
    l2 = jnp.where(lane == i1, neg, l1)
    m2 = jnp.max(l2, axis=-1, keepdims=True)
    i2 = jnp.min(jnp.where(l2 == m2, lane, far), axis=-1, keepdims=True)
    e2 = jnp.exp(m2 - m1)
    w1 = 1.0 / (1.0 + e2)
    w2 = e2 / (1.0 + e2)
    zero = jnp.zeros_like(logits)
    meta_ref[...] = jnp.where(lane == 0, i1,
                    jnp.where(lane == 1, i2,
                    jnp.where(lane == 2, w1, jnp.where(lane == 3, w2, zero))))


def moe_router(x, g, wr, n_experts, tm=512):
    t, d = x.shape
    tm = _pick(t, tm)
    kernel = functools.partial(_router_body, n_experts=n_experts)
    return pl.pallas_call(
        kernel,
        grid=(t // tm,),
        in_specs=[pl.BlockSpec((tm, d), lambda i: (i, 0)),
                  pl.BlockSpec((1, d), lambda i: (0, 0)),
                  pl.BlockSpec((d, LANES), lambda i: (0, 0))],
        out_specs=[pl.BlockSpec((tm, d), lambda i: (i, 0)),
                   pl.BlockSpec((tm, LANES), lambda i: (i, 0))],
        out_shape=[jax.ShapeDtypeStruct((t, d), F32), jax.ShapeDtypeStruct((t, LANES), F32)],
        compiler_params=_params(1),
        name="moe_router",
    )(x, g.reshape(1, d), wr)


GATHER_UNROLL = 8


def _start_row_gather(idx_ref, src_hbm, dst_ref, sem, n_rows):
    def body(i, carry):
        base = pl.multiple_of(i * GATHER_UNROLL, GATHER_UNROLL)
        for k in range(GATHER_UNROLL):
            src_row = idx_ref[0, 0, base + k]
            pltpu.make_async_copy(src_hbm.at[pl.ds(src_row, 1), :],
                                  dst_ref.at[pl.ds(base + k, 1), :], sem).start()
        return carry
    trips = n_rows // GATHER_UNROLL if isinstance(n_rows, int) else lax.div(n_rows, GATHER_UNROLL)
    lax.fori_loop(0, trips, body, 0)


def _moe_body(te_ref, tn_ref, rt_cur_ref, rt_nxt_ref, h_hbm, wg_ref, wu_ref, wd_ref, y_ref,
              xg_ref, xb_ref, sem, *, sub, n_sub):
    i = pl.program_id(0)
    j = pl.program_id(1)
    n_tiles = pl.num_programs(0)
    count = tn_ref[i]

    def gather(rt_ref, tile):
        slot = tile % 2
        _start_row_gather(rt_ref, h_hbm, xg_ref.at[slot], sem.at[slot], tn_ref[tile] * sub)

    @pl.when((j == 0) & (i == 0))
    def _():
        gather(rt_cur_ref, 0)

    @pl.when((j == 0) & (i + 1 < n_tiles))
    def _():
        gather(rt_nxt_ref, i + 1)

    slot = i % 2
    for s in range(n_sub):
        @pl.when((j == 0) & (s < count))
        def _():
            pltpu.make_async_copy(h_hbm.at[pl.ds(0, sub), :], xg_ref.at[slot, pl.ds(s * sub, sub), :],
                                  sem.at[slot]).wait()

    for s in range(n_sub):
        rows = pl.ds(s * sub, sub)

        @pl.when((j == 0) & (s < count))
        def _():
            xb_ref[rows, :] = xg_ref[slot, rows, :].astype(BF16)

        @pl.when(j == 0)
        def _():
            y_ref[rows, :] = jnp.zeros((sub, y_ref.shape[1]), F32)

        @pl.when(s < count)
        def _():
            x = xb_ref[rows, :]
            a = jnp.dot(x, wg_ref[0], preferred_element_type=F32)
            u = jnp.dot(x, wu_ref[0], preferred_element_type=F32)
            act = (a * jax.nn.sigmoid(a) * u).astype(BF16)
            y_ref[rows, :] += jnp.dot(act, wd_ref[0], preferred_element_type=F32)


def moe_experts(h, row_tok, tile_expert, tile_count, wg, wu, wd, sub, tf=512):
    t, d = h.shape
    n_tiles = row_tok.shape[0]
    tile_rows = row_tok.shape[2]
    n_sub = tile_rows // sub
    f = wg.shape[2]
    tf = _pick(f, tf)
    nf = f // tf

    def w_col(i, j, te, tn):
        return (te[i], 0, jnp.where(tn[i] > 0, j, nf - 1))

    def w_row(i, j, te, tn):
        return (te[i], jnp.where(tn[i] > 0, j, nf - 1), 0)

    smem_rows = lambda fn: pl.BlockSpec((1, 1, tile_rows), fn, memory_space=pltpu.SMEM)
    grid_spec = pltpu.PrefetchScalarGridSpec(
        num_scalar_prefetch=2,
        grid=(n_tiles, nf),
        in_specs=[smem_rows(lambda i, j, te, tn: (i, 0, 0)),
                  smem_rows(lambda i, j, te, tn: (jnp.minimum(i + 1, n_tiles - 1), 0, 0)),
                  pl.BlockSpec(memory_space=pl.ANY),
                  pl.BlockSpec((1, d, tf), w_col),
                  pl.BlockSpec((1, d, tf), w_col),
                  pl.BlockSpec((1, tf, d), w_row)],
        out_specs=pl.BlockSpec((tile_rows, d), lambda i, j, te, tn: (i, 0)),
        scratch_shapes=[pltpu.VMEM((2, tile_rows, d), F32), pltpu.VMEM((tile_rows, d), BF16),
                        pltpu.SemaphoreType.DMA((2,))],
    )
    return pl.pallas_call(
        functools.partial(_moe_body, sub=sub, n_sub=n_sub),
        grid_spec=grid_spec,
        out_shape=jax.ShapeDtypeStruct((n_tiles * tile_rows, d), F32),
        compiler_params=_params(2),
        name="moe_experts",
    )(tile_expert, tile_count, row_tok, row_tok, h, wg, wu, wd)


def _combine_body(d_cur_ref, d_nxt_ref, x_ref, meta_ref, g_ref, y_hbm, o_ref, buf_ref, sem,
                  *, tm, final_norm):
    i = pl.program_id(0)
    n = pl.num_programs(0)

    def gather(d_ref, slot):
        _start_row_gather(d_ref, y_hbm, buf_ref.at[slot], sem.at[slot], TOP_K * tm)

    @pl.when(i == 0)
    def _():
        gather(d_cur_ref, 0)

    @pl.when(i + 1 < n)
    def _():
        gather(d_nxt_ref, (i + 1) % 2)

    slot = i % 2
    pltpu.make_async_copy(y_hbm.at[pl.ds(0, TOP_K * tm), :], buf_ref.at[slot], sem.at[slot]).wait()
    meta = meta_ref[...]
    out = (x_ref[...] + meta[:, 2:3] * buf_ref[slot, 0:tm, :]
           + meta[:, 3:4] * buf_ref[slot, tm:2 * tm, :])
    if final_norm:
        out = _rms(out, g_ref[...])
    o_ref[...] = out


def moe_combine(x, meta, dest, y, g, final_norm, tm=256):
    t, d = x.shape
    n = t // tm
    smem = lambda fn: pl.BlockSpec((1, 1, TOP_K * tm), fn, memory_space=pltpu.SMEM)
    return pl.pallas_call(
        functools.partial(_combine_body, tm=tm, final_norm=final_norm),
        grid=(n,),
        in_specs=[smem(lambda i: (i, 0, 0)),
                  smem(lambda i: (jnp.minimum(i + 1, n - 1), 0, 0)),
                  pl.BlockSpec((tm, d), lambda i: (i, 0)),
                  pl.BlockSpec((tm, LANES), lambda i: (i, 0)),
                  pl.BlockSpec((1, d), lambda i: (0, 0)),
                  pl.BlockSpec(memory_space=pl.ANY)],
        out_specs=pl.BlockSpec((tm, d), lambda i: (i, 0)),
        out_shape=jax.ShapeDtypeStruct((t, d), F32),
        scratch_shapes=[pltpu.VMEM((2, TOP_K * tm, d), F32), pltpu.SemaphoreType.DMA((2,))],
        compiler_params=_params(1),
        name="moe_combine",
    )(dest, dest, x, meta, g.reshape(1, d), y)


def _route(meta, n_experts, sub, n_sub, tm_combine):
    t = meta.shape[0]
    tile_rows = sub * n_sub
    flat_e = meta[:, :TOP_K].astype(I32).reshape(-1)
    onehot = (flat_e[:, None] == jnp.arange(n_experts, dtype=I32)[None, :]).astype(I32)
    csum = jnp.cumsum(onehot, axis=0)
    rank = jnp.sum((csum - onehot) * onehot, axis=1)
    subs = (csum[-1] + sub - 1) // sub
    tiles = (subs + n_sub - 1) // n_sub
    tile_ends = jnp.cumsum(tiles)
    tile_starts = tile_ends - tiles
    dest = tile_starts[flat_e] * tile_rows + rank
    n_tiles = ((t * TOP_K) // sub + n_experts + n_sub - 1) // n_sub + n_experts
    idx = jnp.arange(n_tiles, dtype=I32)
    tile_expert = jnp.minimum(jnp.searchsorted(tile_ends, idx, side="right"), n_experts - 1).astype(I32)
    local = idx - tile_starts[tile_expert]
    tile_count = jnp.clip(subs[tile_expert] - local * n_sub, 0, n_sub).astype(I32)
    row_tok = jnp.zeros((n_tiles * tile_rows,), I32).at[dest].set(
        jnp.arange(t * TOP_K, dtype=I32) // TOP_K).reshape(n_tiles, 1, tile_rows)
    dest_tiles = dest.reshape(t // tm_combine, tm_combine, TOP_K).transpose(0, 2, 1).reshape(
        t // tm_combine, 1, TOP_K * tm_combine)
    return row_tok, tile_expert, tile_count, dest_tiles


def moe_layer(x, g, wr, wg, wu, wd, final_g, final_norm, sub=512, n_sub=4, tm_combine=256):
    t, d = x.shape
    n_experts = wr.shape[1]
    tm_combine = min(tm_combine, t)
    wr_pad = jnp.zeros((d, LANES), BF16).at[:, :n_experts].set(wr.astype(BF16))
    h, meta = moe_router(x, g, wr_pad, n_experts)
    row_tok, tile_expert, tile_count, dest_tiles = _route(meta, n_experts, sub, n_sub, tm_combine)
    y = moe_experts(h, row_tok, tile_expert, tile_count, wg, wu, wd, sub)
    return moe_combine(x, meta, dest_tiles, y, final_g, final_norm, tm_combine)


def attention_layer(x, norm_g, w_in, lam_params, subln_g, sinks, w_out, layer):
    b, s, d = x.shape
    diff_heads = d // (4 * HEAD_DIM)
    swa_q_heads = d // (2 * HEAD_DIM)
    swa_kv_heads = max(1, swa_q_heads // 4)
    group = swa_q_heads // swa_kv_heads
    a_width = diff_heads * 2 * HEAD_DIM
    lam_init = 0.8 - 0.6 * math.exp(-0.3 * layer)
    x2 = x.reshape(b * s, d)
    proj = rms_matmul(x2, norm_g, w_in).reshape(b, s, -1)
    oa = diff_attention(proj, lam_params, subln_g, diff_heads, lam_init)
    q_col = 3 * a_width
    k_col = q_col + swa_q_heads * HEAD_DIM
    v_col = k_col + swa_kv_heads * HEAD_DIM
    ob = sliding_window_attention(proj, sinks, q_col, k_col, v_col, swa_kv_heads, group)
    x2 = out_projection(x2, oa.reshape(b * s, -1), ob.reshape(b * s, -1), w_out)
    return x2


def kernel(x, attn_norm_g, w_in_att, diff_lambda, diff_subln_g, attn_sinks, w_out_att, ffn_norm_g, w_ffn_gate, w_ffn_up, w_ffn_down, conv_norm_g, w_pw1, b_pw1, w_dw, b_dw, conv_ln_g, conv_ln_b, w_pw2, b_pw2, moe_norm_g, w_router, w_exp_gate, w_exp_up, w_exp_down, final_norm_g):
    b, s, d = x.shape
    depth = attn_norm_g.shape[0] + conv_norm_g.shape[0]
    assert depth % 2 == 0, "the final RMSNorm is fused into the last expert layer"
    bf = lambda w: w.astype(BF16)
    for layer in range(depth):
        i = layer // 2
        if layer % 2 == 0:
            x2 = attention_layer(x, attn_norm_g[i], bf(w_in_att[i]), diff_lambda[i], diff_subln_g[i],
                                 attn_sinks[i], bf(w_out_att[i]), layer)
            x2 = ffn_swiglu(x2, ffn_norm_g[i], bf(w_ffn_gate[i]), bf(w_ffn_up[i]), bf(w_ffn_down[i]))
            x = x2.reshape(b, s, d)
        else:
            x = conformer_conv(x, conv_norm_g[i], bf(w_pw1[i]), b_pw1[i], w_dw[i], b_dw[i],
                               conv_ln_g[i], conv_ln_b[i], bf(w_pw2[i]), b_pw2[i])
            x2 = moe_layer(x.reshape(b * s, d), moe_norm_g[i], w_router[i], bf(w_exp_gate[i]),
                           bf(w_exp_up[i]), bf(w_exp_down[i]), final_norm_g,
                           final_norm=(layer == depth - 1))
            x = x2.reshape(b, s, d)
    return x
```

```python
import functools
import math

import jax
import jax.numpy as jnp
from jax import lax
from jax.experimental import pallas as pl
from jax.experimental.pallas import tpu as pltpu

BF16 = jnp.bfloat16
F32 = jnp.float32
I32 = jnp.int32

RMS_EPS = 1e-6
LN_EPS = 1e-5
HEAD_DIM = 64
ATTN_BLOCK = 128
CONV_WIDTH = 31
TOP_K = 2
LANES = 128
F32_SUBLANES = 8
CONV_HIST = 32
V7X_VMEM_LIMIT = 56 * 1024 * 1024


def _params(n_axes):
    return pltpu.CompilerParams(dimension_semantics=("arbitrary",) * n_axes,
                                vmem_limit_bytes=V7X_VMEM_LIMIT)


def _rms(x, g):
    return x * lax.rsqrt(jnp.mean(x * x, axis=-1, keepdims=True) + RMS_EPS) * g


def _pick(n, pref):
    t = min(n, pref)
    while n % t:
        t -= LANES if t > LANES else 8
    return t


def _rms_matmul_body(x_ref, g_ref, w_ref, o_ref):
    h = _rms(x_ref[...], g_ref[...]).astype(BF16)
    o_ref[...] = jnp.dot(h, w_ref[...], preferred_element_type=F32).astype(o_ref.dtype)


def rms_matmul(x, g, w, tm=512):
    t, d = x.shape
    n = w.shape[1]
    tm = _pick(t, tm)
    return pl.pallas_call(
        _rms_matmul_body,
        grid=(t // tm,),
        in_specs=[pl.BlockSpec((tm, d), lambda i: (i, 0)),
                  pl.BlockSpec((1, d), lambda i: (0, 0)),
                  pl.BlockSpec((d, n), lambda i: (0, 0))],
        out_specs=pl.BlockSpec((tm, n), lambda i: (i, 0)),
        out_shape=jax.ShapeDtypeStruct((t, n), BF16),
        compiler_params=_params(1),
        name="rms_inproj",
    )(x, g.reshape(1, d), w)


def _diff_attn_body(q_ref, k_ref, v_ref, lam_ref, g_ref, o_ref, q_ref2, m_ref, l_ref, acc_ref,
                    *, blk, lam_init):
    qi = pl.program_id(2)
    hw = 2 * HEAD_DIM
    lane = lax.broadcasted_iota(I32, (blk, hw), 1)
    q = (q_ref[0].astype(F32) * (HEAD_DIM ** -0.5)).astype(BF16)
    zero = jnp.zeros_like(q)
    q_ref2[...] = jnp.concatenate([jnp.where(lane < HEAD_DIM, q, zero),
                                   jnp.where(lane >= HEAD_DIM, q, zero)], axis=0)

    m_ref[...] = jnp.full(m_ref.shape, -jnp.inf, F32)
    l_ref[...] = jnp.zeros(l_ref.shape, F32)
    acc_ref[...] = jnp.zeros(acc_ref.shape, F32)
    n_lane_tiles = blk // LANES

    def block(ki, masked):
        start = pl.multiple_of(ki * blk, blk)
        k = k_ref[0, pl.ds(start, blk), :]
        v = v_ref[0, pl.ds(start, blk), :]
        s = lax.dot_general(q_ref2[...], k, (((1,), (1,)), ((), ())), preferred_element_type=F32)
        if masked:
            row = lax.broadcasted_iota(I32, (2 * blk, blk), 0)
            row = jnp.where(row >= blk, row - blk, row)
            col = lax.broadcasted_iota(I32, (2 * blk, blk), 1)
            s = jnp.where(col <= row, s, -jnp.inf)
        tiles = [s[:, c * LANES:(c + 1) * LANES] for c in range(n_lane_tiles)]
        m_prev = m_ref[...]
        m_new = jnp.maximum(m_prev, jnp.max(functools.reduce(jnp.maximum, tiles), axis=-1, keepdims=True))
        alpha = jnp.exp(m_prev - m_new)
        p_tiles = [jnp.exp(t - m_new) for t in tiles]
        l_ref[...] = alpha * l_ref[...] + functools.reduce(jnp.add, p_tiles)
        p = jnp.concatenate(p_tiles, axis=1).astype(BF16)
        acc_ref[...] = alpha * acc_ref[...] + jnp.dot(p, v, preferred_element_type=F32)
        m_ref[...] = m_new

    def body(ki, carry):
        block(ki, False)
        return carry

    lax.fori_loop(0, qi, body, 0)
    block(qi, True)

    lp = lam_ref[...]
    lam = (jnp.exp(jnp.sum(lp[0:1] * lp[1:2], axis=-1, keepdims=True))
           - jnp.exp(jnp.sum(lp[2:3] * lp[3:4], axis=-1, keepdims=True)) + lam_init)
    o = acc_ref[...] / jnp.sum(l_ref[...], axis=-1, keepdims=True)
    o = o[0:blk] - lam * o[blk:2 * blk]
    o = _rms(o, g_ref[...]) * (1.0 - lam_init)
    o_ref[0] = o.astype(o_ref.dtype)


def diff_attention(proj, lam_params, subln_g, n_heads, lam_init, blk=512):
    b, s, _ = proj.shape
    blk = _pick(s, blk)
    hw = 2 * HEAD_DIM
    kernel = functools.partial(_diff_attn_body, blk=blk, lam_init=lam_init)
    return pl.pallas_call(
        kernel,
        grid=(b, n_heads, s // blk),
        in_specs=[pl.BlockSpec((1, blk, hw), lambda bi, h, qi: (bi, qi, h)),
                  pl.BlockSpec((1, s, hw), lambda bi, h, qi: (bi, 0, n_heads + h)),
                  pl.BlockSpec((1, s, hw), lambda bi, h, qi: (bi, 0, 2 * n_heads + h)),
                  pl.BlockSpec((4, HEAD_DIM), lambda bi, h, qi: (0, 0)),
                  pl.BlockSpec((1, hw), lambda bi, h, qi: (0, 0))],
        out_specs=pl.BlockSpec((1, blk, hw), lambda bi, h, qi: (bi, qi, h)),
        out_shape=jax.ShapeDtypeStruct((b, s, n_heads * hw), BF16),
        scratch_shapes=[pltpu.VMEM((2 * blk, hw), BF16), pltpu.VMEM((2 * blk, LANES), F32),
                        pltpu.VMEM((2 * blk, LANES), F32), pltpu.VMEM((2 * blk, hw), F32)],
        compiler_params=_params(3),
        name="diff_attention",
    )(proj, proj, proj, lam_params, subln_g.reshape(1, hw))


def _swa_body(sink_ref, q_ref, kc_ref, kp_ref, vc_ref, vp_ref, o_ref, *, tq, kv_heads, group):
    t = pl.program_id(1)
    w = ATTN_BLOCK
    row = lax.broadcasted_iota(I32, (group * w, 2 * w), 0) & (w - 1)
    col = lax.broadcasted_iota(I32, (group * w, 2 * w), 1)
    band = (col > row) & (col <= row + w)
    band_first = band & ((col >= w) | (t > 0))
    low_half = lax.broadcasted_iota(I32, (w, LANES), 1) < HEAD_DIM
    for j in range(tq // w):
        rows = slice(j * w, (j + 1) * w)
        if j == 0:
            k2 = jnp.concatenate([kp_ref[0], kc_ref[0, rows, :]], axis=0)
            v2 = jnp.concatenate([vp_ref[0], vc_ref[0, rows, :]], axis=0)
            mask = band_first
        else:
            k2 = kc_ref[0, (j - 1) * w:(j + 1) * w, :]
            v2 = vc_ref[0, (j - 1) * w:(j + 1) * w, :]
            mask = band
        outs = []
        for g in range(kv_heads):
            kg = k2[:, g * HEAD_DIM:(g + 1) * HEAD_DIM]
            vg = v2[:, g * HEAD_DIM:(g + 1) * HEAD_DIM]
            kdup = jnp.concatenate([kg, kg], axis=1)
            vdup = jnp.concatenate([vg, vg], axis=1)
            heads = [g * group + i for i in range(group)]
            q_tiles = []
            for h in heads:
                q = q_ref[0, rows, (h // 2) * LANES:(h // 2 + 1) * LANES]
                q = (q.astype(F32) * (HEAD_DIM ** -0.5)).astype(BF16)
                own_half = low_half if h % 2 == 0 else jnp.logical_not(low_half)
                q_tiles.append(jnp.where(own_half, q, jnp.zeros_like(q)))
            sink = jnp.concatenate([jnp.full((w, LANES), sink_ref[h], F32) for h in heads], axis=0)
            s = lax.dot_general(jnp.concatenate(q_tiles, axis=0), kdup, (((1,), (1,)), ((), ())),
                                preferred_element_type=F32)
            s = jnp.where(mask, s, -jnp.inf)
            s0, s1 = s[:, :w], s[:, w:]
            m = jnp.maximum(jnp.max(jnp.maximum(s0, s1), axis=-1, keepdims=True), sink)
            e0 = jnp.exp(s0 - m)
            e1 = jnp.exp(s1 - m)
            denom = jnp.sum(e0 + e1, axis=-1, keepdims=True) + jnp.exp(sink - m)
            e = jnp.concatenate([e0, e1], axis=1).astype(BF16)
            o = jnp.dot(e, vdup, preferred_element_type=F32) / denom
            outs.extend(o[i * w:(i + 1) * w, :] for i in range(group))
        tiles = [jnp.where(low_half, outs[h], outs[h + 1]) for h in range(0, len(outs), 2)]
        o_ref[0, rows, :] = jnp.concatenate(tiles, axis=1).astype(o_ref.dtype)


def sliding_window_attention(proj, sinks, q_col, k_col, v_col, kv_heads, group, tq=512):
    b, s, _ = proj.shape
    tq = _pick(s, tq)
    qw = kv_heads * group * HEAD_DIM
    kw = kv_heads * HEAD_DIM
    sub = tq // ATTN_BLOCK
    kernel = functools.partial(_swa_body, tq=tq, kv_heads=kv_heads, group=group)
    prev = lambda bi, t: (bi, jnp.maximum(t * sub - 1, 0), 0)
    return pl.pallas_call(
        kernel,
        grid=(b, s // tq),
        in_specs=[pl.BlockSpec(memory_space=pltpu.SMEM),
                  pl.BlockSpec((1, tq, qw), lambda bi, t: (bi, t, q_col // qw)),
                  pl.BlockSpec((1, tq, kw), lambda bi, t: (bi, t, k_col // kw)),
                  pl.BlockSpec((1, ATTN_BLOCK, kw), lambda bi, t: prev(bi, t)[:2] + (k_col // kw,)),
                  pl.BlockSpec((1, tq, kw), lambda bi, t: (bi, t, v_col // kw)),
                  pl.BlockSpec((1, ATTN_BLOCK, kw), lambda bi, t: prev(bi, t)[:2] + (v_col // kw,))],
        out_specs=pl.BlockSpec((1, tq, qw), lambda bi, t: (bi, t, 0)),
        out_shape=jax.ShapeDtypeStruct((b, s, qw), BF16),
        compiler_params=_params(2),
        name="swa_attention",
    )(sinks, proj, proj, proj, proj, proj)


def _outproj_body(x_ref, a_ref, b_ref, w_ref, o_ref):
    mix = jnp.concatenate([a_ref[...], b_ref[...]], axis=1)
    o_ref[...] = x_ref[...] + jnp.dot(mix, w_ref[...], preferred_element_type=F32)


def out_projection(x, oa, ob, w, tm=512):
    t, d = x.shape
    tm = _pick(t, tm)
    return pl.pallas_call(
        _outproj_body,
        grid=(t // tm,),
        in_specs=[pl.BlockSpec((tm, d), lambda i: (i, 0)),
                  pl.BlockSpec((tm, oa.shape[1]), lambda i: (i, 0)),
                  pl.BlockSpec((tm, ob.shape[1]), lambda i: (i, 0)),
                  pl.BlockSpec(w.shape, lambda i: (0, 0))],
        out_specs=pl.BlockSpec((tm, d), lambda i: (i, 0)),
        out_shape=jax.ShapeDtypeStruct((t, d), F32),
        compiler_params=_params(1),
        name="out_projection",
    )(x, oa, ob, w)


def _ffn_body(x_ref, g_ref, wg_ref, wu_ref, wd_ref, o_ref, h_ref, acc_ref):
    j = pl.program_id(1)

    @pl.when(j == 0)
    def _():
        h_ref[...] = _rms(x_ref[...], g_ref[...]).astype(BF16)
        acc_ref[...] = jnp.zeros(acc_ref.shape, F32)

    h = h_ref[...]
    a = jnp.dot(h, wg_ref[...], preferred_element_type=F32)
    u = jnp.dot(h, wu_ref[...], preferred_element_type=F32)
    act = (a * jax.nn.sigmoid(a) * u).astype(BF16)
    acc_ref[...] += jnp.dot(act, wd_ref[...], preferred_element_type=F32)

    @pl.when(j == pl.num_programs(1) - 1)
    def _():
        o_ref[...] = x_ref[...] + acc_ref[...]


def ffn_swiglu(x, g, wg, wu, wd, tm=512, tf=1408):
    t, d = x.shape
    f = wg.shape[1]
    tm = _pick(t, tm)
    tf = _pick(f, tf)
    return pl.pallas_call(
        _ffn_body,
        grid=(t // tm, f // tf),
        in_specs=[pl.BlockSpec((tm, d), lambda i, j: (i, 0)),
                  pl.BlockSpec((1, d), lambda i, j: (0, 0)),
                  pl.BlockSpec((d, tf), lambda i, j: (0, j)),
                  pl.BlockSpec((d, tf), lambda i, j: (0, j)),
                  pl.BlockSpec((tf, d), lambda i, j: (j, 0))],
        out_specs=pl.BlockSpec((tm, d), lambda i, j: (i, 0)),
        out_shape=jax.ShapeDtypeStruct((t, d), F32),
        scratch_shapes=[pltpu.VMEM((tm, d), BF16), pltpu.VMEM((tm, d), F32)],
        compiler_params=_params(2),
        name="ffn_swiglu",
    )(x, g.reshape(1, d), wg, wu, wd)


def _conv_body(x_ref, g_ref, w1_ref, b1_ref, wdw_ref, bdw_ref, lng_ref, lnb_ref, w2_ref, b2_ref,
               o_ref, u_ref, v_ref, wb_ref, c_ref, *, tm, rows_per_chunk):
    d = x_ref.shape[-1]
    t = pl.program_id(1)
    sublanes = wb_ref.shape[1]

    @pl.when(t == 0)
    def _():
        u_ref[:, 0:CONV_HIST, :] = jnp.zeros((d // LANES, CONV_HIST, LANES), F32)
        wb_ref[...] = jnp.broadcast_to(wdw_ref[...][:, None, :], wb_ref.shape)

    h = _rms(x_ref[0], g_ref[...]).astype(BF16)
    z = jnp.dot(h, w1_ref[...], preferred_element_type=F32) + b1_ref[...]
    u = z[:, :d] * jax.nn.sigmoid(z[:, d:])
    n_slabs = d // LANES
    for c in range(n_slabs):
        u_ref[c, CONV_HIST:CONV_HIST + tm, :] = u[:, c * LANES:(c + 1) * LANES]

    first_tap = CONV_HIST - (CONV_WIDTH - 1)
    conv_rows = 4 * sublanes

    for c in range(n_slabs):
        lanes = slice(c * LANES, (c + 1) * LANES)
        w = [wb_ref[j, :, lanes] for j in range(CONV_WIDTH)]
        bias = jnp.zeros((sublanes, LANES), F32) + bdw_ref[:, lanes]

        def rows_block(i, carry, c=c, lanes=lanes, w=w, bias=bias):
            r0 = pl.multiple_of(i * conv_rows, conv_rows)
            for r in range(conv_rows // sublanes):
                sums = [bias, None]
                for j in range(CONV_WIDTH):
                    tap = u_ref[c, pl.ds(r0 + (first_tap + j + r * sublanes), sublanes, stride=1), :] * w[j]
                    sums[j % 2] = tap if sums[j % 2] is None else sums[j % 2] + tap
                c_ref[pl.ds(r0 + r * sublanes, sublanes), lanes] = sums[0] + sums[1]
            return carry

        lax.fori_loop(0, tm // conv_rows, rows_block, 0)

    def chunk(i, carry):
        r0 = pl.multiple_of(i * rows_per_chunk, rows_per_chunk)
        acc = c_ref[pl.ds(r0, rows_per_chunk), :]
        mu = jnp.mean(acc, axis=-1, keepdims=True)
        xc = acc - mu
        var = jnp.mean(xc * xc, axis=-1, keepdims=True)
        y = xc * lax.rsqrt(var + LN_EPS) * lng_ref[...] + lnb_ref[...]
        v_ref[pl.ds(r0, rows_per_chunk), :] = (y * jax.nn.sigmoid(y)).astype(BF16)
        return carry

    lax.fori_loop(0, tm // rows_per_chunk, chunk, 0)
    u_ref[:, 0:CONV_HIST, :] = u_ref[:, tm:tm + CONV_HIST, :]
    o_ref[0] = x_ref[0] + jnp.dot(v_ref[...], w2_ref[...], preferred_element_type=F32) + b2_ref[...]


def conformer_conv(x, g, w1, b1, wdw, bdw, lng, lnb, w2, b2, tm=512, rows_per_chunk=128):
    b, s, d = x.shape
    tm = _pick(s, tm)
    kernel = functools.partial(_conv_body, tm=tm, rows_per_chunk=rows_per_chunk)
    vec = lambda n: pl.BlockSpec((1, n), lambda bi, t: (0, 0))
    return pl.pallas_call(
        kernel,
        grid=(b, s // tm),
        in_specs=[pl.BlockSpec((1, tm, d), lambda bi, t: (bi, t, 0)),
                  vec(d),
                  pl.BlockSpec((d, 2 * d), lambda bi, t: (0, 0)),
                  vec(2 * d),
                  pl.BlockSpec((CONV_WIDTH, d), lambda bi, t: (0, 0)),
                  vec(d), vec(d), vec(d),
                  pl.BlockSpec((d, d), lambda bi, t: (0, 0)),
                  vec(d)],
        out_specs=pl.BlockSpec((1, tm, d), lambda bi, t: (bi, t, 0)),
        out_shape=jax.ShapeDtypeStruct((b, s, d), F32),
        scratch_shapes=[pltpu.VMEM((d // LANES, tm + CONV_HIST, LANES), F32), pltpu.VMEM((tm, d), BF16),
                        pltpu.VMEM((CONV_WIDTH, F32_SUBLANES, d), F32), pltpu.VMEM((tm, d), F32)],
        compiler_params=_params(2),
        name="conformer_conv",
    )(x, g.reshape(1, d), w1, b1.reshape(1, 2 * d), wdw, bdw.reshape(1, d), lng.reshape(1, d),
      lnb.reshape(1, d), w2, b2.reshape(1, d))


def _router_body(x_ref, g_ref, wr_ref, h_ref, meta_ref, *, n_experts):
    h = _rms(x_ref[...], g_ref[...])
    h_ref[...] = h
    logits = jnp.dot(h.astype(BF16), wr_ref[...], preferred_element_type=F32)
    lane = lax.broadcasted_iota(I32, logits.shape, 1).astype(F32)
    neg = jnp.full_like(logits, -jnp.inf)
    far = jnp.full_like(logits, float(LANES))
    l1 = jnp.where(lane < n_experts, logits, neg)
    m1 = jnp.max(l1, axis=-1, keepdims=True)
    i1 = jnp.min(jnp.where(l1 == m1, lane, far), axis=-1, keepdims=True)
    l2 = jnp.where(lane == i1, neg, l1)
    m2 = jnp.max(l2, axis=-1, keepdims=True)
    i2 = jnp.min(jnp.where(l2 == m2, lane, far), axis=-1, keepdims=True)
    e2 = jnp.exp(m2 - m1)
    w1 = 1.0 / (1.0 + e2)
    w2 = e2 / (1.0 + e2)
    zero = jnp.zeros_like(logits)
    meta_ref[...] = jnp.where(lane == 0, i1,
                    jnp.where(lane == 1, i2,
                    jnp.where(lane == 2, w1, jnp.where(lane == 3, w2, zero))))


def moe_router(x, g, wr, n_experts, tm=512):
    t, d = x.shape
    tm = _pick(t, tm)
    kernel = functools.partial(_router_body, n_experts=n_experts)
    return pl.pallas_call(
        kernel,
        grid=(t // tm,),
        in_specs=[pl.BlockSpec((tm, d), lambda i: (i, 0)),
                  pl.BlockSpec((1, d), lambda i: (0, 0)),
                  pl.BlockSpec((d, LANES), lambda i: (0, 0))],
        out_specs=[pl.BlockSpec((tm, d), lambda i: (i, 0)),
                   pl.BlockSpec((tm, LANES), lambda i: (i, 0))],
        out_shape=[jax.ShapeDtypeStruct((t, d), F32), jax.ShapeDtypeStruct((t, LANES), F32)],
        compiler_params=_params(1),
        name="moe_router",
    )(x, g.reshape(1, d), wr)


GATHER_UNROLL = 8


def _start_row_gather(idx_ref, src_hbm, dst_ref, sem, n_rows):
    def body(i, carry):
        base = pl.multiple_of(i * GATHER_UNROLL, GATHER_UNROLL)
        for k in range(GATHER_UNROLL):
            src_row = idx_ref[0, 0, base + k]
            pltpu.make_async_copy(src_hbm.at[pl.ds(src_row, 1), :],
                                  dst_ref.at[pl.ds(base + k, 1), :], sem).start()
        return carry
    trips = n_rows // GATHER_UNROLL if isinstance(n_rows, int) else lax.div(n_rows, GATHER_UNROLL)
    lax.fori_loop(0, trips, body, 0)


def _moe_body(te_ref, tn_ref, rt_cur_ref, rt_nxt_ref, h_hbm, wg_ref, wu_ref, wd_ref, y_ref,
              xg_ref, wgb_ref, wub_ref, wdb_ref, sem, *, sub, n_sub):
    i = pl.program_id(0)
    j = pl.program_id(1)
    n_tiles = pl.num_programs(0)
    count = tn_ref[i]

    def gather(rt_ref, tile):
        slot = tile % 2
        _start_row_gather(rt_ref, h_hbm, xg_ref.at[slot], sem.at[slot], tn_ref[tile] * sub)

    @pl.when((j == 0) & (i == 0))
    def _():
        gather(rt_cur_ref, 0)

    @pl.when((j == 0) & (i + 1 < n_tiles))
    def _():
        gather(rt_nxt_ref, i + 1)

    slot = i % 2
    for s in range(n_sub):
        @pl.when((j == 0) & (s < count))
        def _():
            pltpu.make_async_copy(h_hbm.at[pl.ds(0, sub), :], xg_ref.at[slot, pl.ds(s * sub, sub), :],
                                  sem.at[slot]).wait()

    @pl.when(count > 0)
    def _():
        wgb_ref[...] = wg_ref[0, 0].astype(BF16)
        wub_ref[...] = wu_ref[0, 0].astype(BF16)
        wdb_ref[...] = wd_ref[0, 0].astype(BF16)

    for s in range(n_sub):
        rows = pl.ds(s * sub, sub)

        @pl.when(j == 0)
        def _():
            y_ref[rows, :] = jnp.zeros((sub, y_ref.shape[1]), F32)

        @pl.when(s < count)
        def _():
            x = xg_ref[slot, rows, :].astype(BF16)
            a = jnp.dot(x, wgb_ref[...], preferred_element_type=F32)
            u = jnp.dot(x, wub_ref[...], preferred_element_type=F32)
            act = (a * jax.nn.sigmoid(a) * u).astype(BF16)
            y_ref[rows, :] += jnp.dot(act, wdb_ref[...], preferred_element_type=F32)


def moe_experts(h, row_tok, tile_expert, tile_count, wg, wu, wd, layer, sub, tf=512):
    t, d = h.shape
    n_tiles = row_tok.shape[0]
    tile_rows = row_tok.shape[2]
    n_sub = tile_rows // sub
    f = wg.shape[3]
    tf = _pick(f, tf)
    nf = f // tf

    def w_col(i, j, te, tn):
        return (layer, te[i], 0, jnp.where(tn[i] > 0, j, nf - 1))

    def w_row(i, j, te, tn):
        return (layer, te[i], jnp.where(tn[i] > 0, j, nf - 1), 0)

    smem_rows = lambda fn: pl.BlockSpec((1, 1, tile_rows), fn, memory_space=pltpu.SMEM)
    grid_spec = pltpu.PrefetchScalarGridSpec(
        num_scalar_prefetch=2,
        grid=(n_tiles, nf),
        in_specs=[smem_rows(lambda i, j, te, tn: (i, 0, 0)),
                  smem_rows(lambda i, j, te, tn: (jnp.minimum(i + 1, n_tiles - 1), 0, 0)),
                  pl.BlockSpec(memory_space=pl.ANY),
                  pl.BlockSpec((1, 1, d, tf), w_col),
                  pl.BlockSpec((1, 1, d, tf), w_col),
                  pl.BlockSpec((1, 1, tf, d), w_row)],
        out_specs=pl.BlockSpec((tile_rows, d), lambda i, j, te, tn: (i, 0)),
        scratch_shapes=[pltpu.VMEM((2, tile_rows, d), F32), pltpu.VMEM((d, tf), BF16),
                        pltpu.VMEM((d, tf), BF16), pltpu.VMEM((tf, d), BF16),
                        pltpu.SemaphoreType.DMA((2,))],
    )
    return pl.pallas_call(
        functools.partial(_moe_body, sub=sub, n_sub=n_sub),
        grid_spec=grid_spec,
        out_shape=jax.ShapeDtypeStruct((n_tiles * tile_rows, d), F32),
        compiler_params=_params(2),
        name="moe_experts",
    )(tile_expert, tile_count, row_tok, row_tok, h, wg, wu, wd)


def _combine_body(d_cur_ref, d_nxt_ref, x_ref, meta_ref, g_ref, y_hbm, o_ref, buf_ref, sem,
                  *, tm, final_norm):
    i = pl.program_id(0)
    n = pl.num_programs(0)

    def gather(d_ref, slot):
        _start_row_gather(d_ref, y_hbm, buf_ref.at[slot], sem.at[slot], TOP_K * tm)

    @pl.when(i == 0)
    def _():
        gather(d_cur_ref, 0)

    @pl.when(i + 1 < n)
    def _():
        gather(d_nxt_ref, (i + 1) % 2)

    slot = i % 2
    pltpu.make_async_copy(y_hbm.at[pl.ds(0, TOP_K * tm), :], buf_ref.at[slot], sem.at[slot]).wait()
    meta = meta_ref[...]
    out = (x_ref[...] + meta[:, 2:3] * buf_ref[slot, 0:tm, :]
           + meta[:, 3:4] * buf_ref[slot, tm:2 * tm, :])
    if final_norm:
        out = _rms(out, g_ref[...])
    o_ref[...] = out


def moe_combine(x, meta, dest, y, g, final_norm, tm=256):
    t, d = x.shape
    n = t // tm
    smem = lambda fn: pl.BlockSpec((1, 1, TOP_K * tm), fn, memory_space=pltpu.SMEM)
    return pl.pallas_call(
        functools.partial(_combine_body, tm=tm, final_norm=final_norm),
        grid=(n,),
        in_specs=[smem(lambda i: (i, 0, 0)),
                  smem(lambda i: (jnp.minimum(i + 1, n - 1), 0, 0)),
                  pl.BlockSpec((tm, d), lambda i: (i, 0)),
                  pl.BlockSpec((tm, LANES), lambda i: (i, 0)),
                  pl.BlockSpec((1, d), lambda i: (0, 0)),
                  pl.BlockSpec(memory_space=pl.ANY)],
        out_specs=pl.BlockSpec((tm, d), lambda i: (i, 0)),
        out_shape=jax.ShapeDtypeStruct((t, d), F32),
        scratch_shapes=[pltpu.VMEM((2, TOP_K * tm, d), F32), pltpu.SemaphoreType.DMA((2,))],
        compiler_params=_params(1),
        name="moe_combine",
    )(dest, dest, x, meta, g.reshape(1, d), y)


def _route(meta, n_experts, sub, n_sub, tm_combine):
    t = meta.shape[0]
    tile_rows = sub * n_sub
    flat_e = meta[:, :TOP_K].astype(I32).reshape(-1)
    onehot = (flat_e[:, None] == jnp.arange(n_experts, dtype=I32)[None, :]).astype(I32)
    csum = jnp.cumsum(onehot, axis=0)
    rank = jnp.sum((csum - onehot) * onehot, axis=1)
    subs = (csum[-1] + sub - 1) // sub
    tiles = (subs + n_sub - 1) // n_sub
    tile_ends = jnp.cumsum(tiles)
    tile_starts = tile_ends - tiles
    dest = tile_starts[flat_e] * tile_rows + rank
    n_tiles = ((t * TOP_K) // sub + n_experts + n_sub - 1) // n_sub + n_experts
    idx = jnp.arange(n_tiles, dtype=I32)
    tile_expert = jnp.minimum(jnp.searchsorted(tile_ends, idx, side="right"), n_experts - 1).astype(I32)
    local = idx - tile_starts[tile_expert]
    tile_count = jnp.clip(subs[tile_expert] - local * n_sub, 0, n_sub).astype(I32)
    row_tok = jnp.zeros((n_tiles * tile_rows,), I32).at[dest].set(
        jnp.arange(t * TOP_K, dtype=I32) // TOP_K).reshape(n_tiles, 1, tile_rows)
    dest_tiles = dest.reshape(t // tm_combine, tm_combine, TOP_K).transpose(0, 2, 1).reshape(
        t // tm_combine, 1, TOP_K * tm_combine)
    return row_tok, tile_expert, tile_count, dest_tiles


def moe_layer(x, g, wr, wg, wu, wd, layer, final_g, final_norm, sub=512, n_sub=4, tm_combine=256):
    t, d = x.shape
    n_experts = wr.shape[1]
    tm_combine = min(tm_combine, t)
    wr_pad = jnp.zeros((d, LANES), BF16).at[:, :n_experts].set(wr.astype(BF16))
    h, meta = moe_router(x, g, wr_pad, n_experts)
    row_tok, tile_expert, tile_count, dest_tiles = _route(meta, n_experts, sub, n_sub, tm_combine)
    y = moe_experts(h, row_tok, tile_expert, tile_count, wg, wu, wd, layer, sub)
    return moe_combine(x, meta, dest_tiles, y, final_g, final_norm, tm_combine)


def attention_layer(x, norm_g, w_in, lam_params, subln_g, sinks, w_out, layer):
    b, s, d = x.shape
    diff_heads = d // (4 * HEAD_DIM)
    swa_q_heads = d // (2 * HEAD_DIM)
    swa_kv_heads = max(1, swa_q_heads // 4)
    group = swa_q_heads // swa_kv_heads
    a_width = diff_heads * 2 * HEAD_DIM
    lam_init = 0.8 - 0.6 * math.exp(-0.3 * layer)
    x2 = x.reshape(b * s, d)
    proj = rms_matmul(x2, norm_g, w_in).reshape(b, s, -1)
    oa = diff_attention(proj, lam_params, subln_g, diff_heads, lam_init)
    q_col = 3 * a_width
    k_col = q_col + swa_q_heads * HEAD_DIM
    v_col = k_col + swa_kv_heads * HEAD_DIM
    ob = sliding_window_attention(proj, sinks, q_col, k_col, v_col, swa_kv_heads, group)
    x2 = out_projection(x2, oa.reshape(b * s, -1), ob.reshape(b * s, -1), w_out)
    return x2


def kernel(x, attn_norm_g, w_in_att, diff_lambda, diff_subln_g, attn_sinks, w_out_att, ffn_norm_g, w_ffn_gate, w_ffn_up, w_ffn_down, conv_norm_g, w_pw1, b_pw1, w_dw, b_dw, conv_ln_g, conv_ln_b, w_pw2, b_pw2, moe_norm_g, w_router, w_exp_gate, w_exp_up, w_exp_down, final_norm_g):
    b, s, d = x.shape
    depth = attn_norm_g.shape[0] + conv_norm_g.shape[0]
    assert depth % 2 == 0, "the final RMSNorm is fused into the last expert layer"
    bf = lambda w: w.astype(BF16)
    for layer in range(depth):
        i = layer // 2
        if layer % 2 == 0:
            x2 = attention_layer(x, attn_norm_g[i], bf(w_in_att[i]), diff_lambda[i], diff_subln_g[i],
                                 attn_sinks[i], bf(w_out_att[i]), layer)
            x2 = ffn_swiglu(x2, ffn_norm_g[i], bf(w_ffn_gate[i]), bf(w_ffn_up[i]), bf(w_ffn_down[i]))
            x = x2.reshape(b, s, d)
        else:
            x = conformer_conv(x, conv_norm_g[i], bf(w_pw1[i]), b_pw1[i], w_dw[i], b_dw[i],
                               conv_ln_g[i], conv_ln_b[i], bf(w_pw2[i]), b_pw2[i])
            x2 = moe_layer(x.reshape(b * s, d), moe_norm_g[i], w_router[i], w_exp_gate,
                           w_exp_up, w_exp_down, i, final_norm_g,
                           final_norm=(layer == depth - 1))
            x = x2.reshape(b, s, d)
    return x
```

```python
import functools
import math

import jax
import jax.numpy as jnp
from jax import lax
from jax.experimental import pallas as pl
from jax.experimental.pallas import tpu as pltpu

BF16 = jnp.bfloat16
F32 = jnp.float32
I32 = jnp.int32

LOG2_E = 1.4426950408889634
RMS_EPS = 1e-6
LN_EPS = 1e-5
HEAD_DIM = 64
ATTN_BLOCK = 128
CONV_WIDTH = 31
TOP_K = 2
LANES = 128
F32_SUBLANES = 8
CONV_HIST = 32
V7X_VMEM_LIMIT = 56 * 1024 * 1024


def _params(n_axes, flags=None):
    return pltpu.CompilerParams(dimension_semantics=("arbitrary",) * n_axes,
                                vmem_limit_bytes=V7X_VMEM_LIMIT, flags=flags)


def _rms(x, g):
    return x * lax.rsqrt(jnp.mean(x * x, axis=-1, keepdims=True) + RMS_EPS) * g


def _pick(n, pref):
    t = min(n, pref)
    while n % t:
        t -= LANES if t > LANES else 8
    return t


def _rms_matmul_body(x_ref, g_ref, w_ref, o_ref):
    h = _rms(x_ref[...], g_ref[...]).astype(BF16)
    o_ref[...] = jnp.dot(h, w_ref[...], preferred_element_type=F32).astype(o_ref.dtype)


def rms_matmul(x, g, w, tm=512):
    t, d = x.shape
    n = w.shape[1]
    tm = _pick(t, tm)
    return pl.pallas_call(
        _rms_matmul_body,
        grid=(t // tm,),
        in_specs=[pl.BlockSpec((tm, d), lambda i: (i, 0)),
                  pl.BlockSpec((1, d), lambda i: (0, 0)),
                  pl.BlockSpec((d, n), lambda i: (0, 0))],
        out_specs=pl.BlockSpec((tm, n), lambda i: (i, 0)),
        out_shape=jax.ShapeDtypeStruct((t, n), BF16),
        compiler_params=_params(1),
        name="rms_inproj",
    )(x, g.reshape(1, d), w)


def _diff_attn_body(q_ref, k_ref, v_ref, lam_ref, g_ref, o_ref, q_ref2, m_ref, l_ref, acc_ref,
                    *, blk, lam_init):
    qi = pl.program_id(2)
    hw = 2 * HEAD_DIM
    lane = lax.broadcasted_iota(I32, (blk, hw), 1)
    q = (q_ref[0].astype(F32) * (HEAD_DIM ** -0.5 * LOG2_E)).astype(BF16)
    zero = jnp.zeros_like(q)
    q_ref2[...] = jnp.concatenate([jnp.where(lane < HEAD_DIM, q, zero),
                                   jnp.where(lane >= HEAD_DIM, q, zero)], axis=0)

    m_ref[...] = jnp.full(m_ref.shape, -jnp.inf, F32)
    l_ref[...] = jnp.zeros(l_ref.shape, F32)
    acc_ref[...] = jnp.zeros(acc_ref.shape, F32)
    n_lane_tiles = blk // LANES

    def block(ki, masked):
        start = pl.multiple_of(ki * blk, blk)
        k = k_ref[0, pl.ds(start, blk), :]
        v = v_ref[0, pl.ds(start, blk), :]
        s = lax.dot_general(q_ref2[...], k, (((1,), (1,)), ((), ())), preferred_element_type=F32)
        if masked:
            row = lax.broadcasted_iota(I32, (2 * blk, blk), 0)
            row = jnp.where(row >= blk, row - blk, row)
            col = lax.broadcasted_iota(I32, (2 * blk, blk), 1)
            s = jnp.where(col <= row, s, -jnp.inf)
        tiles = [s[:, c * LANES:(c + 1) * LANES] for c in range(n_lane_tiles)]
        m_prev = m_ref[...]
        m_new = jnp.maximum(m_prev, jnp.max(functools.reduce(jnp.maximum, tiles), axis=-1, keepdims=True))
        alpha = jnp.exp2(m_prev - m_new)
        p_tiles = [jnp.exp2(t - m_new) for t in tiles]
        l_ref[...] = alpha * l_ref[...] + functools.reduce(jnp.add, p_tiles)
        p = jnp.concatenate(p_tiles, axis=1).astype(BF16)
        acc_ref[...] = alpha * acc_ref[...] + jnp.dot(p, v, preferred_element_type=F32)
        m_ref[...] = m_new

    def body(ki, carry):
        block(ki, False)
        return carry

    lax.fori_loop(0, qi, body, 0)
    block(qi, True)

    lp = lam_ref[...]
    lam = (jnp.exp(jnp.sum(lp[0:1] * lp[1:2], axis=-1, keepdims=True))
           - jnp.exp(jnp.sum(lp[2:3] * lp[3:4], axis=-1, keepdims=True)) + lam_init)
    o = acc_ref[...] / jnp.sum(l_ref[...], axis=-1, keepdims=True)
    o = o[0:blk] - lam * o[blk:2 * blk]
    o = _rms(o, g_ref[...]) * (1.0 - lam_init)
    o_ref[0] = o.astype(o_ref.dtype)


def diff_attention(proj, lam_params, subln_g, n_heads, lam_init, blk=512):
    b, s, _ = proj.shape
    blk = _pick(s, blk)
    hw = 2 * HEAD_DIM
    kernel = functools.partial(_diff_attn_body, blk=blk, lam_init=lam_init)
    return pl.pallas_call(
        kernel,
        grid=(b, n_heads, s // blk),
        in_specs=[pl.BlockSpec((1, blk, hw), lambda bi, h, qi: (bi, qi, h)),
                  pl.BlockSpec((1, s, hw), lambda bi, h, qi: (bi, 0, n_heads + h)),
                  pl.BlockSpec((1, s, hw), lambda bi, h, qi: (bi, 0, 2 * n_heads + h)),
                  pl.BlockSpec((4, HEAD_DIM), lambda bi, h, qi: (0, 0)),
                  pl.BlockSpec((1, hw), lambda bi, h, qi: (0, 0))],
        out_specs=pl.BlockSpec((1, blk, hw), lambda bi, h, qi: (bi, qi, h)),
        out_shape=jax.ShapeDtypeStruct((b, s, n_heads * hw), BF16),
        scratch_shapes=[pltpu.VMEM((2 * blk, hw), BF16), pltpu.VMEM((2 * blk, LANES), F32),
                        pltpu.VMEM((2 * blk, LANES), F32), pltpu.VMEM((2 * blk, hw), F32)],
        compiler_params=_params(3),
        name="diff_attention",
    )(proj, proj, proj, lam_params, subln_g.reshape(1, hw))


def _swa_body(sink_ref, q_ref, kc_ref, kp_ref, vc_ref, vp_ref, o_ref, *, tq, kv_heads, group):
    t = pl.program_id(1)
    w = ATTN_BLOCK
    row = lax.broadcasted_iota(I32, (group * w, 2 * w), 0) & (w - 1)
    col = lax.broadcasted_iota(I32, (group * w, 2 * w), 1)
    band = (col > row) & (col <= row + w)
    band_first = band & ((col >= w) | (t > 0))
    low_half = lax.broadcasted_iota(I32, (w, LANES), 1) < HEAD_DIM
    for j in range(tq // w):
        rows = slice(j * w, (j + 1) * w)
        if j == 0:
            k2 = jnp.concatenate([kp_ref[0], kc_ref[0, rows, :]], axis=0)
            v2 = jnp.concatenate([vp_ref[0], vc_ref[0, rows, :]], axis=0)
            mask = band_first
        else:
            k2 = kc_ref[0, (j - 1) * w:(j + 1) * w, :]
            v2 = vc_ref[0, (j - 1) * w:(j + 1) * w, :]
            mask = band
        outs = []
        for g in range(kv_heads):
            kg = k2[:, g * HEAD_DIM:(g + 1) * HEAD_DIM]
            vg = v2[:, g * HEAD_DIM:(g + 1) * HEAD_DIM]
            kdup = jnp.concatenate([kg, kg], axis=1)
            vdup = jnp.concatenate([vg, vg], axis=1)
            heads = [g * group + i for i in range(group)]
            q_tiles = []
            for h in heads:
                q = q_ref[0, rows, (h // 2) * LANES:(h // 2 + 1) * LANES]
                q = (q.astype(F32) * (HEAD_DIM ** -0.5)).astype(BF16)
                own_half = low_half if h % 2 == 0 else jnp.logical_not(low_half)
                q_tiles.append(jnp.where(own_half, q, jnp.zeros_like(q)))
            sink = jnp.concatenate([jnp.full((w, LANES), sink_ref[h], F32) for h in heads], axis=0)
            s = lax.dot_general(jnp.concatenate(q_tiles, axis=0), kdup, (((1,), (1,)), ((), ())),
                                preferred_element_type=F32)
            s = jnp.where(mask, s, -jnp.inf)
            s0, s1 = s[:, :w], s[:, w:]
            m = jnp.maximum(jnp.max(jnp.maximum(s0, s1), axis=-1, keepdims=True), sink)
            e0 = jnp.exp(s0 - m)
            e1 = jnp.exp(s1 - m)
            denom = jnp.sum(e0 + e1, axis=-1, keepdims=True) + jnp.exp(sink - m)
            e = jnp.concatenate([e0, e1], axis=1).astype(BF16)
            o = jnp.dot(e, vdup, preferred_element_type=F32) / denom
            outs.extend(o[i * w:(i + 1) * w, :] for i in range(group))
        tiles = [jnp.where(low_half, outs[h], outs[h + 1]) for h in range(0, len(outs), 2)]
        o_ref[0, rows, :] = jnp.concatenate(tiles, axis=1).astype(o_ref.dtype)


def sliding_window_attention(proj, sinks, q_col, k_col, v_col, kv_heads, group, tq=512):
    b, s, _ = proj.shape
    tq = _pick(s, tq)
    qw = kv_heads * group * HEAD_DIM
    kw = kv_heads * HEAD_DIM
    sub = tq // ATTN_BLOCK
    kernel = functools.partial(_swa_body, tq=tq, kv_heads=kv_heads, group=group)
    prev = lambda bi, t: (bi, jnp.maximum(t * sub - 1, 0), 0)
    return pl.pallas_call(
        kernel,
        grid=(b, s // tq),
        in_specs=[pl.BlockSpec(memory_space=pltpu.SMEM),
                  pl.BlockSpec((1, tq, qw), lambda bi, t: (bi, t, q_col // qw)),
                  pl.BlockSpec((1, tq, kw), lambda bi, t: (bi, t, k_col // kw)),
                  pl.BlockSpec((1, ATTN_BLOCK, kw), lambda bi, t: prev(bi, t)[:2] + (k_col // kw,)),
                  pl.BlockSpec((1, tq, kw), lambda bi, t: (bi, t, v_col // kw)),
                  pl.BlockSpec((1, ATTN_BLOCK, kw), lambda bi, t: prev(bi, t)[:2] + (v_col // kw,))],
        out_specs=pl.BlockSpec((1, tq, qw), lambda bi, t: (bi, t, 0)),
        out_shape=jax.ShapeDtypeStruct((b, s, qw), BF16),
        compiler_params=_params(2),
        name="swa_attention",
    )(sinks, proj, proj, proj, proj, proj)


def _outproj_body(x_ref, a_ref, b_ref, w_ref, o_ref):
    mix = jnp.concatenate([a_ref[...], b_ref[...]], axis=1)
    o_ref[...] = x_ref[...] + jnp.dot(mix, w_ref[...], preferred_element_type=F32)


def out_projection(x, oa, ob, w, tm=512):
    t, d = x.shape
    tm = _pick(t, tm)
    return pl.pallas_call(
        _outproj_body,
        grid=(t // tm,),
        in_specs=[pl.BlockSpec((tm, d), lambda i: (i, 0)),
                  pl.BlockSpec((tm, oa.shape[1]), lambda i: (i, 0)),
                  pl.BlockSpec((tm, ob.shape[1]), lambda i: (i, 0)),
                  pl.BlockSpec(w.shape, lambda i: (0, 0))],
        out_specs=pl.BlockSpec((tm, d), lambda i: (i, 0)),
        out_shape=jax.ShapeDtypeStruct((t, d), F32),
        compiler_params=_params(1),
        name="out_projection",
    )(x, oa, ob, w)


def _ffn_body(x_ref, g_ref, wg_ref, wu_ref, wd_ref, o_ref, h_ref, acc_ref):
    j = pl.program_id(1)

    @pl.when(j == 0)
    def _():
        h_ref[...] = _rms(x_ref[...], g_ref[...]).astype(BF16)
        acc_ref[...] = jnp.zeros(acc_ref.shape, F32)

    h = h_ref[...]
    a = jnp.dot(h, wg_ref[...], preferred_element_type=F32)
    u = jnp.dot(h, wu_ref[...], preferred_element_type=F32)
    act = (a * jax.nn.sigmoid(a) * u).astype(BF16)
    acc_ref[...] += jnp.dot(act, wd_ref[...], preferred_element_type=F32)

    @pl.when(j == pl.num_programs(1) - 1)
    def _():
        o_ref[...] = x_ref[...] + acc_ref[...]


def ffn_swiglu(x, g, wg, wu, wd, tm=512, tf=1408):
    t, d = x.shape
    f = wg.shape[1]
    tm = _pick(t, tm)
    tf = _pick(f, tf)
    return pl.pallas_call(
        _ffn_body,
        grid=(t // tm, f // tf),
        in_specs=[pl.BlockSpec((tm, d), lambda i, j: (i, 0)),
                  pl.BlockSpec((1, d), lambda i, j: (0, 0)),
                  pl.BlockSpec((d, tf), lambda i, j: (0, j)),
                  pl.BlockSpec((d, tf), lambda i, j: (0, j)),
                  pl.BlockSpec((tf, d), lambda i, j: (j, 0))],
        out_specs=pl.BlockSpec((tm, d), lambda i, j: (i, 0)),
        out_shape=jax.ShapeDtypeStruct((t, d), F32),
        scratch_shapes=[pltpu.VMEM((tm, d), BF16), pltpu.VMEM((tm, d), F32)],
        compiler_params=_params(2),
        name="ffn_swiglu",
    )(x, g.reshape(1, d), wg, wu, wd)


def _conv_body(x_ref, g_ref, w1_ref, b1_ref, wdw_ref, bdw_ref, lng_ref, lnb_ref, w2_ref, b2_ref,
               o_ref, u_ref, v_ref, wb_ref, c_ref, *, tm, rows_per_chunk):
    d = x_ref.shape[-1]
    t = pl.program_id(1)
    sublanes = wb_ref.shape[1]

    @pl.when(t == 0)
    def _():
        u_ref[:, 0:CONV_HIST, :] = jnp.zeros((d // LANES, CONV_HIST, LANES), F32)
        wb_ref[...] = jnp.broadcast_to(wdw_ref[...][:, None, :], wb_ref.shape)

    h = _rms(x_ref[0], g_ref[...]).astype(BF16)
    z = jnp.dot(h, w1_ref[...], preferred_element_type=F32) + b1_ref[...]
    u = z[:, :d] * jax.nn.sigmoid(z[:, d:])
    n_slabs = d // LANES
    for c in range(n_slabs):
        u_ref[c, CONV_HIST:CONV_HIST + tm, :] = u[:, c * LANES:(c + 1) * LANES]

    first_tap = CONV_HIST - (CONV_WIDTH - 1)
    conv_rows = 4 * sublanes

    for c in range(n_slabs):
        lanes = slice(c * LANES, (c + 1) * LANES)
        w = [wb_ref[j, :, lanes] for j in range(CONV_WIDTH)]
        bias = jnp.zeros((sublanes, LANES), F32) + bdw_ref[:, lanes]

        def rows_block(i, carry, c=c, lanes=lanes, w=w, bias=bias):
            r0 = pl.multiple_of(i * conv_rows, conv_rows)
            for r in range(conv_rows // sublanes):
                sums = [bias, None]
                for j in range(CONV_WIDTH):
                    tap = u_ref[c, pl.ds(r0 + (first_tap + j + r * sublanes), sublanes, stride=1), :] * w[j]
                    sums[j % 2] = tap if sums[j % 2] is None else sums[j % 2] + tap
                c_ref[pl.ds(r0 + r * sublanes, sublanes), lanes] = sums[0] + sums[1]
            return carry

        lax.fori_loop(0, tm // conv_rows, rows_block, 0)

    def chunk(i, carry):
        r0 = pl.multiple_of(i * rows_per_chunk, rows_per_chunk)
        acc = c_ref[pl.ds(r0, rows_per_chunk), :]
        mu = jnp.mean(acc, axis=-1, keepdims=True)
        xc = acc - mu
        var = jnp.mean(xc * xc, axis=-1, keepdims=True)
        y = xc * lax.rsqrt(var + LN_EPS) * lng_ref[...] + lnb_ref[...]
        v_ref[pl.ds(r0, rows_per_chunk), :] = (y * jax.nn.sigmoid(y)).astype(BF16)
        return carry

    lax.fori_loop(0, tm // rows_per_chunk, chunk, 0)
    u_ref[:, 0:CONV_HIST, :] = u_ref[:, tm:tm + CONV_HIST, :]
    o_ref[0] = x_ref[0] + jnp.dot(v_ref[...], w2_ref[...], preferred_element_type=F32) + b2_ref[...]


def conformer_conv(x, g, w1, b1, wdw, bdw, lng, lnb, w2, b2, tm=512, rows_per_chunk=128):
    b, s, d = x.shape
    tm = _pick(s, tm)
    kernel = functools.partial(_conv_body, tm=tm, rows_per_chunk=rows_per_chunk)
    vec = lambda n: pl.BlockSpec((1, n), lambda bi, t: (0, 0))
    return pl.pallas_call(
        kernel,
        grid=(b, s // tm),
        in_specs=[pl.BlockSpec((1, tm, d), lambda bi, t: (bi, t, 0)),
                  vec(d),
                  pl.BlockSpec((d, 2 * d), lambda bi, t: (0, 0)),
                  vec(2 * d),
                  pl.BlockSpec((CONV_WIDTH, d), lambda bi, t: (0, 0)),
                  vec(d), vec(d), vec(d),
                  pl.BlockSpec((d, d), lambda bi, t: (0, 0)),
                  vec(d)],
        out_specs=pl.BlockSpec((1, tm, d), lambda bi, t: (bi, t, 0)),
        out_shape=jax.ShapeDtypeStruct((b, s, d), F32),
        scratch_shapes=[pltpu.VMEM((d // LANES, tm + CONV_HIST, LANES), F32), pltpu.VMEM((tm, d), BF16),
                        pltpu.VMEM((CONV_WIDTH, F32_SUBLANES, d), F32), pltpu.VMEM((tm, d), F32)],
        compiler_params=_params(2),
        name="conformer_conv",
    )(x, g.reshape(1, d), w1, b1.reshape(1, 2 * d), wdw, bdw.reshape(1, d), lng.reshape(1, d),
      lnb.reshape(1, d), w2, b2.reshape(1, d))


def _router_body(x_ref, g_ref, wr_ref, h_ref, meta_ref, *, n_experts):
    h = _rms(x_ref[...], g_ref[...])
    _store_slabs(h_ref, 0, h)
    logits = jnp.dot(h.astype(BF16), wr_ref[...], preferred_element_type=F32)
    lane = lax.broadcasted_iota(I32, logits.shape, 1).astype(F32)
    neg = jnp.full_like(logits, -jnp.inf)
    far = jnp.full_like(logits, float(LANES))
    l1 = jnp.where(lane < n_experts, logits, neg)
    m1 = jnp.max(l1, axis=-1, keepdims=True)
    i1 = jnp.min(jnp.where(l1 == m1, lane, far), axis=-1, keepdims=True)
    l2 = jnp.where(lane == i1, neg, l1)
    m2 = jnp.max(l2, axis=-1, keepdims=True)
    i2 = jnp.min(jnp.where(l2 == m2, lane, far), axis=-1, keepdims=True)
    e2 = jnp.exp(m2 - m1)
    w1 = 1.0 / (1.0 + e2)
    w2 = e2 / (1.0 + e2)
    zero = jnp.zeros_like(logits)
    meta_ref[...] = jnp.where(lane == 0, i1,
                    jnp.where(lane == 1, i2,
                    jnp.where(lane == 2, w1, jnp.where(lane == 3, w2, zero))))


def moe_router(x, g, wr, n_experts, tm=512):
    t, d = x.shape
    tm = _pick(t, tm)
    kernel = functools.partial(_router_body, n_experts=n_experts)
    return pl.pallas_call(
        kernel,
        grid=(t // tm,),
        in_specs=[pl.BlockSpec((tm, d), lambda i: (i, 0)),
                  pl.BlockSpec((1, d), lambda i: (0, 0)),
                  pl.BlockSpec((d, LANES), lambda i: (0, 0))],
        out_specs=[pl.BlockSpec((tm * (d // LANES), LANES), lambda i: (i, 0)),
                   pl.BlockSpec((tm, LANES), lambda i: (i, 0))],
        out_shape=[jax.ShapeDtypeStruct((t * (d // LANES), LANES), F32),
                   jax.ShapeDtypeStruct((t, LANES), F32)],
        compiler_params=_params(1),
        name="moe_router",
    )(x, g.reshape(1, d), wr)


GATHER_UNROLL = 8


def _store_slabs(ref, first_row, value):
    n, d = value.shape
    n_slabs = d // LANES
    for c in range(n_slabs):
        ref[pl.ds(first_row * n_slabs + c, n, stride=n_slabs), :] = value[:, c * LANES:(c + 1) * LANES]


def _load_slabs(ref, first_row, n, d):
    n_slabs = d // LANES
    return jnp.concatenate([ref[pl.ds(first_row * n_slabs + c, n, stride=n_slabs), :]
                            for c in range(n_slabs)], axis=1)


def _start_row_gather(idx_ref, src_hbm, dst_ref, sem, n_rows, n_slabs):
    def body(i, carry):
        for k in range(GATHER_UNROLL):
            r = i * GATHER_UNROLL + k
            src_row = pl.multiple_of(idx_ref[0, 0, r], n_slabs)
            pltpu.make_async_copy(src_hbm.at[pl.ds(src_row, n_slabs), :],
                                  dst_ref.at[pl.ds(pl.multiple_of(r * n_slabs, n_slabs), n_slabs), :],
                                  sem).start()
        return carry
    trips = n_rows // GATHER_UNROLL if isinstance(n_rows, int) else lax.div(n_rows, GATHER_UNROLL)
    lax.fori_loop(0, trips, body, 0)


def _moe_body(te_ref, tn_ref, rt_cur_ref, rt_nxt_ref, h_hbm, wg_ref, wu_ref, wd_ref, y_ref,
              xg_ref, xb_ref, wgb_ref, wub_ref, wdb_ref, sem, *, sub, n_sub):
    i = pl.program_id(0)
    j = pl.program_id(1)
    n_tiles = pl.num_programs(0)
    count = tn_ref[i]
    d = wg_ref.shape[2]
    n_slabs = d // LANES

    def gather(rt_ref, tile):
        slot = tile % 2
        _start_row_gather(rt_ref, h_hbm, xg_ref.at[slot], sem.at[slot], tn_ref[tile] * sub, n_slabs)

    @pl.when((j == 0) & (i == 0))
    def _():
        gather(rt_cur_ref, 0)

    @pl.when((j == 0) & (i + 1 < n_tiles))
    def _():
        gather(rt_nxt_ref, i + 1)

    slot = i % 2
    for s in range(n_sub):
        @pl.when((j == 0) & (s < count))
        def _():
            pltpu.make_async_copy(h_hbm.at[pl.ds(0, sub * n_slabs), :],
                                  xg_ref.at[slot, pl.ds(s * sub * n_slabs, sub * n_slabs), :],
                                  sem.at[slot]).wait()

    @pl.when(count > 0)
    def _():
        wgb_ref[...] = wg_ref[0, 0].astype(BF16)
        wub_ref[...] = wu_ref[0, 0].astype(BF16)
        wdb_ref[...] = wd_ref[0, 0].astype(BF16)

    for s in range(n_sub):
        rows = pl.ds(s * sub, sub)

        @pl.when((j == 0) & (s < count))
        def _():
            xb_ref[rows, :] = _load_slabs(xg_ref.at[slot], s * sub, sub, d).astype(BF16)

        @pl.when(j == 0)
        def _():
            y_ref[rows, :] = jnp.zeros((sub, d), F32)

        @pl.when(s < count)
        def _():
            x = xb_ref[rows, :]
            a = jnp.dot(x, wgb_ref[...], preferred_element_type=F32)
            u = jnp.dot(x, wub_ref[...], preferred_element_type=F32)
            act = (a * jax.nn.sigmoid(a) * u).astype(BF16)
            y_ref[rows, :] += jnp.dot(act, wdb_ref[...], preferred_element_type=F32)


def moe_experts(h, row_tok, tile_expert, tile_count, wg, wu, wd, layer, sub, tf=512):
    d = wg.shape[2]
    n_slabs = d // LANES
    n_tiles = row_tok.shape[0]
    tile_rows = row_tok.shape[2]
    n_sub = tile_rows // sub
    f = wg.shape[3]
    tf = _pick(f, tf)
    nf = f // tf

    def w_col(i, j, te, tn):
        return (layer, te[i], 0, jnp.where(tn[i] > 0, j, nf - 1))

    def w_row(i, j, te, tn):
        return (layer, te[i], jnp.where(tn[i] > 0, j, nf - 1), 0)

    smem_rows = lambda fn: pl.BlockSpec((1, 1, tile_rows), fn, memory_space=pltpu.SMEM)
    grid_spec = pltpu.PrefetchScalarGridSpec(
        num_scalar_prefetch=2,
        grid=(n_tiles, nf),
        in_specs=[smem_rows(lambda i, j, te, tn: (i, 0, 0)),
                  smem_rows(lambda i, j, te, tn: (jnp.minimum(i + 1, n_tiles - 1), 0, 0)),
                  pl.BlockSpec(memory_space=pl.ANY),
                  pl.BlockSpec((1, 1, d, tf), w_col),
                  pl.BlockSpec((1, 1, d, tf), w_col),
                  pl.BlockSpec((1, 1, tf, d), w_row)],
        out_specs=pl.BlockSpec((tile_rows, d), lambda i, j, te, tn: (i, 0)),
        scratch_shapes=[pltpu.VMEM((2, tile_rows * n_slabs, LANES), F32), pltpu.VMEM((tile_rows, d), BF16),
                        pltpu.VMEM((d, tf), BF16), pltpu.VMEM((d, tf), BF16), pltpu.VMEM((tf, d), BF16),
                        pltpu.SemaphoreType.DMA((2,))],
    )
    return pl.pallas_call(
        functools.partial(_moe_body, sub=sub, n_sub=n_sub),
        grid_spec=grid_spec,
        out_shape=jax.ShapeDtypeStruct((n_tiles * tile_rows, d), F32),
        compiler_params=_params(2),
        name="moe_experts",
    )(tile_expert, tile_count, row_tok, row_tok, h, wg, wu, wd)


def _combine_body(d_cur_ref, d_nxt_ref, x_ref, meta_ref, g_ref, y_hbm, o_ref, buf_ref, sem,
                  *, tm, final_norm):
    i = pl.program_id(0)
    n = pl.num_programs(0)

    def gather(d_ref, slot):
        def body(it, carry):
            for k in range(GATHER_UNROLL):
                r = it * GATHER_UNROLL + k
                pltpu.make_async_copy(y_hbm.at[pl.ds(d_ref[0, 0, r], 1), :],
                                      buf_ref.at[slot, pl.ds(r, 1), :], sem.at[slot]).start()
            return carry
        lax.fori_loop(0, TOP_K * tm // GATHER_UNROLL, body, 0)

    @pl.when(i == 0)
    def _():
        gather(d_cur_ref, 0)

    @pl.when(i + 1 < n)
    def _():
        gather(d_nxt_ref, (i + 1) % 2)

    slot = i % 2
    pltpu.make_async_copy(y_hbm.at[pl.ds(0, TOP_K * tm), :], buf_ref.at[slot], sem.at[slot]).wait()
    meta = meta_ref[...]
    out = (x_ref[...] + meta[:, 2:3] * buf_ref[slot, 0:tm, :]
           + meta[:, 3:4] * buf_ref[slot, tm:2 * tm, :])
    if final_norm:
        out = _rms(out, g_ref[...])
    o_ref[...] = out


def moe_combine(x, meta, dest, y, g, final_norm, tm=256):
    t, d = x.shape
    n = t // tm
    smem = lambda fn: pl.BlockSpec((1, 1, TOP_K * tm), fn, memory_space=pltpu.SMEM)
    return pl.pallas_call(
        functools.partial(_combine_body, tm=tm, final_norm=final_norm),
        grid=(n,),
        in_specs=[smem(lambda i: (i, 0, 0)),
                  smem(lambda i: (jnp.minimum(i + 1, n - 1), 0, 0)),
                  pl.BlockSpec((tm, d), lambda i: (i, 0)),
                  pl.BlockSpec((tm, LANES), lambda i: (i, 0)),
                  pl.BlockSpec((1, d), lambda i: (0, 0)),
                  pl.BlockSpec(memory_space=pl.ANY)],
        out_specs=pl.BlockSpec((tm, d), lambda i: (i, 0)),
        out_shape=jax.ShapeDtypeStruct((t, d), F32),
        scratch_shapes=[pltpu.VMEM((2, TOP_K * tm, d), F32), pltpu.SemaphoreType.DMA((2,))],
        compiler_params=_params(1),
        name="moe_combine",
    )(dest, dest, x, meta, g.reshape(1, d), y)


def _route(meta, n_experts, sub, n_sub, tm_combine, n_slabs):
    t = meta.shape[0]
    tile_rows = sub * n_sub
    flat_e = meta[:, :TOP_K].astype(I32).reshape(-1)
    onehot = (flat_e[:, None] == jnp.arange(n_experts, dtype=I32)[None, :]).astype(I32)
    csum = jnp.cumsum(onehot, axis=0)
    rank = jnp.sum((csum - onehot) * onehot, axis=1)
    subs = (csum[-1] + sub - 1) // sub
    tiles = (subs + n_sub - 1) // n_sub
    tile_ends = jnp.cumsum(tiles)
    tile_starts = tile_ends - tiles
    small = subs // jnp.maximum(tiles, 1)
    extra = subs - small * tiles
    q = rank // sub
    big_part = (extra * (small + 1))[flat_e]
    sm, ex = small[flat_e], extra[flat_e]
    in_big = q < big_part
    tile_local = jnp.where(in_big, q // (sm + 1), ex + (q - big_part) // jnp.maximum(sm, 1))
    sub_local = jnp.where(in_big, q % (sm + 1), (q - big_part) % jnp.maximum(sm, 1))
    dest = (tile_starts[flat_e] + tile_local) * tile_rows + sub_local * sub + rank % sub
    n_tiles = ((t * TOP_K) // sub + n_experts + n_sub - 1) // n_sub + n_experts
    idx = jnp.arange(n_tiles, dtype=I32)
    tile_expert = jnp.minimum(jnp.searchsorted(tile_ends, idx, side="right"), n_experts - 1).astype(I32)
    local = idx - tile_starts[tile_expert]
    tile_count = jnp.where(local < tiles[tile_expert],
                           small[tile_expert] + (local < extra[tile_expert]), 0).astype(I32)
    row_tok = jnp.zeros((n_tiles * tile_rows,), I32).at[dest].set(
        jnp.arange(t * TOP_K, dtype=I32) // TOP_K * n_slabs).reshape(n_tiles, 1, tile_rows)
    dest_tiles = dest.reshape(t // tm_combine, tm_combine, TOP_K).transpose(0, 2, 1).reshape(
        t // tm_combine, 1, TOP_K * tm_combine)
    return row_tok, tile_expert, tile_count, dest_tiles


def moe_layer(x, g, wr, wg, wu, wd, layer, final_g, final_norm, sub=512, n_sub=4, tm_combine=256):
    t, d = x.shape
    n_experts = wr.shape[1]
    tm_combine = min(tm_combine, t)
    wr_pad = jnp.zeros((d, LANES), BF16).at[:, :n_experts].set(wr.astype(BF16))
    h, meta = moe_router(x, g, wr_pad, n_experts)
    row_tok, tile_expert, tile_count, dest_tiles = _route(meta, n_experts, sub, n_sub, tm_combine,
                                                          d // LANES)
    y = moe_experts(h, row_tok, tile_expert, tile_count, wg, wu, wd, layer, sub)
    return moe_combine(x, meta, dest_tiles, y, final_g, final_norm, tm_combine)


def attention_layer(x, norm_g, w_in, lam_params, subln_g, sinks, w_out, layer):
    b, s, d = x.shape
    diff_heads = d // (4 * HEAD_DIM)
    swa_q_heads = d // (2 * HEAD_DIM)
    swa_kv_heads = max(1, swa_q_heads // 4)
    group = swa_q_heads // swa_kv_heads
    a_width = diff_heads * 2 * HEAD_DIM
    lam_init = 0.8 - 0.6 * math.exp(-0.3 * layer)
    x2 = x.reshape(b * s, d)
    proj = rms_matmul(x2, norm_g, w_in).reshape(b, s, -1)
    oa = diff_attention(proj, lam_params, subln_g, diff_heads, lam_init)
    q_col = 3 * a_width
    k_col = q_col + swa_q_heads * HEAD_DIM
    v_col = k_col + swa_kv_heads * HEAD_DIM
    ob = sliding_window_attention(proj, sinks, q_col, k_col, v_col, swa_kv_heads, group)
    x2 = out_projection(x2, oa.reshape(b * s, -1), ob.reshape(b * s, -1), w_out)
    return x2


def kernel(x, attn_norm_g, w_in_att, diff_lambda, diff_subln_g, attn_sinks, w_out_att, ffn_norm_g, w_ffn_gate, w_ffn_up, w_ffn_down, conv_norm_g, w_pw1, b_pw1, w_dw, b_dw, conv_ln_g, conv_ln_b, w_pw2, b_pw2, moe_norm_g, w_router, w_exp_gate, w_exp_up, w_exp_down, final_norm_g):
    b, s, d = x.shape
    depth = attn_norm_g.shape[0] + conv_norm_g.shape[0]
    assert depth % 2 == 0, "the final RMSNorm is fused into the last expert layer"
    bf = lambda w: w.astype(BF16)
    for layer in range(depth):
        i = layer // 2
        if layer % 2 == 0:
            x2 = attention_layer(x, attn_norm_g[i], bf(w_in_att[i]), diff_lambda[i], diff_subln_g[i],
                                 attn_sinks[i], bf(w_out_att[i]), layer)
            x2 = ffn_swiglu(x2, ffn_norm_g[i], bf(w_ffn_gate[i]), bf(w_ffn_up[i]), bf(w_ffn_down[i]))
            x = x2.reshape(b, s, d)
        else:
            x = conformer_conv(x, conv_norm_g[i], bf(w_pw1[i]), b_pw1[i], w_dw[i], b_dw[i],
                               conv_ln_g[i], conv_ln_b[i], bf(w_pw2[i]), b_pw2[i])
            x2 = moe_layer(x.reshape(b * s, d), moe_norm_g[i], w_router[i], w_exp_gate,
                           w_exp_up, w_exp_down, i, final_norm_g,
                           final_norm=(layer == depth - 1))
            x = x2.reshape(b, s, d)
    return x
```

```python
import functools
import math

import jax
import jax.numpy as jnp
from jax import lax
from jax.experimental import pallas as pl
from jax.experimental.pallas import tpu as pltpu

BF16 = jnp.bfloat16
F32 = jnp.float32
I32 = jnp.int32

LOG2_E = 1.4426950408889634
RMS_EPS = 1e-6
LN_EPS = 1e-5
HEAD_DIM = 64
ATTN_BLOCK = 128
CONV_WIDTH = 31
TOP_K = 2
LANES = 128
F32_SUBLANES = 8
CONV_HIST = 32
V7X_VMEM_LIMIT = 56 * 1024 * 1024


def _params(n_axes, flags=None):
    return pltpu.CompilerParams(dimension_semantics=("arbitrary",) * n_axes,
                                vmem_limit_bytes=V7X_VMEM_LIMIT, flags=flags)


def _rms(x, g):
    return x * lax.rsqrt(jnp.mean(x * x, axis=-1, keepdims=True) + RMS_EPS) * g


def _pick(n, pref):
    t = min(n, pref)
    while n % t:
        t -= LANES if t > LANES else 8
    return t


def _rms_matmul_body(x_ref, g_ref, w_ref, o_ref):
    h = _rms(x_ref[...], g_ref[...]).astype(BF16)
    o_ref[...] = jnp.dot(h, w_ref[...], preferred_element_type=F32).astype(o_ref.dtype)


def rms_matmul(x, g, w, tm=512):
    t, d = x.shape
    n = w.shape[1]
    tm = _pick(t, tm)
    return pl.pallas_call(
        _rms_matmul_body,
        grid=(t // tm,),
        in_specs=[pl.BlockSpec((tm, d), lambda i: (i, 0)),
                  pl.BlockSpec((1, d), lambda i: (0, 0)),
                  pl.BlockSpec((d, n), lambda i: (0, 0))],
        out_specs=pl.BlockSpec((tm, n), lambda i: (i, 0)),
        out_shape=jax.ShapeDtypeStruct((t, n), BF16),
        compiler_params=_params(1),
        name="rms_inproj",
    )(x, g.reshape(1, d), w)


def _diff_attn_body(q_ref, k_ref, v_ref, lam_ref, g_ref, o_ref, q_ref2, m_ref, l_ref, acc_ref,
                    *, blk, lam_init):
    qi = pl.program_id(2)
    hw = 2 * HEAD_DIM
    lane = lax.broadcasted_iota(I32, (blk, hw), 1)
    q = (q_ref[0].astype(F32) * (HEAD_DIM ** -0.5 * LOG2_E)).astype(BF16)
    zero = jnp.zeros_like(q)
    q_ref2[...] = jnp.concatenate([jnp.where(lane < HEAD_DIM, q, zero),
                                   jnp.where(lane >= HEAD_DIM, q, zero)], axis=0)

    m_ref[...] = jnp.full(m_ref.shape, -jnp.inf, F32)
    l_ref[...] = jnp.zeros(l_ref.shape, F32)
    acc_ref[...] = jnp.zeros(acc_ref.shape, F32)
    n_lane_tiles = blk // LANES

    def block(ki, masked):
        start = pl.multiple_of(ki * blk, blk)
        k = k_ref[0, pl.ds(start, blk), :]
        v = v_ref[0, pl.ds(start, blk), :]
        s = lax.dot_general(q_ref2[...], k, (((1,), (1,)), ((), ())), preferred_element_type=F32)
        if masked:
            row = lax.broadcasted_iota(I32, (2 * blk, blk), 0)
            row = jnp.where(row >= blk, row - blk, row)
            col = lax.broadcasted_iota(I32, (2 * blk, blk), 1)
            s = jnp.where(col <= row, s, -jnp.inf)
        tiles = [s[:, c * LANES:(c + 1) * LANES] for c in range(n_lane_tiles)]
        m_prev = m_ref[...]
        m_new = jnp.maximum(m_prev, jnp.max(functools.reduce(jnp.maximum, tiles), axis=-1, keepdims=True))
        alpha = jnp.exp2(m_prev - m_new)
        p_tiles = [jnp.exp2(t - m_new) for t in tiles]
        l_ref[...] = alpha * l_ref[...] + functools.reduce(jnp.add, p_tiles)
        p = jnp.concatenate(p_tiles, axis=1).astype(BF16)
        acc_ref[...] = alpha * acc_ref[...] + jnp.dot(p, v, preferred_element_type=F32)
        m_ref[...] = m_new

    def body(ki, carry):
        block(ki, False)
        return carry

    lax.fori_loop(0, qi, body, 0)
    block(qi, True)

    lp = lam_ref[...]
    lam = (jnp.exp(jnp.sum(lp[0:1] * lp[1:2], axis=-1, keepdims=True))
           - jnp.exp(jnp.sum(lp[2:3] * lp[3:4], axis=-1, keepdims=True)) + lam_init)
    o = acc_ref[...] / jnp.sum(l_ref[...], axis=-1, keepdims=True)
    o = o[0:blk] - lam * o[blk:2 * blk]
    o = _rms(o, g_ref[...]) * (1.0 - lam_init)
    o_ref[0] = o.astype(o_ref.dtype)


def diff_attention(proj, lam_params, subln_g, n_heads, lam_init, blk=512):
    b, s, _ = proj.shape
    blk = _pick(s, blk)
    hw = 2 * HEAD_DIM
    kernel = functools.partial(_diff_attn_body, blk=blk, lam_init=lam_init)
    return pl.pallas_call(
        kernel,
        grid=(b, n_heads, s // blk),
        in_specs=[pl.BlockSpec((1, blk, hw), lambda bi, h, qi: (bi, qi, h)),
                  pl.BlockSpec((1, s, hw), lambda bi, h, qi: (bi, 0, n_heads + h)),
                  pl.BlockSpec((1, s, hw), lambda bi, h, qi: (bi, 0, 2 * n_heads + h)),
                  pl.BlockSpec((4, HEAD_DIM), lambda bi, h, qi: (0, 0)),
                  pl.BlockSpec((1, hw), lambda bi, h, qi: (0, 0))],
        out_specs=pl.BlockSpec((1, blk, hw), lambda bi, h, qi: (bi, qi, h)),
        out_shape=jax.ShapeDtypeStruct((b, s, n_heads * hw), BF16),
        scratch_shapes=[pltpu.VMEM((2 * blk, hw), BF16), pltpu.VMEM((2 * blk, LANES), F32),
                        pltpu.VMEM((2 * blk, LANES), F32), pltpu.VMEM((2 * blk, hw), F32)],
        compiler_params=_params(3),
        name="diff_attention",
    )(proj, proj, proj, lam_params, subln_g.reshape(1, hw))


def _swa_body(sink_ref, q_ref, kc_ref, kp_ref, vc_ref, vp_ref, o_ref, *, tq, kv_heads, group):
    t = pl.program_id(1)
    w = ATTN_BLOCK
    row = lax.broadcasted_iota(I32, (group * w, 2 * w), 0) & (w - 1)
    col = lax.broadcasted_iota(I32, (group * w, 2 * w), 1)
    band = (col > row) & (col <= row + w)
    band_first = band & ((col >= w) | (t > 0))
    low_half = lax.broadcasted_iota(I32, (w, LANES), 1) < HEAD_DIM
    for j in range(tq // w):
        rows = slice(j * w, (j + 1) * w)
        if j == 0:
            k2 = jnp.concatenate([kp_ref[0], kc_ref[0, rows, :]], axis=0)
            v2 = jnp.concatenate([vp_ref[0], vc_ref[0, rows, :]], axis=0)
            mask = band_first
        else:
            k2 = kc_ref[0, (j - 1) * w:(j + 1) * w, :]
            v2 = vc_ref[0, (j - 1) * w:(j + 1) * w, :]
            mask = band
        outs = []
        for g in range(kv_heads):
            kg = k2[:, g * HEAD_DIM:(g + 1) * HEAD_DIM]
            vg = v2[:, g * HEAD_DIM:(g + 1) * HEAD_DIM]
            kdup = jnp.concatenate([kg, kg], axis=1)
            vdup = jnp.concatenate([vg, vg], axis=1)
            heads = [g * group + i for i in range(group)]
            q_tiles = []
            for h in heads:
                q = q_ref[0, rows, (h // 2) * LANES:(h // 2 + 1) * LANES]
                q = (q.astype(F32) * (HEAD_DIM ** -0.5)).astype(BF16)
                own_half = low_half if h % 2 == 0 else jnp.logical_not(low_half)
                q_tiles.append(jnp.where(own_half, q, jnp.zeros_like(q)))
            sink = jnp.concatenate([jnp.full((w, LANES), sink_ref[h], F32) for h in heads], axis=0)
            s = lax.dot_general(jnp.concatenate(q_tiles, axis=0), kdup, (((1,), (1,)), ((), ())),
                                preferred_element_type=F32)
            s = jnp.where(mask, s, -jnp.inf)
            s0, s1 = s[:, :w], s[:, w:]
            m = jnp.maximum(jnp.max(jnp.maximum(s0, s1), axis=-1, keepdims=True), sink)
            e0 = jnp.exp(s0 - m)
            e1 = jnp.exp(s1 - m)
            denom = jnp.sum(e0 + e1, axis=-1, keepdims=True) + jnp.exp(sink - m)
            e = jnp.concatenate([e0, e1], axis=1).astype(BF16)
            o = jnp.dot(e, vdup, preferred_element_type=F32) / denom
            outs.extend(o[i * w:(i + 1) * w, :] for i in range(group))
        tiles = [jnp.where(low_half, outs[h], outs[h + 1]) for h in range(0, len(outs), 2)]
        o_ref[0, rows, :] = jnp.concatenate(tiles, axis=1).astype(o_ref.dtype)


def sliding_window_attention(proj, sinks, q_col, k_col, v_col, kv_heads, group, tq=512):
    b, s, _ = proj.shape
    tq = _pick(s, tq)
    qw = kv_heads * group * HEAD_DIM
    kw = kv_heads * HEAD_DIM
    sub = tq // ATTN_BLOCK
    kernel = functools.partial(_swa_body, tq=tq, kv_heads=kv_heads, group=group)
    prev = lambda bi, t: (bi, jnp.maximum(t * sub - 1, 0), 0)
    return pl.pallas_call(
        kernel,
        grid=(b, s // tq),
        in_specs=[pl.BlockSpec(memory_space=pltpu.SMEM),
                  pl.BlockSpec((1, tq, qw), lambda bi, t: (bi, t, q_col // qw)),
                  pl.BlockSpec((1, tq, kw), lambda bi, t: (bi, t, k_col // kw)),
                  pl.BlockSpec((1, ATTN_BLOCK, kw), lambda bi, t: prev(bi, t)[:2] + (k_col // kw,)),
                  pl.BlockSpec((1, tq, kw), lambda bi, t: (bi, t, v_col // kw)),
                  pl.BlockSpec((1, ATTN_BLOCK, kw), lambda bi, t: prev(bi, t)[:2] + (v_col // kw,))],
        out_specs=pl.BlockSpec((1, tq, qw), lambda bi, t: (bi, t, 0)),
        out_shape=jax.ShapeDtypeStruct((b, s, qw), BF16),
        compiler_params=_params(2),
        name="swa_attention",
    )(sinks, proj, proj, proj, proj, proj)


def _outproj_body(x_ref, a_ref, b_ref, w_ref, o_ref):
    mix = jnp.concatenate([a_ref[...], b_ref[...]], axis=1)
    o_ref[...] = x_ref[...] + jnp.dot(mix, w_ref[...], preferred_element_type=F32)


def out_projection(x, oa, ob, w, tm=512):
    t, d = x.shape
    tm = _pick(t, tm)
    return pl.pallas_call(
        _outproj_body,
        grid=(t // tm,),
        in_specs=[pl.BlockSpec((tm, d), lambda i: (i, 0)),
                  pl.BlockSpec((tm, oa.shape[1]), lambda i: (i, 0)),
                  pl.BlockSpec((tm, ob.shape[1]), lambda i: (i, 0)),
                  pl.BlockSpec(w.shape, lambda i: (0, 0))],
        out_specs=pl.BlockSpec((tm, d), lambda i: (i, 0)),
        out_shape=jax.ShapeDtypeStruct((t, d), F32),
        compiler_params=_params(1),
        name="out_projection",
    )(x, oa, ob, w)


def _ffn_body(x_ref, g_ref, wg_ref, wu_ref, wd_ref, o_ref, h_ref, acc_ref):
    j = pl.program_id(1)

    @pl.when(j == 0)
    def _():
        h_ref[...] = _rms(x_ref[...], g_ref[...]).astype(BF16)
        acc_ref[...] = jnp.zeros(acc_ref.shape, F32)

    h = h_ref[...]
    a = jnp.dot(h, wg_ref[...], preferred_element_type=F32)
    u = jnp.dot(h, wu_ref[...], preferred_element_type=F32)
    act = (a * jax.nn.sigmoid(a) * u).astype(BF16)
    acc_ref[...] += jnp.dot(act, wd_ref[...], preferred_element_type=F32)

    @pl.when(j == pl.num_programs(1) - 1)
    def _():
        o_ref[...] = x_ref[...] + acc_ref[...]


def ffn_swiglu(x, g, wg, wu, wd, tm=512, tf=1408):
    t, d = x.shape
    f = wg.shape[1]
    tm = _pick(t, tm)
    tf = _pick(f, tf)
    return pl.pallas_call(
        _ffn_body,
        grid=(t // tm, f // tf),
        in_specs=[pl.BlockSpec((tm, d), lambda i, j: (i, 0)),
                  pl.BlockSpec((1, d), lambda i, j: (0, 0)),
                  pl.BlockSpec((d, tf), lambda i, j: (0, j)),
                  pl.BlockSpec((d, tf), lambda i, j: (0, j)),
                  pl.BlockSpec((tf, d), lambda i, j: (j, 0))],
        out_specs=pl.BlockSpec((tm, d), lambda i, j: (i, 0)),
        out_shape=jax.ShapeDtypeStruct((t, d), F32),
        scratch_shapes=[pltpu.VMEM((tm, d), BF16), pltpu.VMEM((tm, d), F32)],
        compiler_params=_params(2),
        name="ffn_swiglu",
    )(x, g.reshape(1, d), wg, wu, wd)


def _mix_ffn_body(x_ref, a_ref, b_ref, wo_ref, g_ref, wg_ref, wu_ref, wd_ref, o_ref, *, ff_chunk):
    mix = jnp.concatenate([a_ref[...], b_ref[...]], axis=1)
    x1 = x_ref[...] + jnp.dot(mix, wo_ref[...], preferred_element_type=F32)
    h = _rms(x1, g_ref[...]).astype(BF16)
    y = x1
    for c in range(wg_ref.shape[1] // ff_chunk):
        cols = slice(c * ff_chunk, (c + 1) * ff_chunk)
        a = jnp.dot(h, wg_ref[:, cols], preferred_element_type=F32)
        u = jnp.dot(h, wu_ref[:, cols], preferred_element_type=F32)
        act = (a * jax.nn.sigmoid(a) * u).astype(BF16)
        y = y + jnp.dot(act, wd_ref[cols, :], preferred_element_type=F32)
    o_ref[...] = y


def mix_ffn(x, oa, ob, wo, g, wg, wu, wd, tm=512, ff_chunk=1408):
    t, d = x.shape
    f = wg.shape[1]
    tm = _pick(t, tm)
    ff_chunk = _pick(f, ff_chunk)
    resident = lambda shape: pl.BlockSpec(shape, lambda i: (0, 0), pipeline_mode=pl.Buffered(1))
    return pl.pallas_call(
        functools.partial(_mix_ffn_body, ff_chunk=ff_chunk),
        grid=(t // tm,),
        in_specs=[pl.BlockSpec((tm, d), lambda i: (i, 0)),
                  pl.BlockSpec((tm, oa.shape[1]), lambda i: (i, 0)),
                  pl.BlockSpec((tm, ob.shape[1]), lambda i: (i, 0)),
                  resident(wo.shape), resident((1, d)),
                  resident(wg.shape), resident(wu.shape), resident(wd.shape)],
        out_specs=pl.BlockSpec((tm, d), lambda i: (i, 0)),
        out_shape=jax.ShapeDtypeStruct((t, d), F32),
        compiler_params=_params(1),
        name="mix_ffn",
    )(x, oa, ob, wo, g.reshape(1, d), wg, wu, wd)


def _conv_body(x_ref, g_ref, w1_ref, b1_ref, wdw_ref, bdw_ref, lng_ref, lnb_ref, w2_ref, b2_ref,
               o_ref, u_ref, v_ref, wb_ref, c_ref, *, tm, rows_per_chunk):
    d = x_ref.shape[-1]
    t = pl.program_id(1)
    sublanes = wb_ref.shape[1]

    @pl.when(t == 0)
    def _():
        u_ref[:, 0:CONV_HIST, :] = jnp.zeros((d // LANES, CONV_HIST, LANES), F32)
        wb_ref[...] = jnp.broadcast_to(wdw_ref[...][:, None, :], wb_ref.shape)

    h = _rms(x_ref[0], g_ref[...]).astype(BF16)
    z = jnp.dot(h, w1_ref[...], preferred_element_type=F32) + b1_ref[...]
    u = z[:, :d] * jax.nn.sigmoid(z[:, d:])
    n_slabs = d // LANES
    for c in range(n_slabs):
        u_ref[c, CONV_HIST:CONV_HIST + tm, :] = u[:, c * LANES:(c + 1) * LANES]

    first_tap = CONV_HIST - (CONV_WIDTH - 1)
    conv_rows = 4 * sublanes

    for c in range(n_slabs):
        lanes = slice(c * LANES, (c + 1) * LANES)
        w = [wb_ref[j, :, lanes] for j in range(CONV_WIDTH)]
        bias = jnp.zeros((sublanes, LANES), F32) + bdw_ref[:, lanes]

        def rows_block(i, carry, c=c, lanes=lanes, w=w, bias=bias):
            r0 = pl.multiple_of(i * conv_rows, conv_rows)
            for r in range(conv_rows // sublanes):
                sums = [bias, None]
                for j in range(CONV_WIDTH):
                    tap = u_ref[c, pl.ds(r0 + (first_tap + j + r * sublanes), sublanes, stride=1), :] * w[j]
                    sums[j % 2] = tap if sums[j % 2] is None else sums[j % 2] + tap
                c_ref[pl.ds(r0 + r * sublanes, sublanes), lanes] = sums[0] + sums[1]
            return carry

        lax.fori_loop(0, tm // conv_rows, rows_block, 0)

    def chunk(i, carry):
        r0 = pl.multiple_of(i * rows_per_chunk, rows_per_chunk)
        acc = c_ref[pl.ds(r0, rows_per_chunk), :]
        mu = jnp.mean(acc, axis=-1, keepdims=True)
        xc = acc - mu
        var = jnp.mean(xc * xc, axis=-1, keepdims=True)
        y = xc * lax.rsqrt(var + LN_EPS) * lng_ref[...] + lnb_ref[...]
        v_ref[pl.ds(r0, rows_per_chunk), :] = (y * jax.nn.sigmoid(y)).astype(BF16)
        return carry

    lax.fori_loop(0, tm // rows_per_chunk, chunk, 0)
    u_ref[:, 0:CONV_HIST, :] = u_ref[:, tm:tm + CONV_HIST, :]
    o_ref[0] = x_ref[0] + jnp.dot(v_ref[...], w2_ref[...], preferred_element_type=F32) + b2_ref[...]


def conformer_conv(x, g, w1, b1, wdw, bdw, lng, lnb, w2, b2, tm=512, rows_per_chunk=128):
    b, s, d = x.shape
    tm = _pick(s, tm)
    kernel = functools.partial(_conv_body, tm=tm, rows_per_chunk=rows_per_chunk)
    vec = lambda n: pl.BlockSpec((1, n), lambda bi, t: (0, 0))
    return pl.pallas_call(
        kernel,
        grid=(b, s // tm),
        in_specs=[pl.BlockSpec((1, tm, d), lambda bi, t: (bi, t, 0)),
                  vec(d),
                  pl.BlockSpec((d, 2 * d), lambda bi, t: (0, 0)),
                  vec(2 * d),
                  pl.BlockSpec((CONV_WIDTH, d), lambda bi, t: (0, 0)),
                  vec(d), vec(d), vec(d),
                  pl.BlockSpec((d, d), lambda bi, t: (0, 0)),
                  vec(d)],
        out_specs=pl.BlockSpec((1, tm, d), lambda bi, t: (bi, t, 0)),
        out_shape=jax.ShapeDtypeStruct((b, s, d), F32),
        scratch_shapes=[pltpu.VMEM((d // LANES, tm + CONV_HIST, LANES), F32), pltpu.VMEM((tm, d), BF16),
                        pltpu.VMEM((CONV_WIDTH, F32_SUBLANES, d), F32), pltpu.VMEM((tm, d), F32)],
        compiler_params=_params(2),
        name="conformer_conv",
    )(x, g.reshape(1, d), w1, b1.reshape(1, 2 * d), wdw, bdw.reshape(1, d), lng.reshape(1, d),
      lnb.reshape(1, d), w2, b2.reshape(1, d))


def _router_body(x_ref, g_ref, wr_ref, h_ref, meta_ref, *, n_experts):
    h = _rms(x_ref[...], g_ref[...])
    _store_slabs(h_ref, 0, h)
    logits = jnp.dot(h.astype(BF16), wr_ref[...], preferred_element_type=F32)
    lane = lax.broadcasted_iota(I32, logits.shape, 1).astype(F32)
    neg = jnp.full_like(logits, -jnp.inf)
    far = jnp.full_like(logits, float(LANES))
    l1 = jnp.where(lane < n_experts, logits, neg)
    m1 = jnp.max(l1, axis=-1, keepdims=True)
    i1 = jnp.min(jnp.where(l1 == m1, lane, far), axis=-1, keepdims=True)
    l2 = jnp.where(lane == i1, neg, l1)
    m2 = jnp.max(l2, axis=-1, keepdims=True)
    i2 = jnp.min(jnp.where(l2 == m2, lane, far), axis=-1, keepdims=True)
    e2 = jnp.exp(m2 - m1)
    w1 = 1.0 / (1.0 + e2)
    w2 = e2 / (1.0 + e2)
    zero = jnp.zeros_like(logits)
    meta_ref[...] = jnp.where(lane == 0, i1,
                    jnp.where(lane == 1, i2,
                    jnp.where(lane == 2, w1, jnp.where(lane == 3, w2, zero))))


def moe_router(x, g, wr, n_experts, tm=512):
    t, d = x.shape
    tm = _pick(t, tm)
    kernel = functools.partial(_router_body, n_experts=n_experts)
    return pl.pallas_call(
        kernel,
        grid=(t // tm,),
        in_specs=[pl.BlockSpec((tm, d), lambda i: (i, 0)),
                  pl.BlockSpec((1, d), lambda i: (0, 0)),
                  pl.BlockSpec((d, LANES), lambda i: (0, 0))],
        out_specs=[pl.BlockSpec((tm * (d // LANES), LANES), lambda i: (i, 0)),
                   pl.BlockSpec((tm, LANES), lambda i: (i, 0))],
        out_shape=[jax.ShapeDtypeStruct((t * (d // LANES), LANES), F32),
                   jax.ShapeDtypeStruct((t, LANES), F32)],
        compiler_params=_params(1),
        name="moe_router",
    )(x, g.reshape(1, d), wr)


GATHER_UNROLL = 8


def _store_slabs(ref, first_row, value):
    n, d = value.shape
    n_slabs = d // LANES
    for c in range(n_slabs):
        ref[pl.ds(first_row * n_slabs + c, n, stride=n_slabs), :] = value[:, c * LANES:(c + 1) * LANES]


def _load_slabs(ref, first_row, n, d):
    n_slabs = d // LANES
    return jnp.concatenate([ref[pl.ds(first_row * n_slabs + c, n, stride=n_slabs), :]
                            for c in range(n_slabs)], axis=1)


def _start_row_gather(idx_ref, src_hbm, dst_ref, sem, n_rows, n_slabs):
    def body(i, carry):
        for k in range(GATHER_UNROLL):
            r = i * GATHER_UNROLL + k
            src_row = pl.multiple_of(idx_ref[0, 0, r], n_slabs)
            pltpu.make_async_copy(src_hbm.at[pl.ds(src_row, n_slabs), :],
                                  dst_ref.at[pl.ds(pl.multiple_of(r * n_slabs, n_slabs), n_slabs), :],
                                  sem).start()
        return carry
    trips = n_rows // GATHER_UNROLL if isinstance(n_rows, int) else lax.div(n_rows, GATHER_UNROLL)
    lax.fori_loop(0, trips, body, 0)


def _moe_body(te_ref, tn_ref, rt_cur_ref, rt_nxt_ref, h_hbm, wg_ref, wu_ref, wd_ref, y_ref,
              xg_ref, xb_ref, wgb_ref, wub_ref, wdb_ref, sem, *, sub, n_sub):
    i = pl.program_id(0)
    j = pl.program_id(1)
    n_tiles = pl.num_programs(0)
    count = tn_ref[i]
    d = wg_ref.shape[2]
    n_slabs = d // LANES

    def gather(rt_ref, tile):
        slot = tile % 2
        _start_row_gather(rt_ref, h_hbm, xg_ref.at[slot], sem.at[slot], tn_ref[tile] * sub, n_slabs)

    @pl.when((j == 0) & (i == 0))
    def _():
        gather(rt_cur_ref, 0)

    @pl.when((j == 0) & (i + 1 < n_tiles))
    def _():
        gather(rt_nxt_ref, i + 1)

    slot = i % 2
    for s in range(n_sub):
        @pl.when((j == 0) & (s < count))
        def _():
            pltpu.make_async_copy(h_hbm.at[pl.ds(0, sub * n_slabs), :],
                                  xg_ref.at[slot, pl.ds(s * sub * n_slabs, sub * n_slabs), :],
                                  sem.at[slot]).wait()

    @pl.when(count > 0)
    def _():
        wgb_ref[...] = wg_ref[0, 0].astype(BF16)
        wub_ref[...] = wu_ref[0, 0].astype(BF16)
        wdb_ref[...] = wd_ref[0, 0].astype(BF16)

    for s in range(n_sub):
        rows = pl.ds(s * sub, sub)

        @pl.when((j == 0) & (s < count))
        def _():
            xb_ref[rows, :] = _load_slabs(xg_ref.at[slot], s * sub, sub, d).astype(BF16)

        @pl.when(j == 0)
        def _():
            y_ref[rows, :] = jnp.zeros((sub, d), F32)

        @pl.when(s < count)
        def _():
            x = xb_ref[rows, :]
            a = jnp.dot(x, wgb_ref[...], preferred_element_type=F32)
            u = jnp.dot(x, wub_ref[...], preferred_element_type=F32)
            act = (a * jax.nn.sigmoid(a) * u).astype(BF16)
            y_ref[rows, :] += jnp.dot(act, wdb_ref[...], preferred_element_type=F32)


def moe_experts(h, row_tok, tile_expert, tile_count, wg, wu, wd, layer, sub, tf=512):
    d = wg.shape[2]
    n_slabs = d // LANES
    n_tiles = row_tok.shape[0]
    tile_rows = row_tok.shape[2]
    n_sub = tile_rows // sub
    f = wg.shape[3]
    tf = _pick(f, tf)
    nf = f // tf

    def w_col(i, j, te, tn):
        return (layer, te[i], 0, jnp.where(tn[i] > 0, j, nf - 1))

    def w_row(i, j, te, tn):
        return (layer, te[i], jnp.where(tn[i] > 0, j, nf - 1), 0)

    smem_rows = lambda fn: pl.BlockSpec((1, 1, tile_rows), fn, memory_space=pltpu.SMEM)
    grid_spec = pltpu.PrefetchScalarGridSpec(
        num_scalar_prefetch=2,
        grid=(n_tiles, nf),
        in_specs=[smem_rows(lambda i, j, te, tn: (i, 0, 0)),
                  smem_rows(lambda i, j, te, tn: (jnp.minimum(i + 1, n_tiles - 1), 0, 0)),
                  pl.BlockSpec(memory_space=pl.ANY),
                  pl.BlockSpec((1, 1, d, tf), w_col),
                  pl.BlockSpec((1, 1, d, tf), w_col),
                  pl.BlockSpec((1, 1, tf, d), w_row)],
        out_specs=pl.BlockSpec((tile_rows, d), lambda i, j, te, tn: (i, 0)),
        scratch_shapes=[pltpu.VMEM((2, tile_rows * n_slabs, LANES), F32), pltpu.VMEM((tile_rows, d), BF16),
                        pltpu.VMEM((d, tf), BF16), pltpu.VMEM((d, tf), BF16), pltpu.VMEM((tf, d), BF16),
                        pltpu.SemaphoreType.DMA((2,))],
    )
    return pl.pallas_call(
        functools.partial(_moe_body, sub=sub, n_sub=n_sub),
        grid_spec=grid_spec,
        out_shape=jax.ShapeDtypeStruct((n_tiles * tile_rows, d), F32),
        compiler_params=_params(2),
        name="moe_experts",
    )(tile_expert, tile_count, row_tok, row_tok, h, wg, wu, wd)


def _combine_body(d_cur_ref, d_nxt_ref, x_ref, meta_ref, g_ref, y_hbm, o_ref, buf_ref, sem,
                  *, tm, final_norm):
    i = pl.program_id(0)
    n = pl.num_programs(0)

    def gather(d_ref, slot):
        def body(it, carry):
            for k in range(GATHER_UNROLL):
                r = it * GATHER_UNROLL + k
                pltpu.make_async_copy(y_hbm.at[pl.ds(d_ref[0, 0, r], 1), :],
                                      buf_ref.at[slot, pl.ds(r, 1), :], sem.at[slot]).start()
            return carry
        lax.fori_loop(0, TOP_K * tm // GATHER_UNROLL, body, 0)

    @pl.when(i == 0)
    def _():
        gather(d_cur_ref, 0)

    @pl.when(i + 1 < n)
    def _():
        gather(d_nxt_ref, (i + 1) % 2)

    slot = i % 2
    pltpu.make_async_copy(y_hbm.at[pl.ds(0, TOP_K * tm), :], buf_ref.at[slot], sem.at[slot]).wait()
    meta = meta_ref[...]
    out = (x_ref[...] + meta[:, 2:3] * buf_ref[slot, 0:tm, :]
           + meta[:, 3:4] * buf_ref[slot, tm:2 * tm, :])
    if final_norm:
        out = _rms(out, g_ref[...])
    o_ref[...] = out


def moe_combine(x, meta, dest, y, g, final_norm, tm=256):
    t, d = x.shape
    n = t // tm
    smem = lambda fn: pl.BlockSpec((1, 1, TOP_K * tm), fn, memory_space=pltpu.SMEM)
    return pl.pallas_call(
        functools.partial(_combine_body, tm=tm, final_norm=final_norm),
        grid=(n,),
        in_specs=[smem(lambda i: (i, 0, 0)),
                  smem(lambda i: (jnp.minimum(i + 1, n - 1), 0, 0)),
                  pl.BlockSpec((tm, d), lambda i: (i, 0)),
                  pl.BlockSpec((tm, LANES), lambda i: (i, 0)),
                  pl.BlockSpec((1, d), lambda i: (0, 0)),
                  pl.BlockSpec(memory_space=pl.ANY)],
        out_specs=pl.BlockSpec((tm, d), lambda i: (i, 0)),
        out_shape=jax.ShapeDtypeStruct((t, d), F32),
        scratch_shapes=[pltpu.VMEM((2, TOP_K * tm, d), F32), pltpu.SemaphoreType.DMA((2,))],
        compiler_params=_params(1),
        name="moe_combine",
    )(dest, dest, x, meta, g.reshape(1, d), y)


def _route(meta, n_experts, sub, n_sub, tm_combine, n_slabs):
    t = meta.shape[0]
    tile_rows = sub * n_sub
    flat_e = meta[:, :TOP_K].astype(I32).reshape(-1)
    onehot = (flat_e[:, None] == jnp.arange(n_experts, dtype=I32)[None, :]).astype(I32)
    csum = jnp.cumsum(onehot, axis=0)
    rank = jnp.sum((csum - onehot) * onehot, axis=1)
    subs = (csum[-1] + sub - 1) // sub
    tiles = (subs + n_sub - 1) // n_sub
    tile_ends = jnp.cumsum(tiles)
    tile_starts = tile_ends - tiles
    dest = tile_starts[flat_e] * tile_rows + rank
    n_tiles = ((t * TOP_K) // sub + n_experts + n_sub - 1) // n_sub + n_experts
    idx = jnp.arange(n_tiles, dtype=I32)
    tile_expert = jnp.minimum(jnp.searchsorted(tile_ends, idx, side="right"), n_experts - 1).astype(I32)
    local = idx - tile_starts[tile_expert]
    tile_count = jnp.clip(subs[tile_expert] - local * n_sub, 0, n_sub).astype(I32)
    row_tok = jnp.zeros((n_tiles * tile_rows,), I32).at[dest].set(
        jnp.arange(t * TOP_K, dtype=I32) // TOP_K * n_slabs).reshape(n_tiles, 1, tile_rows)
    dest_tiles = dest.reshape(t // tm_combine, tm_combine, TOP_K).transpose(0, 2, 1).reshape(
        t // tm_combine, 1, TOP_K * tm_combine)
    return row_tok, tile_expert, tile_count, dest_tiles


def moe_layer(x, g, wr, wg, wu, wd, layer, final_g, final_norm, sub=512, n_sub=4, tm_combine=256):
    t, d = x.shape
    n_experts = wr.shape[1]
    tm_combine = min(tm_combine, t)
    wr_pad = jnp.zeros((d, LANES), BF16).at[:, :n_experts].set(wr.astype(BF16))
    h, meta = moe_router(x, g, wr_pad, n_experts)
    row_tok, tile_expert, tile_count, dest_tiles = _route(meta, n_experts, sub, n_sub, tm_combine,
                                                          d // LANES)
    y = moe_experts(h, row_tok, tile_expert, tile_count, wg, wu, wd, layer, sub)
    return moe_combine(x, meta, dest_tiles, y, final_g, final_norm, tm_combine)


def attention_mixers(x, norm_g, w_in, lam_params, subln_g, sinks, layer):
    b, s, d = x.shape
    diff_heads = d // (4 * HEAD_DIM)
    swa_q_heads = d // (2 * HEAD_DIM)
    swa_kv_heads = max(1, swa_q_heads // 4)
    group = swa_q_heads // swa_kv_heads
    a_width = diff_heads * 2 * HEAD_DIM
    lam_init = 0.8 - 0.6 * math.exp(-0.3 * layer)
    x2 = x.reshape(b * s, d)
    proj = rms_matmul(x2, norm_g, w_in).reshape(b, s, -1)
    oa = diff_attention(proj, lam_params, subln_g, diff_heads, lam_init)
    q_col = 3 * a_width
    k_col = q_col + swa_q_heads * HEAD_DIM
    v_col = k_col + swa_kv_heads * HEAD_DIM
    ob = sliding_window_attention(proj, sinks, q_col, k_col, v_col, swa_kv_heads, group)
    return oa.reshape(b * s, -1), ob.reshape(b * s, -1)


def attention_layer(x, norm_g, w_in, lam_params, subln_g, sinks, w_out, layer):
    b, s, d = x.shape
    oa, ob = attention_mixers(x, norm_g, w_in, lam_params, subln_g, sinks, layer)
    return out_projection(x.reshape(b * s, d), oa, ob, w_out)


def kernel(x, attn_norm_g, w_in_att, diff_lambda, diff_subln_g, attn_sinks, w_out_att, ffn_norm_g, w_ffn_gate, w_ffn_up, w_ffn_down, conv_norm_g, w_pw1, b_pw1, w_dw, b_dw, conv_ln_g, conv_ln_b, w_pw2, b_pw2, moe_norm_g, w_router, w_exp_gate, w_exp_up, w_exp_down, final_norm_g):
    b, s, d = x.shape
    depth = attn_norm_g.shape[0] + conv_norm_g.shape[0]
    assert depth % 2 == 0, "the final RMSNorm is fused into the last expert layer"
    bf = lambda w: w.astype(BF16)
    for layer in range(depth):
        i = layer // 2
        if layer % 2 == 0:
            oa, ob = attention_mixers(x, attn_norm_g[i], bf(w_in_att[i]), diff_lambda[i], diff_subln_g[i],
                                      attn_sinks[i], layer)
            x2 = mix_ffn(x.reshape(b * s, d), oa, ob, bf(w_out_att[i]), ffn_norm_g[i],
                         bf(w_ffn_gate[i]), bf(w_ffn_up[i]), bf(w_ffn_down[i]))
            x = x2.reshape(b, s, d)
        else:
            x = conformer_conv(x, conv_norm_g[i], bf(w_pw1[i]), b_pw1[i], w_dw[i], b_dw[i],
                               conv_ln_g[i], conv_ln_b[i], bf(w_pw2[i]), b_pw2[i])
            x2 = moe_layer(x.reshape(b * s, d), moe_norm_g[i], w_router[i], w_exp_gate,
                           w_exp_up, w_exp_down, i, final_norm_g,
                           final_norm=(layer == depth - 1))
            x = x2.reshape(b, s, d)
    return x
```

```python
import functools
import math

import jax
import jax.numpy as jnp
from jax import lax
from jax.experimental import pallas as pl
from jax.experimental.pallas import tpu as pltpu

BF16 = jnp.bfloat16
F32 = jnp.float32
I32 = jnp.int32

LOG2_E = 1.4426950408889634
RMS_EPS = 1e-6
LN_EPS = 1e-5
HEAD_DIM = 64
ATTN_BLOCK = 128
CONV_WIDTH = 31
TOP_K = 2
LANES = 128
F32_SUBLANES = 8
CONV_HIST = 32
V7X_VMEM_LIMIT = 56 * 1024 * 1024


def _params(n_axes, flags=None):
    return pltpu.CompilerParams(dimension_semantics=("arbitrary",) * n_axes,
                                vmem_limit_bytes=V7X_VMEM_LIMIT, flags=flags)


def _rms(x, g):
    return x * lax.rsqrt(jnp.mean(x * x, axis=-1, keepdims=True) + RMS_EPS) * g


def _pick(n, pref):
    t = min(n, pref)
    while n % t:
        t -= LANES if t > LANES else 8
    return t


def _rms_matmul_body(x_ref, g_ref, w_ref, o_ref):
    h = _rms(x_ref[...], g_ref[...]).astype(BF16)
    o_ref[...] = jnp.dot(h, w_ref[...], preferred_element_type=F32).astype(o_ref.dtype)


def rms_matmul(x, g, w, tm=512):
    t, d = x.shape
    n = w.shape[1]
    tm = _pick(t, tm)
    return pl.pallas_call(
        _rms_matmul_body,
        grid=(t // tm,),
        in_specs=[pl.BlockSpec((tm, d), lambda i: (i, 0)),
                  pl.BlockSpec((1, d), lambda i: (0, 0)),
                  pl.BlockSpec((d, n), lambda i: (0, 0))],
        out_specs=pl.BlockSpec((tm, n), lambda i: (i, 0)),
        out_shape=jax.ShapeDtypeStruct((t, n), BF16),
        compiler_params=_params(1),
        name="rms_inproj",
    )(x, g.reshape(1, d), w)


def _diff_attn_body(q_ref, k_ref, v_ref, lam_ref, g_ref, o_ref, q_ref2, m_ref, l_ref, acc_ref,
                    *, blk, lam_init):
    qi = pl.program_id(2)
    hw = 2 * HEAD_DIM
    lane = lax.broadcasted_iota(I32, (blk, hw), 1)
    q = (q_ref[0].astype(F32) * (HEAD_DIM ** -0.5 * LOG2_E)).astype(BF16)
    zero = jnp.zeros_like(q)
    q_ref2[...] = jnp.concatenate([jnp.where(lane < HEAD_DIM, q, zero),
                                   jnp.where(lane >= HEAD_DIM, q, zero)], axis=0)

    m_ref[...] = jnp.full(m_ref.shape, -jnp.inf, F32)
    l_ref[...] = jnp.zeros(l_ref.shape, F32)
    acc_ref[...] = jnp.zeros(acc_ref.shape, F32)
    n_lane_tiles = blk // LANES

    def block(ki, masked):
        start = pl.multiple_of(ki * blk, blk)
        k = k_ref[0, pl.ds(start, blk), :]
        v = v_ref[0, pl.ds(start, blk), :]
        s = lax.dot_general(q_ref2[...], k, (((1,), (1,)), ((), ())), preferred_element_type=F32)
        if masked:
            row = lax.broadcasted_iota(I32, (2 * blk, blk), 0)
            row = jnp.where(row >= blk, row - blk, row)
            col = lax.broadcasted_iota(I32, (2 * blk, blk), 1)
            s = jnp.where(col <= row, s, -jnp.inf)
        tiles = [s[:, c * LANES:(c + 1) * LANES] for c in range(n_lane_tiles)]
        m_prev = m_ref[...]
        m_new = jnp.maximum(m_prev, jnp.max(functools.reduce(jnp.maximum, tiles), axis=-1, keepdims=True))
        alpha = jnp.exp2(m_prev - m_new)
        p_tiles = [jnp.exp2(t - m_new) for t in tiles]
        l_ref[...] = alpha * l_ref[...] + functools.reduce(jnp.add, p_tiles)
        p = jnp.concatenate(p_tiles, axis=1).astype(BF16)
        acc_ref[...] = alpha * acc_ref[...] + jnp.dot(p, v, preferred_element_type=F32)
        m_ref[...] = m_new

    def body(ki, carry):
        block(ki, False)
        return carry

    lax.fori_loop(0, qi, body, 0)
    block(qi, True)

    lp = lam_ref[...]
    lam = (jnp.exp(jnp.sum(lp[0:1] * lp[1:2], axis=-1, keepdims=True))
           - jnp.exp(jnp.sum(lp[2:3] * lp[3:4], axis=-1, keepdims=True)) + lam_init)
    o = acc_ref[...] / jnp.sum(l_ref[...], axis=-1, keepdims=True)
    o = o[0:blk] - lam * o[blk:2 * blk]
    o = _rms(o, g_ref[...]) * (1.0 - lam_init)
    o_ref[0] = o.astype(o_ref.dtype)


def diff_attention(proj, lam_params, subln_g, n_heads, lam_init, blk=1024):
    b, s, _ = proj.shape
    blk = _pick(s, blk)
    hw = 2 * HEAD_DIM
    kernel = functools.partial(_diff_attn_body, blk=blk, lam_init=lam_init)
    return pl.pallas_call(
        kernel,
        grid=(b, n_heads, s // blk),
        in_specs=[pl.BlockSpec((1, blk, hw), lambda bi, h, qi: (bi, qi, h)),
                  pl.BlockSpec((1, s, hw), lambda bi, h, qi: (bi, 0, n_heads + h)),
                  pl.BlockSpec((1, s, hw), lambda bi, h, qi: (bi, 0, 2 * n_heads + h)),
                  pl.BlockSpec((4, HEAD_DIM), lambda bi, h, qi: (0, 0)),
                  pl.BlockSpec((1, hw), lambda bi, h, qi: (0, 0))],
        out_specs=pl.BlockSpec((1, blk, hw), lambda bi, h, qi: (bi, qi, h)),
        out_shape=jax.ShapeDtypeStruct((b, s, n_heads * hw), BF16),
        scratch_shapes=[pltpu.VMEM((2 * blk, hw), BF16), pltpu.VMEM((2 * blk, LANES), F32),
                        pltpu.VMEM((2 * blk, LANES), F32), pltpu.VMEM((2 * blk, hw), F32)],
        compiler_params=_params(3),
        name="diff_attention",
    )(proj, proj, proj, lam_params, subln_g.reshape(1, hw))


def _swa_body(sink_ref, q_ref, kc_ref, kp_ref, vc_ref, vp_ref, o_ref, *, tq, kv_heads, group):
    t = pl.program_id(1)
    w = ATTN_BLOCK
    row = lax.broadcasted_iota(I32, (group * w, 2 * w), 0) & (w - 1)
    col = lax.broadcasted_iota(I32, (group * w, 2 * w), 1)
    band = (col > row) & (col <= row + w)
    band_first = band & ((col >= w) | (t > 0))
    low_half = lax.broadcasted_iota(I32, (w, LANES), 1) < HEAD_DIM
    for j in range(tq // w):
        rows = slice(j * w, (j + 1) * w)
        if j == 0:
            k2 = jnp.concatenate([kp_ref[0], kc_ref[0, rows, :]], axis=0)
            v2 = jnp.concatenate([vp_ref[0], vc_ref[0, rows, :]], axis=0)
            mask = band_first
        else:
            k2 = kc_ref[0, (j - 1) * w:(j + 1) * w, :]
            v2 = vc_ref[0, (j - 1) * w:(j + 1) * w, :]
            mask = band
        outs = []
        for g in range(kv_heads):
            kg = k2[:, g * HEAD_DIM:(g + 1) * HEAD_DIM]
            vg = v2[:, g * HEAD_DIM:(g + 1) * HEAD_DIM]
            kdup = jnp.concatenate([kg, kg], axis=1)
            vdup = jnp.concatenate([vg, vg], axis=1)
            heads = [g * group + i for i in range(group)]
            q_tiles = []
            for h in heads:
                q = q_ref[0, rows, (h // 2) * LANES:(h // 2 + 1) * LANES]
                q = (q.astype(F32) * (HEAD_DIM ** -0.5)).astype(BF16)
                own_half = low_half if h % 2 == 0 else jnp.logical_not(low_half)
                q_tiles.append(jnp.where(own_half, q, jnp.zeros_like(q)))
            sink = jnp.concatenate([jnp.full((w, LANES), sink_ref[h], F32) for h in heads], axis=0)
            s = lax.dot_general(jnp.concatenate(q_tiles, axis=0), kdup, (((1,), (1,)), ((), ())),
                                preferred_element_type=F32)
            s = jnp.where(mask, s, -jnp.inf)
            s0, s1 = s[:, :w], s[:, w:]
            m = jnp.maximum(jnp.max(jnp.maximum(s0, s1), axis=-1, keepdims=True), sink)
            e0 = jnp.exp(s0 - m)
            e1 = jnp.exp(s1 - m)
            denom = jnp.sum(e0 + e1, axis=-1, keepdims=True) + jnp.exp(sink - m)
            e = jnp.concatenate([e0, e1], axis=1).astype(BF16)
            o = jnp.dot(e, vdup, preferred_element_type=F32) / denom
            outs.extend(o[i * w:(i + 1) * w, :] for i in range(group))
        tiles = [jnp.where(low_half, outs[h], outs[h + 1]) for h in range(0, len(outs), 2)]
        o_ref[0, rows, :] = jnp.concatenate(tiles, axis=1).astype(o_ref.dtype)


def sliding_window_attention(proj, sinks, q_col, k_col, v_col, kv_heads, group, tq=512):
    b, s, _ = proj.shape
    tq = _pick(s, tq)
    qw = kv_heads * group * HEAD_DIM
    kw = kv_heads * HEAD_DIM
    sub = tq // ATTN_BLOCK
    kernel = functools.partial(_swa_body, tq=tq, kv_heads=kv_heads, group=group)
    prev = lambda bi, t: (bi, jnp.maximum(t * sub - 1, 0), 0)
    return pl.pallas_call(
        kernel,
        grid=(b, s // tq),
        in_specs=[pl.BlockSpec(memory_space=pltpu.SMEM),
                  pl.BlockSpec((1, tq, qw), lambda bi, t: (bi, t, q_col // qw)),
                  pl.BlockSpec((1, tq, kw), lambda bi, t: (bi, t, k_col // kw)),
                  pl.BlockSpec((1, ATTN_BLOCK, kw), lambda bi, t: prev(bi, t)[:2] + (k_col // kw,)),
                  pl.BlockSpec((1, tq, kw), lambda bi, t: (bi, t, v_col // kw)),
                  pl.BlockSpec((1, ATTN_BLOCK, kw), lambda bi, t: prev(bi, t)[:2] + (v_col // kw,))],
        out_specs=pl.BlockSpec((1, tq, qw), lambda bi, t: (bi, t, 0)),
        out_shape=jax.ShapeDtypeStruct((b, s, qw), BF16),
        compiler_params=_params(2),
        name="swa_attention",
    )(sinks, proj, proj, proj, proj, proj)


def _outproj_body(x_ref, a_ref, b_ref, w_ref, o_ref):
    mix = jnp.concatenate([a_ref[...], b_ref[...]], axis=1)
    o_ref[...] = x_ref[...] + jnp.dot(mix, w_ref[...], preferred_element_type=F32)


def out_projection(x, oa, ob, w, tm=512):
    t, d = x.shape
    tm = _pick(t, tm)
    return pl.pallas_call(
        _outproj_body,
        grid=(t // tm,),
        in_specs=[pl.BlockSpec((tm, d), lambda i: (i, 0)),
                  pl.BlockSpec((tm, oa.shape[1]), lambda i: (i, 0)),
                  pl.BlockSpec((tm, ob.shape[1]), lambda i: (i, 0)),
                  pl.BlockSpec(w.shape, lambda i: (0, 0))],
        out_specs=pl.BlockSpec((tm, d), lambda i: (i, 0)),
        out_shape=jax.ShapeDtypeStruct((t, d), F32),
        compiler_params=_params(1),
        name="out_projection",
    )(x, oa, ob, w)


def _ffn_body(x_ref, g_ref, wg_ref, wu_ref, wd_ref, o_ref, h_ref, acc_ref):
    j = pl.program_id(1)

    @pl.when(j == 0)
    def _():
        h_ref[...] = _rms(x_ref[...], g_ref[...]).astype(BF16)
        acc_ref[...] = jnp.zeros(acc_ref.shape, F32)

    h = h_ref[...]
    a = jnp.dot(h, wg_ref[...], preferred_element_type=F32)
    u = jnp.dot(h, wu_ref[...], preferred_element_type=F32)
    act = (a * jax.nn.sigmoid(a) * u).astype(BF16)
    acc_ref[...] += jnp.dot(act, wd_ref[...], preferred_element_type=F32)

    @pl.when(j == pl.num_programs(1) - 1)
    def _():
        o_ref[...] = x_ref[...] + acc_ref[...]


def ffn_swiglu(x, g, wg, wu, wd, tm=512, tf=1408):
    t, d = x.shape
    f = wg.shape[1]
    tm = _pick(t, tm)
    tf = _pick(f, tf)
    return pl.pallas_call(
        _ffn_body,
        grid=(t // tm, f // tf),
        in_specs=[pl.BlockSpec((tm, d), lambda i, j: (i, 0)),
                  pl.BlockSpec((1, d), lambda i, j: (0, 0)),
                  pl.BlockSpec((d, tf), lambda i, j: (0, j)),
                  pl.BlockSpec((d, tf), lambda i, j: (0, j)),
                  pl.BlockSpec((tf, d), lambda i, j: (j, 0))],
        out_specs=pl.BlockSpec((tm, d), lambda i, j: (i, 0)),
        out_shape=jax.ShapeDtypeStruct((t, d), F32),
        scratch_shapes=[pltpu.VMEM((tm, d), BF16), pltpu.VMEM((tm, d), F32)],
        compiler_params=_params(2),
        name="ffn_swiglu",
    )(x, g.reshape(1, d), wg, wu, wd)


def _mix_ffn_body(x_ref, a_ref, b_ref, wo_ref, g_ref, wg_ref, wu_ref, wd_ref, o_ref, *, ff_chunk):
    mix = jnp.concatenate([a_ref[...], b_ref[...]], axis=1)
    x1 = x_ref[...] + jnp.dot(mix, wo_ref[...], preferred_element_type=F32)
    h = _rms(x1, g_ref[...]).astype(BF16)
    y = x1
    for c in range(wg_ref.shape[1] // ff_chunk):
        cols = slice(c * ff_chunk, (c + 1) * ff_chunk)
        a = jnp.dot(h, wg_ref[:, cols], preferred_element_type=F32)
        u = jnp.dot(h, wu_ref[:, cols], preferred_element_type=F32)
        act = (a * jax.nn.sigmoid(a) * u).astype(BF16)
        y = y + jnp.dot(act, wd_ref[cols, :], preferred_element_type=F32)
    o_ref[...] = y


def mix_ffn(x, oa, ob, wo, g, wg, wu, wd, tm=512, ff_chunk=1408):
    t, d = x.shape
    f = wg.shape[1]
    tm = _pick(t, tm)
    ff_chunk = _pick(f, ff_chunk)
    resident = lambda shape: pl.BlockSpec(shape, lambda i: (0, 0), pipeline_mode=pl.Buffered(1))
    return pl.pallas_call(
        functools.partial(_mix_ffn_body, ff_chunk=ff_chunk),
        grid=(t // tm,),
        in_specs=[pl.BlockSpec((tm, d), lambda i: (i, 0)),
                  pl.BlockSpec((tm, oa.shape[1]), lambda i: (i, 0)),
                  pl.BlockSpec((tm, ob.shape[1]), lambda i: (i, 0)),
                  resident(wo.shape), resident((1, d)),
                  resident(wg.shape), resident(wu.shape), resident(wd.shape)],
        out_specs=pl.BlockSpec((tm, d), lambda i: (i, 0)),
        out_shape=jax.ShapeDtypeStruct((t, d), F32),
        compiler_params=_params(1),
        name="mix_ffn",
    )(x, oa, ob, wo, g.reshape(1, d), wg, wu, wd)


def _conv_body(x_ref, g_ref, w1_ref, b1_ref, wdw_ref, bdw_ref, lng_ref, lnb_ref, w2_ref, b2_ref,
               o_ref, u_ref, v_ref, wb_ref, c_ref, *, tm, rows_per_chunk):
    d = x_ref.shape[-1]
    t = pl.program_id(1)
    sublanes = wb_ref.shape[1]

    @pl.when(t == 0)
    def _():
        u_ref[:, 0:CONV_HIST, :] = jnp.zeros((d // LANES, CONV_HIST, LANES), F32)
        wb_ref[...] = jnp.broadcast_to(wdw_ref[...][:, None, :], wb_ref.shape)

    h = _rms(x_ref[0], g_ref[...]).astype(BF16)
    z = jnp.dot(h, w1_ref[...], preferred_element_type=F32) + b1_ref[...]
    u = z[:, :d] * jax.nn.sigmoid(z[:, d:])
    n_slabs = d // LANES
    for c in range(n_slabs):
        u_ref[c, CONV_HIST:CONV_HIST + tm, :] = u[:, c * LANES:(c + 1) * LANES]

    first_tap = CONV_HIST - (CONV_WIDTH - 1)
    conv_rows = 4 * sublanes

    for c in range(n_slabs):
        lanes = slice(c * LANES, (c + 1) * LANES)
        w = [wb_ref[j, :, lanes] for j in range(CONV_WIDTH)]
        bias = jnp.zeros((sublanes, LANES), F32) + bdw_ref[:, lanes]

        def rows_block(i, carry, c=c, lanes=lanes, w=w, bias=bias):
            r0 = pl.multiple_of(i * conv_rows, conv_rows)
            for r in range(conv_rows // sublanes):
                sums = [bias, None]
                for j in range(CONV_WIDTH):
                    tap = u_ref[c, pl.ds(r0 + (first_tap + j + r * sublanes), sublanes, stride=1), :] * w[j]
                    sums[j % 2] = tap if sums[j % 2] is None else sums[j % 2] + tap
                c_ref[pl.ds(r0 + r * sublanes, sublanes), lanes] = sums[0] + sums[1]
            return carry

        lax.fori_loop(0, tm // conv_rows, rows_block, 0)

    def chunk(i, carry):
        r0 = pl.multiple_of(i * rows_per_chunk, rows_per_chunk)
        acc = c_ref[pl.ds(r0, rows_per_chunk), :]
        mu = jnp.mean(acc, axis=-1, keepdims=True)
        xc = acc - mu
        var = jnp.mean(xc * xc, axis=-1, keepdims=True)
        y = xc * lax.rsqrt(var + LN_EPS) * lng_ref[...] + lnb_ref[...]
        v_ref[pl.ds(r0, rows_per_chunk), :] = (y * jax.nn.sigmoid(y)).astype(BF16)
        return carry

    lax.fori_loop(0, tm // rows_per_chunk, chunk, 0)
    u_ref[:, 0:CONV_HIST, :] = u_ref[:, tm:tm + CONV_HIST, :]
    o_ref[0] = x_ref[0] + jnp.dot(v_ref[...], w2_ref[...], preferred_element_type=F32) + b2_ref[...]


def conformer_conv(x, g, w1, b1, wdw, bdw, lng, lnb, w2, b2, tm=512, rows_per_chunk=128):
    b, s, d = x.shape
    tm = _pick(s, tm)
    kernel = functools.partial(_conv_body, tm=tm, rows_per_chunk=rows_per_chunk)
    vec = lambda n: pl.BlockSpec((1, n), lambda bi, t: (0, 0))
    return pl.pallas_call(
        kernel,
        grid=(b, s // tm),
        in_specs=[pl.BlockSpec((1, tm, d), lambda bi, t: (bi, t, 0)),
                  vec(d),
                  pl.BlockSpec((d, 2 * d), lambda bi, t: (0, 0)),
                  vec(2 * d),
                  pl.BlockSpec((CONV_WIDTH, d), lambda bi, t: (0, 0)),
                  vec(d), vec(d), vec(d),
                  pl.BlockSpec((d, d), lambda bi, t: (0, 0)),
                  vec(d)],
        out_specs=pl.BlockSpec((1, tm, d), lambda bi, t: (bi, t, 0)),
        out_shape=jax.ShapeDtypeStruct((b, s, d), F32),
        scratch_shapes=[pltpu.VMEM((d // LANES, tm + CONV_HIST, LANES), F32), pltpu.VMEM((tm, d), BF16),
                        pltpu.VMEM((CONV_WIDTH, F32_SUBLANES, d), F32), pltpu.VMEM((tm, d), F32)],
        compiler_params=_params(2),
        name="conformer_conv",
    )(x, g.reshape(1, d), w1, b1.reshape(1, 2 * d), wdw, bdw.reshape(1, d), lng.reshape(1, d),
      lnb.reshape(1, d), w2, b2.reshape(1, d))


def _router_body(x_ref, g_ref, wr_ref, h_ref, meta_ref, *, n_experts):
    h = _rms(x_ref[...], g_ref[...])
    _store_slabs(h_ref, 0, h)
    logits = jnp.dot(h.astype(BF16), wr_ref[...], preferred_element_type=F32)
    lane = lax.broadcasted_iota(I32, logits.shape, 1).astype(F32)
    neg = jnp.full_like(logits, -jnp.inf)
    far = jnp.full_like(logits, float(LANES))
    l1 = jnp.where(lane < n_experts, logits, neg)
    m1 = jnp.max(l1, axis=-1, keepdims=True)
    i1 = jnp.min(jnp.where(l1 == m1, lane, far), axis=-1, keepdims=True)
    l2 = jnp.where(lane == i1, neg, l1)
    m2 = jnp.max(l2, axis=-1, keepdims=True)
    i2 = jnp.min(jnp.where(l2 == m2, lane, far), axis=-1, keepdims=True)
    e2 = jnp.exp(m2 - m1)
    w1 = 1.0 / (1.0 + e2)
    w2 = e2 / (1.0 + e2)
    zero = jnp.zeros_like(logits)
    meta_ref[...] = jnp.where(lane == 0, i1,
                    jnp.where(lane == 1, i2,
                    jnp.where(lane == 2, w1, jnp.where(lane == 3, w2, zero))))


def moe_router(x, g, wr, n_experts, tm=512):
    t, d = x.shape
    tm = _pick(t, tm)
    kernel = functools.partial(_router_body, n_experts=n_experts)
    return pl.pallas_call(
        kernel,
        grid=(t // tm,),
        in_specs=[pl.BlockSpec((tm, d), lambda i: (i, 0)),
                  pl.BlockSpec((1, d), lambda i: (0, 0)),
                  pl.BlockSpec((d, LANES), lambda i: (0, 0))],
        out_specs=[pl.BlockSpec((tm * (d // LANES), LANES), lambda i: (i, 0)),
                   pl.BlockSpec((tm, LANES), lambda i: (i, 0))],
        out_shape=[jax.ShapeDtypeStruct((t * (d // LANES), LANES), F32),
                   jax.ShapeDtypeStruct((t, LANES), F32)],
        compiler_params=_params(1),
        name="moe_router",
    )(x, g.reshape(1, d), wr)


GATHER_UNROLL = 8


def _store_slabs(ref, first_row, value):
    n, d = value.shape
    n_slabs = d // LANES
    for c in range(n_slabs):
        ref[pl.ds(first_row * n_slabs + c, n, stride=n_slabs), :] = value[:, c * LANES:(c + 1) * LANES]


def _load_slabs(ref, first_row, n, d):
    n_slabs = d // LANES
    return jnp.concatenate([ref[pl.ds(first_row * n_slabs + c, n, stride=n_slabs), :]
                            for c in range(n_slabs)], axis=1)


def _dispatch_body(pad_ref, tn_ref, d_ref, h_hbm, xs_hbm, zero_ref, sem, zsem,
                   *, tm, n_slabs, n_experts, n_sub):
    i = pl.program_id(0)
    n = pl.num_programs(0)
    rows = tm * n_slabs

    def tile_copy(slot):
        return pltpu.make_async_copy(h_hbm.at[pl.ds(0, rows), :], xs_hbm.at[pl.ds(0, rows), :], sem.at[slot])

    slot = i % 2
    for rank in range(TOP_K):
        def body(it, carry, rank=rank):
            base = it * GATHER_UNROLL
            dsts = [pl.multiple_of(d_ref[0, 0, rank * tm + base + k], n_slabs) for k in range(GATHER_UNROLL)]
            for k in range(GATHER_UNROLL):
                src = pl.multiple_of((i * tm + base + k) * n_slabs, n_slabs)
                pltpu.make_async_copy(h_hbm.at[pl.ds(src, n_slabs), :],
                                      xs_hbm.at[pl.ds(dsts[k], n_slabs), :], sem.at[slot]).start()
            return carry
        lax.fori_loop(0, tm // GATHER_UNROLL, body, 0)

    @pl.when(i == 0)
    def _():
        zero_ref[...] = jnp.zeros(zero_ref.shape, F32)
        zero_row = zero_ref.at[pl.ds(0, n_slabs), :]
        sub_rows = zero_ref.shape[0]

        def unused_sub_tiles(fn):
            def tbody(tile, carry):
                for s in range(n_sub):
                    @pl.when(s >= tn_ref[tile])
                    def _():
                        dst = pl.multiple_of((tile * n_sub + s) * sub_rows, sub_rows)
                        fn(pltpu.make_async_copy(zero_ref, xs_hbm.at[pl.ds(dst, sub_rows), :], zsem))
                return carry
            lax.fori_loop(0, tn_ref.shape[0], tbody, 0)

        def tail_rows(fn):
            for e in range(n_experts):
                def zbody(r, carry, e=e):
                    dst = pl.multiple_of((pad_ref[0, e] + r) * n_slabs, n_slabs)
                    fn(pltpu.make_async_copy(zero_row, xs_hbm.at[pl.ds(dst, n_slabs), :], zsem))
                    return carry
                lax.fori_loop(0, pad_ref[1, e], zbody, 0)

        unused_sub_tiles(lambda cp: cp.start())
        tail_rows(lambda cp: cp.start())
        unused_sub_tiles(lambda cp: cp.wait())
        tail_rows(lambda cp: cp.wait())

    @pl.when(i > 0)
    def _():
        for rank in range(TOP_K):
            tile_copy(1 - slot).wait()

    @pl.when(i == n - 1)
    def _():
        for rank in range(TOP_K):
            tile_copy(slot).wait()


def moe_dispatch(h, dest, pad, tile_count, sub, n_sub, n_slabs):
    n_tok_tiles = dest.shape[0]
    tm = dest.shape[2] // TOP_K
    n_experts = pad.shape[1]
    n_rows = tile_count.shape[0] * n_sub * sub
    grid_spec = pltpu.PrefetchScalarGridSpec(
        num_scalar_prefetch=2,
        grid=(n_tok_tiles,),
        in_specs=[pl.BlockSpec((1, 1, TOP_K * tm), lambda i, pad, tn: (i, 0, 0), memory_space=pltpu.SMEM),
                  pl.BlockSpec(memory_space=pl.ANY)],
        out_specs=pl.BlockSpec(memory_space=pl.ANY),
        scratch_shapes=[pltpu.VMEM((sub * n_slabs, LANES), F32), pltpu.SemaphoreType.DMA((2,)),
                        pltpu.SemaphoreType.DMA(())],
    )
    return pl.pallas_call(
        functools.partial(_dispatch_body, tm=tm, n_slabs=n_slabs, n_experts=n_experts, n_sub=n_sub),
        grid_spec=grid_spec,
        out_shape=jax.ShapeDtypeStruct((n_rows * n_slabs, LANES), F32),
        compiler_params=_params(1),
        name="moe_dispatch",
    )(pad, tile_count, dest, h)


def _moe_body(te_ref, tn_ref, xs_ref, wg_ref, wu_ref, wd_ref, y_ref,
              xb_ref, wgb_ref, wub_ref, wdb_ref, *, sub, n_sub):
    i = pl.program_id(0)
    j = pl.program_id(1)
    count = tn_ref[i]
    d = wg_ref.shape[2]

    for s in range(n_sub):
        rows = pl.ds(s * sub, sub)

        @pl.when((j == 0) & (s < count))
        def _():
            xb_ref[rows, :] = _load_slabs(xs_ref, s * sub, sub, d).astype(BF16)

        @pl.when(j == 0)
        def _():
            y_ref[rows, :] = jnp.zeros((sub, d), F32)

        @pl.when(s < count)
        def _():
            if s == 0:
                wg, wu, wd = (r[0, 0].astype(BF16) for r in (wg_ref, wu_ref, wd_ref))
                wgb_ref[...], wub_ref[...], wdb_ref[...] = wg, wu, wd
            else:
                wg, wu, wd = wgb_ref[...], wub_ref[...], wdb_ref[...]
            x = xb_ref[rows, :]
            a = jnp.dot(x, wg, preferred_element_type=F32)
            u = jnp.dot(x, wu, preferred_element_type=F32)
            act = (a * jax.nn.sigmoid(a) * u).astype(BF16)
            y_ref[rows, :] += jnp.dot(act, wd, preferred_element_type=F32)


def moe_experts(xs, tile_expert, tile_count, wg, wu, wd, layer, sub, n_sub, tf=512):
    d = wg.shape[2]
    n_slabs = d // LANES
    n_tiles = tile_expert.shape[0]
    tile_rows = n_sub * sub
    f = wg.shape[3]
    tf = _pick(f, tf)
    nf = f // tf

    def w_col(i, j, te, tn):
        return (layer, te[i], 0, jnp.where(tn[i] > 0, j, nf - 1))

    def w_row(i, j, te, tn):
        return (layer, te[i], jnp.where(tn[i] > 0, j, nf - 1), 0)

    grid_spec = pltpu.PrefetchScalarGridSpec(
        num_scalar_prefetch=2,
        grid=(n_tiles, nf),
        in_specs=[pl.BlockSpec((tile_rows * n_slabs, LANES), lambda i, j, te, tn: (i, 0)),
                  pl.BlockSpec((1, 1, d, tf), w_col),
                  pl.BlockSpec((1, 1, d, tf), w_col),
                  pl.BlockSpec((1, 1, tf, d), w_row)],
        out_specs=pl.BlockSpec((tile_rows, d), lambda i, j, te, tn: (i, 0)),
        scratch_shapes=[pltpu.VMEM((tile_rows, d), BF16),
                        pltpu.VMEM((d, tf), BF16), pltpu.VMEM((d, tf), BF16), pltpu.VMEM((tf, d), BF16)],
    )
    return pl.pallas_call(
        functools.partial(_moe_body, sub=sub, n_sub=n_sub),
        grid_spec=grid_spec,
        out_shape=jax.ShapeDtypeStruct((n_tiles * tile_rows, d), F32),
        compiler_params=_params(2),
        name="moe_experts",
    )(tile_expert, tile_count, xs, wg, wu, wd)


def _combine_body(d_cur_ref, d_nxt_ref, x_ref, meta_ref, g_ref, y_hbm, o_ref, buf_ref, sem,
                  *, tm, final_norm):
    i = pl.program_id(0)
    n = pl.num_programs(0)

    def gather(d_ref, slot):
        def body(it, carry):
            for k in range(GATHER_UNROLL):
                r = it * GATHER_UNROLL + k
                pltpu.make_async_copy(y_hbm.at[pl.ds(d_ref[0, 0, r], 1), :],
                                      buf_ref.at[slot, pl.ds(r, 1), :], sem.at[slot]).start()
            return carry
        lax.fori_loop(0, TOP_K * tm // GATHER_UNROLL, body, 0)

    @pl.when(i == 0)
    def _():
        gather(d_cur_ref, 0)

    @pl.when(i + 1 < n)
    def _():
        gather(d_nxt_ref, (i + 1) % 2)

    slot = i % 2
    pltpu.make_async_copy(y_hbm.at[pl.ds(0, TOP_K * tm), :], buf_ref.at[slot], sem.at[slot]).wait()
    meta = meta_ref[...]
    out = (x_ref[...] + meta[:, 2:3] * buf_ref[slot, 0:tm, :]
           + meta[:, 3:4] * buf_ref[slot, tm:2 * tm, :])
    if final_norm:
        out = _rms(out, g_ref[...])
    o_ref[...] = out


def moe_combine(x, meta, dest, y, g, final_norm, tm=256):
    t, d = x.shape
    n = t // tm
    smem = lambda fn: pl.BlockSpec((1, 1, TOP_K * tm), fn, memory_space=pltpu.SMEM)
    return pl.pallas_call(
        functools.partial(_combine_body, tm=tm, final_norm=final_norm),
        grid=(n,),
        in_specs=[smem(lambda i: (i, 0, 0)),
                  smem(lambda i: (jnp.minimum(i + 1, n - 1), 0, 0)),
                  pl.BlockSpec((tm, d), lambda i: (i, 0)),
                  pl.BlockSpec((tm, LANES), lambda i: (i, 0)),
                  pl.BlockSpec((1, d), lambda i: (0, 0)),
                  pl.BlockSpec(memory_space=pl.ANY)],
        out_specs=pl.BlockSpec((tm, d), lambda i: (i, 0)),
        out_shape=jax.ShapeDtypeStruct((t, d), F32),
        scratch_shapes=[pltpu.VMEM((2, TOP_K * tm, d), F32), pltpu.SemaphoreType.DMA((2,))],
        compiler_params=_params(1),
        name="moe_combine",
    )(dest, dest, x, meta, g.reshape(1, d), y)


def _route(meta, n_experts, sub, n_sub, tm_combine):
    t = meta.shape[0]
    tile_rows = sub * n_sub
    flat_e = meta[:, :TOP_K].astype(I32).reshape(-1)
    onehot = (flat_e[:, None] == jnp.arange(n_experts, dtype=I32)[None, :]).astype(I32)
    csum = jnp.cumsum(onehot, axis=0)
    rank = jnp.sum((csum - onehot) * onehot, axis=1)
    counts = csum[-1]
    subs = (counts + sub - 1) // sub
    tiles = (subs + n_sub - 1) // n_sub
    tile_ends = jnp.cumsum(tiles)
    tile_starts = tile_ends - tiles
    dest = tile_starts[flat_e] * tile_rows + rank
    n_tiles = ((t * TOP_K) // sub + n_experts + n_sub - 1) // n_sub + n_experts
    idx = jnp.arange(n_tiles, dtype=I32)
    tile_expert = jnp.minimum(jnp.searchsorted(tile_ends, idx, side="right"), n_experts - 1).astype(I32)
    local = idx - tile_starts[tile_expert]
    tile_count = jnp.clip(subs[tile_expert] - local * n_sub, 0, n_sub).astype(I32)
    dest_tiles = dest.reshape(t // tm_combine, tm_combine, TOP_K).transpose(0, 2, 1).reshape(
        t // tm_combine, 1, TOP_K * tm_combine)
    pad = jnp.stack([tile_starts * tile_rows + counts, subs * sub - counts]).astype(I32)
    return tile_expert, tile_count, dest_tiles, pad, n_tiles


def moe_layer(x, g, wr, wg, wu, wd, layer, final_g, final_norm, sub=512, n_sub=4, tm_combine=256):
    t, d = x.shape
    n_experts = wr.shape[1]
    tm_combine = min(tm_combine, t)
    wr_pad = jnp.zeros((d, LANES), BF16).at[:, :n_experts].set(wr.astype(BF16))
    h, meta = moe_router(x, g, wr_pad, n_experts)
    n_slabs = d // LANES
    tile_expert, tile_count, dest_tiles, pad, n_tiles = _route(meta, n_experts, sub, n_sub, tm_combine)
    xs = moe_dispatch(h, dest_tiles * n_slabs, pad, tile_count, sub, n_sub, n_slabs)
    y = moe_experts(xs, tile_expert, tile_count, wg, wu, wd, layer, sub, n_sub)
    return moe_combine(x, meta, dest_tiles, y, final_g, final_norm, tm_combine)


def attention_mixers(x, norm_g, w_in, lam_params, subln_g, sinks, layer):
    b, s, d = x.shape
    diff_heads = d // (4 * HEAD_DIM)
    swa_q_heads = d // (2 * HEAD_DIM)
    swa_kv_heads = max(1, swa_q_heads // 4)
    group = swa_q_heads // swa_kv_heads
    a_width = diff_heads * 2 * HEAD_DIM
    lam_init = 0.8 - 0.6 * math.exp(-0.3 * layer)
    x2 = x.reshape(b * s, d)
    proj = rms_matmul(x2, norm_g, w_in).reshape(b, s, -1)
    oa = diff_attention(proj, lam_params, subln_g, diff_heads, lam_init)
    q_col = 3 * a_width
    k_col = q_col + swa_q_heads * HEAD_DIM
    v_col = k_col + swa_kv_heads * HEAD_DIM
    ob = sliding_window_attention(proj, sinks, q_col, k_col, v_col, swa_kv_heads, group)
    return oa.reshape(b * s, -1), ob.reshape(b * s, -1)


def attention_layer(x, norm_g, w_in, lam_params, subln_g, sinks, w_out, layer):
    b, s, d = x.shape
    oa, ob = attention_mixers(x, norm_g, w_in, lam_params, subln_g, sinks, layer)
    return out_projection(x.reshape(b * s, d), oa, ob, w_out)


def kernel(x, attn_norm_g, w_in_att, diff_lambda, diff_subln_g, attn_sinks, w_out_att, ffn_norm_g, w_ffn_gate, w_ffn_up, w_ffn_down, conv_norm_g, w_pw1, b_pw1, w_dw, b_dw, conv_ln_g, conv_ln_b, w_pw2, b_pw2, moe_norm_g, w_router, w_exp_gate, w_exp_up, w_exp_down, final_norm_g):
    b, s, d = x.shape
    depth = attn_norm_g.shape[0] + conv_norm_g.shape[0]
    assert depth % 2 == 0, "the final RMSNorm is fused into the last expert layer"
    bf = lambda w: w.astype(BF16)
    for layer in range(depth):
        i = layer // 2
        if layer % 2 == 0:
            oa, ob = attention_mixers(x, attn_norm_g[i], bf(w_in_att[i]), diff_lambda[i], diff_subln_g[i],
                                      attn_sinks[i], layer)
            x2 = mix_ffn(x.reshape(b * s, d), oa, ob, bf(w_out_att[i]), ffn_norm_g[i],
                         bf(w_ffn_gate[i]), bf(w_ffn_up[i]), bf(w_ffn_down[i]))
            x = x2.reshape(b, s, d)
        else:
            x = conformer_conv(x, conv_norm_g[i], bf(w_pw1[i]), b_pw1[i], w_dw[i], b_dw[i],
                               conv_ln_g[i], conv_ln_b[i], bf(w_pw2[i]), b_pw2[i])
            x2 = moe_layer(x.reshape(b * s, d), moe_norm_g[i], w_router[i], w_exp_gate,
                           w_exp_up, w_exp_down, i, final_norm_g,
                           final_norm=(layer == depth - 1))
            x = x2.reshape(b, s, d)
    return x
```

```python
import functools
import math

import jax
import jax.numpy as jnp
from jax import lax
from jax.experimental import pallas as pl
from jax.experimental.pallas import tpu as pltpu

BF16 = jnp.bfloat16
F32 = jnp.float32
I32 = jnp.int32

LOG2_E = 1.4426950408889634
RMS_EPS = 1e-6
LN_EPS = 1e-5
HEAD_DIM = 64
ATTN_BLOCK = 128
CONV_WIDTH = 31
TOP_K = 2
LANES = 128
F32_SUBLANES = 8
CONV_HIST = 32
V7X_VMEM_LIMIT = 56 * 1024 * 1024


def _params(n_axes, flags=None):
    return pltpu.CompilerParams(dimension_semantics=("arbitrary",) * n_axes,
                                vmem_limit_bytes=V7X_VMEM_LIMIT, flags=flags)


def _rms(x, g):
    return x * lax.rsqrt(jnp.mean(x * x, axis=-1, keepdims=True) + RMS_EPS) * g


def _pick(n, pref):
    t = min(n, pref)
    while n % t:
        t -= LANES if t > LANES else 8
    return t


def _rms_matmul_body(x_ref, g_ref, w_ref, o_ref):
    h = _rms(x_ref[...], g_ref[...]).astype(BF16)
    o_ref[...] = jnp.dot(h, w_ref[...], preferred_element_type=F32).astype(o_ref.dtype)


def rms_matmul(x, g, w, tm=512):
    t, d = x.shape
    n = w.shape[1]
    tm = _pick(t, tm)
    return pl.pallas_call(
        _rms_matmul_body,
        grid=(t // tm,),
        in_specs=[pl.BlockSpec((tm, d), lambda i: (i, 0)),
                  pl.BlockSpec((1, d), lambda i: (0, 0)),
                  pl.BlockSpec((d, n), lambda i: (0, 0))],
        out_specs=pl.BlockSpec((tm, n), lambda i: (i, 0)),
        out_shape=jax.ShapeDtypeStruct((t, n), BF16),
        compiler_params=_params(1),
        name="rms_inproj",
    )(x, g.reshape(1, d), w)


def _diff_attn_body(q_ref, k_ref, v_ref, lam_ref, g_ref, o_ref, q_ref2, m_ref, l_ref, acc_ref,
                    *, blk, lam_init):
    qi = pl.program_id(2)
    hw = 2 * HEAD_DIM
    lane = lax.broadcasted_iota(I32, (blk, hw), 1)
    q = (q_ref[0].astype(F32) * (HEAD_DIM ** -0.5 * LOG2_E)).astype(BF16)
    zero = jnp.zeros_like(q)
    q_ref2[...] = jnp.concatenate([jnp.where(lane < HEAD_DIM, q, zero),
                                   jnp.where(lane >= HEAD_DIM, q, zero)], axis=0)

    m_ref[...] = jnp.full(m_ref.shape, -jnp.inf, F32)
    l_ref[...] = jnp.zeros(l_ref.shape, F32)
    acc_ref[...] = jnp.zeros(acc_ref.shape, F32)
    n_lane_tiles = blk // LANES

    def block(ki, masked):
        start = pl.multiple_of(ki * blk, blk)
        k = k_ref[0, pl.ds(start, blk), :]
        v = v_ref[0, pl.ds(start, blk), :]
        s = lax.dot_general(q_ref2[...], k, (((1,), (1,)), ((), ())), preferred_element_type=F32)
        if masked:
            row = lax.broadcasted_iota(I32, (2 * blk, blk), 0)
            row = jnp.where(row >= blk, row - blk, row)
            col = lax.broadcasted_iota(I32, (2 * blk, blk), 1)
            s = jnp.where(col <= row, s, -jnp.inf)
        tiles = [s[:, c * LANES:(c + 1) * LANES] for c in range(n_lane_tiles)]
        m_prev = m_ref[...]
        m_new = jnp.maximum(m_prev, jnp.max(functools.reduce(jnp.maximum, tiles), axis=-1, keepdims=True))
        alpha = jnp.exp2(m_prev - m_new)
        p_tiles = [jnp.exp2(t - m_new) for t in tiles]
        l_ref[...] = alpha * l_ref[...] + functools.reduce(jnp.add, p_tiles)
        p = jnp.concatenate(p_tiles, axis=1).astype(BF16)
        acc_ref[...] = alpha * acc_ref[...] + jnp.dot(p, v, preferred_element_type=F32)
        m_ref[...] = m_new

    def body(ki, carry):
        block(ki, False)
        return carry

    lax.fori_loop(0, qi, body, 0)
    block(qi, True)

    lp = lam_ref[...]
    lam = (jnp.exp(jnp.sum(lp[0:1] * lp[1:2], axis=-1, keepdims=True))
           - jnp.exp(jnp.sum(lp[2:3] * lp[3:4], axis=-1, keepdims=True)) + lam_init)
    o = acc_ref[...] / jnp.sum(l_ref[...], axis=-1, keepdims=True)
    o = o[0:blk] - lam * o[blk:2 * blk]
    o = _rms(o, g_ref[...]) * (1.0 - lam_init)
    o_ref[0] = o.astype(o_ref.dtype)


def diff_attention(proj, lam_params, subln_g, n_heads, lam_init, blk=1024):
    b, s, _ = proj.shape
    blk = _pick(s, blk)
    hw = 2 * HEAD_DIM
    kernel = functools.partial(_diff_attn_body, blk=blk, lam_init=lam_init)
    return pl.pallas_call(
        kernel,
        grid=(b, n_heads, s // blk),
        in_specs=[pl.BlockSpec((1, blk, hw), lambda bi, h, qi: (bi, qi, h)),
                  pl.BlockSpec((1, s, hw), lambda bi, h, qi: (bi, 0, n_heads + h)),
                  pl.BlockSpec((1, s, hw), lambda bi, h, qi: (bi, 0, 2 * n_heads + h)),
                  pl.BlockSpec((4, HEAD_DIM), lambda bi, h, qi: (0, 0)),
                  pl.BlockSpec((1, hw), lambda bi, h, qi: (0, 0))],
        out_specs=pl.BlockSpec((1, blk, hw), lambda bi, h, qi: (bi, qi, h)),
        out_shape=jax.ShapeDtypeStruct((b, s, n_heads * hw), BF16),
        scratch_shapes=[pltpu.VMEM((2 * blk, hw), BF16), pltpu.VMEM((2 * blk, LANES), F32),
                        pltpu.VMEM((2 * blk, LANES), F32), pltpu.VMEM((2 * blk, hw), F32)],
        compiler_params=_params(3),
        name="diff_attention",
    )(proj, proj, proj, lam_params, subln_g.reshape(1, hw))


def _swa_body(sink_ref, q_ref, kc_ref, kp_ref, vc_ref, vp_ref, o_ref, *, tq, kv_heads, group):
    t = pl.program_id(1)
    w = ATTN_BLOCK
    row = lax.broadcasted_iota(I32, (group * w, 2 * w), 0) & (w - 1)
    col = lax.broadcasted_iota(I32, (group * w, 2 * w), 1)
    band = (col > row) & (col <= row + w)
    band_first = band & ((col >= w) | (t > 0))
    low_half = lax.broadcasted_iota(I32, (w, LANES), 1) < HEAD_DIM
    for j in range(tq // w):
        rows = slice(j * w, (j + 1) * w)
        if j == 0:
            k2 = jnp.concatenate([kp_ref[0], kc_ref[0, rows, :]], axis=0)
            v2 = jnp.concatenate([vp_ref[0], vc_ref[0, rows, :]], axis=0)
            mask = band_first
        else:
            k2 = kc_ref[0, (j - 1) * w:(j + 1) * w, :]
            v2 = vc_ref[0, (j - 1) * w:(j + 1) * w, :]
            mask = band
        outs = []
        for g in range(kv_heads):
            kg = k2[:, g * HEAD_DIM:(g + 1) * HEAD_DIM]
            vg = v2[:, g * HEAD_DIM:(g + 1) * HEAD_DIM]
            kdup = jnp.concatenate([kg, kg], axis=1)
            vdup = jnp.concatenate([vg, vg], axis=1)
            heads = [g * group + i for i in range(group)]
            q_tiles = []
            for h in heads:
                q = q_ref[0, rows, (h // 2) * LANES:(h // 2 + 1) * LANES]
                q = (q.astype(F32) * (HEAD_DIM ** -0.5)).astype(BF16)
                own_half = low_half if h % 2 == 0 else jnp.logical_not(low_half)
                q_tiles.append(jnp.where(own_half, q, jnp.zeros_like(q)))
            sink = jnp.concatenate([jnp.full((w, LANES), sink_ref[h], F32) for h in heads], axis=0)
            s = lax.dot_general(jnp.concatenate(q_tiles, axis=0), kdup, (((1,), (1,)), ((), ())),
                                preferred_element_type=F32)
            s = jnp.where(mask, s, -jnp.inf)
            s0, s1 = s[:, :w], s[:, w:]
            m = jnp.maximum(jnp.max(jnp.maximum(s0, s1), axis=-1, keepdims=True), sink)
            e0 = jnp.exp(s0 - m)
            e1 = jnp.exp(s1 - m)
            denom = jnp.sum(e0 + e1, axis=-1, keepdims=True) + jnp.exp(sink - m)
            e = jnp.concatenate([e0, e1], axis=1).astype(BF16)
            o = jnp.dot(e, vdup, preferred_element_type=F32) / denom
            outs.extend(o[i * w:(i + 1) * w, :] for i in range(group))
        tiles = [jnp.where(low_half, outs[h], outs[h + 1]) for h in range(0, len(outs), 2)]
        o_ref[0, rows, :] = jnp.concatenate(tiles, axis=1).astype(o_ref.dtype)


def sliding_window_attention(proj, sinks, q_col, k_col, v_col, kv_heads, group, tq=512):
    b, s, _ = proj.shape
    tq = _pick(s, tq)
    qw = kv_heads * group * HEAD_DIM
    kw = kv_heads * HEAD_DIM
    sub = tq // ATTN_BLOCK
    kernel = functools.partial(_swa_body, tq=tq, kv_heads=kv_heads, group=group)
    prev = lambda bi, t: (bi, jnp.maximum(t * sub - 1, 0), 0)
    return pl.pallas_call(
        kernel,
        grid=(b, s // tq),
        in_specs=[pl.BlockSpec(memory_space=pltpu.SMEM),
                  pl.BlockSpec((1, tq, qw), lambda bi, t: (bi, t, q_col // qw)),
                  pl.BlockSpec((1, tq, kw), lambda bi, t: (bi, t, k_col // kw)),
                  pl.BlockSpec((1, ATTN_BLOCK, kw), lambda bi, t: prev(bi, t)[:2] + (k_col // kw,)),
                  pl.BlockSpec((1, tq, kw), lambda bi, t: (bi, t, v_col // kw)),
                  pl.BlockSpec((1, ATTN_BLOCK, kw), lambda bi, t: prev(bi, t)[:2] + (v_col // kw,))],
        out_specs=pl.BlockSpec((1, tq, qw), lambda bi, t: (bi, t, 0)),
        out_shape=jax.ShapeDtypeStruct((b, s, qw), BF16),
        compiler_params=_params(2),
        name="swa_attention",
    )(sinks, proj, proj, proj, proj, proj)


def _outproj_body(x_ref, a_ref, b_ref, w_ref, o_ref):
    mix = jnp.concatenate([a_ref[...], b_ref[...]], axis=1)
    o_ref[...] = x_ref[...] + jnp.dot(mix, w_ref[...], preferred_element_type=F32)


def out_projection(x, oa, ob, w, tm=512):
    t, d = x.shape
    tm = _pick(t, tm)
    return pl.pallas_call(
        _outproj_body,
        grid=(t // tm,),
        in_specs=[pl.BlockSpec((tm, d), lambda i: (i, 0)),
                  pl.BlockSpec((tm, oa.shape[1]), lambda i: (i, 0)),
                  pl.BlockSpec((tm, ob.shape[1]), lambda i: (i, 0)),
                  pl.BlockSpec(w.shape, lambda i: (0, 0))],
        out_specs=pl.BlockSpec((tm, d), lambda i: (i, 0)),
        out_shape=jax.ShapeDtypeStruct((t, d), F32),
        compiler_params=_params(1),
        name="out_projection",
    )(x, oa, ob, w)


def _ffn_body(x_ref, g_ref, wg_ref, wu_ref, wd_ref, o_ref, h_ref, acc_ref):
    j = pl.program_id(1)

    @pl.when(j == 0)
    def _():
        h_ref[...] = _rms(x_ref[...], g_ref[...]).astype(BF16)
        acc_ref[...] = jnp.zeros(acc_ref.shape, F32)

    h = h_ref[...]
    a = jnp.dot(h, wg_ref[...], preferred_element_type=F32)
    u = jnp.dot(h, wu_ref[...], preferred_element_type=F32)
    act = (a * jax.nn.sigmoid(a) * u).astype(BF16)
    acc_ref[...] += jnp.dot(act, wd_ref[...], preferred_element_type=F32)

    @pl.when(j == pl.num_programs(1) - 1)
    def _():
        o_ref[...] = x_ref[...] + acc_ref[...]


def ffn_swiglu(x, g, wg, wu, wd, tm=512, tf=1408):
    t, d = x.shape
    f = wg.shape[1]
    tm = _pick(t, tm)
    tf = _pick(f, tf)
    return pl.pallas_call(
        _ffn_body,
        grid=(t // tm, f // tf),
        in_specs=[pl.BlockSpec((tm, d), lambda i, j: (i, 0)),
                  pl.BlockSpec((1, d), lambda i, j: (0, 0)),
                  pl.BlockSpec((d, tf), lambda i, j: (0, j)),
                  pl.BlockSpec((d, tf), lambda i, j: (0, j)),
                  pl.BlockSpec((tf, d), lambda i, j: (j, 0))],
        out_specs=pl.BlockSpec((tm, d), lambda i, j: (i, 0)),
        out_shape=jax.ShapeDtypeStruct((t, d), F32),
        scratch_shapes=[pltpu.VMEM((tm, d), BF16), pltpu.VMEM((tm, d), F32)],
        compiler_params=_params(2),
        name="ffn_swiglu",
    )(x, g.reshape(1, d), wg, wu, wd)


def _mix_ffn_body(x_ref, a_ref, b_ref, wo_ref, g_ref, wg_ref, wu_ref, wd_ref, o_ref, *, ff_chunk):
    mix = jnp.concatenate([a_ref[...], b_ref[...]], axis=1)
    x1 = x_ref[...] + jnp.dot(mix, wo_ref[...], preferred_element_type=F32)
    h = _rms(x1, g_ref[...]).astype(BF16)
    y = x1
    for c in range(wg_ref.shape[1] // ff_chunk):
        cols = slice(c * ff_chunk, (c + 1) * ff_chunk)
        a = jnp.dot(h, wg_ref[:, cols], preferred_element_type=F32)
        u = jnp.dot(h, wu_ref[:, cols], preferred_element_type=F32)
        act = (a * jax.nn.sigmoid(a) * u).astype(BF16)
        y = y + jnp.dot(act, wd_ref[cols, :], preferred_element_type=F32)
    o_ref[...] = y


def mix_ffn(x, oa, ob, wo, g, wg, wu, wd, tm=512, ff_chunk=1408):
    t, d = x.shape
    f = wg.shape[1]
    tm = _pick(t, tm)
    ff_chunk = _pick(f, ff_chunk)
    resident = lambda shape: pl.BlockSpec(shape, lambda i: (0, 0), pipeline_mode=pl.Buffered(1))
    return pl.pallas_call(
        functools.partial(_mix_ffn_body, ff_chunk=ff_chunk),
        grid=(t // tm,),
        in_specs=[pl.BlockSpec((tm, d), lambda i: (i, 0)),
                  pl.BlockSpec((tm, oa.shape[1]), lambda i: (i, 0)),
                  pl.BlockSpec((tm, ob.shape[1]), lambda i: (i, 0)),
                  resident(wo.shape), resident((1, d)),
                  resident(wg.shape), resident(wu.shape), resident(wd.shape)],
        out_specs=pl.BlockSpec((tm, d), lambda i: (i, 0)),
        out_shape=jax.ShapeDtypeStruct((t, d), F32),
        compiler_params=_params(1),
        name="mix_ffn",
    )(x, oa, ob, wo, g.reshape(1, d), wg, wu, wd)


def _conv_body(x_ref, g_ref, w1_ref, b1_ref, wdw_ref, bdw_ref, lng_ref, lnb_ref, w2_ref, b2_ref,
               o_ref, u_ref, v_ref, wb_ref, c_ref, *, tm, rows_per_chunk):
    d = x_ref.shape[-1]
    t = pl.program_id(1)
    sublanes = wb_ref.shape[1]

    @pl.when(t == 0)
    def _():
        u_ref[:, 0:CONV_HIST, :] = jnp.zeros((d // LANES, CONV_HIST, LANES), F32)
        wb_ref[...] = jnp.broadcast_to(wdw_ref[...][:, None, :], wb_ref.shape)

    h = _rms(x_ref[0], g_ref[...]).astype(BF16)
    z = jnp.dot(h, w1_ref[...], preferred_element_type=F32) + b1_ref[...]
    u = z[:, :d] * jax.nn.sigmoid(z[:, d:])
    n_slabs = d // LANES
    for c in range(n_slabs):
        u_ref[c, CONV_HIST:CONV_HIST + tm, :] = u[:, c * LANES:(c + 1) * LANES]

    first_tap = CONV_HIST - (CONV_WIDTH - 1)
    conv_rows = 4 * sublanes

    for c in range(n_slabs):
        lanes = slice(c * LANES, (c + 1) * LANES)
        w = [wb_ref[j, :, lanes] for j in range(CONV_WIDTH)]
        bias = jnp.zeros((sublanes, LANES), F32) + bdw_ref[:, lanes]

        def rows_block(i, carry, c=c, lanes=lanes, w=w, bias=bias):
            r0 = pl.multiple_of(i * conv_rows, conv_rows)
            for r in range(conv_rows // sublanes):
                sums = [bias, None]
                for j in range(CONV_WIDTH):
                    tap = u_ref[c, pl.ds(r0 + (first_tap + j + r * sublanes), sublanes, stride=1), :] * w[j]
                    sums[j % 2] = tap if sums[j % 2] is None else sums[j % 2] + tap
                c_ref[pl.ds(r0 + r * sublanes, sublanes), lanes] = sums[0] + sums[1]
            return carry

        lax.fori_loop(0, tm // conv_rows, rows_block, 0)

    def chunk(i, carry):
        r0 = pl.multiple_of(i * rows_per_chunk, rows_per_chunk)
        acc = c_ref[pl.ds(r0, rows_per_chunk), :]
        mu = jnp.mean(acc, axis=-1, keepdims=True)
        xc = acc - mu
        var = jnp.mean(xc * xc, axis=-1, keepdims=True)
        y = xc * lax.rsqrt(var + LN_EPS) * lng_ref[...] + lnb_ref[...]
        v_ref[pl.ds(r0, rows_per_chunk), :] = (y * jax.nn.sigmoid(y)).astype(BF16)
        return carry

    lax.fori_loop(0, tm // rows_per_chunk, chunk, 0)
    u_ref[:, 0:CONV_HIST, :] = u_ref[:, tm:tm + CONV_HIST, :]
    o_ref[0] = x_ref[0] + jnp.dot(v_ref[...], w2_ref[...], preferred_element_type=F32) + b2_ref[...]


def conformer_conv(x, g, w1, b1, wdw, bdw, lng, lnb, w2, b2, tm=512, rows_per_chunk=128):
    b, s, d = x.shape
    tm = _pick(s, tm)
    kernel = functools.partial(_conv_body, tm=tm, rows_per_chunk=rows_per_chunk)
    vec = lambda n: pl.BlockSpec((1, n), lambda bi, t: (0, 0))
    return pl.pallas_call(
        kernel,
        grid=(b, s // tm),
        in_specs=[pl.BlockSpec((1, tm, d), lambda bi, t: (bi, t, 0)),
                  vec(d),
                  pl.BlockSpec((d, 2 * d), lambda bi, t: (0, 0)),
                  vec(2 * d),
                  pl.BlockSpec((CONV_WIDTH, d), lambda bi, t: (0, 0)),
                  vec(d), vec(d), vec(d),
                  pl.BlockSpec((d, d), lambda bi, t: (0, 0)),
                  vec(d)],
        out_specs=pl.BlockSpec((1, tm, d), lambda bi, t: (bi, t, 0)),
        out_shape=jax.ShapeDtypeStruct((b, s, d), F32),
        scratch_shapes=[pltpu.VMEM((d // LANES, tm + CONV_HIST, LANES), F32), pltpu.VMEM((tm, d), BF16),
                        pltpu.VMEM((CONV_WIDTH, F32_SUBLANES, d), F32), pltpu.VMEM((tm, d), F32)],
        compiler_params=_params(2),
        name="conformer_conv",
    )(x, g.reshape(1, d), w1, b1.reshape(1, 2 * d), wdw, bdw.reshape(1, d), lng.reshape(1, d),
      lnb.reshape(1, d), w2, b2.reshape(1, d))


def _router_body(x_ref, g_ref, wr_ref, h_ref, meta_ref, *, n_experts):
    h = _rms(x_ref[...], g_ref[...])
    _store_slabs(h_ref, 0, h)
    logits = jnp.dot(h.astype(BF16), wr_ref[...], preferred_element_type=F32)
    lane = lax.broadcasted_iota(I32, logits.shape, 1).astype(F32)
    neg = jnp.full_like(logits, -jnp.inf)
    far = jnp.full_like(logits, float(LANES))
    l1 = jnp.where(lane < n_experts, logits, neg)
    m1 = jnp.max(l1, axis=-1, keepdims=True)
    i1 = jnp.min(jnp.where(l1 == m1, lane, far), axis=-1, keepdims=True)
    l2 = jnp.where(lane == i1, neg, l1)
    m2 = jnp.max(l2, axis=-1, keepdims=True)
    i2 = jnp.min(jnp.where(l2 == m2, lane, far), axis=-1, keepdims=True)
    e2 = jnp.exp(m2 - m1)
    w1 = 1.0 / (1.0 + e2)
    w2 = e2 / (1.0 + e2)
    zero = jnp.zeros_like(logits)
    meta_ref[...] = jnp.where(lane == 0, i1,
                    jnp.where(lane == 1, i2,
                    jnp.where(lane == 2, w1, jnp.where(lane == 3, w2, zero))))


def moe_router(x, g, wr, n_experts, tm=512):
    t, d = x.shape
    tm = _pick(t, tm)
    kernel = functools.partial(_router_body, n_experts=n_experts)
    return pl.pallas_call(
        kernel,
        grid=(t // tm,),
        in_specs=[pl.BlockSpec((tm, d), lambda i: (i, 0)),
                  pl.BlockSpec((1, d), lambda i: (0, 0)),
                  pl.BlockSpec((d, LANES), lambda i: (0, 0))],
        out_specs=[pl.BlockSpec((tm * (d // LANES), LANES), lambda i: (i, 0)),
                   pl.BlockSpec((tm, LANES), lambda i: (i, 0))],
        out_shape=[jax.ShapeDtypeStruct((t * (d // LANES), LANES), F32),
                   jax.ShapeDtypeStruct((t, LANES), F32)],
        compiler_params=_params(1),
        name="moe_router",
    )(x, g.reshape(1, d), wr)


GATHER_UNROLL = 8


def _store_slabs(ref, first_row, value):
    n, d = value.shape
    n_slabs = d // LANES
    for c in range(n_slabs):
        ref[pl.ds(first_row * n_slabs + c, n, stride=n_slabs), :] = value[:, c * LANES:(c + 1) * LANES]


def _load_slabs(ref, first_row, n, d):
    n_slabs = d // LANES
    return jnp.concatenate([ref[pl.ds(first_row * n_slabs + c, n, stride=n_slabs), :]
                            for c in range(n_slabs)], axis=1)


def _dispatch_body(pad_ref, tn_ref, d_ref, h_hbm, xs_hbm, zero_ref, buf_ref, in_sem, sem, zsem,
                   *, tm, n_slabs, n_experts, n_sub):
    i = pl.program_id(0)
    n = pl.num_programs(0)
    rows = tm * n_slabs
    n_ring = buf_ref.shape[0]

    def stage(tile):
        return pltpu.make_async_copy(h_hbm.at[pl.ds(pl.multiple_of(tile * rows, rows), rows), :],
                                     buf_ref.at[tile % n_ring], in_sem.at[tile % n_ring])

    def tile_copy(slot):
        return pltpu.make_async_copy(buf_ref.at[0], xs_hbm.at[pl.ds(0, rows), :], sem.at[slot])

    @pl.when(i == 0)
    def _():
        stage(0).start()

    @pl.when(i + 1 < n)
    def _():
        stage(i + 1).start()

    stage(i).wait()
    slot = i % 2
    ring = i % n_ring
    for rank in range(TOP_K):
        def body(it, carry, rank=rank):
            base = it * GATHER_UNROLL
            dsts = [pl.multiple_of(d_ref[0, 0, rank * tm + base + k], n_slabs) for k in range(GATHER_UNROLL)]
            for k in range(GATHER_UNROLL):
                src = pl.multiple_of((base + k) * n_slabs, n_slabs)
                pltpu.make_async_copy(buf_ref.at[ring, pl.ds(src, n_slabs), :],
                                      xs_hbm.at[pl.ds(dsts[k], n_slabs), :], sem.at[slot]).start()
            return carry
        lax.fori_loop(0, tm // GATHER_UNROLL, body, 0)

    @pl.when(i == 0)
    def _():
        zero_ref[...] = jnp.zeros(zero_ref.shape, F32)
        zero_row = zero_ref.at[pl.ds(0, n_slabs), :]
        sub_rows = zero_ref.shape[0]

        def unused_sub_tiles(fn):
            def tbody(tile, carry):
                for s in range(n_sub):
                    @pl.when(s >= tn_ref[tile])
                    def _():
                        dst = pl.multiple_of((tile * n_sub + s) * sub_rows, sub_rows)
                        fn(pltpu.make_async_copy(zero_ref, xs_hbm.at[pl.ds(dst, sub_rows), :], zsem))
                return carry
            lax.fori_loop(0, tn_ref.shape[0], tbody, 0)

        def tail_rows(fn):
            for e in range(n_experts):
                def zbody(r, carry, e=e):
                    dst = pl.multiple_of((pad_ref[0, e] + r) * n_slabs, n_slabs)
                    fn(pltpu.make_async_copy(zero_row, xs_hbm.at[pl.ds(dst, n_slabs), :], zsem))
                    return carry
                lax.fori_loop(0, pad_ref[1, e], zbody, 0)

        unused_sub_tiles(lambda cp: cp.start())
        tail_rows(lambda cp: cp.start())
        unused_sub_tiles(lambda cp: cp.wait())
        tail_rows(lambda cp: cp.wait())

    @pl.when(i > 0)
    def _():
        for rank in range(TOP_K):
            tile_copy(1 - slot).wait()

    @pl.when(i == n - 1)
    def _():
        for rank in range(TOP_K):
            tile_copy(slot).wait()


def moe_dispatch(h, dest, pad, tile_count, sub, n_sub, n_slabs):
    n_tok_tiles = dest.shape[0]
    tm = dest.shape[2] // TOP_K
    n_experts = pad.shape[1]
    n_rows = tile_count.shape[0] * n_sub * sub
    grid_spec = pltpu.PrefetchScalarGridSpec(
        num_scalar_prefetch=2,
        grid=(n_tok_tiles,),
        in_specs=[pl.BlockSpec((1, 1, TOP_K * tm), lambda i, pad, tn: (i, 0, 0), memory_space=pltpu.SMEM),
                  pl.BlockSpec(memory_space=pl.ANY)],
        out_specs=pl.BlockSpec(memory_space=pl.ANY),
        scratch_shapes=[pltpu.VMEM((sub * n_slabs, LANES), F32), pltpu.VMEM((3, tm * n_slabs, LANES), F32),
                        pltpu.SemaphoreType.DMA((3,)), pltpu.SemaphoreType.DMA((2,)),
                        pltpu.SemaphoreType.DMA(())],
    )
    return pl.pallas_call(
        functools.partial(_dispatch_body, tm=tm, n_slabs=n_slabs, n_experts=n_experts, n_sub=n_sub),
        grid_spec=grid_spec,
        out_shape=jax.ShapeDtypeStruct((n_rows * n_slabs, LANES), F32),
        compiler_params=_params(1),
        name="moe_dispatch",
    )(pad, tile_count, dest, h)


def _moe_body(te_ref, tn_ref, xs_ref, wg_ref, wu_ref, wd_ref, y_ref,
              xb_ref, wgb_ref, wub_ref, wdb_ref, *, sub, n_sub):
    i = pl.program_id(0)
    j = pl.program_id(1)
    count = tn_ref[i]
    d = wg_ref.shape[2]

    for s in range(n_sub):
        rows = pl.ds(s * sub, sub)

        @pl.when((j == 0) & (s < count))
        def _():
            xb_ref[rows, :] = _load_slabs(xs_ref, s * sub, sub, d).astype(BF16)

        @pl.when(j == 0)
        def _():
            y_ref[rows, :] = jnp.zeros((sub, d), F32)

        @pl.when(s < count)
        def _():
            if s == 0:
                wg, wu, wd = (r[0, 0].astype(BF16) for r in (wg_ref, wu_ref, wd_ref))
                wgb_ref[...], wub_ref[...], wdb_ref[...] = wg, wu, wd
            else:
                wg, wu, wd = wgb_ref[...], wub_ref[...], wdb_ref[...]
            x = xb_ref[rows, :]
            a = jnp.dot(x, wg, preferred_element_type=F32)
            u = jnp.dot(x, wu, preferred_element_type=F32)
            act = (a * jax.nn.sigmoid(a) * u).astype(BF16)
            y_ref[rows, :] += jnp.dot(act, wd, preferred_element_type=F32)


def moe_experts(xs, tile_expert, tile_count, wg, wu, wd, layer, sub, n_sub, tf=512):
    d = wg.shape[2]
    n_slabs = d // LANES
    n_tiles = tile_expert.shape[0]
    tile_rows = n_sub * sub
    f = wg.shape[3]
    tf = _pick(f, tf)
    nf = f // tf

    def w_col(i, j, te, tn):
        return (layer, te[i], 0, jnp.where(tn[i] > 0, j, nf - 1))

    def w_row(i, j, te, tn):
        return (layer, te[i], jnp.where(tn[i] > 0, j, nf - 1), 0)

    grid_spec = pltpu.PrefetchScalarGridSpec(
        num_scalar_prefetch=2,
        grid=(n_tiles, nf),
        in_specs=[pl.BlockSpec((tile_rows * n_slabs, LANES), lambda i, j, te, tn: (i, 0)),
                  pl.BlockSpec((1, 1, d, tf), w_col),
                  pl.BlockSpec((1, 1, d, tf), w_col),
                  pl.BlockSpec((1, 1, tf, d), w_row)],
        out_specs=pl.BlockSpec((tile_rows, d), lambda i, j, te, tn: (i, 0)),
        scratch_shapes=[pltpu.VMEM((tile_rows, d), BF16),
                        pltpu.VMEM((d, tf), BF16), pltpu.VMEM((d, tf), BF16), pltpu.VMEM((tf, d), BF16)],
    )
    return pl.pallas_call(
        functools.partial(_moe_body, sub=sub, n_sub=n_sub),
        grid_spec=grid_spec,
        out_shape=jax.ShapeDtypeStruct((n_tiles * tile_rows, d), F32),
        compiler_params=_params(2),
        name="moe_experts",
    )(tile_expert, tile_count, xs, wg, wu, wd)


def _combine_body(d_cur_ref, d_nxt_ref, x_ref, meta_ref, g_ref, y_hbm, o_ref, buf_ref, sem,
                  *, tm, final_norm):
    i = pl.program_id(0)
    n = pl.num_programs(0)

    def gather(d_ref, slot):
        def body(it, carry):
            for k in range(GATHER_UNROLL):
                r = it * GATHER_UNROLL + k
                pltpu.make_async_copy(y_hbm.at[pl.ds(d_ref[0, 0, r], 1), :],
                                      buf_ref.at[slot, pl.ds(r, 1), :], sem.at[slot]).start()
            return carry
        lax.fori_loop(0, TOP_K * tm // GATHER_UNROLL, body, 0)

    @pl.when(i == 0)
    def _():
        gather(d_cur_ref, 0)

    @pl.when(i + 1 < n)
    def _():
        gather(d_nxt_ref, (i + 1) % 2)

    slot = i % 2
    pltpu.make_async_copy(y_hbm.at[pl.ds(0, TOP_K * tm), :], buf_ref.at[slot], sem.at[slot]).wait()
    meta = meta_ref[...]
    out = (x_ref[...] + meta[:, 2:3] * buf_ref[slot, 0:tm, :]
           + meta[:, 3:4] * buf_ref[slot, tm:2 * tm, :])
    if final_norm:
        out = _rms(out, g_ref[...])
    o_ref[...] = out


def moe_combine(x, meta, dest, y, g, final_norm, tm=256):
    t, d = x.shape
    n = t // tm
    smem = lambda fn: pl.BlockSpec((1, 1, TOP_K * tm), fn, memory_space=pltpu.SMEM)
    return pl.pallas_call(
        functools.partial(_combine_body, tm=tm, final_norm=final_norm),
        grid=(n,),
        in_specs=[smem(lambda i: (i, 0, 0)),
                  smem(lambda i: (jnp.minimum(i + 1, n - 1), 0, 0)),
                  pl.BlockSpec((tm, d), lambda i: (i, 0)),
                  pl.BlockSpec((tm, LANES), lambda i: (i, 0)),
                  pl.BlockSpec((1, d), lambda i: (0, 0)),
                  pl.BlockSpec(memory_space=pl.ANY)],
        out_specs=pl.BlockSpec((tm, d), lambda i: (i, 0)),
        out_shape=jax.ShapeDtypeStruct((t, d), F32),
        scratch_shapes=[pltpu.VMEM((2, TOP_K * tm, d), F32), pltpu.SemaphoreType.DMA((2,))],
        compiler_params=_params(1),
        name="moe_combine",
    )(dest, dest, x, meta, g.reshape(1, d), y)


def _route(meta, n_experts, sub, n_sub, tm_combine):
    t = meta.shape[0]
    tile_rows = sub * n_sub
    flat_e = meta[:, :TOP_K].astype(I32).reshape(-1)
    onehot = (flat_e[:, None] == jnp.arange(n_experts, dtype=I32)[None, :]).astype(I32)
    csum = jnp.cumsum(onehot, axis=0)
    rank = jnp.sum((csum - onehot) * onehot, axis=1)
    counts = csum[-1]
    subs = (counts + sub - 1) // sub
    tiles = (subs + n_sub - 1) // n_sub
    tile_ends = jnp.cumsum(tiles)
    tile_starts = tile_ends - tiles
    dest = tile_starts[flat_e] * tile_rows + rank
    n_tiles = ((t * TOP_K) // sub + n_experts + n_sub - 1) // n_sub + n_experts
    idx = jnp.arange(n_tiles, dtype=I32)
    tile_expert = jnp.minimum(jnp.searchsorted(tile_ends, idx, side="right"), n_experts - 1).astype(I32)
    local = idx - tile_starts[tile_expert]
    tile_count = jnp.clip(subs[tile_expert] - local * n_sub, 0, n_sub).astype(I32)
    dest_tiles = dest.reshape(t // tm_combine, tm_combine, TOP_K).transpose(0, 2, 1).reshape(
        t // tm_combine, 1, TOP_K * tm_combine)
    pad = jnp.stack([tile_starts * tile_rows + counts, subs * sub - counts]).astype(I32)
    return tile_expert, tile_count, dest_tiles, pad, n_tiles


def moe_layer(x, g, wr, wg, wu, wd, layer, final_g, final_norm, sub=512, n_sub=4, tm_combine=256):
    t, d = x.shape
    n_experts = wr.shape[1]
    tm_combine = min(tm_combine, t)
    wr_pad = jnp.zeros((d, LANES), BF16).at[:, :n_experts].set(wr.astype(BF16))
    h, meta = moe_router(x, g, wr_pad, n_experts)
    n_slabs = d // LANES
    tile_expert, tile_count, dest_tiles, pad, n_tiles = _route(meta, n_experts, sub, n_sub, tm_combine)
    xs = moe_dispatch(h, dest_tiles * n_slabs, pad, tile_count, sub, n_sub, n_slabs)
    y = moe_experts(xs, tile_expert, tile_count, wg, wu, wd, layer, sub, n_sub)
    return moe_combine(x, meta, dest_tiles, y, final_g, final_norm, tm_combine)


def attention_mixers(x, norm_g, w_in, lam_params, subln_g, sinks, layer):
    b, s, d = x.shape
    diff_heads = d // (4 * HEAD_DIM)
    swa_q_heads = d // (2 * HEAD_DIM)
    swa_kv_heads = max(1, swa_q_heads // 4)
    group = swa_q_heads // swa_kv_heads
    a_width = diff_heads * 2 * HEAD_DIM
    lam_init = 0.8 - 0.6 * math.exp(-0.3 * layer)
    x2 = x.reshape(b * s, d)
    proj = rms_matmul(x2, norm_g, w_in).reshape(b, s, -1)
    oa = diff_attention(proj, lam_params, subln_g, diff_heads, lam_init)
    q_col = 3 * a_width
    k_col = q_col + swa_q_heads * HEAD_DIM
    v_col = k_col + swa_kv_heads * HEAD_DIM
    ob = sliding_window_attention(proj, sinks, q_col, k_col, v_col, swa_kv_heads, group)
    return oa.reshape(b * s, -1), ob.reshape(b * s, -1)


def attention_layer(x, norm_g, w_in, lam_params, subln_g, sinks, w_out, layer):
    b, s, d = x.shape
    oa, ob = attention_mixers(x, norm_g, w_in, lam_params, subln_g, sinks, layer)
    return out_projection(x.reshape(b * s, d), oa, ob, w_out)


def kernel(x, attn_norm_g, w_in_att, diff_lambda, diff_subln_g, attn_sinks, w_out_att, ffn_norm_g, w_ffn_gate, w_ffn_up, w_ffn_down, conv_norm_g, w_pw1, b_pw1, w_dw, b_dw, conv_ln_g, conv_ln_b, w_pw2, b_pw2, moe_norm_g, w_router, w_exp_gate, w_exp_up, w_exp_down, final_norm_g):
    b, s, d = x.shape
    depth = attn_norm_g.shape[0] + conv_norm_g.shape[0]
    assert depth % 2 == 0, "the final RMSNorm is fused into the last expert layer"
    bf = lambda w: w.astype(BF16)
    for layer in range(depth):
        i = layer // 2
        if layer % 2 == 0:
            oa, ob = attention_mixers(x, attn_norm_g[i], bf(w_in_att[i]), diff_lambda[i], diff_subln_g[i],
                                      attn_sinks[i], layer)
            x2 = mix_ffn(x.reshape(b * s, d), oa, ob, bf(w_out_att[i]), ffn_norm_g[i],
                         bf(w_ffn_gate[i]), bf(w_ffn_up[i]), bf(w_ffn_down[i]))
            x = x2.reshape(b, s, d)
        else:
            x = conformer_conv(x, conv_norm_g[i], bf(w_pw1[i]), b_pw1[i], w_dw[i], b_dw[i],
                               conv_ln_g[i], conv_ln_b[i], bf(w_pw2[i]), b_pw2[i])
            x2 = moe_layer(x.reshape(b * s, d), moe_norm_g[i], w_router[i], w_exp_gate,
                           w_exp_up, w_exp_down, i, final_norm_g,
                           final_norm=(layer == depth - 1))
            x = x2.reshape(b, s, d)
    return x
```

```python
import functools
import math

import jax
import jax.numpy as jnp
from jax import lax
from jax.experimental import pallas as pl
from jax.experimental.pallas import tpu as pltpu

BF16 = jnp.bfloat16
F32 = jnp.float32
I32 = jnp.int32

LOG2_E = 1.4426950408889634
RMS_EPS = 1e-6
LN_EPS = 1e-5
HEAD_DIM = 64
ATTN_BLOCK = 128
CONV_WIDTH = 31
TOP_K = 2
LANES = 128
F32_SUBLANES = 8
CONV_HIST = 32
V7X_VMEM_LIMIT = 56 * 1024 * 1024


def _params(n_axes, flags=None):
    return pltpu.CompilerParams(dimension_semantics=("arbitrary",) * n_axes,
                                vmem_limit_bytes=V7X_VMEM_LIMIT, flags=flags)


def _rms(x, g):
    return x * lax.rsqrt(jnp.mean(x * x, axis=-1, keepdims=True) + RMS_EPS) * g


def _pick(n, pref):
    t = min(n, pref)
    while n % t:
        t -= LANES if t > LANES else 8
    return t


def _rms_matmul_body(x_ref, g_ref, w_ref, o_ref):
    h = _rms(x_ref[...], g_ref[...]).astype(BF16)
    o_ref[...] = jnp.dot(h, w_ref[...], preferred_element_type=F32).astype(o_ref.dtype)


def rms_matmul(x, g, w, tm=512):
    t, d = x.shape
    n = w.shape[1]
    tm = _pick(t, tm)
    return pl.pallas_call(
        _rms_matmul_body,
        grid=(t // tm,),
        in_specs=[pl.BlockSpec((tm, d), lambda i: (i, 0)),
                  pl.BlockSpec((1, d), lambda i: (0, 0)),
                  pl.BlockSpec((d, n), lambda i: (0, 0))],
        out_specs=pl.BlockSpec((tm, n), lambda i: (i, 0)),
        out_shape=jax.ShapeDtypeStruct((t, n), BF16),
        compiler_params=_params(1),
        name="rms_inproj",
    )(x, g.reshape(1, d), w)


def _diff_attn_body(q_ref, k_ref, v_ref, lam_ref, g_ref, o_ref, q_ref2, m_ref, l_ref, acc_ref,
                    *, blk, lam_init):
    qi = pl.program_id(2)
    hw = 2 * HEAD_DIM
    half = blk // 2
    lane = lax.broadcasted_iota(I32, (half, hw), 1)
    q = (q_ref[0].astype(F32) * (HEAD_DIM ** -0.5 * LOG2_E)).astype(BF16)
    zero = jnp.zeros((half, hw), BF16)
    q_ref2[...] = jnp.concatenate(
        [jnp.where(keep, q[r:r + half], zero)
         for r in (0, half) for keep in (lane < HEAD_DIM, lane >= HEAD_DIM)], axis=0)

    m_ref[...] = jnp.full(m_ref.shape, -jnp.inf, F32)
    l_ref[...] = jnp.zeros(l_ref.shape, F32)
    acc_ref[...] = jnp.zeros(acc_ref.shape, F32)

    def block(key_start, n_keys, row0, n_rows, masked):
        rows = pl.ds(row0, n_rows)
        k = k_ref[0, pl.ds(key_start, n_keys), :]
        v = v_ref[0, pl.ds(key_start, n_keys), :]
        s = lax.dot_general(q_ref2[rows, :], k, (((1,), (1,)), ((), ())), preferred_element_type=F32)
        if masked:
            row = lax.broadcasted_iota(I32, (n_rows, n_keys), 0) & (half - 1)
            col = lax.broadcasted_iota(I32, (n_rows, n_keys), 1)
            s = jnp.where(col <= row, s, -jnp.inf)
        tiles = [s[:, c * LANES:(c + 1) * LANES] for c in range(n_keys // LANES)]
        m_prev = m_ref[rows, :]
        m_new = jnp.maximum(m_prev, jnp.max(functools.reduce(jnp.maximum, tiles), axis=-1, keepdims=True))
        alpha = jnp.exp2(m_prev - m_new)
        p_tiles = [jnp.exp2(t - m_new) for t in tiles]
        l_ref[rows, :] = alpha * l_ref[rows, :] + functools.reduce(jnp.add, p_tiles)
        p = jnp.concatenate(p_tiles, axis=1).astype(BF16)
        acc_ref[rows, :] = alpha * acc_ref[rows, :] + jnp.dot(p, v, preferred_element_type=F32)
        m_ref[rows, :] = m_new

    def full_block(ki):
        block(pl.multiple_of(ki * blk, blk), blk, 0, 2 * blk, False)

    def pair(j, carry):
        full_block(2 * j)
        full_block(2 * j + 1)
        return carry

    lax.fori_loop(0, qi // 2, pair, 0)

    @pl.when(qi % 2 == 1)
    def _():
        full_block(qi - 1)

    diag = pl.multiple_of(qi * blk, blk)
    block(diag, half, 0, 2 * half, True)
    block(diag, half, 2 * half, 2 * half, False)
    block(diag + half, half, 2 * half, 2 * half, True)

    lp = lam_ref[...]
    lam = (jnp.exp(jnp.sum(lp[0:1] * lp[1:2], axis=-1, keepdims=True))
           - jnp.exp(jnp.sum(lp[2:3] * lp[3:4], axis=-1, keepdims=True)) + lam_init)
    o = acc_ref[...] / jnp.sum(l_ref[...], axis=-1, keepdims=True)
    o = (jnp.concatenate([o[0:half], o[2 * half:3 * half]], axis=0)
         - lam * jnp.concatenate([o[half:2 * half], o[3 * half:4 * half]], axis=0))
    o = _rms(o, g_ref[...]) * (1.0 - lam_init)
    o_ref[0] = o.astype(o_ref.dtype)


def diff_attention(proj, lam_params, subln_g, n_heads, lam_init, blk=1024):
    b, s, _ = proj.shape
    blk = _pick(s, blk)
    assert blk & (blk - 1) == 0, "the causal mask uses power-of-two half blocks"
    hw = 2 * HEAD_DIM
    kernel = functools.partial(_diff_attn_body, blk=blk, lam_init=lam_init)
    return pl.pallas_call(
        kernel,
        grid=(b, n_heads, s // blk),
        in_specs=[pl.BlockSpec((1, blk, hw), lambda bi, h, qi: (bi, qi, h)),
                  pl.BlockSpec((1, s, hw), lambda bi, h, qi: (bi, 0, n_heads + h)),
                  pl.BlockSpec((1, s, hw), lambda bi, h, qi: (bi, 0, 2 * n_heads + h)),
                  pl.BlockSpec((4, HEAD_DIM), lambda bi, h, qi: (0, 0)),
                  pl.BlockSpec((1, hw), lambda bi, h, qi: (0, 0))],
        out_specs=pl.BlockSpec((1, blk, hw), lambda bi, h, qi: (bi, qi, h)),
        out_shape=jax.ShapeDtypeStruct((b, s, n_heads * hw), BF16),
        scratch_shapes=[pltpu.VMEM((2 * blk, hw), BF16), pltpu.VMEM((2 * blk, LANES), F32),
                        pltpu.VMEM((2 * blk, LANES), F32), pltpu.VMEM((2 * blk, hw), F32)],
        compiler_params=_params(3),
        name="diff_attention",
    )(proj, proj, proj, lam_params, subln_g.reshape(1, hw))


def _swa_body(sink_ref, q_ref, kc_ref, kp_ref, vc_ref, vp_ref, o_ref, *, tq, kv_heads, group):
    t = pl.program_id(1)
    w = ATTN_BLOCK
    row = lax.broadcasted_iota(I32, (group * w, 2 * w), 0) & (w - 1)
    col = lax.broadcasted_iota(I32, (group * w, 2 * w), 1)
    band = (col > row) & (col <= row + w)
    band_first = band & ((col >= w) | (t > 0))
    low_half = lax.broadcasted_iota(I32, (w, LANES), 1) < HEAD_DIM
    for j in range(tq // w):
        rows = slice(j * w, (j + 1) * w)
        if j == 0:
            k2 = jnp.concatenate([kp_ref[0], kc_ref[0, rows, :]], axis=0)
            v2 = jnp.concatenate([vp_ref[0], vc_ref[0, rows, :]], axis=0)
            mask = band_first
        else:
            k2 = kc_ref[0, (j - 1) * w:(j + 1) * w, :]
            v2 = vc_ref[0, (j - 1) * w:(j + 1) * w, :]
            mask = band
        outs = []
        for g in range(kv_heads):
            kg = k2[:, g * HEAD_DIM:(g + 1) * HEAD_DIM]
            vg = v2[:, g * HEAD_DIM:(g + 1) * HEAD_DIM]
            kdup = jnp.concatenate([kg, kg], axis=1)
            vdup = jnp.concatenate([vg, vg], axis=1)
            heads = [g * group + i for i in range(group)]
            q_tiles = []
            for h in heads:
                q = q_ref[0, rows, (h // 2) * LANES:(h // 2 + 1) * LANES]
                q = (q.astype(F32) * (HEAD_DIM ** -0.5)).astype(BF16)
                own_half = low_half if h % 2 == 0 else jnp.logical_not(low_half)
                q_tiles.append(jnp.where(own_half, q, jnp.zeros_like(q)))
            sink = jnp.concatenate([jnp.full((w, LANES), sink_ref[h], F32) for h in heads], axis=0)
            s = lax.dot_general(jnp.concatenate(q_tiles, axis=0), kdup, (((1,), (1,)), ((), ())),
                                preferred_element_type=F32)
            s = jnp.where(mask, s, -jnp.inf)
            s0, s1 = s[:, :w], s[:, w:]
            m = jnp.maximum(jnp.max(jnp.maximum(s0, s1), axis=-1, keepdims=True), sink)
            e0 = jnp.exp(s0 - m)
            e1 = jnp.exp(s1 - m)
            denom = jnp.sum(e0 + e1, axis=-1, keepdims=True) + jnp.exp(sink - m)
            e = jnp.concatenate([e0, e1], axis=1).astype(BF16)
            o = jnp.dot(e, vdup, preferred_element_type=F32) / denom
            outs.extend(o[i * w:(i + 1) * w, :] for i in range(group))
        tiles = [jnp.where(low_half, outs[h], outs[h + 1]) for h in range(0, len(outs), 2)]
        o_ref[0, rows, :] = jnp.concatenate(tiles, axis=1).astype(o_ref.dtype)


def sliding_window_attention(proj, sinks, q_col, k_col, v_col, kv_heads, group, tq=512):
    b, s, _ = proj.shape
    tq = _pick(s, tq)
    qw = kv_heads * group * HEAD_DIM
    kw = kv_heads * HEAD_DIM
    sub = tq // ATTN_BLOCK
    kernel = functools.partial(_swa_body, tq=tq, kv_heads=kv_heads, group=group)
    prev = lambda bi, t: (bi, jnp.maximum(t * sub - 1, 0), 0)
    return pl.pallas_call(
        kernel,
        grid=(b, s // tq),
        in_specs=[pl.BlockSpec(memory_space=pltpu.SMEM),
                  pl.BlockSpec((1, tq, qw), lambda bi, t: (bi, t, q_col // qw)),
                  pl.BlockSpec((1, tq, kw), lambda bi, t: (bi, t, k_col // kw)),
                  pl.BlockSpec((1, ATTN_BLOCK, kw), lambda bi, t: prev(bi, t)[:2] + (k_col // kw,)),
                  pl.BlockSpec((1, tq, kw), lambda bi, t: (bi, t, v_col // kw)),
                  pl.BlockSpec((1, ATTN_BLOCK, kw), lambda bi, t: prev(bi, t)[:2] + (v_col // kw,))],
        out_specs=pl.BlockSpec((1, tq, qw), lambda bi, t: (bi, t, 0)),
        out_shape=jax.ShapeDtypeStruct((b, s, qw), BF16),
        compiler_params=_params(2),
        name="swa_attention",
    )(sinks, proj, proj, proj, proj, proj)


def _outproj_body(x_ref, a_ref, b_ref, w_ref, o_ref):
    mix = jnp.concatenate([a_ref[...], b_ref[...]], axis=1)
    o_ref[...] = x_ref[...] + jnp.dot(mix, w_ref[...], preferred_element_type=F32)


def out_projection(x, oa, ob, w, tm=512):
    t, d = x.shape
    tm = _pick(t, tm)
    return pl.pallas_call(
        _outproj_body,
        grid=(t // tm,),
        in_specs=[pl.BlockSpec((tm, d), lambda i: (i, 0)),
                  pl.BlockSpec((tm, oa.shape[1]), lambda i: (i, 0)),
                  pl.BlockSpec((tm, ob.shape[1]), lambda i: (i, 0)),
                  pl.BlockSpec(w.shape, lambda i: (0, 0))],
        out_specs=pl.BlockSpec((tm, d), lambda i: (i, 0)),
        out_shape=jax.ShapeDtypeStruct((t, d), F32),
        compiler_params=_params(1),
        name="out_projection",
    )(x, oa, ob, w)


def _ffn_body(x_ref, g_ref, wg_ref, wu_ref, wd_ref, o_ref, h_ref, acc_ref):
    j = pl.program_id(1)

    @pl.when(j == 0)
    def _():
        h_ref[...] = _rms(x_ref[...], g_ref[...]).astype(BF16)
        acc_ref[...] = jnp.zeros(acc_ref.shape, F32)

    h = h_ref[...]
    a = jnp.dot(h, wg_ref[...], preferred_element_type=F32)
    u = jnp.dot(h, wu_ref[...], preferred_element_type=F32)
    act = (a * jax.nn.sigmoid(a) * u).astype(BF16)
    acc_ref[...] += jnp.dot(act, wd_ref[...], preferred_element_type=F32)

    @pl.when(j == pl.num_programs(1) - 1)
    def _():
        o_ref[...] = x_ref[...] + acc_ref[...]


def ffn_swiglu(x, g, wg, wu, wd, tm=512, tf=1408):
    t, d = x.shape
    f = wg.shape[1]
    tm = _pick(t, tm)
    tf = _pick(f, tf)
    return pl.pallas_call(
        _ffn_body,
        grid=(t // tm, f // tf),
        in_specs=[pl.BlockSpec((tm, d), lambda i, j: (i, 0)),
                  pl.BlockSpec((1, d), lambda i, j: (0, 0)),
                  pl.BlockSpec((d, tf), lambda i, j: (0, j)),
                  pl.BlockSpec((d, tf), lambda i, j: (0, j)),
                  pl.BlockSpec((tf, d), lambda i, j: (j, 0))],
        out_specs=pl.BlockSpec((tm, d), lambda i, j: (i, 0)),
        out_shape=jax.ShapeDtypeStruct((t, d), F32),
        scratch_shapes=[pltpu.VMEM((tm, d), BF16), pltpu.VMEM((tm, d), F32)],
        compiler_params=_params(2),
        name="ffn_swiglu",
    )(x, g.reshape(1, d), wg, wu, wd)


def _mix_ffn_body(x_ref, a_ref, b_ref, wo_ref, g_ref, wg_ref, wu_ref, wd_ref, o_ref, *, ff_chunk):
    mix = jnp.concatenate([a_ref[...], b_ref[...]], axis=1)
    x1 = x_ref[...] + jnp.dot(mix, wo_ref[...], preferred_element_type=F32)
    h = _rms(x1, g_ref[...]).astype(BF16)
    y = x1
    for c in range(wg_ref.shape[1] // ff_chunk):
        cols = slice(c * ff_chunk, (c + 1) * ff_chunk)
        a = jnp.dot(h, wg_ref[:, cols], preferred_element_type=F32)
        u = jnp.dot(h, wu_ref[:, cols], preferred_element_type=F32)
        act = (a * jax.nn.sigmoid(a) * u).astype(BF16)
        y = y + jnp.dot(act, wd_ref[cols, :], preferred_element_type=F32)
    o_ref[...] = y


def mix_ffn(x, oa, ob, wo, g, wg, wu, wd, tm=512, ff_chunk=1408):
    t, d = x.shape
    f = wg.shape[1]
    tm = _pick(t, tm)
    ff_chunk = _pick(f, ff_chunk)
    resident = lambda shape: pl.BlockSpec(shape, lambda i: (0, 0), pipeline_mode=pl.Buffered(1))
    return pl.pallas_call(
        functools.partial(_mix_ffn_body, ff_chunk=ff_chunk),
        grid=(t // tm,),
        in_specs=[pl.BlockSpec((tm, d), lambda i: (i, 0)),
                  pl.BlockSpec((tm, oa.shape[1]), lambda i: (i, 0)),
                  pl.BlockSpec((tm, ob.shape[1]), lambda i: (i, 0)),
                  resident(wo.shape), resident((1, d)),
                  resident(wg.shape), resident(wu.shape), resident(wd.shape)],
        out_specs=pl.BlockSpec((tm, d), lambda i: (i, 0)),
        out_shape=jax.ShapeDtypeStruct((t, d), F32),
        compiler_params=_params(1),
        name="mix_ffn",
    )(x, oa, ob, wo, g.reshape(1, d), wg, wu, wd)


def _conv_body(x_ref, g_ref, w1_ref, b1_ref, wdw_ref, bdw_ref, lng_ref, lnb_ref, w2_ref, b2_ref,
               o_ref, u_ref, v_ref, wb_ref, c_ref, *, tm, rows_per_chunk):
    d = x_ref.shape[-1]
    t = pl.program_id(1)
    sublanes = wb_ref.shape[1]

    @pl.when(t == 0)
    def _():
        u_ref[:, 0:CONV_HIST, :] = jnp.zeros((d // LANES, CONV_HIST, LANES), F32)
        wb_ref[...] = jnp.broadcast_to(wdw_ref[...][:, None, :], wb_ref.shape)

    h = _rms(x_ref[0], g_ref[...]).astype(BF16)
    z = jnp.dot(h, w1_ref[...], preferred_element_type=F32) + b1_ref[...]
    u = z[:, :d] * jax.nn.sigmoid(z[:, d:])
    n_slabs = d // LANES
    for c in range(n_slabs):
        u_ref[c, CONV_HIST:CONV_HIST + tm, :] = u[:, c * LANES:(c + 1) * LANES]

    first_tap = CONV_HIST - (CONV_WIDTH - 1)
    conv_rows = 4 * sublanes

    for c in range(n_slabs):
        lanes = slice(c * LANES, (c + 1) * LANES)
        w = [wb_ref[j, :, lanes] for j in range(CONV_WIDTH)]
        bias = jnp.zeros((sublanes, LANES), F32) + bdw_ref[:, lanes]

        def rows_block(i, carry, c=c, lanes=lanes, w=w, bias=bias):
            r0 = pl.multiple_of(i * conv_rows, conv_rows)
            for r in range(conv_rows // sublanes):
                sums = [bias, None]
                for j in range(CONV_WIDTH):
                    tap = u_ref[c, pl.ds(r0 + (first_tap + j + r * sublanes), sublanes, stride=1), :] * w[j]
                    sums[j % 2] = tap if sums[j % 2] is None else sums[j % 2] + tap
                c_ref[pl.ds(r0 + r * sublanes, sublanes), lanes] = sums[0] + sums[1]
            return carry

        lax.fori_loop(0, tm // conv_rows, rows_block, 0)

    def chunk(i, carry):
        r0 = pl.multiple_of(i * rows_per_chunk, rows_per_chunk)
        acc = c_ref[pl.ds(r0, rows_per_chunk), :]
        mu = jnp.mean(acc, axis=-1, keepdims=True)
        xc = acc - mu
        var = jnp.mean(xc * xc, axis=-1, keepdims=True)
        y = xc * lax.rsqrt(var + LN_EPS) * lng_ref[...] + lnb_ref[...]
        v_ref[pl.ds(r0, rows_per_chunk), :] = (y * jax.nn.sigmoid(y)).astype(BF16)
        return carry

    lax.fori_loop(0, tm // rows_per_chunk, chunk, 0)
    u_ref[:, 0:CONV_HIST, :] = u_ref[:, tm:tm + CONV_HIST, :]
    o_ref[0] = x_ref[0] + jnp.dot(v_ref[...], w2_ref[...], preferred_element_type=F32) + b2_ref[...]


def conformer_conv(x, g, w1, b1, wdw, bdw, lng, lnb, w2, b2, tm=512, rows_per_chunk=128):
    b, s, d = x.shape
    tm = _pick(s, tm)
    kernel = functools.partial(_conv_body, tm=tm, rows_per_chunk=rows_per_chunk)
    vec = lambda n: pl.BlockSpec((1, n), lambda bi, t: (0, 0))
    return pl.pallas_call(
        kernel,
        grid=(b, s // tm),
        in_specs=[pl.BlockSpec((1, tm, d), lambda bi, t: (bi, t, 0)),
                  vec(d),
                  pl.BlockSpec((d, 2 * d), lambda bi, t: (0, 0)),
                  vec(2 * d),
                  pl.BlockSpec((CONV_WIDTH, d), lambda bi, t: (0, 0)),
                  vec(d), vec(d), vec(d),
                  pl.BlockSpec((d, d), lambda bi, t: (0, 0)),
                  vec(d)],
        out_specs=pl.BlockSpec((1, tm, d), lambda bi, t: (bi, t, 0)),
        out_shape=jax.ShapeDtypeStruct((b, s, d), F32),
        scratch_shapes=[pltpu.VMEM((d // LANES, tm + CONV_HIST, LANES), F32), pltpu.VMEM((tm, d), BF16),
                        pltpu.VMEM((CONV_WIDTH, F32_SUBLANES, d), F32), pltpu.VMEM((tm, d), F32)],
        compiler_params=_params(2),
        name="conformer_conv",
    )(x, g.reshape(1, d), w1, b1.reshape(1, 2 * d), wdw, bdw.reshape(1, d), lng.reshape(1, d),
      lnb.reshape(1, d), w2, b2.reshape(1, d))


def _router_body(x_ref, g_ref, wr_ref, h_ref, meta_ref, *, n_experts):
    h = _rms(x_ref[...], g_ref[...])
    _store_slabs(h_ref, 0, h)
    logits = jnp.dot(h.astype(BF16), wr_ref[...], preferred_element_type=F32)
    lane = lax.broadcasted_iota(I32, logits.shape, 1).astype(F32)
    neg = jnp.full_like(logits, -jnp.inf)
    far = jnp.full_like(logits, float(LANES))
    l1 = jnp.where(lane < n_experts, logits, neg)
    m1 = jnp.max(l1, axis=-1, keepdims=True)
    i1 = jnp.min(jnp.where(l1 == m1, lane, far), axis=-1, keepdims=True)
    l2 = jnp.where(lane == i1, neg, l1)
    m2 = jnp.max(l2, axis=-1, keepdims=True)
    i2 = jnp.min(jnp.where(l2 == m2, lane, far), axis=-1, keepdims=True)
    e2 = jnp.exp(m2 - m1)
    w1 = 1.0 / (1.0 + e2)
    w2 = e2 / (1.0 + e2)
    zero = jnp.zeros_like(logits)
    meta_ref[...] = jnp.where(lane == 0, i1,
                    jnp.where(lane == 1, i2,
                    jnp.where(lane == 2, w1, jnp.where(lane == 3, w2, zero))))


def moe_router(x, g, wr, n_experts, tm=512):
    t, d = x.shape
    tm = _pick(t, tm)
    kernel = functools.partial(_router_body, n_experts=n_experts)
    return pl.pallas_call(
        kernel,
        grid=(t // tm,),
        in_specs=[pl.BlockSpec((tm, d), lambda i: (i, 0)),
                  pl.BlockSpec((1, d), lambda i: (0, 0)),
                  pl.BlockSpec((d, LANES), lambda i: (0, 0))],
        out_specs=[pl.BlockSpec((tm * (d // LANES), LANES), lambda i: (i, 0)),
                   pl.BlockSpec((tm, LANES), lambda i: (i, 0))],
        out_shape=[jax.ShapeDtypeStruct((t * (d // LANES), LANES), F32),
                   jax.ShapeDtypeStruct((t, LANES), F32)],
        compiler_params=_params(1),
        name="moe_router",
    )(x, g.reshape(1, d), wr)


GATHER_UNROLL = 8


def _store_slabs(ref, first_row, value):
    n, d = value.shape
    n_slabs = d // LANES
    for c in range(n_slabs):
        ref[pl.ds(first_row * n_slabs + c, n, stride=n_slabs), :] = value[:, c * LANES:(c + 1) * LANES]


def _load_slabs(ref, first_row, n, d):
    n_slabs = d // LANES
    return jnp.concatenate([ref[pl.ds(first_row * n_slabs + c, n, stride=n_slabs), :]
                            for c in range(n_slabs)], axis=1)


def _dispatch_body(pad_ref, tn_ref, d_ref, h_hbm, xs_hbm, zero_ref, buf_ref, in_sem, sem, zsem,
                   *, tm, n_slabs, n_experts, n_sub):
    i = pl.program_id(0)
    n = pl.num_programs(0)
    rows = tm * n_slabs
    n_ring = buf_ref.shape[0]

    def stage(tile):
        return pltpu.make_async_copy(h_hbm.at[pl.ds(pl.multiple_of(tile * rows, rows), rows), :],
                                     buf_ref.at[tile % n_ring], in_sem.at[tile % n_ring])

    def tile_copy(slot):
        return pltpu.make_async_copy(buf_ref.at[0], xs_hbm.at[pl.ds(0, rows), :], sem.at[slot])

    @pl.when(i == 0)
    def _():
        stage(0).start()

    @pl.when(i + 1 < n)
    def _():
        stage(i + 1).start()

    stage(i).wait()
    slot = i % 2
    ring = i % n_ring
    for rank in range(TOP_K):
        def body(it, carry, rank=rank):
            base = it * GATHER_UNROLL
            dsts = [pl.multiple_of(d_ref[0, 0, rank * tm + base + k], n_slabs) for k in range(GATHER_UNROLL)]
            for k in range(GATHER_UNROLL):
                src = pl.multiple_of((base + k) * n_slabs, n_slabs)
                pltpu.make_async_copy(buf_ref.at[ring, pl.ds(src, n_slabs), :],
                                      xs_hbm.at[pl.ds(dsts[k], n_slabs), :], sem.at[slot]).start()
            return carry
        lax.fori_loop(0, tm // GATHER_UNROLL, body, 0)

    @pl.when(i == 0)
    def _():
        zero_ref[...] = jnp.zeros(zero_ref.shape, F32)
        zero_row = zero_ref.at[pl.ds(0, n_slabs), :]
        sub_rows = zero_ref.shape[0]

        def unused_sub_tiles(fn):
            def tbody(tile, carry):
                for s in range(n_sub):
                    @pl.when(s >= tn_ref[tile])
                    def _():
                        dst = pl.multiple_of((tile * n_sub + s) * sub_rows, sub_rows)
                        fn(pltpu.make_async_copy(zero_ref, xs_hbm.at[pl.ds(dst, sub_rows), :], zsem))
                return carry
            lax.fori_loop(0, tn_ref.shape[0], tbody, 0)

        def tail_rows(fn):
            for e in range(n_experts):
                def zbody(r, carry, e=e):
                    dst = pl.multiple_of((pad_ref[0, e] + r) * n_slabs, n_slabs)
                    fn(pltpu.make_async_copy(zero_row, xs_hbm.at[pl.ds(dst, n_slabs), :], zsem))
                    return carry
                lax.fori_loop(0, pad_ref[1, e], zbody, 0)

        unused_sub_tiles(lambda cp: cp.start())
        tail_rows(lambda cp: cp.start())
        unused_sub_tiles(lambda cp: cp.wait())
        tail_rows(lambda cp: cp.wait())

    @pl.when(i > 0)
    def _():
        for rank in range(TOP_K):
            tile_copy(1 - slot).wait()

    @pl.when(i == n - 1)
    def _():
        for rank in range(TOP_K):
            tile_copy(slot).wait()


def moe_dispatch(h, dest, pad, tile_count, sub, n_sub, n_slabs):
    n_tok_tiles = dest.shape[0]
    tm = dest.shape[2] // TOP_K
    n_experts = pad.shape[1]
    n_rows = tile_count.shape[0] * n_sub * sub
    grid_spec = pltpu.PrefetchScalarGridSpec(
        num_scalar_prefetch=2,
        grid=(n_tok_tiles,),
        in_specs=[pl.BlockSpec((1, 1, TOP_K * tm), lambda i, pad, tn: (i, 0, 0), memory_space=pltpu.SMEM),
                  pl.BlockSpec(memory_space=pl.ANY)],
        out_specs=pl.BlockSpec(memory_space=pl.ANY),
        scratch_shapes=[pltpu.VMEM((sub * n_slabs, LANES), F32), pltpu.VMEM((3, tm * n_slabs, LANES), F32),
                        pltpu.SemaphoreType.DMA((3,)), pltpu.SemaphoreType.DMA((2,)),
                        pltpu.SemaphoreType.DMA(())],
    )
    return pl.pallas_call(
        functools.partial(_dispatch_body, tm=tm, n_slabs=n_slabs, n_experts=n_experts, n_sub=n_sub),
        grid_spec=grid_spec,
        out_shape=jax.ShapeDtypeStruct((n_rows * n_slabs, LANES), F32),
        compiler_params=_params(1),
        name="moe_dispatch",
    )(pad, tile_count, dest, h)


def _moe_body(te_ref, tn_ref, xs_ref, wg_ref, wu_ref, wd_ref, y_ref,
              xb_ref, wgb_ref, wub_ref, wdb_ref, *, sub, n_sub):
    i = pl.program_id(0)
    j = pl.program_id(1)
    count = tn_ref[i]
    d = wg_ref.shape[2]

    for s in range(n_sub):
        rows = pl.ds(s * sub, sub)

        @pl.when((j == 0) & (s < count))
        def _():
            xb_ref[rows, :] = _load_slabs(xs_ref, s * sub, sub, d).astype(BF16)

        @pl.when(j == 0)
        def _():
            y_ref[rows, :] = jnp.zeros((sub, d), F32)

        @pl.when(s < count)
        def _():
            if s == 0:
                wg, wu, wd = (r[0, 0].astype(BF16) for r in (wg_ref, wu_ref, wd_ref))
                wgb_ref[...], wub_ref[...], wdb_ref[...] = wg, wu, wd
            else:
                wg, wu, wd = wgb_ref[...], wub_ref[...], wdb_ref[...]
            x = xb_ref[rows, :]
            a = jnp.dot(x, wg, preferred_element_type=F32)
            u = jnp.dot(x, wu, preferred_element_type=F32)
            act = (a * jax.nn.sigmoid(a) * u).astype(BF16)
            y_ref[rows, :] += jnp.dot(act, wd, preferred_element_type=F32)


def moe_experts(xs, tile_expert, tile_count, wg, wu, wd, layer, sub, n_sub, tf=512):
    d = wg.shape[2]
    n_slabs = d // LANES
    n_tiles = tile_expert.shape[0]
    tile_rows = n_sub * sub
    f = wg.shape[3]
    tf = _pick(f, tf)
    nf = f // tf

    def w_col(i, j, te, tn):
        return (layer, te[i], 0, jnp.where(tn[i] > 0, j, nf - 1))

    def w_row(i, j, te, tn):
        return (layer, te[i], jnp.where(tn[i] > 0, j, nf - 1), 0)

    grid_spec = pltpu.PrefetchScalarGridSpec(
        num_scalar_prefetch=2,
        grid=(n_tiles, nf),
        in_specs=[pl.BlockSpec((tile_rows * n_slabs, LANES), lambda i, j, te, tn: (i, 0)),
                  pl.BlockSpec((1, 1, d, tf), w_col),
                  pl.BlockSpec((1, 1, d, tf), w_col),
                  pl.BlockSpec((1, 1, tf, d), w_row)],
        out_specs=pl.BlockSpec((tile_rows, d), lambda i, j, te, tn: (i, 0)),
        scratch_shapes=[pltpu.VMEM((tile_rows, d), BF16),
                        pltpu.VMEM((d, tf), BF16), pltpu.VMEM((d, tf), BF16), pltpu.VMEM((tf, d), BF16)],
    )
    return pl.pallas_call(
        functools.partial(_moe_body, sub=sub, n_sub=n_sub),
        grid_spec=grid_spec,
        out_shape=jax.ShapeDtypeStruct((n_tiles * tile_rows, d), F32),
        compiler_params=_params(2),
        name="moe_experts",
    )(tile_expert, tile_count, xs, wg, wu, wd)


def _combine_body(d_cur_ref, d_nxt_ref, x_ref, meta_ref, g_ref, y_hbm, o_ref, buf_ref, sem,
                  *, tm, final_norm):
    i = pl.program_id(0)
    n = pl.num_programs(0)

    def gather(d_ref, slot):
        def body(it, carry):
            for k in range(GATHER_UNROLL):
                r = it * GATHER_UNROLL + k
                pltpu.make_async_copy(y_hbm.at[pl.ds(d_ref[0, 0, r], 1), :],
                                      buf_ref.at[slot, pl.ds(r, 1), :], sem.at[slot]).start()
            return carry
        lax.fori_loop(0, TOP_K * tm // GATHER_UNROLL, body, 0)

    @pl.when(i == 0)
    def _():
        gather(d_cur_ref, 0)

    @pl.when(i + 1 < n)
    def _():
        gather(d_nxt_ref, (i + 1) % 2)

    slot = i % 2
    pltpu.make_async_copy(y_hbm.at[pl.ds(0, TOP_K * tm), :], buf_ref.at[slot], sem.at[slot]).wait()
    meta = meta_ref[...]
    out = (x_ref[...] + meta[:, 2:3] * buf_ref[slot, 0:tm, :]
           + meta[:, 3:4] * buf_ref[slot, tm:2 * tm, :])
    if final_norm:
        out = _rms(out, g_ref[...])
    o_ref[...] = out


def moe_combine(x, meta, dest, y, g, final_norm, tm=256):
    t, d = x.shape
    n = t // tm
    smem = lambda fn: pl.BlockSpec((1, 1, TOP_K * tm), fn, memory_space=pltpu.SMEM)
    return pl.pallas_call(
        functools.partial(_combine_body, tm=tm, final_norm=final_norm),
        grid=(n,),
        in_specs=[smem(lambda i: (i, 0, 0)),
                  smem(lambda i: (jnp.minimum(i + 1, n - 1), 0, 0)),
                  pl.BlockSpec((tm, d), lambda i: (i, 0)),
                  pl.BlockSpec((tm, LANES), lambda i: (i, 0)),
                  pl.BlockSpec((1, d), lambda i: (0, 0)),
                  pl.BlockSpec(memory_space=pl.ANY)],
        out_specs=pl.BlockSpec((tm, d), lambda i: (i, 0)),
        out_shape=jax.ShapeDtypeStruct((t, d), F32),
        scratch_shapes=[pltpu.VMEM((2, TOP_K * tm, d), F32), pltpu.SemaphoreType.DMA((2,))],
        compiler_params=_params(1),
        name="moe_combine",
    )(dest, dest, x, meta, g.reshape(1, d), y)


def _route(meta, n_experts, sub, n_sub, tm_combine):
    t = meta.shape[0]
    tile_rows = sub * n_sub
    flat_e = meta[:, :TOP_K].astype(I32).reshape(-1)
    onehot = (flat_e[:, None] == jnp.arange(n_experts, dtype=I32)[None, :]).astype(I32)
    csum = jnp.cumsum(onehot, axis=0)
    rank = jnp.sum((csum - onehot) * onehot, axis=1)
    counts = csum[-1]
    subs = (counts + sub - 1) // sub
    tiles = (subs + n_sub - 1) // n_sub
    tile_ends = jnp.cumsum(tiles)
    tile_starts = tile_ends - tiles
    small = subs // jnp.maximum(tiles, 1)
    extra = subs - small * tiles
    max_tiles = (t // sub + n_sub - 1) // n_sub + 1
    k = jnp.arange(1, max_tiles, dtype=I32)[None, :]
    unreachable = 1 << 20
    bounds = jnp.where(k < tiles[:, None], k * small[:, None] + jnp.minimum(k, extra[:, None]), unreachable)
    widths = bounds - jnp.concatenate([jnp.zeros((n_experts, 1), I32), bounds[:, :-1]], axis=1)

    def place(sel, e, r):
        pick = lambda table: jnp.dot(sel, table.astype(F32), precision=lax.Precision.HIGHEST).astype(I32)
        q = r // sub
        hit = q[:, None] >= pick(bounds)
        tile_local = jnp.sum(hit, axis=1)
        sub_start = jnp.sum(jnp.where(hit, pick(widths), 0), axis=1)
        return (tile_starts[e] + tile_local) * tile_rows + (q - sub_start) * sub + r % sub

    dest = place(onehot.astype(F32), flat_e, rank)
    n_tiles = ((t * TOP_K) // sub + n_experts + n_sub - 1) // n_sub + n_experts
    idx = jnp.arange(n_tiles, dtype=I32)
    tile_expert = jnp.minimum(jnp.searchsorted(tile_ends, idx, side="right"), n_experts - 1).astype(I32)
    local = idx - tile_starts[tile_expert]
    tile_count = jnp.where(local < tiles[tile_expert],
                           small[tile_expert] + (local < extra[tile_expert]), 0).astype(I32)
    dest_tiles = dest.reshape(t // tm_combine, tm_combine, TOP_K).transpose(0, 2, 1).reshape(
        t // tm_combine, 1, TOP_K * tm_combine)
    experts = jnp.arange(n_experts, dtype=I32)
    last_row = place(jnp.eye(n_experts, dtype=F32), experts, jnp.maximum(counts - 1, 0))
    pad = jnp.stack([last_row + 1, subs * sub - counts]).astype(I32)
    return tile_expert, tile_count, dest_tiles, pad, n_tiles


def moe_layer(x, g, wr, wg, wu, wd, layer, final_g, final_norm, sub=512, n_sub=4, tm_combine=256):
    t, d = x.shape
    n_experts = wr.shape[1]
    tm_combine = min(tm_combine, t)
    wr_pad = jnp.zeros((d, LANES), BF16).at[:, :n_experts].set(wr.astype(BF16))
    h, meta = moe_router(x, g, wr_pad, n_experts)
    n_slabs = d // LANES
    tile_expert, tile_count, dest_tiles, pad, n_tiles = _route(meta, n_experts, sub, n_sub, tm_combine)
    xs = moe_dispatch(h, dest_tiles * n_slabs, pad, tile_count, sub, n_sub, n_slabs)
    y = moe_experts(xs, tile_expert, tile_count, wg, wu, wd, layer, sub, n_sub)
    return moe_combine(x, meta, dest_tiles, y, final_g, final_norm, tm_combine)


def attention_mixers(x, norm_g, w_in, lam_params, subln_g, sinks, layer):
    b, s, d = x.shape
    diff_heads = d // (4 * HEAD_DIM)
    swa_q_heads = d // (2 * HEAD_DIM)
    swa_kv_heads = max(1, swa_q_heads // 4)
    group = swa_q_heads // swa_kv_heads
    a_width = diff_heads * 2 * HEAD_DIM
    lam_init = 0.8 - 0.6 * math.exp(-0.3 * layer)
    x2 = x.reshape(b * s, d)
    proj = rms_matmul(x2, norm_g, w_in).reshape(b, s, -1)
    oa = diff_attention(proj, lam_params, subln_g, diff_heads, lam_init)
    q_col = 3 * a_width
    k_col = q_col + swa_q_heads * HEAD_DIM
    v_col = k_col + swa_kv_heads * HEAD_DIM
    ob = sliding_window_attention(proj, sinks, q_col, k_col, v_col, swa_kv_heads, group)
    return oa.reshape(b * s, -1), ob.reshape(b * s, -1)


def attention_layer(x, norm_g, w_in, lam_params, subln_g, sinks, w_out, layer):
    b, s, d = x.shape
    oa, ob = attention_mixers(x, norm_g, w_in, lam_params, subln_g, sinks, layer)
    return out_projection(x.reshape(b * s, d), oa, ob, w_out)


def kernel(x, attn_norm_g, w_in_att, diff_lambda, diff_subln_g, attn_sinks, w_out_att, ffn_norm_g, w_ffn_gate, w_ffn_up, w_ffn_down, conv_norm_g, w_pw1, b_pw1, w_dw, b_dw, conv_ln_g, conv_ln_b, w_pw2, b_pw2, moe_norm_g, w_router, w_exp_gate, w_exp_up, w_exp_down, final_norm_g):
    b, s, d = x.shape
    depth = attn_norm_g.shape[0] + conv_norm_g.shape[0]
    assert depth % 2 == 0, "the final RMSNorm is fused into the last expert layer"
    bf = lambda w: w.astype(BF16)
    for layer in range(depth):
        i = layer // 2
        if layer % 2 == 0:
            oa, ob = attention_mixers(x, attn_norm_g[i], bf(w_in_att[i]), diff_lambda[i], diff_subln_g[i],
                                      attn_sinks[i], layer)
            x2 = mix_ffn(x.reshape(b * s, d), oa, ob, bf(w_out_att[i]), ffn_norm_g[i],
                         bf(w_ffn_gate[i]), bf(w_ffn_up[i]), bf(w_ffn_down[i]))
            x = x2.reshape(b, s, d)
        else:
            x = conformer_conv(x, conv_norm_g[i], bf(w_pw1[i]), b_pw1[i], w_dw[i], b_dw[i],
                               conv_ln_g[i], conv_ln_b[i], bf(w_pw2[i]), b_pw2[i])
            x2 = moe_layer(x.reshape(b * s, d), moe_norm_g[i], w_router[i], w_exp_gate,
                           w_exp_up, w_exp_down, i, final_norm_g,
                           final_norm=(layer == depth - 1))
            x = x2.reshape(b, s, d)
    return x
```

```python
import functools
import math

import jax
import jax.numpy as jnp
from jax import lax
from jax.experimental import pallas as pl
from jax.experimental.pallas import tpu as pltpu

BF16 = jnp.bfloat16
F32 = jnp.float32
I32 = jnp.int32

LOG2_E = 1.4426950408889634
RMS_EPS = 1e-6
LN_EPS = 1e-5
HEAD_DIM = 64
ATTN_BLOCK = 128
CONV_WIDTH = 31
TOP_K = 2
LANES = 128
F32_SUBLANES = 8
CONV_HIST = 32
V7X_VMEM_LIMIT = 56 * 1024 * 1024


def _params(n_axes, flags=None):
    return pltpu.CompilerParams(dimension_semantics=("arbitrary",) * n_axes,
                                vmem_limit_bytes=V7X_VMEM_LIMIT, flags=flags)


def _rms(x, g):
    return x * lax.rsqrt(jnp.mean(x * x, axis=-1, keepdims=True) + RMS_EPS) * g


def _pick(n, pref):
    t = min(n, pref)
    while n % t:
        t -= LANES if t > LANES else 8
    return t


def _rms_matmul_body(x_ref, g_ref, w_ref, o_ref):
    h = _rms(x_ref[...], g_ref[...]).astype(BF16)
    o_ref[...] = jnp.dot(h, w_ref[...], preferred_element_type=F32).astype(o_ref.dtype)


def rms_matmul(x, g, w, tm=512):
    t, d = x.shape
    n = w.shape[1]
    tm = _pick(t, tm)
    return pl.pallas_call(
        _rms_matmul_body,
        grid=(t // tm,),
        in_specs=[pl.BlockSpec((tm, d), lambda i: (i, 0)),
                  pl.BlockSpec((1, d), lambda i: (0, 0)),
                  pl.BlockSpec((d, n), lambda i: (0, 0))],
        out_specs=pl.BlockSpec((tm, n), lambda i: (i, 0)),
        out_shape=jax.ShapeDtypeStruct((t, n), BF16),
        compiler_params=_params(1),
        name="rms_inproj",
    )(x, g.reshape(1, d), w)


def _diff_attn_body(q_ref, k_ref, v_ref, lam_ref, g_ref, o_ref, q_ref2, m_ref, l_ref, acc_ref,
                    *, blk, lam_init):
    qi = pl.program_id(2)
    hw = 2 * HEAD_DIM
    half = blk // 2
    lane = lax.broadcasted_iota(I32, (half, hw), 1)
    q = (q_ref[0].astype(F32) * (HEAD_DIM ** -0.5 * LOG2_E)).astype(BF16)
    zero = jnp.zeros((half, hw), BF16)
    q_ref2[...] = jnp.concatenate(
        [jnp.where(keep, q[r:r + half], zero)
         for r in (0, half) for keep in (lane < HEAD_DIM, lane >= HEAD_DIM)], axis=0)

    m_ref[...] = jnp.full(m_ref.shape, -jnp.inf, F32)
    l_ref[...] = jnp.zeros(l_ref.shape, F32)
    acc_ref[...] = jnp.zeros(acc_ref.shape, F32)

    def block(key_start, n_keys, row0, n_rows, masked):
        rows = pl.ds(row0, n_rows)
        k = k_ref[0, pl.ds(key_start, n_keys), :]
        v = v_ref[0, pl.ds(key_start, n_keys), :]
        s = lax.dot_general(q_ref2[rows, :], k, (((1,), (1,)), ((), ())), preferred_element_type=F32)
        if masked:
            row = lax.broadcasted_iota(I32, (n_rows, n_keys), 0) & (half - 1)
            col = lax.broadcasted_iota(I32, (n_rows, n_keys), 1)
            s = jnp.where(col <= row, s, -jnp.inf)
        tiles = [s[:, c * LANES:(c + 1) * LANES] for c in range(n_keys // LANES)]
        m_prev = m_ref[rows, :]
        m_new = jnp.maximum(m_prev, jnp.max(functools.reduce(jnp.maximum, tiles), axis=-1, keepdims=True))
        alpha = jnp.exp2(m_prev - m_new)
        p_tiles = [jnp.exp2(t - m_new) for t in tiles]
        l_ref[rows, :] = alpha * l_ref[rows, :] + functools.reduce(jnp.add, p_tiles)
        p = jnp.concatenate(p_tiles, axis=1).astype(BF16)
        acc_ref[rows, :] = alpha * acc_ref[rows, :] + jnp.dot(p, v, preferred_element_type=F32)
        m_ref[rows, :] = m_new

    def full_block(ki):
        block(pl.multiple_of(ki * blk, blk), blk, 0, 2 * blk, False)

    def pair(j, carry):
        full_block(2 * j)
        full_block(2 * j + 1)
        return carry

    lax.fori_loop(0, qi // 2, pair, 0)

    @pl.when(qi % 2 == 1)
    def _():
        full_block(qi - 1)

    diag = pl.multiple_of(qi * blk, blk)
    block(diag, half, 0, 2 * half, True)
    block(diag, half, 2 * half, 2 * half, False)
    block(diag + half, half, 2 * half, 2 * half, True)

    lp = lam_ref[...]
    lam = (jnp.exp(jnp.sum(lp[0:1] * lp[1:2], axis=-1, keepdims=True))
           - jnp.exp(jnp.sum(lp[2:3] * lp[3:4], axis=-1, keepdims=True)) + lam_init)
    o = acc_ref[...] / jnp.sum(l_ref[...], axis=-1, keepdims=True)
    o = (jnp.concatenate([o[0:half], o[2 * half:3 * half]], axis=0)
         - lam * jnp.concatenate([o[half:2 * half], o[3 * half:4 * half]], axis=0))
    o = _rms(o, g_ref[...]) * (1.0 - lam_init)
    o_ref[0] = o.astype(o_ref.dtype)


def diff_attention(proj, lam_params, subln_g, n_heads, lam_init, blk=1024):
    b, s, _ = proj.shape
    blk = _pick(s, blk)
    assert blk & (blk - 1) == 0, "the causal mask uses power-of-two half blocks"
    hw = 2 * HEAD_DIM
    kernel = functools.partial(_diff_attn_body, blk=blk, lam_init=lam_init)
    return pl.pallas_call(
        kernel,
        grid=(b, n_heads, s // blk),
        in_specs=[pl.BlockSpec((1, blk, hw), lambda bi, h, qi: (bi, qi, h)),
                  pl.BlockSpec((1, s, hw), lambda bi, h, qi: (bi, 0, n_heads + h)),
                  pl.BlockSpec((1, s, hw), lambda bi, h, qi: (bi, 0, 2 * n_heads + h)),
                  pl.BlockSpec((4, HEAD_DIM), lambda bi, h, qi: (0, 0)),
                  pl.BlockSpec((1, hw), lambda bi, h, qi: (0, 0))],
        out_specs=pl.BlockSpec((1, blk, hw), lambda bi, h, qi: (bi, qi, h)),
        out_shape=jax.ShapeDtypeStruct((b, s, n_heads * hw), BF16),
        scratch_shapes=[pltpu.VMEM((2 * blk, hw), BF16), pltpu.VMEM((2 * blk, LANES), F32),
                        pltpu.VMEM((2 * blk, LANES), F32), pltpu.VMEM((2 * blk, hw), F32)],
        compiler_params=_params(3),
        name="diff_attention",
    )(proj, proj, proj, lam_params, subln_g.reshape(1, hw))


def _swa_body(sink_ref, q_ref, kc_ref, kp_ref, vc_ref, vp_ref, o_ref, *, tq, kv_heads, group):
    t = pl.program_id(1)
    w = ATTN_BLOCK
    row = lax.broadcasted_iota(I32, (group * w, 2 * w), 0) & (w - 1)
    col = lax.broadcasted_iota(I32, (group * w, 2 * w), 1)
    band = (col > row) & (col <= row + w)
    band_first = band & ((col >= w) | (t > 0))
    low_half = lax.broadcasted_iota(I32, (w, LANES), 1) < HEAD_DIM
    for j in range(tq // w):
        rows = slice(j * w, (j + 1) * w)
        if j == 0:
            k2 = jnp.concatenate([kp_ref[0], kc_ref[0, rows, :]], axis=0)
            v2 = jnp.concatenate([vp_ref[0], vc_ref[0, rows, :]], axis=0)
            mask = band_first
        else:
            k2 = kc_ref[0, (j - 1) * w:(j + 1) * w, :]
            v2 = vc_ref[0, (j - 1) * w:(j + 1) * w, :]
            mask = band
        outs = []
        for g in range(kv_heads):
            kg = k2[:, g * HEAD_DIM:(g + 1) * HEAD_DIM]
            vg = v2[:, g * HEAD_DIM:(g + 1) * HEAD_DIM]
            kdup = jnp.concatenate([kg, kg], axis=1)
            vdup = jnp.concatenate([vg, vg], axis=1)
            heads = [g * group + i for i in range(group)]
            q_tiles = []
            for h in heads:
                q = q_ref[0, rows, (h // 2) * LANES:(h // 2 + 1) * LANES]
                q = (q.astype(F32) * (HEAD_DIM ** -0.5)).astype(BF16)
                own_half = low_half if h % 2 == 0 else jnp.logical_not(low_half)
                q_tiles.append(jnp.where(own_half, q, jnp.zeros_like(q)))
            sink = jnp.concatenate([jnp.full((w, LANES), sink_ref[h], F32) for h in heads], axis=0)
            s = lax.dot_general(jnp.concatenate(q_tiles, axis=0), kdup, (((1,), (1,)), ((), ())),
                                preferred_element_type=F32)
            s = jnp.where(mask, s, -jnp.inf)
            s0, s1 = s[:, :w], s[:, w:]
            m = jnp.maximum(jnp.max(jnp.maximum(s0, s1), axis=-1, keepdims=True), sink)
            e0 = jnp.exp(s0 - m)
            e1 = jnp.exp(s1 - m)
            denom = jnp.sum(e0 + e1, axis=-1, keepdims=True) + jnp.exp(sink - m)
            e = jnp.concatenate([e0, e1], axis=1).astype(BF16)
            o = jnp.dot(e, vdup, preferred_element_type=F32) / denom
            outs.extend(o[i * w:(i + 1) * w, :] for i in range(group))
        tiles = [jnp.where(low_half, outs[h], outs[h + 1]) for h in range(0, len(outs), 2)]
        o_ref[0, rows, :] = jnp.concatenate(tiles, axis=1).astype(o_ref.dtype)


def sliding_window_attention(proj, sinks, q_col, k_col, v_col, kv_heads, group, tq=512):
    b, s, _ = proj.shape
    tq = _pick(s, tq)
    qw = kv_heads * group * HEAD_DIM
    kw = kv_heads * HEAD_DIM
    sub = tq // ATTN_BLOCK
    kernel = functools.partial(_swa_body, tq=tq, kv_heads=kv_heads, group=group)
    prev = lambda bi, t: (bi, jnp.maximum(t * sub - 1, 0), 0)
    return pl.pallas_call(
        kernel,
        grid=(b, s // tq),
        in_specs=[pl.BlockSpec(memory_space=pltpu.SMEM),
                  pl.BlockSpec((1, tq, qw), lambda bi, t: (bi, t, q_col // qw)),
                  pl.BlockSpec((1, tq, kw), lambda bi, t: (bi, t, k_col // kw)),
                  pl.BlockSpec((1, ATTN_BLOCK, kw), lambda bi, t: prev(bi, t)[:2] + (k_col // kw,)),
                  pl.BlockSpec((1, tq, kw), lambda bi, t: (bi, t, v_col // kw)),
                  pl.BlockSpec((1, ATTN_BLOCK, kw), lambda bi, t: prev(bi, t)[:2] + (v_col // kw,))],
        out_specs=pl.BlockSpec((1, tq, qw), lambda bi, t: (bi, t, 0)),
        out_shape=jax.ShapeDtypeStruct((b, s, qw), BF16),
        compiler_params=_params(2),
        name="swa_attention",
    )(sinks, proj, proj, proj, proj, proj)


def _outproj_body(x_ref, a_ref, b_ref, w_ref, o_ref):
    mix = jnp.concatenate([a_ref[...], b_ref[...]], axis=1)
    o_ref[...] = x_ref[...] + jnp.dot(mix, w_ref[...], preferred_element_type=F32)


def out_projection(x, oa, ob, w, tm=512):
    t, d = x.shape
    tm = _pick(t, tm)
    return pl.pallas_call(
        _outproj_body,
        grid=(t // tm,),
        in_specs=[pl.BlockSpec((tm, d), lambda i: (i, 0)),
                  pl.BlockSpec((tm, oa.shape[1]), lambda i: (i, 0)),
                  pl.BlockSpec((tm, ob.shape[1]), lambda i: (i, 0)),
                  pl.BlockSpec(w.shape, lambda i: (0, 0))],
        out_specs=pl.BlockSpec((tm, d), lambda i: (i, 0)),
        out_shape=jax.ShapeDtypeStruct((t, d), F32),
        compiler_params=_params(1),
        name="out_projection",
    )(x, oa, ob, w)


def _ffn_body(x_ref, g_ref, wg_ref, wu_ref, wd_ref, o_ref, h_ref, acc_ref):
    j = pl.program_id(1)

    @pl.when(j == 0)
    def _():
        h_ref[...] = _rms(x_ref[...], g_ref[...]).astype(BF16)
        acc_ref[...] = jnp.zeros(acc_ref.shape, F32)

    h = h_ref[...]
    a = jnp.dot(h, wg_ref[...], preferred_element_type=F32)
    u = jnp.dot(h, wu_ref[...], preferred_element_type=F32)
    act = (a * jax.nn.sigmoid(a) * u).astype(BF16)
    acc_ref[...] += jnp.dot(act, wd_ref[...], preferred_element_type=F32)

    @pl.when(j == pl.num_programs(1) - 1)
    def _():
        o_ref[...] = x_ref[...] + acc_ref[...]


def ffn_swiglu(x, g, wg, wu, wd, tm=512, tf=1408):
    t, d = x.shape
    f = wg.shape[1]
    tm = _pick(t, tm)
    tf = _pick(f, tf)
    return pl.pallas_call(
        _ffn_body,
        grid=(t // tm, f // tf),
        in_specs=[pl.BlockSpec((tm, d), lambda i, j: (i, 0)),
                  pl.BlockSpec((1, d), lambda i, j: (0, 0)),
                  pl.BlockSpec((d, tf), lambda i, j: (0, j)),
                  pl.BlockSpec((d, tf), lambda i, j: (0, j)),
                  pl.BlockSpec((tf, d), lambda i, j: (j, 0))],
        out_specs=pl.BlockSpec((tm, d), lambda i, j: (i, 0)),
        out_shape=jax.ShapeDtypeStruct((t, d), F32),
        scratch_shapes=[pltpu.VMEM((tm, d), BF16), pltpu.VMEM((tm, d), F32)],
        compiler_params=_params(2),
        name="ffn_swiglu",
    )(x, g.reshape(1, d), wg, wu, wd)


def _mix_ffn_body(x_ref, a_ref, b_ref, wo_ref, g_ref, wg_ref, wu_ref, wd_ref, o_ref, *, ff_chunk):
    mix = jnp.concatenate([a_ref[...], b_ref[...]], axis=1)
    x1 = x_ref[...] + jnp.dot(mix, wo_ref[...], preferred_element_type=F32)
    h = _rms(x1, g_ref[...]).astype(BF16)
    y = x1
    for c in range(wg_ref.shape[1] // ff_chunk):
        cols = slice(c * ff_chunk, (c + 1) * ff_chunk)
        a = jnp.dot(h, wg_ref[:, cols], preferred_element_type=F32)
        u = jnp.dot(h, wu_ref[:, cols], preferred_element_type=F32)
        act = (a * jax.nn.sigmoid(a) * u).astype(BF16)
        y = y + jnp.dot(act, wd_ref[cols, :], preferred_element_type=F32)
    o_ref[...] = y


def mix_ffn(x, oa, ob, wo, g, wg, wu, wd, tm=512, ff_chunk=1408):
    t, d = x.shape
    f = wg.shape[1]
    tm = _pick(t, tm)
    ff_chunk = _pick(f, ff_chunk)
    resident = lambda shape: pl.BlockSpec(shape, lambda i: (0, 0), pipeline_mode=pl.Buffered(1))
    return pl.pallas_call(
        functools.partial(_mix_ffn_body, ff_chunk=ff_chunk),
        grid=(t // tm,),
        in_specs=[pl.BlockSpec((tm, d), lambda i: (i, 0)),
                  pl.BlockSpec((tm, oa.shape[1]), lambda i: (i, 0)),
                  pl.BlockSpec((tm, ob.shape[1]), lambda i: (i, 0)),
                  resident(wo.shape), resident((1, d)),
                  resident(wg.shape), resident(wu.shape), resident(wd.shape)],
        out_specs=pl.BlockSpec((tm, d), lambda i: (i, 0)),
        out_shape=jax.ShapeDtypeStruct((t, d), F32),
        compiler_params=_params(1),
        name="mix_ffn",
    )(x, oa, ob, wo, g.reshape(1, d), wg, wu, wd)


def _conv_body(x_ref, g_ref, w1_ref, b1_ref, wdw_ref, bdw_ref, lng_ref, lnb_ref, w2_ref, b2_ref,
               o_ref, u_ref, v_ref, wb_ref, c_ref, *, tm, rows_per_chunk):
    d = x_ref.shape[-1]
    t = pl.program_id(1)
    sublanes = wb_ref.shape[1]

    @pl.when(t == 0)
    def _():
        u_ref[:, 0:CONV_HIST, :] = jnp.zeros((d // LANES, CONV_HIST, LANES), F32)
        wb_ref[...] = jnp.broadcast_to(wdw_ref[...][:, None, :], wb_ref.shape)

    h = _rms(x_ref[0], g_ref[...]).astype(BF16)
    z = jnp.dot(h, w1_ref[...], preferred_element_type=F32) + b1_ref[...]
    u = z[:, :d] * jax.nn.sigmoid(z[:, d:])
    n_slabs = d // LANES
    for c in range(n_slabs):
        u_ref[c, CONV_HIST:CONV_HIST + tm, :] = u[:, c * LANES:(c + 1) * LANES]

    first_tap = CONV_HIST - (CONV_WIDTH - 1)
    conv_rows = 4 * sublanes

    for c in range(n_slabs):
        lanes = slice(c * LANES, (c + 1) * LANES)
        w = [wb_ref[j, :, lanes] for j in range(CONV_WIDTH)]
        bias = jnp.zeros((sublanes, LANES), F32) + bdw_ref[:, lanes]

        def rows_block(i, carry, c=c, lanes=lanes, w=w, bias=bias):
            r0 = pl.multiple_of(i * conv_rows, conv_rows)
            for r in range(conv_rows // sublanes):
                sums = [bias, None]
                for j in range(CONV_WIDTH):
                    tap = u_ref[c, pl.ds(r0 + (first_tap + j + r * sublanes), sublanes, stride=1), :] * w[j]
                    sums[j % 2] = tap if sums[j % 2] is None else sums[j % 2] + tap
                c_ref[pl.ds(r0 + r * sublanes, sublanes), lanes] = sums[0] + sums[1]
            return carry

        lax.fori_loop(0, tm // conv_rows, rows_block, 0)

    def chunk(i, carry):
        r0 = pl.multiple_of(i * rows_per_chunk, rows_per_chunk)
        acc = c_ref[pl.ds(r0, rows_per_chunk), :]
        mu = jnp.mean(acc, axis=-1, keepdims=True)
        xc = acc - mu
        var = jnp.mean(xc * xc, axis=-1, keepdims=True)
        y = xc * lax.rsqrt(var + LN_EPS) * lng_ref[...] + lnb_ref[...]
        v_ref[pl.ds(r0, rows_per_chunk), :] = (y * jax.nn.sigmoid(y)).astype(BF16)
        return carry

    lax.fori_loop(0, tm // rows_per_chunk, chunk, 0)
    u_ref[:, 0:CONV_HIST, :] = u_ref[:, tm:tm + CONV_HIST, :]
    o_ref[0] = x_ref[0] + jnp.dot(v_ref[...], w2_ref[...], preferred_element_type=F32) + b2_ref[...]


def conformer_conv(x, g, w1, b1, wdw, bdw, lng, lnb, w2, b2, tm=512, rows_per_chunk=128):
    b, s, d = x.shape
    tm = _pick(s, tm)
    kernel = functools.partial(_conv_body, tm=tm, rows_per_chunk=rows_per_chunk)
    vec = lambda n: pl.BlockSpec((1, n), lambda bi, t: (0, 0))
    return pl.pallas_call(
        kernel,
        grid=(b, s // tm),
        in_specs=[pl.BlockSpec((1, tm, d), lambda bi, t: (bi, t, 0)),
                  vec(d),
                  pl.BlockSpec((d, 2 * d), lambda bi, t: (0, 0)),
                  vec(2 * d),
                  pl.BlockSpec((CONV_WIDTH, d), lambda bi, t: (0, 0)),
                  vec(d), vec(d), vec(d),
                  pl.BlockSpec((d, d), lambda bi, t: (0, 0)),
                  vec(d)],
        out_specs=pl.BlockSpec((1, tm, d), lambda bi, t: (bi, t, 0)),
        out_shape=jax.ShapeDtypeStruct((b, s, d), F32),
        scratch_shapes=[pltpu.VMEM((d // LANES, tm + CONV_HIST, LANES), F32), pltpu.VMEM((tm, d), BF16),
                        pltpu.VMEM((CONV_WIDTH, F32_SUBLANES, d), F32), pltpu.VMEM((tm, d), F32)],
        compiler_params=_params(2),
        name="conformer_conv",
    )(x, g.reshape(1, d), w1, b1.reshape(1, 2 * d), wdw, bdw.reshape(1, d), lng.reshape(1, d),
      lnb.reshape(1, d), w2, b2.reshape(1, d))


def _router_body(x_ref, g_ref, wr_ref, h_ref, meta_ref, counts_ref, *, n_experts):
    @pl.when(pl.program_id(0) == 0)
    def _():
        counts_ref[...] = jnp.zeros(counts_ref.shape, F32)

    h = _rms(x_ref[...], g_ref[...])
    _store_slabs(h_ref, 0, h)
    logits = jnp.dot(h.astype(BF16), wr_ref[...], preferred_element_type=F32)
    lane = lax.broadcasted_iota(I32, logits.shape, 1).astype(F32)
    neg = jnp.full_like(logits, -jnp.inf)
    far = jnp.full_like(logits, float(LANES))
    l1 = jnp.where(lane < n_experts, logits, neg)
    m1 = jnp.max(l1, axis=-1, keepdims=True)
    i1 = jnp.min(jnp.where(l1 == m1, lane, far), axis=-1, keepdims=True)
    l2 = jnp.where(lane == i1, neg, l1)
    m2 = jnp.max(l2, axis=-1, keepdims=True)
    i2 = jnp.min(jnp.where(l2 == m2, lane, far), axis=-1, keepdims=True)
    e2 = jnp.exp(m2 - m1)
    w1 = 1.0 / (1.0 + e2)
    w2 = e2 / (1.0 + e2)
    tm = logits.shape[0]
    hot1 = (lane == i1).astype(F32)
    hot2 = (lane == i2).astype(F32)
    both = (hot1 + hot2).astype(BF16)
    earlier = (lax.broadcasted_iota(I32, (tm, tm), 1) < lax.broadcasted_iota(I32, (tm, tm), 0)).astype(BF16)
    before = jnp.dot(earlier, both, preferred_element_type=F32) + counts_ref[...]
    r1 = jnp.sum(hot1 * before, axis=-1, keepdims=True)
    r2 = jnp.sum(hot2 * before, axis=-1, keepdims=True)
    counts_ref[...] += jnp.sum(hot1 + hot2, axis=0, keepdims=True)
    zero = jnp.zeros_like(logits)
    meta_ref[...] = jnp.where(lane == 0, i1,
                    jnp.where(lane == 1, i2,
                    jnp.where(lane == 2, w1,
                    jnp.where(lane == 3, w2,
                    jnp.where(lane == 4, r1, jnp.where(lane == 5, r2, zero))))))


def moe_router(x, g, wr, n_experts, tm=512):
    t, d = x.shape
    tm = _pick(t, tm)
    kernel = functools.partial(_router_body, n_experts=n_experts)
    return pl.pallas_call(
        kernel,
        grid=(t // tm,),
        in_specs=[pl.BlockSpec((tm, d), lambda i: (i, 0)),
                  pl.BlockSpec((1, d), lambda i: (0, 0)),
                  pl.BlockSpec((d, LANES), lambda i: (0, 0))],
        out_specs=[pl.BlockSpec((tm * (d // LANES), LANES), lambda i: (i, 0)),
                   pl.BlockSpec((tm, LANES), lambda i: (i, 0)),
                   pl.BlockSpec((1, LANES), lambda i: (0, 0))],
        out_shape=[jax.ShapeDtypeStruct((t * (d // LANES), LANES), F32),
                   jax.ShapeDtypeStruct((t, LANES), F32),
                   jax.ShapeDtypeStruct((1, LANES), F32)],
        compiler_params=_params(1),
        name="moe_router",
    )(x, g.reshape(1, d), wr)


GATHER_UNROLL = 8


def _store_slabs(ref, first_row, value):
    n, d = value.shape
    n_slabs = d // LANES
    for c in range(n_slabs):
        ref[pl.ds(first_row * n_slabs + c, n, stride=n_slabs), :] = value[:, c * LANES:(c + 1) * LANES]


def _load_slabs(ref, first_row, n, d):
    n_slabs = d // LANES
    return jnp.concatenate([ref[pl.ds(first_row * n_slabs + c, n, stride=n_slabs), :]
                            for c in range(n_slabs)], axis=1)


def _dispatch_body(pad_ref, tn_ref, d_ref, h_hbm, xs_hbm, zero_ref, buf_ref, in_sem, sem, zsem,
                   *, tm, n_slabs, n_experts, n_sub):
    i = pl.program_id(0)
    n = pl.num_programs(0)
    rows = tm * n_slabs
    n_ring = buf_ref.shape[0]

    def stage(tile):
        return pltpu.make_async_copy(h_hbm.at[pl.ds(pl.multiple_of(tile * rows, rows), rows), :],
                                     buf_ref.at[tile % n_ring], in_sem.at[tile % n_ring])

    def tile_copy(slot):
        return pltpu.make_async_copy(buf_ref.at[0], xs_hbm.at[pl.ds(0, rows), :], sem.at[slot])

    @pl.when(i == 0)
    def _():
        stage(0).start()

    @pl.when(i + 1 < n)
    def _():
        stage(i + 1).start()

    stage(i).wait()
    slot = i % 2
    ring = i % n_ring
    for rank in range(TOP_K):
        def body(it, carry, rank=rank):
            base = it * GATHER_UNROLL
            dsts = [pl.multiple_of(d_ref[0, 0, rank * tm + base + k], n_slabs) for k in range(GATHER_UNROLL)]
            for k in range(GATHER_UNROLL):
                src = pl.multiple_of((base + k) * n_slabs, n_slabs)
                pltpu.make_async_copy(buf_ref.at[ring, pl.ds(src, n_slabs), :],
                                      xs_hbm.at[pl.ds(dsts[k], n_slabs), :], sem.at[slot]).start()
            return carry
        lax.fori_loop(0, tm // GATHER_UNROLL, body, 0)

    @pl.when(i == 0)
    def _():
        zero_ref[...] = jnp.zeros(zero_ref.shape, F32)
        zero_row = zero_ref.at[pl.ds(0, n_slabs), :]
        sub_rows = zero_ref.shape[0]

        def unused_sub_tiles(fn):
            def tbody(tile, carry):
                for s in range(n_sub):
                    @pl.when(s >= tn_ref[tile])
                    def _():
                        dst = pl.multiple_of((tile * n_sub + s) * sub_rows, sub_rows)
                        fn(pltpu.make_async_copy(zero_ref, xs_hbm.at[pl.ds(dst, sub_rows), :], zsem))
                return carry
            lax.fori_loop(0, tn_ref.shape[0], tbody, 0)

        def tail_rows(fn):
            for e in range(n_experts):
                def zbody(r, carry, e=e):
                    dst = pl.multiple_of((pad_ref[0, e] + r) * n_slabs, n_slabs)
                    fn(pltpu.make_async_copy(zero_row, xs_hbm.at[pl.ds(dst, n_slabs), :], zsem))
                    return carry
                lax.fori_loop(0, pad_ref[1, e], zbody, 0)

        unused_sub_tiles(lambda cp: cp.start())
        tail_rows(lambda cp: cp.start())
        unused_sub_tiles(lambda cp: cp.wait())
        tail_rows(lambda cp: cp.wait())

    @pl.when(i > 0)
    def _():
        for rank in range(TOP_K):
            tile_copy(1 - slot).wait()

    @pl.when(i == n - 1)
    def _():
        for rank in range(TOP_K):
            tile_copy(slot).wait()


def moe_dispatch(h, dest, pad, tile_count, sub, n_sub, n_slabs):
    n_tok_tiles = dest.shape[0]
    tm = dest.shape[2] // TOP_K
    n_experts = pad.shape[1]
    n_rows = tile_count.shape[0] * n_sub * sub
    grid_spec = pltpu.PrefetchScalarGridSpec(
        num_scalar_prefetch=2,
        grid=(n_tok_tiles,),
        in_specs=[pl.BlockSpec((1, 1, TOP_K * tm), lambda i, pad, tn: (i, 0, 0), memory_space=pltpu.SMEM),
                  pl.BlockSpec(memory_space=pl.ANY)],
        out_specs=pl.BlockSpec(memory_space=pl.ANY),
        scratch_shapes=[pltpu.VMEM((sub * n_slabs, LANES), F32), pltpu.VMEM((3, tm * n_slabs, LANES), F32),
                        pltpu.SemaphoreType.DMA((3,)), pltpu.SemaphoreType.DMA((2,)),
                        pltpu.SemaphoreType.DMA(())],
    )
    return pl.pallas_call(
        functools.partial(_dispatch_body, tm=tm, n_slabs=n_slabs, n_experts=n_experts, n_sub=n_sub),
        grid_spec=grid_spec,
        out_shape=jax.ShapeDtypeStruct((n_rows * n_slabs, LANES), F32),
        compiler_params=_params(1),
        name="moe_dispatch",
    )(pad, tile_count, dest, h)


def _moe_body(te_ref, tn_ref, xs_ref, wg_ref, wu_ref, wd_ref, y_ref,
              xb_ref, wgb_ref, wub_ref, wdb_ref, *, sub, n_sub):
    i = pl.program_id(0)
    j = pl.program_id(1)
    count = tn_ref[i]
    d = wg_ref.shape[2]

    for s in range(n_sub):
        rows = pl.ds(s * sub, sub)

        @pl.when((j == 0) & (s < count))
        def _():
            xb_ref[rows, :] = _load_slabs(xs_ref, s * sub, sub, d).astype(BF16)

        @pl.when(j == 0)
        def _():
            y_ref[rows, :] = jnp.zeros((sub, d), F32)

        @pl.when(s < count)
        def _():
            if s == 0:
                wg, wu, wd = (r[0, 0].astype(BF16) for r in (wg_ref, wu_ref, wd_ref))
                wgb_ref[...], wub_ref[...], wdb_ref[...] = wg, wu, wd
            else:
                wg, wu, wd = wgb_ref[...], wub_ref[...], wdb_ref[...]
            x = xb_ref[rows, :]
            a = jnp.dot(x, wg, preferred_element_type=F32)
            u = jnp.dot(x, wu, preferred_element_type=F32)
            act = (a * jax.nn.sigmoid(a) * u).astype(BF16)
            y_ref[rows, :] += jnp.dot(act, wd, preferred_element_type=F32)


def moe_experts(xs, tile_expert, tile_count, wg, wu, wd, layer, sub, n_sub, tf=512):
    d = wg.shape[2]
    n_slabs = d // LANES
    n_tiles = tile_expert.shape[0]
    tile_rows = n_sub * sub
    f = wg.shape[3]
    tf = _pick(f, tf)
    nf = f // tf

    def w_col(i, j, te, tn):
        return (layer, te[i], 0, jnp.where(tn[i] > 0, j, nf - 1))

    def w_row(i, j, te, tn):
        return (layer, te[i], jnp.where(tn[i] > 0, j, nf - 1), 0)

    grid_spec = pltpu.PrefetchScalarGridSpec(
        num_scalar_prefetch=2,
        grid=(n_tiles, nf),
        in_specs=[pl.BlockSpec((tile_rows * n_slabs, LANES), lambda i, j, te, tn: (i, 0)),
                  pl.BlockSpec((1, 1, d, tf), w_col),
                  pl.BlockSpec((1, 1, d, tf), w_col),
                  pl.BlockSpec((1, 1, tf, d), w_row)],
        out_specs=pl.BlockSpec((tile_rows, d), lambda i, j, te, tn: (i, 0)),
        scratch_shapes=[pltpu.VMEM((tile_rows, d), BF16),
                        pltpu.VMEM((d, tf), BF16), pltpu.VMEM((d, tf), BF16), pltpu.VMEM((tf, d), BF16)],
    )
    return pl.pallas_call(
        functools.partial(_moe_body, sub=sub, n_sub=n_sub),
        grid_spec=grid_spec,
        out_shape=jax.ShapeDtypeStruct((n_tiles * tile_rows, d), F32),
        compiler_params=_params(2),
        name="moe_experts",
    )(tile_expert, tile_count, xs, wg, wu, wd)


def _combine_body(d_cur_ref, d_nxt_ref, x_ref, meta_ref, g_ref, y_hbm, o_ref, buf_ref, sem,
                  *, tm, final_norm):
    i = pl.program_id(0)
    n = pl.num_programs(0)

    def gather(d_ref, slot):
        def body(it, carry):
            for k in range(GATHER_UNROLL):
                r = it * GATHER_UNROLL + k
                pltpu.make_async_copy(y_hbm.at[pl.ds(d_ref[0, 0, r], 1), :],
                                      buf_ref.at[slot, pl.ds(r, 1), :], sem.at[slot]).start()
            return carry
        lax.fori_loop(0, TOP_K * tm // GATHER_UNROLL, body, 0)

    @pl.when(i == 0)
    def _():
        gather(d_cur_ref, 0)

    @pl.when(i + 1 < n)
    def _():
        gather(d_nxt_ref, (i + 1) % 2)

    slot = i % 2
    pltpu.make_async_copy(y_hbm.at[pl.ds(0, TOP_K * tm), :], buf_ref.at[slot], sem.at[slot]).wait()
    meta = meta_ref[...]
    out = (x_ref[...] + meta[:, 2:3] * buf_ref[slot, 0:tm, :]
           + meta[:, 3:4] * buf_ref[slot, tm:2 * tm, :])
    if final_norm:
        out = _rms(out, g_ref[...])
    o_ref[...] = out


def moe_combine(x, meta, dest, y, g, final_norm, tm=256):
    t, d = x.shape
    n = t // tm
    smem = lambda fn: pl.BlockSpec((1, 1, TOP_K * tm), fn, memory_space=pltpu.SMEM)
    return pl.pallas_call(
        functools.partial(_combine_body, tm=tm, final_norm=final_norm),
        grid=(n,),
        in_specs=[smem(lambda i: (i, 0, 0)),
                  smem(lambda i: (jnp.minimum(i + 1, n - 1), 0, 0)),
                  pl.BlockSpec((tm, d), lambda i: (i, 0)),
                  pl.BlockSpec((tm, LANES), lambda i: (i, 0)),
                  pl.BlockSpec((1, d), lambda i: (0, 0)),
                  pl.BlockSpec(memory_space=pl.ANY)],
        out_specs=pl.BlockSpec((tm, d), lambda i: (i, 0)),
        out_shape=jax.ShapeDtypeStruct((t, d), F32),
        scratch_shapes=[pltpu.VMEM((2, TOP_K * tm, d), F32), pltpu.SemaphoreType.DMA((2,))],
        compiler_params=_params(1),
        name="moe_combine",
    )(dest, dest, x, meta, g.reshape(1, d), y)


def _route(meta, counts, n_experts, sub, n_sub, tm_combine):
    t = meta.shape[0]
    tile_rows = sub * n_sub
    picks = meta[:, :3 * TOP_K].astype(I32)
    flat_e = picks[:, :TOP_K].reshape(-1)
    rank = picks[:, 2 * TOP_K:].reshape(-1)
    counts = counts[0, :n_experts].astype(I32)
    subs = (counts + sub - 1) // sub
    tiles = (subs + n_sub - 1) // n_sub
    tile_ends = jnp.cumsum(tiles)
    tile_starts = tile_ends - tiles
    dest = tile_starts[flat_e] * tile_rows + rank
    n_tiles = ((t * TOP_K) // sub + n_experts + n_sub - 1) // n_sub + n_experts
    idx = jnp.arange(n_tiles, dtype=I32)
    tile_expert = jnp.minimum(jnp.searchsorted(tile_ends, idx, side="right"), n_experts - 1).astype(I32)
    local = idx - tile_starts[tile_expert]
    tile_count = jnp.clip(subs[tile_expert] - local * n_sub, 0, n_sub).astype(I32)
    dest_tiles = dest.reshape(t // tm_combine, tm_combine, TOP_K).transpose(0, 2, 1).reshape(
        t // tm_combine, 1, TOP_K * tm_combine)
    pad = jnp.stack([tile_starts * tile_rows + counts, subs * sub - counts]).astype(I32)
    return tile_expert, tile_count, dest_tiles, pad, n_tiles


def moe_layer(x, g, wr, wg, wu, wd, layer, final_g, final_norm, sub=512, n_sub=4, tm_combine=256):
    t, d = x.shape
    n_experts = wr.shape[1]
    tm_combine = min(tm_combine, t)
    wr_pad = jnp.zeros((d, LANES), BF16).at[:, :n_experts].set(wr.astype(BF16))
    h, meta, counts = moe_router(x, g, wr_pad, n_experts)
    n_slabs = d // LANES
    tile_expert, tile_count, dest_tiles, pad, n_tiles = _route(meta, counts, n_experts, sub, n_sub,
                                                               tm_combine)
    xs = moe_dispatch(h, dest_tiles * n_slabs, pad, tile_count, sub, n_sub, n_slabs)
    y = moe_experts(xs, tile_expert, tile_count, wg, wu, wd, layer, sub, n_sub)
    return moe_combine(x, meta, dest_tiles, y, final_g, final_norm, tm_combine)


def attention_mixers(x, norm_g, w_in, lam_params, subln_g, sinks, layer):
    b, s, d = x.shape
    diff_heads = d // (4 * HEAD_DIM)
    swa_q_heads = d // (2 * HEAD_DIM)
    swa_kv_heads = max(1, swa_q_heads // 4)
    group = swa_q_heads // swa_kv_heads
    a_width = diff_heads * 2 * HEAD_DIM
    lam_init = 0.8 - 0.6 * math.exp(-0.3 * layer)
    x2 = x.reshape(b * s, d)
    proj = rms_matmul(x2, norm_g, w_in).reshape(b, s, -1)
    oa = diff_attention(proj, lam_params, subln_g, diff_heads, lam_init)
    q_col = 3 * a_width
    k_col = q_col + swa_q_heads * HEAD_DIM
    v_col = k_col + swa_kv_heads * HEAD_DIM
    ob = sliding_window_attention(proj, sinks, q_col, k_col, v_col, swa_kv_heads, group)
    return oa.reshape(b * s, -1), ob.reshape(b * s, -1)


def attention_layer(x, norm_g, w_in, lam_params, subln_g, sinks, w_out, layer):
    b, s, d = x.shape
    oa, ob = attention_mixers(x, norm_g, w_in, lam_params, subln_g, sinks, layer)
    return out_projection(x.reshape(b * s, d), oa, ob, w_out)


def kernel(x, attn_norm_g, w_in_att, diff_lambda, diff_subln_g, attn_sinks, w_out_att, ffn_norm_g, w_ffn_gate, w_ffn_up, w_ffn_down, conv_norm_g, w_pw1, b_pw1, w_dw, b_dw, conv_ln_g, conv_ln_b, w_pw2, b_pw2, moe_norm_g, w_router, w_exp_gate, w_exp_up, w_exp_down, final_norm_g):
    b, s, d = x.shape
    depth = attn_norm_g.shape[0] + conv_norm_g.shape[0]
    assert depth % 2 == 0, "the final RMSNorm is fused into the last expert layer"
    bf = lambda w: w.astype(BF16)
    for layer in range(depth):
        i = layer // 2
        if layer % 2 == 0:
            oa, ob = attention_mixers(x, attn_norm_g[i], bf(w_in_att[i]), diff_lambda[i], diff_subln_g[i],
                                      attn_sinks[i], layer)
            x2 = mix_ffn(x.reshape(b * s, d), oa, ob, bf(w_out_att[i]), ffn_norm_g[i],
                         bf(w_ffn_gate[i]), bf(w_ffn_up[i]), bf(w_ffn_down[i]))
            x = x2.reshape(b, s, d)
        else:
            x = conformer_conv(x, conv_norm_g[i], bf(w_pw1[i]), b_pw1[i], w_dw[i], b_dw[i],
                               conv_ln_g[i], conv_ln_b[i], bf(w_pw2[i]), b_pw2[i])
            x2 = moe_layer(x.reshape(b * s, d), moe_norm_g[i], w_router[i], w_exp_gate,
                           w_exp_up, w_exp_down, i, final_norm_g,
                           final_norm=(layer == depth - 1))
            x = x2.reshape(b, s, d)
    return x
```

```python
import functools
import math

import jax
import jax.numpy as jnp
from jax import lax
from jax.experimental import pallas as pl
from jax.experimental.pallas import tpu as pltpu

BF16 = jnp.bfloat16
F32 = jnp.float32
I32 = jnp.int32

LOG2_E = 1.4426950408889634
RMS_EPS = 1e-6
LN_EPS = 1e-5
HEAD_DIM = 64
ATTN_BLOCK = 128
CONV_WIDTH = 31
TOP_K = 2
LANES = 128
F32_SUBLANES = 8
CONV_HIST = 32
V7X_VMEM_LIMIT = 56 * 1024 * 1024


def _params(n_axes, flags=None):
    return pltpu.CompilerParams(dimension_semantics=("arbitrary",) * n_axes,
                                vmem_limit_bytes=V7X_VMEM_LIMIT, flags=flags)


def _rms(x, g):
    return x * lax.rsqrt(jnp.mean(x * x, axis=-1, keepdims=True) + RMS_EPS) * g


def _pick(n, pref):
    t = min(n, pref)
    while n % t:
        t -= LANES if t > LANES else 8
    return t


def _rms_matmul_body(x_ref, g_ref, w_ref, o_ref):
    h = _rms(x_ref[...], g_ref[...]).astype(BF16)
    o_ref[...] = jnp.dot(h, w_ref[...], preferred_element_type=F32).astype(o_ref.dtype)


def rms_matmul(x, g, w, tm=512):
    t, d = x.shape
    n = w.shape[1]
    tm = _pick(t, tm)
    return pl.pallas_call(
        _rms_matmul_body,
        grid=(t // tm,),
        in_specs=[pl.BlockSpec((tm, d), lambda i: (i, 0)),
                  pl.BlockSpec((1, d), lambda i: (0, 0)),
                  pl.BlockSpec((d, n), lambda i: (0, 0))],
        out_specs=pl.BlockSpec((tm, n), lambda i: (i, 0)),
        out_shape=jax.ShapeDtypeStruct((t, n), BF16),
        compiler_params=_params(1),
        name="rms_inproj",
    )(x, g.reshape(1, d), w)


def _diff_attn_body(q_ref, k_ref, v_ref, lam_ref, g_ref, o_ref, q_ref2, m_ref, l_ref, acc_ref,
                    *, blk, lam_init):
    qi = pl.program_id(2)
    hw = 2 * HEAD_DIM
    half = blk // 2
    lane = lax.broadcasted_iota(I32, (half, hw), 1)
    q = (q_ref[0].astype(F32) * (HEAD_DIM ** -0.5 * LOG2_E)).astype(BF16)
    zero = jnp.zeros((half, hw), BF16)
    q_ref2[...] = jnp.concatenate(
        [jnp.where(keep, q[r:r + half], zero)
         for r in (0, half) for keep in (lane < HEAD_DIM, lane >= HEAD_DIM)], axis=0)

    m_ref[...] = jnp.full(m_ref.shape, -jnp.inf, F32)
    l_ref[...] = jnp.zeros(l_ref.shape, F32)
    acc_ref[...] = jnp.zeros(acc_ref.shape, F32)

    def block(key_start, n_keys, row0, n_rows, masked):
        rows = pl.ds(row0, n_rows)
        k = k_ref[0, pl.ds(key_start, n_keys), :]
        v = v_ref[0, pl.ds(key_start, n_keys), :]
        s = lax.dot_general(q_ref2[rows, :], k, (((1,), (1,)), ((), ())), preferred_element_type=F32)
        if masked:
            row = lax.broadcasted_iota(I32, (n_rows, n_keys), 0) & (half - 1)
            col = lax.broadcasted_iota(I32, (n_rows, n_keys), 1)
            s = jnp.where(col <= row, s, -jnp.inf)
        tiles = [s[:, c * LANES:(c + 1) * LANES] for c in range(n_keys // LANES)]
        m_prev = m_ref[rows, :]
        m_new = jnp.maximum(m_prev, jnp.max(functools.reduce(jnp.maximum, tiles), axis=-1, keepdims=True))
        alpha = jnp.exp2(m_prev - m_new)
        p_tiles = [jnp.exp2(t - m_new) for t in tiles]
        l_ref[rows, :] = alpha * l_ref[rows, :] + functools.reduce(jnp.add, p_tiles)
        p = jnp.concatenate(p_tiles, axis=1).astype(BF16)
        acc_ref[rows, :] = alpha * acc_ref[rows, :] + jnp.dot(p, v, preferred_element_type=F32)
        m_ref[rows, :] = m_new

    def full_block(ki):
        block(pl.multiple_of(ki * blk, blk), blk, 0, 2 * blk, False)

    def pair(j, carry):
        full_block(2 * j)
        full_block(2 * j + 1)
        return carry

    lax.fori_loop(0, qi // 2, pair, 0)

    @pl.when(qi % 2 == 1)
    def _():
        full_block(qi - 1)

    diag = pl.multiple_of(qi * blk, blk)
    block(diag, half, 0, 2 * half, True)
    block(diag, half, 2 * half, 2 * half, False)
    block(diag + half, half, 2 * half, 2 * half, True)

    lp = lam_ref[...]
    lam = (jnp.exp(jnp.sum(lp[0:1] * lp[1:2], axis=-1, keepdims=True))
           - jnp.exp(jnp.sum(lp[2:3] * lp[3:4], axis=-1, keepdims=True)) + lam_init)
    o = acc_ref[...] / jnp.sum(l_ref[...], axis=-1, keepdims=True)
    o = (jnp.concatenate([o[0:half], o[2 * half:3 * half]], axis=0)
         - lam * jnp.concatenate([o[half:2 * half], o[3 * half:4 * half]], axis=0))
    o = _rms(o, g_ref[...]) * (1.0 - lam_init)
    o_ref[0] = o.astype(o_ref.dtype)


def diff_attention(proj, lam_params, subln_g, n_heads, lam_init, blk=1024):
    b, s, _ = proj.shape
    blk = _pick(s, blk)
    assert blk & (blk - 1) == 0, "the causal mask uses power-of-two half blocks"
    hw = 2 * HEAD_DIM
    kernel = functools.partial(_diff_attn_body, blk=blk, lam_init=lam_init)
    return pl.pallas_call(
        kernel,
        grid=(b, n_heads, s // blk),
        in_specs=[pl.BlockSpec((1, blk, hw), lambda bi, h, qi: (bi, qi, h)),
                  pl.BlockSpec((1, s, hw), lambda bi, h, qi: (bi, 0, n_heads + h)),
                  pl.BlockSpec((1, s, hw), lambda bi, h, qi: (bi, 0, 2 * n_heads + h)),
                  pl.BlockSpec((4, HEAD_DIM), lambda bi, h, qi: (0, 0)),
                  pl.BlockSpec((1, hw), lambda bi, h, qi: (0, 0))],
        out_specs=pl.BlockSpec((1, blk, hw), lambda bi, h, qi: (bi, qi, h)),
        out_shape=jax.ShapeDtypeStruct((b, s, n_heads * hw), BF16),
        scratch_shapes=[pltpu.VMEM((2 * blk, hw), BF16), pltpu.VMEM((2 * blk, LANES), F32),
                        pltpu.VMEM((2 * blk, LANES), F32), pltpu.VMEM((2 * blk, hw), F32)],
        compiler_params=_params(3),
        name="diff_attention",
    )(proj, proj, proj, lam_params, subln_g.reshape(1, hw))


def _swa_body(sink_ref, q_ref, kc_ref, kp_ref, vc_ref, vp_ref, o_ref, *, tq, kv_heads, group):
    t = pl.program_id(1)
    w = ATTN_BLOCK
    row = lax.broadcasted_iota(I32, (group * w, 2 * w), 0) & (w - 1)
    col = lax.broadcasted_iota(I32, (group * w, 2 * w), 1)
    band = (col > row) & (col <= row + w)
    band_first = band & ((col >= w) | (t > 0))
    low_half = lax.broadcasted_iota(I32, (w, LANES), 1) < HEAD_DIM
    for j in range(tq // w):
        rows = slice(j * w, (j + 1) * w)
        if j == 0:
            k2 = jnp.concatenate([kp_ref[0], kc_ref[0, rows, :]], axis=0)
            v2 = jnp.concatenate([vp_ref[0], vc_ref[0, rows, :]], axis=0)
            mask = band_first
        else:
            k2 = kc_ref[0, (j - 1) * w:(j + 1) * w, :]
            v2 = vc_ref[0, (j - 1) * w:(j + 1) * w, :]
            mask = band
        outs = []
        for g in range(kv_heads):
            kg = k2[:, g * HEAD_DIM:(g + 1) * HEAD_DIM]
            vg = v2[:, g * HEAD_DIM:(g + 1) * HEAD_DIM]
            kdup = jnp.concatenate([kg, kg], axis=1)
            vdup = jnp.concatenate([vg, vg], axis=1)
            heads = [g * group + i for i in range(group)]
            q_tiles = []
            for h in heads:
                q = q_ref[0, rows, (h // 2) * LANES:(h // 2 + 1) * LANES]
                q = (q.astype(F32) * (HEAD_DIM ** -0.5)).astype(BF16)
                own_half = low_half if h % 2 == 0 else jnp.logical_not(low_half)
                q_tiles.append(jnp.where(own_half, q, jnp.zeros_like(q)))
            sink = jnp.concatenate([jnp.full((w, LANES), sink_ref[h], F32) for h in heads], axis=0)
            s = lax.dot_general(jnp.concatenate(q_tiles, axis=0), kdup, (((1,), (1,)), ((), ())),
                                preferred_element_type=F32)
            s = jnp.where(mask, s, -jnp.inf)
            s0, s1 = s[:, :w], s[:, w:]
            m = jnp.maximum(jnp.max(jnp.maximum(s0, s1), axis=-1, keepdims=True), sink)
            e0 = jnp.exp(s0 - m)
            e1 = jnp.exp(s1 - m)
            denom = jnp.sum(e0 + e1, axis=-1, keepdims=True) + jnp.exp(sink - m)
            e = jnp.concatenate([e0, e1], axis=1).astype(BF16)
            o = jnp.dot(e, vdup, preferred_element_type=F32) / denom
            outs.extend(o[i * w:(i + 1) * w, :] for i in range(group))
        tiles = [jnp.where(low_half, outs[h], outs[h + 1]) for h in range(0, len(outs), 2)]
        o_ref[0, rows, :] = jnp.concatenate(tiles, axis=1).astype(o_ref.dtype)


def sliding_window_attention(proj, sinks, q_col, k_col, v_col, kv_heads, group, tq=512):
    b, s, _ = proj.shape
    tq = _pick(s, tq)
    qw = kv_heads * group * HEAD_DIM
    kw = kv_heads * HEAD_DIM
    sub = tq // ATTN_BLOCK
    kernel = functools.partial(_swa_body, tq=tq, kv_heads=kv_heads, group=group)
    prev = lambda bi, t: (bi, jnp.maximum(t * sub - 1, 0), 0)
    return pl.pallas_call(
        kernel,
        grid=(b, s // tq),
        in_specs=[pl.BlockSpec(memory_space=pltpu.SMEM),
                  pl.BlockSpec((1, tq, qw), lambda bi, t: (bi, t, q_col // qw)),
                  pl.BlockSpec((1, tq, kw), lambda bi, t: (bi, t, k_col // kw)),
                  pl.BlockSpec((1, ATTN_BLOCK, kw), lambda bi, t: prev(bi, t)[:2] + (k_col // kw,)),
                  pl.BlockSpec((1, tq, kw), lambda bi, t: (bi, t, v_col // kw)),
                  pl.BlockSpec((1, ATTN_BLOCK, kw), lambda bi, t: prev(bi, t)[:2] + (v_col // kw,))],
        out_specs=pl.BlockSpec((1, tq, qw), lambda bi, t: (bi, t, 0)),
        out_shape=jax.ShapeDtypeStruct((b, s, qw), BF16),
        compiler_params=_params(2),
        name="swa_attention",
    )(sinks, proj, proj, proj, proj, proj)


def _outproj_body(x_ref, a_ref, b_ref, w_ref, o_ref):
    mix = jnp.concatenate([a_ref[...], b_ref[...]], axis=1)
    o_ref[...] = x_ref[...] + jnp.dot(mix, w_ref[...], preferred_element_type=F32)


def out_projection(x, oa, ob, w, tm=512):
    t, d = x.shape
    tm = _pick(t, tm)
    return pl.pallas_call(
        _outproj_body,
        grid=(t // tm,),
        in_specs=[pl.BlockSpec((tm, d), lambda i: (i, 0)),
                  pl.BlockSpec((tm, oa.shape[1]), lambda i: (i, 0)),
                  pl.BlockSpec((tm, ob.shape[1]), lambda i: (i, 0)),
                  pl.BlockSpec(w.shape, lambda i: (0, 0))],
        out_specs=pl.BlockSpec((tm, d), lambda i: (i, 0)),
        out_shape=jax.ShapeDtypeStruct((t, d), F32),
        compiler_params=_params(1),
        name="out_projection",
    )(x, oa, ob, w)


def _ffn_body(x_ref, g_ref, wg_ref, wu_ref, wd_ref, o_ref, h_ref, acc_ref):
    j = pl.program_id(1)

    @pl.when(j == 0)
    def _():
        h_ref[...] = _rms(x_ref[...], g_ref[...]).astype(BF16)
        acc_ref[...] = jnp.zeros(acc_ref.shape, F32)

    h = h_ref[...]
    a = jnp.dot(h, wg_ref[...], preferred_element_type=F32)
    u = jnp.dot(h, wu_ref[...], preferred_element_type=F32)
    act = (a * jax.nn.sigmoid(a) * u).astype(BF16)
    acc_ref[...] += jnp.dot(act, wd_ref[...], preferred_element_type=F32)

    @pl.when(j == pl.num_programs(1) - 1)
    def _():
        o_ref[...] = x_ref[...] + acc_ref[...]


def ffn_swiglu(x, g, wg, wu, wd, tm=512, tf=1408):
    t, d = x.shape
    f = wg.shape[1]
    tm = _pick(t, tm)
    tf = _pick(f, tf)
    return pl.pallas_call(
        _ffn_body,
        grid=(t // tm, f // tf),
        in_specs=[pl.BlockSpec((tm, d), lambda i, j: (i, 0)),
                  pl.BlockSpec((1, d), lambda i, j: (0, 0)),
                  pl.BlockSpec((d, tf), lambda i, j: (0, j)),
                  pl.BlockSpec((d, tf), lambda i, j: (0, j)),
                  pl.BlockSpec((tf, d), lambda i, j: (j, 0))],
        out_specs=pl.BlockSpec((tm, d), lambda i, j: (i, 0)),
        out_shape=jax.ShapeDtypeStruct((t, d), F32),
        scratch_shapes=[pltpu.VMEM((tm, d), BF16), pltpu.VMEM((tm, d), F32)],
        compiler_params=_params(2),
        name="ffn_swiglu",
    )(x, g.reshape(1, d), wg, wu, wd)


def _mix_ffn_body(x_ref, a_ref, b_ref, wo_ref, g_ref, wg_ref, wu_ref, wd_ref, o_ref, *, ff_chunk):
    mix = jnp.concatenate([a_ref[...], b_ref[...]], axis=1)
    x1 = x_ref[...] + jnp.dot(mix, wo_ref[...], preferred_element_type=F32)
    h = _rms(x1, g_ref[...]).astype(BF16)
    y = x1
    for c in range(wg_ref.shape[1] // ff_chunk):
        cols = slice(c * ff_chunk, (c + 1) * ff_chunk)
        a = jnp.dot(h, wg_ref[:, cols], preferred_element_type=F32)
        u = jnp.dot(h, wu_ref[:, cols], preferred_element_type=F32)
        act = (a * jax.nn.sigmoid(a) * u).astype(BF16)
        y = y + jnp.dot(act, wd_ref[cols, :], preferred_element_type=F32)
    o_ref[...] = y


def mix_ffn(x, oa, ob, wo, g, wg, wu, wd, tm=512, ff_chunk=1408):
    t, d = x.shape
    f = wg.shape[1]
    tm = _pick(t, tm)
    ff_chunk = _pick(f, ff_chunk)
    resident = lambda shape: pl.BlockSpec(shape, lambda i: (0, 0), pipeline_mode=pl.Buffered(1))
    return pl.pallas_call(
        functools.partial(_mix_ffn_body, ff_chunk=ff_chunk),
        grid=(t // tm,),
        in_specs=[pl.BlockSpec((tm, d), lambda i: (i, 0)),
                  pl.BlockSpec((tm, oa.shape[1]), lambda i: (i, 0)),
                  pl.BlockSpec((tm, ob.shape[1]), lambda i: (i, 0)),
                  resident(wo.shape), resident((1, d)),
                  resident(wg.shape), resident(wu.shape), resident(wd.shape)],
        out_specs=pl.BlockSpec((tm, d), lambda i: (i, 0)),
        out_shape=jax.ShapeDtypeStruct((t, d), F32),
        compiler_params=_params(1),
        name="mix_ffn",
    )(x, oa, ob, wo, g.reshape(1, d), wg, wu, wd)


def _conv_body(x_ref, g_ref, w1_ref, b1_ref, wdw_ref, bdw_ref, lng_ref, lnb_ref, w2_ref, b2_ref,
               o_ref, u_ref, v_ref, wb_ref, c_ref, *, tm, rows_per_chunk):
    d = x_ref.shape[-1]
    t = pl.program_id(1)
    sublanes = wb_ref.shape[1]

    @pl.when(t == 0)
    def _():
        u_ref[:, 0:CONV_HIST, :] = jnp.zeros((d // LANES, CONV_HIST, LANES), F32)
        wb_ref[...] = jnp.broadcast_to(wdw_ref[...][:, None, :], wb_ref.shape)

    h = _rms(x_ref[0], g_ref[...]).astype(BF16)
    z = jnp.dot(h, w1_ref[...], preferred_element_type=F32) + b1_ref[...]
    u = z[:, :d] * jax.nn.sigmoid(z[:, d:])
    n_slabs = d // LANES
    for c in range(n_slabs):
        u_ref[c, CONV_HIST:CONV_HIST + tm, :] = u[:, c * LANES:(c + 1) * LANES]

    first_tap = CONV_HIST - (CONV_WIDTH - 1)
    conv_rows = 8 * sublanes

    for c in range(n_slabs):
        lanes = slice(c * LANES, (c + 1) * LANES)
        w = [wb_ref[j, :, lanes] for j in range(CONV_WIDTH)]
        bias = jnp.zeros((sublanes, LANES), F32) + bdw_ref[:, lanes]

        def rows_block(i, carry, c=c, lanes=lanes, w=w, bias=bias):
            r0 = pl.multiple_of(i * conv_rows, conv_rows)
            for r in range(conv_rows // sublanes):
                sums = [bias, None]
                for j in range(CONV_WIDTH):
                    tap = u_ref[c, pl.ds(r0 + (first_tap + j + r * sublanes), sublanes, stride=1), :] * w[j]
                    sums[j % 2] = tap if sums[j % 2] is None else sums[j % 2] + tap
                c_ref[pl.ds(r0 + r * sublanes, sublanes), lanes] = sums[0] + sums[1]
            return carry

        lax.fori_loop(0, tm // conv_rows, rows_block, 0)

    def chunk(i, carry):
        r0 = pl.multiple_of(i * rows_per_chunk, rows_per_chunk)
        acc = c_ref[pl.ds(r0, rows_per_chunk), :]
        mu = jnp.mean(acc, axis=-1, keepdims=True)
        xc = acc - mu
        var = jnp.mean(xc * xc, axis=-1, keepdims=True)
        y = xc * lax.rsqrt(var + LN_EPS) * lng_ref[...] + lnb_ref[...]
        v_ref[pl.ds(r0, rows_per_chunk), :] = (y * jax.nn.sigmoid(y)).astype(BF16)
        return carry

    lax.fori_loop(0, tm // rows_per_chunk, chunk, 0)
    u_ref[:, 0:CONV_HIST, :] = u_ref[:, tm:tm + CONV_HIST, :]
    o_ref[0] = x_ref[0] + jnp.dot(v_ref[...], w2_ref[...], preferred_element_type=F32) + b2_ref[...]


def conformer_conv(x, g, w1, b1, wdw, bdw, lng, lnb, w2, b2, tm=512, rows_per_chunk=128):
    b, s, d = x.shape
    tm = _pick(s, tm)
    kernel = functools.partial(_conv_body, tm=tm, rows_per_chunk=rows_per_chunk)
    vec = lambda n: pl.BlockSpec((1, n), lambda bi, t: (0, 0))
    return pl.pallas_call(
        kernel,
        grid=(b, s // tm),
        in_specs=[pl.BlockSpec((1, tm, d), lambda bi, t: (bi, t, 0)),
                  vec(d),
                  pl.BlockSpec((d, 2 * d), lambda bi, t: (0, 0)),
                  vec(2 * d),
                  pl.BlockSpec((CONV_WIDTH, d), lambda bi, t: (0, 0)),
                  vec(d), vec(d), vec(d),
                  pl.BlockSpec((d, d), lambda bi, t: (0, 0)),
                  vec(d)],
        out_specs=pl.BlockSpec((1, tm, d), lambda bi, t: (bi, t, 0)),
        out_shape=jax.ShapeDtypeStruct((b, s, d), F32),
        scratch_shapes=[pltpu.VMEM((d // LANES, tm + CONV_HIST, LANES), F32), pltpu.VMEM((tm, d), BF16),
                        pltpu.VMEM((CONV_WIDTH, F32_SUBLANES, d), F32), pltpu.VMEM((tm, d), F32)],
        compiler_params=_params(2),
        name="conformer_conv",
    )(x, g.reshape(1, d), w1, b1.reshape(1, 2 * d), wdw, bdw.reshape(1, d), lng.reshape(1, d),
      lnb.reshape(1, d), w2, b2.reshape(1, d))


def _router_body(x_ref, g_ref, wr_ref, h_ref, meta_ref, counts_ref, *, n_experts):
    @pl.when(pl.program_id(0) == 0)
    def _():
        counts_ref[...] = jnp.zeros(counts_ref.shape, F32)

    h = _rms(x_ref[...], g_ref[...])
    _store_slabs(h_ref, 0, h)
    logits = jnp.dot(h.astype(BF16), wr_ref[...], preferred_element_type=F32)
    lane = lax.broadcasted_iota(I32, logits.shape, 1).astype(F32)
    neg = jnp.full_like(logits, -jnp.inf)
    far = jnp.full_like(logits, float(LANES))
    l1 = jnp.where(lane < n_experts, logits, neg)
    m1 = jnp.max(l1, axis=-1, keepdims=True)
    i1 = jnp.min(jnp.where(l1 == m1, lane, far), axis=-1, keepdims=True)
    l2 = jnp.where(lane == i1, neg, l1)
    m2 = jnp.max(l2, axis=-1, keepdims=True)
    i2 = jnp.min(jnp.where(l2 == m2, lane, far), axis=-1, keepdims=True)
    e2 = jnp.exp(m2 - m1)
    w1 = 1.0 / (1.0 + e2)
    w2 = e2 / (1.0 + e2)
    tm = logits.shape[0]
    hot1 = (lane == i1).astype(F32)
    hot2 = (lane == i2).astype(F32)
    both = (hot1 + hot2).astype(BF16)
    earlier = (lax.broadcasted_iota(I32, (tm, tm), 1) < lax.broadcasted_iota(I32, (tm, tm), 0)).astype(BF16)
    before = jnp.dot(earlier, both, preferred_element_type=F32) + counts_ref[...]
    r1 = jnp.sum(hot1 * before, axis=-1, keepdims=True)
    r2 = jnp.sum(hot2 * before, axis=-1, keepdims=True)
    counts_ref[...] += jnp.sum(hot1 + hot2, axis=0, keepdims=True)
    zero = jnp.zeros_like(logits)
    meta_ref[...] = jnp.where(lane == 0, i1,
                    jnp.where(lane == 1, i2,
                    jnp.where(lane == 2, w1,
                    jnp.where(lane == 3, w2,
                    jnp.where(lane == 4, r1, jnp.where(lane == 5, r2, zero))))))


def moe_router(x, g, wr, n_experts, tm=512):
    t, d = x.shape
    tm = _pick(t, tm)
    kernel = functools.partial(_router_body, n_experts=n_experts)
    return pl.pallas_call(
        kernel,
        grid=(t // tm,),
        in_specs=[pl.BlockSpec((tm, d), lambda i: (i, 0)),
                  pl.BlockSpec((1, d), lambda i: (0, 0)),
                  pl.BlockSpec((d, LANES), lambda i: (0, 0))],
        out_specs=[pl.BlockSpec((tm * (d // LANES), LANES), lambda i: (i, 0)),
                   pl.BlockSpec((tm, LANES), lambda i: (i, 0)),
                   pl.BlockSpec((1, LANES), lambda i: (0, 0))],
        out_shape=[jax.ShapeDtypeStruct((t * (d // LANES), LANES), F32),
                   jax.ShapeDtypeStruct((t, LANES), F32),
                   jax.ShapeDtypeStruct((1, LANES), F32)],
        compiler_params=_params(1),
        name="moe_router",
    )(x, g.reshape(1, d), wr)


GATHER_UNROLL = 8


def _store_slabs(ref, first_row, value):
    n, d = value.shape
    n_slabs = d // LANES
    for c in range(n_slabs):
        ref[pl.ds(first_row * n_slabs + c, n, stride=n_slabs), :] = value[:, c * LANES:(c + 1) * LANES]


def _load_slabs(ref, first_row, n, d):
    n_slabs = d // LANES
    return jnp.concatenate([ref[pl.ds(first_row * n_slabs + c, n, stride=n_slabs), :]
                            for c in range(n_slabs)], axis=1)


def _dispatch_body(pad_ref, tn_ref, d_ref, h_hbm, xs_hbm, zero_ref, buf_ref, in_sem, sem, zsem,
                   *, tm, n_slabs, n_experts, n_sub):
    i = pl.program_id(0)
    n = pl.num_programs(0)
    rows = tm * n_slabs
    n_ring = buf_ref.shape[0]

    def stage(tile):
        return pltpu.make_async_copy(h_hbm.at[pl.ds(pl.multiple_of(tile * rows, rows), rows), :],
                                     buf_ref.at[tile % n_ring], in_sem.at[tile % n_ring])

    def tile_copy(slot):
        return pltpu.make_async_copy(buf_ref.at[0], xs_hbm.at[pl.ds(0, rows), :], sem.at[slot])

    @pl.when(i == 0)
    def _():
        stage(0).start()

    @pl.when(i + 1 < n)
    def _():
        stage(i + 1).start()

    stage(i).wait()
    slot = i % 2
    ring = i % n_ring
    for rank in range(TOP_K):
        def body(it, carry, rank=rank):
            base = it * GATHER_UNROLL
            dsts = [pl.multiple_of(d_ref[0, 0, rank * tm + base + k], n_slabs) for k in range(GATHER_UNROLL)]
            for k in range(GATHER_UNROLL):
                src = pl.multiple_of((base + k) * n_slabs, n_slabs)
                pltpu.make_async_copy(buf_ref.at[ring, pl.ds(src, n_slabs), :],
                                      xs_hbm.at[pl.ds(dsts[k], n_slabs), :], sem.at[slot]).start()
            return carry
        lax.fori_loop(0, tm // GATHER_UNROLL, body, 0)

    @pl.when(i == 0)
    def _():
        zero_ref[...] = jnp.zeros(zero_ref.shape, F32)
        zero_row = zero_ref.at[pl.ds(0, n_slabs), :]
        sub_rows = zero_ref.shape[0]

        def unused_sub_tiles(fn):
            def tbody(tile, carry):
                for s in range(n_sub):
                    @pl.when(s >= tn_ref[tile])
                    def _():
                        dst = pl.multiple_of((tile * n_sub + s) * sub_rows, sub_rows)
                        fn(pltpu.make_async_copy(zero_ref, xs_hbm.at[pl.ds(dst, sub_rows), :], zsem))
                return carry
            lax.fori_loop(0, tn_ref.shape[0], tbody, 0)

        def tail_rows(fn):
            for e in range(n_experts):
                def zbody(r, carry, e=e):
                    dst = pl.multiple_of((pad_ref[0, e] + r) * n_slabs, n_slabs)
                    fn(pltpu.make_async_copy(zero_row, xs_hbm.at[pl.ds(dst, n_slabs), :], zsem))
                    return carry
                lax.fori_loop(0, pad_ref[1, e], zbody, 0)

        unused_sub_tiles(lambda cp: cp.start())
        tail_rows(lambda cp: cp.start())
        unused_sub_tiles(lambda cp: cp.wait())
        tail_rows(lambda cp: cp.wait())

    @pl.when(i > 0)
    def _():
        for rank in range(TOP_K):
            tile_copy(1 - slot).wait()

    @pl.when(i == n - 1)
    def _():
        for rank in range(TOP_K):
            tile_copy(slot).wait()


def moe_dispatch(h, dest, pad, tile_count, sub, n_sub, n_slabs):
    n_tok_tiles = dest.shape[0]
    tm = dest.shape[2] // TOP_K
    n_experts = pad.shape[1]
    n_rows = tile_count.shape[0] * n_sub * sub
    grid_spec = pltpu.PrefetchScalarGridSpec(
        num_scalar_prefetch=2,
        grid=(n_tok_tiles,),
        in_specs=[pl.BlockSpec((1, 1, TOP_K * tm), lambda i, pad, tn: (i, 0, 0), memory_space=pltpu.SMEM),
                  pl.BlockSpec(memory_space=pl.ANY)],
        out_specs=pl.BlockSpec(memory_space=pl.ANY),
        scratch_shapes=[pltpu.VMEM((sub * n_slabs, LANES), F32), pltpu.VMEM((3, tm * n_slabs, LANES), F32),
                        pltpu.SemaphoreType.DMA((3,)), pltpu.SemaphoreType.DMA((2,)),
                        pltpu.SemaphoreType.DMA(())],
    )
    return pl.pallas_call(
        functools.partial(_dispatch_body, tm=tm, n_slabs=n_slabs, n_experts=n_experts, n_sub=n_sub),
        grid_spec=grid_spec,
        out_shape=jax.ShapeDtypeStruct((n_rows * n_slabs, LANES), F32),
        compiler_params=_params(1),
        name="moe_dispatch",
    )(pad, tile_count, dest, h)


def _moe_body(te_ref, tn_ref, xs_ref, wg_ref, wu_ref, wd_ref, y_hbm,
              y_ref, xb_ref, wgb_ref, wub_ref, wdb_ref, stage_ref, zero_ref, osem, zsem, *, sub, n_sub):
    i = pl.program_id(0)
    j = pl.program_id(1)
    last = j == pl.num_programs(1) - 1
    count = tn_ref[i]
    d = wg_ref.shape[2]
    slab_rows = stage_ref.shape[1]

    @pl.when((i == 0) & (j == 0))
    def _():
        zero_ref[...] = jnp.zeros(zero_ref.shape, F32)

    for s in range(n_sub):
        rows = pl.ds(s * sub, sub)

        @pl.when((j == 0) & (s < count))
        def _():
            xb_ref[rows, :] = _load_slabs(xs_ref, s * sub, sub, d).astype(BF16)
            y_ref[rows, :] = jnp.zeros((sub, d), F32)

        @pl.when(s < count)
        def _():
            if s == 0:
                wg, wu, wd = (r[0, 0].astype(BF16) for r in (wg_ref, wu_ref, wd_ref))
                wgb_ref[...], wub_ref[...], wdb_ref[...] = wg, wu, wd
            else:
                wg, wu, wd = wgb_ref[...], wub_ref[...], wdb_ref[...]
            x = xb_ref[rows, :]
            a = jnp.dot(x, wg, preferred_element_type=F32)
            u = jnp.dot(x, wu, preferred_element_type=F32)
            act = (a * jax.nn.sigmoid(a) * u).astype(BF16)
            y_ref[rows, :] += jnp.dot(act, wd, preferred_element_type=F32)

    def out_copy(src, s):
        dst = pl.multiple_of((i * n_sub + s) * slab_rows, slab_rows)
        return src, y_hbm.at[pl.ds(dst, slab_rows), :]

    for s in range(n_sub):
        @pl.when(last & (s < count))
        def _():
            slot = s % 2
            if s >= 2:
                pltpu.make_async_copy(*out_copy(stage_ref.at[slot], s - 2), osem.at[slot]).wait()
            _store_slabs(stage_ref.at[slot], 0, y_ref[pl.ds(s * sub, sub), :])
            pltpu.make_async_copy(*out_copy(stage_ref.at[slot], s), osem.at[slot]).start()

        @pl.when(last & (s >= count))
        def _():
            pltpu.make_async_copy(*out_copy(zero_ref, s), zsem).start()

    for s in range(n_sub):
        @pl.when(last & (s < count) & (s + 2 >= count))
        def _():
            pltpu.make_async_copy(*out_copy(stage_ref.at[s % 2], s), osem.at[s % 2]).wait()

        @pl.when(last & (s >= count))
        def _():
            pltpu.make_async_copy(*out_copy(zero_ref, s), zsem).wait()


def moe_experts(xs, tile_expert, tile_count, wg, wu, wd, layer, sub, n_sub, tf=512):
    d = wg.shape[2]
    n_slabs = d // LANES
    n_tiles = tile_expert.shape[0]
    tile_rows = n_sub * sub
    f = wg.shape[3]
    tf = _pick(f, tf)
    nf = f // tf

    def w_col(i, j, te, tn):
        return (layer, te[i], 0, jnp.where(tn[i] > 0, j, nf - 1))

    def w_row(i, j, te, tn):
        return (layer, te[i], jnp.where(tn[i] > 0, j, nf - 1), 0)

    grid_spec = pltpu.PrefetchScalarGridSpec(
        num_scalar_prefetch=2,
        grid=(n_tiles, nf),
        in_specs=[pl.BlockSpec((tile_rows * n_slabs, LANES), lambda i, j, te, tn: (i, 0)),
                  pl.BlockSpec((1, 1, d, tf), w_col),
                  pl.BlockSpec((1, 1, d, tf), w_col),
                  pl.BlockSpec((1, 1, tf, d), w_row)],
        out_specs=pl.BlockSpec(memory_space=pl.ANY),
        scratch_shapes=[pltpu.VMEM((tile_rows, d), F32), pltpu.VMEM((tile_rows, d), BF16),
                        pltpu.VMEM((d, tf), BF16), pltpu.VMEM((d, tf), BF16), pltpu.VMEM((tf, d), BF16),
                        pltpu.VMEM((2, sub * n_slabs, LANES), F32), pltpu.VMEM((sub * n_slabs, LANES), F32),
                        pltpu.SemaphoreType.DMA((2,)), pltpu.SemaphoreType.DMA(())],
    )
    return pl.pallas_call(
        functools.partial(_moe_body, sub=sub, n_sub=n_sub),
        grid_spec=grid_spec,
        out_shape=jax.ShapeDtypeStruct((n_tiles * tile_rows * n_slabs, LANES), F32),
        compiler_params=_params(2),
        name="moe_experts",
    )(tile_expert, tile_count, xs, wg, wu, wd)


def _combine_body(d_cur_ref, d_nxt_ref, x_ref, meta_ref, g_ref, y_hbm, o_ref, buf_ref, sem,
                  *, tm, final_norm):
    i = pl.program_id(0)
    n = pl.num_programs(0)
    d = x_ref.shape[1]
    n_slabs = d // LANES

    def gather(d_ref, slot):
        def body(it, carry):
            base = it * GATHER_UNROLL
            srcs = [pl.multiple_of(d_ref[0, 0, base + k], n_slabs) for k in range(GATHER_UNROLL)]
            for k in range(GATHER_UNROLL):
                dst = pl.multiple_of((base + k) * n_slabs, n_slabs)
                pltpu.make_async_copy(y_hbm.at[pl.ds(srcs[k], n_slabs), :],
                                      buf_ref.at[slot, pl.ds(dst, n_slabs), :], sem.at[slot]).start()
            return carry
        lax.fori_loop(0, TOP_K * tm // GATHER_UNROLL, body, 0)

    @pl.when(i == 0)
    def _():
        gather(d_cur_ref, 0)

    @pl.when(i + 1 < n)
    def _():
        gather(d_nxt_ref, (i + 1) % 2)

    slot = i % 2
    pltpu.make_async_copy(y_hbm.at[pl.ds(0, TOP_K * tm * n_slabs), :], buf_ref.at[slot],
                          sem.at[slot]).wait()
    meta = meta_ref[...]
    out = (x_ref[...] + meta[:, 2:3] * _load_slabs(buf_ref.at[slot], 0, tm, d)
           + meta[:, 3:4] * _load_slabs(buf_ref.at[slot], tm, tm, d))
    if final_norm:
        out = _rms(out, g_ref[...])
    o_ref[...] = out


def moe_combine(x, meta, dest, y, g, final_norm, tm=256):
    t, d = x.shape
    n = t // tm
    n_slabs = d // LANES
    smem = lambda fn: pl.BlockSpec((1, 1, TOP_K * tm), fn, memory_space=pltpu.SMEM)
    return pl.pallas_call(
        functools.partial(_combine_body, tm=tm, final_norm=final_norm),
        grid=(n,),
        in_specs=[smem(lambda i: (i, 0, 0)),
                  smem(lambda i: (jnp.minimum(i + 1, n - 1), 0, 0)),
                  pl.BlockSpec((tm, d), lambda i: (i, 0)),
                  pl.BlockSpec((tm, LANES), lambda i: (i, 0)),
                  pl.BlockSpec((1, d), lambda i: (0, 0)),
                  pl.BlockSpec(memory_space=pl.ANY)],
        out_specs=pl.BlockSpec((tm, d), lambda i: (i, 0)),
        out_shape=jax.ShapeDtypeStruct((t, d), F32),
        scratch_shapes=[pltpu.VMEM((2, TOP_K * tm * n_slabs, LANES), F32), pltpu.SemaphoreType.DMA((2,))],
        compiler_params=_params(1),
        name="moe_combine",
    )(dest, dest, x, meta, g.reshape(1, d), y)


def _route(meta, counts, n_experts, sub, n_sub, tm_combine):
    t = meta.shape[0]
    tile_rows = sub * n_sub
    picks = meta[:, :3 * TOP_K].astype(I32)
    flat_e = picks[:, :TOP_K].reshape(-1)
    rank = picks[:, 2 * TOP_K:].reshape(-1)
    counts = counts[0, :n_experts].astype(I32)
    subs = (counts + sub - 1) // sub
    tiles = (subs + n_sub - 1) // n_sub
    tile_ends = jnp.cumsum(tiles)
    tile_starts = tile_ends - tiles
    dest = tile_starts[flat_e] * tile_rows + rank
    n_tiles = ((t * TOP_K) // sub + n_experts + n_sub - 1) // n_sub + n_experts
    idx = jnp.arange(n_tiles, dtype=I32)
    tile_expert = jnp.minimum(jnp.searchsorted(tile_ends, idx, side="right"), n_experts - 1).astype(I32)
    local = idx - tile_starts[tile_expert]
    tile_count = jnp.clip(subs[tile_expert] - local * n_sub, 0, n_sub).astype(I32)
    dest_tiles = dest.reshape(t // tm_combine, tm_combine, TOP_K).transpose(0, 2, 1).reshape(
        t // tm_combine, 1, TOP_K * tm_combine)
    pad = jnp.stack([tile_starts * tile_rows + counts, subs * sub - counts]).astype(I32)
    return tile_expert, tile_count, dest_tiles, pad, n_tiles


def moe_layer(x, g, wr, wg, wu, wd, layer, final_g, final_norm, sub=512, n_sub=4, tm_combine=256):
    t, d = x.shape
    n_experts = wr.shape[1]
    tm_combine = min(tm_combine, t)
    wr_pad = jnp.zeros((d, LANES), BF16).at[:, :n_experts].set(wr.astype(BF16))
    h, meta, counts = moe_router(x, g, wr_pad, n_experts)
    n_slabs = d // LANES
    tile_expert, tile_count, dest_tiles, pad, n_tiles = _route(meta, counts, n_experts, sub, n_sub,
                                                               tm_combine)
    slab_dest = dest_tiles * n_slabs
    xs = moe_dispatch(h, slab_dest, pad, tile_count, sub, n_sub, n_slabs)
    y = moe_experts(xs, tile_expert, tile_count, wg, wu, wd, layer, sub, n_sub)
    return moe_combine(x, meta, slab_dest, y, final_g, final_norm, tm_combine)


def attention_mixers(x, norm_g, w_in, lam_params, subln_g, sinks, layer):
    b, s, d = x.shape
    diff_heads = d // (4 * HEAD_DIM)
    swa_q_heads = d // (2 * HEAD_DIM)
    swa_kv_heads = max(1, swa_q_heads // 4)
    group = swa_q_heads // swa_kv_heads
    a_width = diff_heads * 2 * HEAD_DIM
    lam_init = 0.8 - 0.6 * math.exp(-0.3 * layer)
    x2 = x.reshape(b * s, d)
    proj = rms_matmul(x2, norm_g, w_in).reshape(b, s, -1)
    oa = diff_attention(proj, lam_params, subln_g, diff_heads, lam_init)
    q_col = 3 * a_width
    k_col = q_col + swa_q_heads * HEAD_DIM
    v_col = k_col + swa_kv_heads * HEAD_DIM
    ob = sliding_window_attention(proj, sinks, q_col, k_col, v_col, swa_kv_heads, group)
    return oa.reshape(b * s, -1), ob.reshape(b * s, -1)


def attention_layer(x, norm_g, w_in, lam_params, subln_g, sinks, w_out, layer):
    b, s, d = x.shape
    oa, ob = attention_mixers(x, norm_g, w_in, lam_params, subln_g, sinks, layer)
    return out_projection(x.reshape(b * s, d), oa, ob, w_out)


def kernel(x, attn_norm_g, w_in_att, diff_lambda, diff_subln_g, attn_sinks, w_out_att, ffn_norm_g, w_ffn_gate, w_ffn_up, w_ffn_down, conv_norm_g, w_pw1, b_pw1, w_dw, b_dw, conv_ln_g, conv_ln_b, w_pw2, b_pw2, moe_norm_g, w_router, w_exp_gate, w_exp_up, w_exp_down, final_norm_g):
    b, s, d = x.shape
    depth = attn_norm_g.shape[0] + conv_norm_g.shape[0]
    assert depth % 2 == 0, "the final RMSNorm is fused into the last expert layer"
    bf = lambda w: w.astype(BF16)
    for layer in range(depth):
        i = layer // 2
        if layer % 2 == 0:
            oa, ob = attention_mixers(x, attn_norm_g[i], bf(w_in_att[i]), diff_lambda[i], diff_subln_g[i],
                                      attn_sinks[i], layer)
            x2 = mix_ffn(x.reshape(b * s, d), oa, ob, bf(w_out_att[i]), ffn_norm_g[i],
                         bf(w_ffn_gate[i]), bf(w_ffn_up[i]), bf(w_ffn_down[i]))
            x = x2.reshape(b, s, d)
        else:
            x = conformer_conv(x, conv_norm_g[i], bf(w_pw1[i]), b_pw1[i], w_dw[i], b_dw[i],
                               conv_ln_g[i], conv_ln_b[i], bf(w_pw2[i]), b_pw2[i])
            x2 = moe_layer(x.reshape(b * s, d), moe_norm_g[i], w_router[i], w_exp_gate,
                           w_exp_up, w_exp_down, i, final_norm_g,
                           final_norm=(layer == depth - 1))
            x = x2.reshape(b, s, d)
    return x
```

```python
import functools
import math

import jax
import jax.numpy as jnp
from jax import lax
from jax.experimental import pallas as pl
from jax.experimental.pallas import tpu as pltpu

BF16 = jnp.bfloat16
F32 = jnp.float32
I32 = jnp.int32

LOG2_E = 1.4426950408889634
RMS_EPS = 1e-6
LN_EPS = 1e-5
HEAD_DIM = 64
ATTN_BLOCK = 128
CONV_WIDTH = 31
TOP_K = 2
LANES = 128
F32_SUBLANES = 8
CONV_HIST = 32
V7X_VMEM_LIMIT = 56 * 1024 * 1024


def _params(n_axes, flags=None):
    return pltpu.CompilerParams(dimension_semantics=("arbitrary",) * n_axes,
                                vmem_limit_bytes=V7X_VMEM_LIMIT, flags=flags)


def _rms(x, g):
    return x * lax.rsqrt(jnp.mean(x * x, axis=-1, keepdims=True) + RMS_EPS) * g


def _pick(n, pref):
    t = min(n, pref)
    while n % t:
        t -= LANES if t > LANES else 8
    return t


def _rms_matmul_body(x_ref, g_ref, w_ref, o_ref):
    h = _rms(x_ref[...], g_ref[...]).astype(BF16)
    o_ref[...] = jnp.dot(h, w_ref[...], preferred_element_type=F32).astype(o_ref.dtype)


def rms_matmul(x, g, w, tm=512):
    t, d = x.shape
    n = w.shape[1]
    tm = _pick(t, tm)
    return pl.pallas_call(
        _rms_matmul_body,
        grid=(t // tm,),
        in_specs=[pl.BlockSpec((tm, d), lambda i: (i, 0)),
                  pl.BlockSpec((1, d), lambda i: (0, 0)),
                  pl.BlockSpec((d, n), lambda i: (0, 0))],
        out_specs=pl.BlockSpec((tm, n), lambda i: (i, 0)),
        out_shape=jax.ShapeDtypeStruct((t, n), BF16),
        compiler_params=_params(1),
        name="rms_inproj",
    )(x, g.reshape(1, d), w)


def _diff_attn_body(q_ref, k_ref, v_ref, lam_ref, g_ref, o_ref, q_ref2, m_ref, l_ref, acc_ref,
                    *, blk, lam_init):
    qi = pl.program_id(2)
    hw = 2 * HEAD_DIM
    half = blk // 2
    lane = lax.broadcasted_iota(I32, (half, hw), 1)
    q = (q_ref[0].astype(F32) * (HEAD_DIM ** -0.5 * LOG2_E)).astype(BF16)
    zero = jnp.zeros((half, hw), BF16)
    q_ref2[...] = jnp.concatenate(
        [jnp.where(keep, q[r:r + half], zero)
         for r in (0, half) for keep in (lane < HEAD_DIM, lane >= HEAD_DIM)], axis=0)

    m_ref[...] = jnp.full(m_ref.shape, -jnp.inf, F32)
    l_ref[...] = jnp.zeros(l_ref.shape, F32)
    acc_ref[...] = jnp.zeros(acc_ref.shape, F32)

    def block(key_start, n_keys, row0, n_rows, masked):
        rows = pl.ds(row0, n_rows)
        k = k_ref[0, pl.ds(key_start, n_keys), :]
        v = v_ref[0, pl.ds(key_start, n_keys), :]
        s = lax.dot_general(q_ref2[rows, :], k, (((1,), (1,)), ((), ())), preferred_element_type=F32)
        if masked:
            row = lax.broadcasted_iota(I32, (n_rows, n_keys), 0) & (half - 1)
            col = lax.broadcasted_iota(I32, (n_rows, n_keys), 1)
            s = jnp.where(col <= row, s, -jnp.inf)
        tiles = [s[:, c * LANES:(c + 1) * LANES] for c in range(n_keys // LANES)]
        m_prev = m_ref[rows, :]
        m_new = jnp.maximum(m_prev, jnp.max(functools.reduce(jnp.maximum, tiles), axis=-1, keepdims=True))
        alpha = jnp.exp2(m_prev - m_new)
        p_tiles = [jnp.exp2(t - m_new) for t in tiles]
        l_ref[rows, :] = alpha * l_ref[rows, :] + functools.reduce(jnp.add, p_tiles)
        p = jnp.concatenate(p_tiles, axis=1).astype(BF16)
        acc_ref[rows, :] = alpha * acc_ref[rows, :] + jnp.dot(p, v, preferred_element_type=F32)
        m_ref[rows, :] = m_new

    def full_block(ki):
        block(pl.multiple_of(ki * blk, blk), blk, 0, 2 * blk, False)

    def pair(j, carry):
        full_block(2 * j)
        full_block(2 * j + 1)
        return carry

    lax.fori_loop(0, qi // 2, pair, 0)

    @pl.when(qi % 2 == 1)
    def _():
        full_block(qi - 1)

    diag = pl.multiple_of(qi * blk, blk)
    block(diag, half, 0, 2 * half, True)
    block(diag, half, 2 * half, 2 * half, False)
    block(diag + half, half, 2 * half, 2 * half, True)

    lp = lam_ref[...]
    lam = (jnp.exp(jnp.sum(lp[0:1] * lp[1:2], axis=-1, keepdims=True))
           - jnp.exp(jnp.sum(lp[2:3] * lp[3:4], axis=-1, keepdims=True)) + lam_init)
    o = acc_ref[...] / jnp.sum(l_ref[...], axis=-1, keepdims=True)
    o = (jnp.concatenate([o[0:half], o[2 * half:3 * half]], axis=0)
         - lam * jnp.concatenate([o[half:2 * half], o[3 * half:4 * half]], axis=0))
    o = _rms(o, g_ref[...]) * (1.0 - lam_init)
    o_ref[0] = o.astype(o_ref.dtype)


def diff_attention(proj, lam_params, subln_g, n_heads, lam_init, blk=1024):
    b, s, _ = proj.shape
    blk = _pick(s, blk)
    assert blk & (blk - 1) == 0, "the causal mask uses power-of-two half blocks"
    hw = 2 * HEAD_DIM
    kernel = functools.partial(_diff_attn_body, blk=blk, lam_init=lam_init)
    return pl.pallas_call(
        kernel,
        grid=(b, n_heads, s // blk),
        in_specs=[pl.BlockSpec((1, blk, hw), lambda bi, h, qi: (bi, qi, h)),
                  pl.BlockSpec((1, s, hw), lambda bi, h, qi: (bi, 0, n_heads + h)),
                  pl.BlockSpec((1, s, hw), lambda bi, h, qi: (bi, 0, 2 * n_heads + h)),
                  pl.BlockSpec((4, HEAD_DIM), lambda bi, h, qi: (0, 0)),
                  pl.BlockSpec((1, hw), lambda bi, h, qi: (0, 0))],
        out_specs=pl.BlockSpec((1, blk, hw), lambda bi, h, qi: (bi, qi, h)),
        out_shape=jax.ShapeDtypeStruct((b, s, n_heads * hw), BF16),
        scratch_shapes=[pltpu.VMEM((2 * blk, hw), BF16), pltpu.VMEM((2 * blk, LANES), F32),
                        pltpu.VMEM((2 * blk, LANES), F32), pltpu.VMEM((2 * blk, hw), F32)],
        compiler_params=_params(3),
        name="diff_attention",
    )(proj, proj, proj, lam_params, subln_g.reshape(1, hw))


def _swa_body(sink_ref, q_ref, kc_ref, kp_ref, vc_ref, vp_ref, o_ref, *, tq, kv_heads, group):
    t = pl.program_id(1)
    w = ATTN_BLOCK
    row = lax.broadcasted_iota(I32, (group * w, 2 * w), 0) & (w - 1)
    col = lax.broadcasted_iota(I32, (group * w, 2 * w), 1)
    band = (col > row) & (col <= row + w)
    band_first = band & ((col >= w) | (t > 0))
    low_half = lax.broadcasted_iota(I32, (w, LANES), 1) < HEAD_DIM
    for j in range(tq // w):
        rows = slice(j * w, (j + 1) * w)
        if j == 0:
            k2 = jnp.concatenate([kp_ref[0], kc_ref[0, rows, :]], axis=0)
            v2 = jnp.concatenate([vp_ref[0], vc_ref[0, rows, :]], axis=0)
            mask = band_first
        else:
            k2 = kc_ref[0, (j - 1) * w:(j + 1) * w, :]
            v2 = vc_ref[0, (j - 1) * w:(j + 1) * w, :]
            mask = band
        outs = []
        for g in range(kv_heads):
            kg = k2[:, g * HEAD_DIM:(g + 1) * HEAD_DIM]
            vg = v2[:, g * HEAD_DIM:(g + 1) * HEAD_DIM]
            kdup = jnp.concatenate([kg, kg], axis=1)
            vdup = jnp.concatenate([vg, vg], axis=1)
            heads = [g * group + i for i in range(group)]
            q_tiles = []
            for h in heads:
                q = q_ref[0, rows, (h // 2) * LANES:(h // 2 + 1) * LANES]
                q = (q.astype(F32) * (HEAD_DIM ** -0.5)).astype(BF16)
                own_half = low_half if h % 2 == 0 else jnp.logical_not(low_half)
                q_tiles.append(jnp.where(own_half, q, jnp.zeros_like(q)))
            sink = jnp.concatenate([jnp.full((w, LANES), sink_ref[h], F32) for h in heads], axis=0)
            s = lax.dot_general(jnp.concatenate(q_tiles, axis=0), kdup, (((1,), (1,)), ((), ())),
                                preferred_element_type=F32)
            s = jnp.where(mask, s, -jnp.inf)
            s0, s1 = s[:, :w], s[:, w:]
            m = jnp.maximum(jnp.max(jnp.maximum(s0, s1), axis=-1, keepdims=True), sink)
            e0 = jnp.exp(s0 - m)
            e1 = jnp.exp(s1 - m)
            denom = jnp.sum(e0 + e1, axis=-1, keepdims=True) + jnp.exp(sink - m)
            e = jnp.concatenate([e0, e1], axis=1).astype(BF16)
            o = jnp.dot(e, vdup, preferred_element_type=F32) / denom
            outs.extend(o[i * w:(i + 1) * w, :] for i in range(group))
        tiles = [jnp.where(low_half, outs[h], outs[h + 1]) for h in range(0, len(outs), 2)]
        o_ref[0, rows, :] = jnp.concatenate(tiles, axis=1).astype(o_ref.dtype)


def sliding_window_attention(proj, sinks, q_col, k_col, v_col, kv_heads, group, tq=512):
    b, s, _ = proj.shape
    tq = _pick(s, tq)
    qw = kv_heads * group * HEAD_DIM
    kw = kv_heads * HEAD_DIM
    sub = tq // ATTN_BLOCK
    kernel = functools.partial(_swa_body, tq=tq, kv_heads=kv_heads, group=group)
    prev = lambda bi, t: (bi, jnp.maximum(t * sub - 1, 0), 0)
    return pl.pallas_call(
        kernel,
        grid=(b, s // tq),
        in_specs=[pl.BlockSpec(memory_space=pltpu.SMEM),
                  pl.BlockSpec((1, tq, qw), lambda bi, t: (bi, t, q_col // qw)),
                  pl.BlockSpec((1, tq, kw), lambda bi, t: (bi, t, k_col // kw)),
                  pl.BlockSpec((1, ATTN_BLOCK, kw), lambda bi, t: prev(bi, t)[:2] + (k_col // kw,)),
                  pl.BlockSpec((1, tq, kw), lambda bi, t: (bi, t, v_col // kw)),
                  pl.BlockSpec((1, ATTN_BLOCK, kw), lambda bi, t: prev(bi, t)[:2] + (v_col // kw,))],
        out_specs=pl.BlockSpec((1, tq, qw), lambda bi, t: (bi, t, 0)),
        out_shape=jax.ShapeDtypeStruct((b, s, qw), BF16),
        compiler_params=_params(2),
        name="swa_attention",
    )(sinks, proj, proj, proj, proj, proj)


def _outproj_body(x_ref, a_ref, b_ref, w_ref, o_ref):
    mix = jnp.concatenate([a_ref[...], b_ref[...]], axis=1)
    o_ref[...] = x_ref[...] + jnp.dot(mix, w_ref[...], preferred_element_type=F32)


def out_projection(x, oa, ob, w, tm=512):
    t, d = x.shape
    tm = _pick(t, tm)
    return pl.pallas_call(
        _outproj_body,
        grid=(t // tm,),
        in_specs=[pl.BlockSpec((tm, d), lambda i: (i, 0)),
                  pl.BlockSpec((tm, oa.shape[1]), lambda i: (i, 0)),
                  pl.BlockSpec((tm, ob.shape[1]), lambda i: (i, 0)),
                  pl.BlockSpec(w.shape, lambda i: (0, 0))],
        out_specs=pl.BlockSpec((tm, d), lambda i: (i, 0)),
        out_shape=jax.ShapeDtypeStruct((t, d), F32),
        compiler_params=_params(1),
        name="out_projection",
    )(x, oa, ob, w)


def _ffn_body(x_ref, g_ref, wg_ref, wu_ref, wd_ref, o_ref, h_ref, acc_ref):
    j = pl.program_id(1)

    @pl.when(j == 0)
    def _():
        h_ref[...] = _rms(x_ref[...], g_ref[...]).astype(BF16)
        acc_ref[...] = jnp.zeros(acc_ref.shape, F32)

    h = h_ref[...]
    a = jnp.dot(h, wg_ref[...], preferred_element_type=F32)
    u = jnp.dot(h, wu_ref[...], preferred_element_type=F32)
    act = (a * jax.nn.sigmoid(a) * u).astype(BF16)
    acc_ref[...] += jnp.dot(act, wd_ref[...], preferred_element_type=F32)

    @pl.when(j == pl.num_programs(1) - 1)
    def _():
        o_ref[...] = x_ref[...] + acc_ref[...]


def ffn_swiglu(x, g, wg, wu, wd, tm=512, tf=1408):
    t, d = x.shape
    f = wg.shape[1]
    tm = _pick(t, tm)
    tf = _pick(f, tf)
    return pl.pallas_call(
        _ffn_body,
        grid=(t // tm, f // tf),
        in_specs=[pl.BlockSpec((tm, d), lambda i, j: (i, 0)),
                  pl.BlockSpec((1, d), lambda i, j: (0, 0)),
                  pl.BlockSpec((d, tf), lambda i, j: (0, j)),
                  pl.BlockSpec((d, tf), lambda i, j: (0, j)),
                  pl.BlockSpec((tf, d), lambda i, j: (j, 0))],
        out_specs=pl.BlockSpec((tm, d), lambda i, j: (i, 0)),
        out_shape=jax.ShapeDtypeStruct((t, d), F32),
        scratch_shapes=[pltpu.VMEM((tm, d), BF16), pltpu.VMEM((tm, d), F32)],
        compiler_params=_params(2),
        name="ffn_swiglu",
    )(x, g.reshape(1, d), wg, wu, wd)


def _mix_ffn_body(x_ref, a_ref, b_ref, wo_ref, g_ref, wg_ref, wu_ref, wd_ref, o_ref, *, ff_chunk):
    mix = jnp.concatenate([a_ref[...], b_ref[...]], axis=1)
    x1 = x_ref[...] + jnp.dot(mix, wo_ref[...], preferred_element_type=F32)
    h = _rms(x1, g_ref[...]).astype(BF16)
    y = x1
    for c in range(wg_ref.shape[1] // ff_chunk):
        cols = slice(c * ff_chunk, (c + 1) * ff_chunk)
        a = jnp.dot(h, wg_ref[:, cols], preferred_element_type=F32)
        u = jnp.dot(h, wu_ref[:, cols], preferred_element_type=F32)
        act = (a * jax.nn.sigmoid(a) * u).astype(BF16)
        y = y + jnp.dot(act, wd_ref[cols, :], preferred_element_type=F32)
    o_ref[...] = y


def mix_ffn(x, oa, ob, wo, g, wg, wu, wd, tm=512, ff_chunk=1408):
    t, d = x.shape
    f = wg.shape[1]
    tm = _pick(t, tm)
    ff_chunk = _pick(f, ff_chunk)
    resident = lambda shape: pl.BlockSpec(shape, lambda i: (0, 0), pipeline_mode=pl.Buffered(1))
    return pl.pallas_call(
        functools.partial(_mix_ffn_body, ff_chunk=ff_chunk),
        grid=(t // tm,),
        in_specs=[pl.BlockSpec((tm, d), lambda i: (i, 0)),
                  pl.BlockSpec((tm, oa.shape[1]), lambda i: (i, 0)),
                  pl.BlockSpec((tm, ob.shape[1]), lambda i: (i, 0)),
                  resident(wo.shape), resident((1, d)),
                  resident(wg.shape), resident(wu.shape), resident(wd.shape)],
        out_specs=pl.BlockSpec((tm, d), lambda i: (i, 0)),
        out_shape=jax.ShapeDtypeStruct((t, d), F32),
        compiler_params=_params(1),
        name="mix_ffn",
    )(x, oa, ob, wo, g.reshape(1, d), wg, wu, wd)


def _conv_body(x_ref, g_ref, w1_ref, b1_ref, wdw_ref, bdw_ref, lng_ref, lnb_ref, w2_ref, b2_ref,
               o_ref, u_ref, v_ref, wb_ref, c_ref, *, tm, rows_per_chunk):
    d = x_ref.shape[-1]
    t = pl.program_id(1)
    sublanes = wb_ref.shape[1]

    @pl.when(t == 0)
    def _():
        u_ref[:, 0:CONV_HIST, :] = jnp.zeros((d // LANES, CONV_HIST, LANES), F32)
        wb_ref[...] = jnp.broadcast_to(wdw_ref[...][:, None, :], wb_ref.shape)

    h = _rms(x_ref[0], g_ref[...]).astype(BF16)
    z = jnp.dot(h, w1_ref[...], preferred_element_type=F32) + b1_ref[...]
    u = z[:, :d] * jax.nn.sigmoid(z[:, d:])
    n_slabs = d // LANES
    for c in range(n_slabs):
        u_ref[c, CONV_HIST:CONV_HIST + tm, :] = u[:, c * LANES:(c + 1) * LANES]

    first_tap = CONV_HIST - (CONV_WIDTH - 1)
    conv_rows = 8 * sublanes

    for c in range(n_slabs):
        lanes = slice(c * LANES, (c + 1) * LANES)
        w = [wb_ref[j, :, lanes] for j in range(CONV_WIDTH)]
        bias = jnp.zeros((sublanes, LANES), F32) + bdw_ref[:, lanes]

        def rows_block(i, carry, c=c, lanes=lanes, w=w, bias=bias):
            r0 = pl.multiple_of(i * conv_rows, conv_rows)
            for r in range(conv_rows // sublanes):
                sums = [bias, None]
                for j in range(CONV_WIDTH):
                    tap = u_ref[c, pl.ds(r0 + (first_tap + j + r * sublanes), sublanes, stride=1), :] * w[j]
                    sums[j % 2] = tap if sums[j % 2] is None else sums[j % 2] + tap
                c_ref[pl.ds(r0 + r * sublanes, sublanes), lanes] = sums[0] + sums[1]
            return carry

        lax.fori_loop(0, tm // conv_rows, rows_block, 0)

    def chunk(i, carry):
        r0 = pl.multiple_of(i * rows_per_chunk, rows_per_chunk)
        acc = c_ref[pl.ds(r0, rows_per_chunk), :]
        mu = jnp.mean(acc, axis=-1, keepdims=True)
        xc = acc - mu
        var = jnp.mean(xc * xc, axis=-1, keepdims=True)
        y = xc * lax.rsqrt(var + LN_EPS) * lng_ref[...] + lnb_ref[...]
        v_ref[pl.ds(r0, rows_per_chunk), :] = (y * jax.nn.sigmoid(y)).astype(BF16)
        return carry

    lax.fori_loop(0, tm // rows_per_chunk, chunk, 0)
    u_ref[:, 0:CONV_HIST, :] = u_ref[:, tm:tm + CONV_HIST, :]
    o_ref[0] = x_ref[0] + jnp.dot(v_ref[...], w2_ref[...], preferred_element_type=F32) + b2_ref[...]


def conformer_conv(x, g, w1, b1, wdw, bdw, lng, lnb, w2, b2, tm=512, rows_per_chunk=128):
    b, s, d = x.shape
    tm = _pick(s, tm)
    kernel = functools.partial(_conv_body, tm=tm, rows_per_chunk=rows_per_chunk)
    vec = lambda n: pl.BlockSpec((1, n), lambda bi, t: (0, 0))
    return pl.pallas_call(
        kernel,
        grid=(b, s // tm),
        in_specs=[pl.BlockSpec((1, tm, d), lambda bi, t: (bi, t, 0)),
                  vec(d),
                  pl.BlockSpec((d, 2 * d), lambda bi, t: (0, 0)),
                  vec(2 * d),
                  pl.BlockSpec((CONV_WIDTH, d), lambda bi, t: (0, 0)),
                  vec(d), vec(d), vec(d),
                  pl.BlockSpec((d, d), lambda bi, t: (0, 0)),
                  vec(d)],
        out_specs=pl.BlockSpec((1, tm, d), lambda bi, t: (bi, t, 0)),
        out_shape=jax.ShapeDtypeStruct((b, s, d), F32),
        scratch_shapes=[pltpu.VMEM((d // LANES, tm + CONV_HIST, LANES), F32), pltpu.VMEM((tm, d), BF16),
                        pltpu.VMEM((CONV_WIDTH, F32_SUBLANES, d), F32), pltpu.VMEM((tm, d), F32)],
        compiler_params=_params(2),
        name="conformer_conv",
    )(x, g.reshape(1, d), w1, b1.reshape(1, 2 * d), wdw, bdw.reshape(1, d), lng.reshape(1, d),
      lnb.reshape(1, d), w2, b2.reshape(1, d))


def _router_body(x_ref, g_ref, wr_ref, h_ref, meta_ref, counts_ref, *, n_experts):
    @pl.when(pl.program_id(0) == 0)
    def _():
        counts_ref[...] = jnp.zeros(counts_ref.shape, F32)

    h = _rms(x_ref[...], g_ref[...])
    _store_slabs(h_ref, 0, h)
    logits = jnp.dot(h.astype(BF16), wr_ref[...], preferred_element_type=F32)
    lane = lax.broadcasted_iota(I32, logits.shape, 1).astype(F32)
    neg = jnp.full_like(logits, -jnp.inf)
    far = jnp.full_like(logits, float(LANES))
    l1 = jnp.where(lane < n_experts, logits, neg)
    m1 = jnp.max(l1, axis=-1, keepdims=True)
    i1 = jnp.min(jnp.where(l1 == m1, lane, far), axis=-1, keepdims=True)
    l2 = jnp.where(lane == i1, neg, l1)
    m2 = jnp.max(l2, axis=-1, keepdims=True)
    i2 = jnp.min(jnp.where(l2 == m2, lane, far), axis=-1, keepdims=True)
    e2 = jnp.exp(m2 - m1)
    w1 = 1.0 / (1.0 + e2)
    w2 = e2 / (1.0 + e2)
    tm = logits.shape[0]
    hot1 = (lane == i1).astype(F32)
    hot2 = (lane == i2).astype(F32)
    both = (hot1 + hot2).astype(BF16)
    earlier = (lax.broadcasted_iota(I32, (tm, tm), 1) < lax.broadcasted_iota(I32, (tm, tm), 0)).astype(BF16)
    before = jnp.dot(earlier, both, preferred_element_type=F32) + counts_ref[...]
    r1 = jnp.sum(hot1 * before, axis=-1, keepdims=True)
    r2 = jnp.sum(hot2 * before, axis=-1, keepdims=True)
    counts_ref[...] += jnp.sum(hot1 + hot2, axis=0, keepdims=True)
    zero = jnp.zeros_like(logits)
    meta_ref[...] = jnp.where(lane == 0, i1,
                    jnp.where(lane == 1, i2,
                    jnp.where(lane == 2, w1,
                    jnp.where(lane == 3, w2,
                    jnp.where(lane == 4, r1, jnp.where(lane == 5, r2, zero))))))


def moe_router(x, g, wr, n_experts, tm=512):
    t, d = x.shape
    tm = _pick(t, tm)
    kernel = functools.partial(_router_body, n_experts=n_experts)
    return pl.pallas_call(
        kernel,
        grid=(t // tm,),
        in_specs=[pl.BlockSpec((tm, d), lambda i: (i, 0)),
                  pl.BlockSpec((1, d), lambda i: (0, 0)),
                  pl.BlockSpec((d, LANES), lambda i: (0, 0))],
        out_specs=[pl.BlockSpec((tm * (d // LANES), LANES), lambda i: (i, 0)),
                   pl.BlockSpec((tm, LANES), lambda i: (i, 0)),
                   pl.BlockSpec((1, LANES), lambda i: (0, 0))],
        out_shape=[jax.ShapeDtypeStruct((t * (d // LANES), LANES), F32),
                   jax.ShapeDtypeStruct((t, LANES), F32),
                   jax.ShapeDtypeStruct((1, LANES), F32)],
        compiler_params=_params(1),
        name="moe_router",
    )(x, g.reshape(1, d), wr)


GATHER_UNROLL = 8


def _store_slabs(ref, first_row, value):
    n, d = value.shape
    n_slabs = d // LANES
    for c in range(n_slabs):
        ref[pl.ds(first_row * n_slabs + c, n, stride=n_slabs), :] = value[:, c * LANES:(c + 1) * LANES]


def _load_slabs(ref, first_row, n, d):
    n_slabs = d // LANES
    return jnp.concatenate([ref[pl.ds(first_row * n_slabs + c, n, stride=n_slabs), :]
                            for c in range(n_slabs)], axis=1)


def _dispatch_body(pad_ref, tn_ref, d_ref, h_hbm, xs_hbm, zero_ref, buf_ref, in_sem, sem, zsem,
                   *, tm, n_slabs, n_experts, n_sub):
    i = pl.program_id(0)
    n = pl.num_programs(0)
    rows = tm * n_slabs
    n_ring = buf_ref.shape[0]

    def stage(tile):
        return pltpu.make_async_copy(h_hbm.at[pl.ds(pl.multiple_of(tile * rows, rows), rows), :],
                                     buf_ref.at[tile % n_ring], in_sem.at[tile % n_ring])

    def tile_copy(slot):
        return pltpu.make_async_copy(buf_ref.at[0], xs_hbm.at[pl.ds(0, rows), :], sem.at[slot])

    @pl.when(i == 0)
    def _():
        stage(0).start()

    @pl.when(i + 1 < n)
    def _():
        stage(i + 1).start()

    stage(i).wait()
    slot = i % 2
    ring = i % n_ring
    for rank in range(TOP_K):
        def body(it, carry, rank=rank):
            base = it * GATHER_UNROLL
            dsts = [pl.multiple_of(d_ref[0, 0, rank * tm + base + k], n_slabs) for k in range(GATHER_UNROLL)]
            for k in range(GATHER_UNROLL):
                src = pl.multiple_of((base + k) * n_slabs, n_slabs)
                pltpu.make_async_copy(buf_ref.at[ring, pl.ds(src, n_slabs), :],
                                      xs_hbm.at[pl.ds(dsts[k], n_slabs), :], sem.at[slot]).start()
            return carry
        lax.fori_loop(0, tm // GATHER_UNROLL, body, 0)

    @pl.when(i == 0)
    def _():
        zero_ref[...] = jnp.zeros(zero_ref.shape, F32)
        zero_row = zero_ref.at[pl.ds(0, n_slabs), :]
        sub_rows = zero_ref.shape[0]

        def unused_sub_tiles(fn):
            def tbody(tile, carry):
                for s in range(n_sub):
                    @pl.when(s >= tn_ref[tile])
                    def _():
                        dst = pl.multiple_of((tile * n_sub + s) * sub_rows, sub_rows)
                        fn(pltpu.make_async_copy(zero_ref, xs_hbm.at[pl.ds(dst, sub_rows), :], zsem))
                return carry
            lax.fori_loop(0, tn_ref.shape[0], tbody, 0)

        def tail_rows(fn):
            for e in range(n_experts):
                def zbody(r, carry, e=e):
                    dst = pl.multiple_of((pad_ref[0, e] + r) * n_slabs, n_slabs)
                    fn(pltpu.make_async_copy(zero_row, xs_hbm.at[pl.ds(dst, n_slabs), :], zsem))
                    return carry
                lax.fori_loop(0, pad_ref[1, e], zbody, 0)

        unused_sub_tiles(lambda cp: cp.start())
        tail_rows(lambda cp: cp.start())
        unused_sub_tiles(lambda cp: cp.wait())
        tail_rows(lambda cp: cp.wait())

    @pl.when(i > 0)
    def _():
        for rank in range(TOP_K):
            tile_copy(1 - slot).wait()

    @pl.when(i == n - 1)
    def _():
        for rank in range(TOP_K):
            tile_copy(slot).wait()


def moe_dispatch(h, dest, pad, tile_count, sub, n_sub, n_slabs):
    n_tok_tiles = dest.shape[0]
    tm = dest.shape[2] // TOP_K
    n_experts = pad.shape[1]
    n_rows = tile_count.shape[0] * n_sub * sub
    grid_spec = pltpu.PrefetchScalarGridSpec(
        num_scalar_prefetch=2,
        grid=(n_tok_tiles,),
        in_specs=[pl.BlockSpec((1, 1, TOP_K * tm), lambda i, pad, tn: (i, 0, 0), memory_space=pltpu.SMEM),
                  pl.BlockSpec(memory_space=pl.ANY)],
        out_specs=pl.BlockSpec(memory_space=pl.ANY),
        scratch_shapes=[pltpu.VMEM((sub * n_slabs, LANES), F32), pltpu.VMEM((3, tm * n_slabs, LANES), F32),
                        pltpu.SemaphoreType.DMA((3,)), pltpu.SemaphoreType.DMA((2,)),
                        pltpu.SemaphoreType.DMA(())],
    )
    return pl.pallas_call(
        functools.partial(_dispatch_body, tm=tm, n_slabs=n_slabs, n_experts=n_experts, n_sub=n_sub),
        grid_spec=grid_spec,
        out_shape=jax.ShapeDtypeStruct((n_rows * n_slabs, LANES), F32),
        compiler_params=_params(1),
        name="moe_dispatch",
    )(pad, tile_count, dest, h)


def _moe_body(te_ref, tn_ref, xs_ref, wg_ref, wu_ref, wd_ref, y_hbm,
              y_ref, xb_ref, wgb_ref, wub_ref, wdb_ref, stage_ref, zero_ref, osem, zsem, *, sub, n_sub):
    i = pl.program_id(0)
    j = pl.program_id(1)
    last = j == pl.num_programs(1) - 1
    count = tn_ref[i]
    d = wg_ref.shape[2]
    slab_rows = stage_ref.shape[1]

    @pl.when((i == 0) & (j == 0))
    def _():
        zero_ref[...] = jnp.zeros(zero_ref.shape, F32)

    for s in range(n_sub):
        rows = pl.ds(s * sub, sub)

        @pl.when((j == 0) & (s < count))
        def _():
            xb_ref[rows, :] = _load_slabs(xs_ref, s * sub, sub, d).astype(BF16)
            y_ref[rows, :] = jnp.zeros((sub, d), F32)

        @pl.when(s < count)
        def _():
            if s == 0:
                wg, wu, wd = (r[0, 0].astype(BF16) for r in (wg_ref, wu_ref, wd_ref))
                wgb_ref[...], wub_ref[...], wdb_ref[...] = wg, wu, wd
            else:
                wg, wu, wd = wgb_ref[...], wub_ref[...], wdb_ref[...]
            x = xb_ref[rows, :]
            a = jnp.dot(x, wg, preferred_element_type=F32)
            u = jnp.dot(x, wu, preferred_element_type=F32)
            act = (a * jax.nn.sigmoid(a) * u).astype(BF16)
            y_ref[rows, :] += jnp.dot(act, wd, preferred_element_type=F32)

    def out_copy(src, s):
        dst = pl.multiple_of((i * n_sub + s) * slab_rows, slab_rows)
        return src, y_hbm.at[pl.ds(dst, slab_rows), :]

    n_tiles = pl.num_programs(0)
    prev_count = tn_ref[jnp.maximum(i - 1, 0)]
    next_count = tn_ref[jnp.minimum(i + 1, n_tiles - 1)]
    final_tile = (i == n_tiles - 1) | (next_count == 0)

    def stage_copy(s):
        return pltpu.make_async_copy(*out_copy(stage_ref.at[s % 2], s), osem.at[s % 2])

    for slot in range(2):
        @pl.when(last & (count > 0) & (i > 0) & (prev_count > slot))
        def _():
            stage_copy(slot).wait()

    for s in range(n_sub):
        @pl.when(last & (s < count))
        def _():
            if s >= 2:
                stage_copy(s - 2).wait()
            _store_slabs(stage_ref.at[s % 2], 0, y_ref[pl.ds(s * sub, sub), :])
            stage_copy(s).start()

        @pl.when(last & (s >= count))
        def _():
            pltpu.make_async_copy(*out_copy(zero_ref, s), zsem).start()

    for s in range(n_sub):
        @pl.when(last & final_tile & (s < count) & (s + 2 >= count))
        def _():
            stage_copy(s).wait()

        @pl.when(last & (s >= count))
        def _():
            pltpu.make_async_copy(*out_copy(zero_ref, s), zsem).wait()


def moe_experts(xs, tile_expert, tile_count, wg, wu, wd, layer, sub, n_sub, tf=512):
    d = wg.shape[2]
    n_slabs = d // LANES
    n_tiles = tile_expert.shape[0]
    tile_rows = n_sub * sub
    f = wg.shape[3]
    tf = _pick(f, tf)
    nf = f // tf

    def w_col(i, j, te, tn):
        return (layer, te[i], 0, jnp.where(tn[i] > 0, j, nf - 1))

    def w_row(i, j, te, tn):
        return (layer, te[i], jnp.where(tn[i] > 0, j, nf - 1), 0)

    grid_spec = pltpu.PrefetchScalarGridSpec(
        num_scalar_prefetch=2,
        grid=(n_tiles, nf),
        in_specs=[pl.BlockSpec((tile_rows * n_slabs, LANES), lambda i, j, te, tn: (i, 0)),
                  pl.BlockSpec((1, 1, d, tf), w_col),
                  pl.BlockSpec((1, 1, d, tf), w_col),
                  pl.BlockSpec((1, 1, tf, d), w_row)],
        out_specs=pl.BlockSpec(memory_space=pl.ANY),
        scratch_shapes=[pltpu.VMEM((tile_rows, d), F32), pltpu.VMEM((tile_rows, d), BF16),
                        pltpu.VMEM((d, tf), BF16), pltpu.VMEM((d, tf), BF16), pltpu.VMEM((tf, d), BF16),
                        pltpu.VMEM((2, sub * n_slabs, LANES), F32), pltpu.VMEM((sub * n_slabs, LANES), F32),
                        pltpu.SemaphoreType.DMA((2,)), pltpu.SemaphoreType.DMA(())],
    )
    return pl.pallas_call(
        functools.partial(_moe_body, sub=sub, n_sub=n_sub),
        grid_spec=grid_spec,
        out_shape=jax.ShapeDtypeStruct((n_tiles * tile_rows * n_slabs, LANES), F32),
        compiler_params=_params(2),
        name="moe_experts",
    )(tile_expert, tile_count, xs, wg, wu, wd)


def _combine_body(d_cur_ref, d_nxt_ref, x_ref, meta_ref, g_ref, y_hbm, o_ref, buf_ref, sem,
                  *, tm, final_norm):
    i = pl.program_id(0)
    n = pl.num_programs(0)
    d = x_ref.shape[1]
    n_slabs = d // LANES

    def gather(d_ref, slot):
        def body(it, carry):
            base = it * GATHER_UNROLL
            srcs = [pl.multiple_of(d_ref[0, 0, base + k], n_slabs) for k in range(GATHER_UNROLL)]
            for k in range(GATHER_UNROLL):
                dst = pl.multiple_of((base + k) * n_slabs, n_slabs)
                pltpu.make_async_copy(y_hbm.at[pl.ds(srcs[k], n_slabs), :],
                                      buf_ref.at[slot, pl.ds(dst, n_slabs), :], sem.at[slot]).start()
            return carry
        lax.fori_loop(0, TOP_K * tm // GATHER_UNROLL, body, 0)

    @pl.when(i == 0)
    def _():
        gather(d_cur_ref, 0)

    @pl.when(i + 1 < n)
    def _():
        gather(d_nxt_ref, (i + 1) % 2)

    slot = i % 2
    pltpu.make_async_copy(y_hbm.at[pl.ds(0, TOP_K * tm * n_slabs), :], buf_ref.at[slot],
                          sem.at[slot]).wait()
    meta = meta_ref[...]
    out = (x_ref[...] + meta[:, 2:3] * _load_slabs(buf_ref.at[slot], 0, tm, d)
           + meta[:, 3:4] * _load_slabs(buf_ref.at[slot], tm, tm, d))
    if final_norm:
        out = _rms(out, g_ref[...])
    o_ref[...] = out


def moe_combine(x, meta, dest, y, g, final_norm, tm=256):
    t, d = x.shape
    n = t // tm
    n_slabs = d // LANES
    smem = lambda fn: pl.BlockSpec((1, 1, TOP_K * tm), fn, memory_space=pltpu.SMEM)
    return pl.pallas_call(
        functools.partial(_combine_body, tm=tm, final_norm=final_norm),
        grid=(n,),
        in_specs=[smem(lambda i: (i, 0, 0)),
                  smem(lambda i: (jnp.minimum(i + 1, n - 1), 0, 0)),
                  pl.BlockSpec((tm, d), lambda i: (i, 0)),
                  pl.BlockSpec((tm, LANES), lambda i: (i, 0)),
                  pl.BlockSpec((1, d), lambda i: (0, 0)),
                  pl.BlockSpec(memory_space=pl.ANY)],
        out_specs=pl.BlockSpec((tm, d), lambda i: (i, 0)),
        out_shape=jax.ShapeDtypeStruct((t, d), F32),
        scratch_shapes=[pltpu.VMEM((2, TOP_K * tm * n_slabs, LANES), F32), pltpu.SemaphoreType.DMA((2,))],
        compiler_params=_params(1),
        name="moe_combine",
    )(dest, dest, x, meta, g.reshape(1, d), y)


def _route(meta, counts, n_experts, sub, n_sub, tm_combine):
    t = meta.shape[0]
    tile_rows = sub * n_sub
    picks = meta[:, :3 * TOP_K].astype(I32)
    flat_e = picks[:, :TOP_K].reshape(-1)
    rank = picks[:, 2 * TOP_K:].reshape(-1)
    counts = counts[0, :n_experts].astype(I32)
    subs = (counts + sub - 1) // sub
    tiles = (subs + n_sub - 1) // n_sub
    tile_ends = jnp.cumsum(tiles)
    tile_starts = tile_ends - tiles
    dest = tile_starts[flat_e] * tile_rows + rank
    n_tiles = ((t * TOP_K) // sub + n_experts + n_sub - 1) // n_sub + n_experts
    idx = jnp.arange(n_tiles, dtype=I32)
    tile_expert = jnp.minimum(jnp.searchsorted(tile_ends, idx, side="right"), n_experts - 1).astype(I32)
    local = idx - tile_starts[tile_expert]
    tile_count = jnp.clip(subs[tile_expert] - local * n_sub, 0, n_sub).astype(I32)
    dest_tiles = dest.reshape(t // tm_combine, tm_combine, TOP_K).transpose(0, 2, 1).reshape(
        t // tm_combine, 1, TOP_K * tm_combine)
    pad = jnp.stack([tile_starts * tile_rows + counts, subs * sub - counts]).astype(I32)
    return tile_expert, tile_count, dest_tiles, pad, n_tiles


def moe_layer(x, g, wr, wg, wu, wd, layer, final_g, final_norm, sub=512, n_sub=4, tm_combine=256):
    t, d = x.shape
    n_experts = wr.shape[1]
    tm_combine = min(tm_combine, t)
    wr_pad = jnp.zeros((d, LANES), BF16).at[:, :n_experts].set(wr.astype(BF16))
    h, meta, counts = moe_router(x, g, wr_pad, n_experts)
    n_slabs = d // LANES
    tile_expert, tile_count, dest_tiles, pad, n_tiles = _route(meta, counts, n_experts, sub, n_sub,
                                                               tm_combine)
    slab_dest = dest_tiles * n_slabs
    xs = moe_dispatch(h, slab_dest, pad, tile_count, sub, n_sub, n_slabs)
    y = moe_experts(xs, tile_expert, tile_count, wg, wu, wd, layer, sub, n_sub)
    return moe_combine(x, meta, slab_dest, y, final_g, final_norm, tm_combine)


def attention_mixers(x, norm_g, w_in, lam_params, subln_g, sinks, layer):
    b, s, d = x.shape
    diff_heads = d // (4 * HEAD_DIM)
    swa_q_heads = d // (2 * HEAD_DIM)
    swa_kv_heads = max(1, swa_q_heads // 4)
    group = swa_q_heads // swa_kv_heads
    a_width = diff_heads * 2 * HEAD_DIM
    lam_init = 0.8 - 0.6 * math.exp(-0.3 * layer)
    x2 = x.reshape(b * s, d)
    proj = rms_matmul(x2, norm_g, w_in).reshape(b, s, -1)
    oa = diff_attention(proj, lam_params, subln_g, diff_heads, lam_init)
    q_col = 3 * a_width
    k_col = q_col + swa_q_heads * HEAD_DIM
    v_col = k_col + swa_kv_heads * HEAD_DIM
    ob = sliding_window_attention(proj, sinks, q_col, k_col, v_col, swa_kv_heads, group)
    return oa.reshape(b * s, -1), ob.reshape(b * s, -1)


def attention_layer(x, norm_g, w_in, lam_params, subln_g, sinks, w_out, layer):
    b, s, d = x.shape
    oa, ob = attention_mixers(x, norm_g, w_in, lam_params, subln_g, sinks, layer)
    return out_projection(x.reshape(b * s, d), oa, ob, w_out)


def kernel(x, attn_norm_g, w_in_att, diff_lambda, diff_subln_g, attn_sinks, w_out_att, ffn_norm_g, w_ffn_gate, w_ffn_up, w_ffn_down, conv_norm_g, w_pw1, b_pw1, w_dw, b_dw, conv_ln_g, conv_ln_b, w_pw2, b_pw2, moe_norm_g, w_router, w_exp_gate, w_exp_up, w_exp_down, final_norm_g):
    b, s, d = x.shape
    depth = attn_norm_g.shape[0] + conv_norm_g.shape[0]
    assert depth % 2 == 0, "the final RMSNorm is fused into the last expert layer"
    bf = lambda w: w.astype(BF16)
    for layer in range(depth):
        i = layer // 2
        if layer % 2 == 0:
            oa, ob = attention_mixers(x, attn_norm_g[i], bf(w_in_att[i]), diff_lambda[i], diff_subln_g[i],
                                      attn_sinks[i], layer)
            x2 = mix_ffn(x.reshape(b * s, d), oa, ob, bf(w_out_att[i]), ffn_norm_g[i],
                         bf(w_ffn_gate[i]), bf(w_ffn_up[i]), bf(w_ffn_down[i]))
            x = x2.reshape(b, s, d)
        else:
            x = conformer_conv(x, conv_norm_g[i], bf(w_pw1[i]), b_pw1[i], w_dw[i], b_dw[i],
                               conv_ln_g[i], conv_ln_b[i], bf(w_pw2[i]), b_pw2[i])
            x2 = moe_layer(x.reshape(b * s, d), moe_norm_g[i], w_router[i], w_exp_gate,
                           w_exp_up, w_exp_down, i, final_norm_g,
                           final_norm=(layer == depth - 1))
            x = x2.reshape(b, s, d)
    return x
```

```python
import functools
import math

import jax
import jax.numpy as jnp
from jax import lax
from jax.experimental import pallas as pl
from jax.experimental.pallas import tpu as pltpu

BF16 = jnp.bfloat16
F32 = jnp.float32
I32 = jnp.int32

LOG2_E = 1.4426950408889634
RMS_EPS = 1e-6
LN_EPS = 1e-5
HEAD_DIM = 64
ATTN_BLOCK = 128
CONV_WIDTH = 31
TOP_K = 2
LANES = 128
F32_SUBLANES = 8
CONV_HIST = 32
V7X_VMEM_LIMIT = 56 * 1024 * 1024


def _params(n_axes, flags=None):
    return pltpu.CompilerParams(dimension_semantics=("arbitrary",) * n_axes,
                                vmem_limit_bytes=V7X_VMEM_LIMIT, flags=flags)


def _rms(x, g):
    return x * lax.rsqrt(jnp.mean(x * x, axis=-1, keepdims=True) + RMS_EPS) * g


def _pick(n, pref):
    t = min(n, pref)
    while n % t:
        t -= LANES if t > LANES else 8
    return t


def _rms_matmul_body(x_ref, g_ref, w_ref, o_ref):
    h = _rms(x_ref[...], g_ref[...]).astype(BF16)
    o_ref[...] = jnp.dot(h, w_ref[...], preferred_element_type=F32).astype(o_ref.dtype)


def rms_matmul(x, g, w, tm=512):
    t, d = x.shape
    n = w.shape[1]
    tm = _pick(t, tm)
    return pl.pallas_call(
        _rms_matmul_body,
        grid=(t // tm,),
        in_specs=[pl.BlockSpec((tm, d), lambda i: (i, 0)),
                  pl.BlockSpec((1, d), lambda i: (0, 0)),
                  pl.BlockSpec((d, n), lambda i: (0, 0))],
        out_specs=pl.BlockSpec((tm, n), lambda i: (i, 0)),
        out_shape=jax.ShapeDtypeStruct((t, n), BF16),
        compiler_params=_params(1),
        name="rms_inproj",
    )(x, g.reshape(1, d), w)


def _diff_attn_body(q_ref, k_ref, v_ref, lam_ref, g_ref, o_ref, q_ref2, m_ref, l_ref, acc_ref,
                    *, blk, lam_init):
    qi = pl.program_id(2)
    hw = 2 * HEAD_DIM
    half = blk // 2
    lane = lax.broadcasted_iota(I32, (half, hw), 1)
    q = (q_ref[0].astype(F32) * (HEAD_DIM ** -0.5 * LOG2_E)).astype(BF16)
    zero = jnp.zeros((half, hw), BF16)
    q_ref2[...] = jnp.concatenate(
        [jnp.where(keep, q[r:r + half], zero)
         for r in (0, half) for keep in (lane < HEAD_DIM, lane >= HEAD_DIM)], axis=0)

    m_ref[...] = jnp.full(m_ref.shape, -jnp.inf, F32)
    l_ref[...] = jnp.zeros(l_ref.shape, F32)
    acc_ref[...] = jnp.zeros(acc_ref.shape, F32)

    def block(key_start, n_keys, row0, n_rows, masked):
        rows = pl.ds(row0, n_rows)
        k = k_ref[0, pl.ds(key_start, n_keys), :]
        v = v_ref[0, pl.ds(key_start, n_keys), :]
        s = lax.dot_general(q_ref2[rows, :], k, (((1,), (1,)), ((), ())), preferred_element_type=F32)
        if masked:
            row = lax.broadcasted_iota(I32, (n_rows, n_keys), 0) & (half - 1)
            col = lax.broadcasted_iota(I32, (n_rows, n_keys), 1)
            s = jnp.where(col <= row, s, -jnp.inf)
        tiles = [s[:, c * LANES:(c + 1) * LANES] for c in range(n_keys // LANES)]
        m_prev = m_ref[rows, :]
        m_new = jnp.maximum(m_prev, jnp.max(functools.reduce(jnp.maximum, tiles), axis=-1, keepdims=True))
        alpha = jnp.exp2(m_prev - m_new)
        p_tiles = [jnp.exp2(t - m_new) for t in tiles]
        l_ref[rows, :] = alpha * l_ref[rows, :] + functools.reduce(jnp.add, p_tiles)
        p = jnp.concatenate(p_tiles, axis=1).astype(BF16)
        acc_ref[rows, :] = alpha * acc_ref[rows, :] + jnp.dot(p, v, preferred_element_type=F32)
        m_ref[rows, :] = m_new

    def full_block(ki):
        block(pl.multiple_of(ki * blk, blk), blk, 0, 2 * blk, False)

    def pair(j, carry):
        full_block(2 * j)
        full_block(2 * j + 1)
        return carry

    lax.fori_loop(0, qi // 2, pair, 0)

    @pl.when(qi % 2 == 1)
    def _():
        full_block(qi - 1)

    diag = pl.multiple_of(qi * blk, blk)
    block(diag, half, 0, 2 * half, True)
    block(diag, half, 2 * half, 2 * half, False)
    block(diag + half, half, 2 * half, 2 * half, True)

    lp = lam_ref[...]
    lam = (jnp.exp(jnp.sum(lp[0:1] * lp[1:2], axis=-1, keepdims=True))
           - jnp.exp(jnp.sum(lp[2:3] * lp[3:4], axis=-1, keepdims=True)) + lam_init)
    o = acc_ref[...] / jnp.sum(l_ref[...], axis=-1, keepdims=True)
    o = (jnp.concatenate([o[0:half], o[2 * half:3 * half]], axis=0)
         - lam * jnp.concatenate([o[half:2 * half], o[3 * half:4 * half]], axis=0))
    o = _rms(o, g_ref[...]) * (1.0 - lam_init)
    o_ref[0] = o.astype(o_ref.dtype)


def diff_attention(proj, lam_params, subln_g, n_heads, lam_init, blk=1024):
    b, s, _ = proj.shape
    blk = _pick(s, blk)
    assert blk & (blk - 1) == 0, "the causal mask uses power-of-two half blocks"
    hw = 2 * HEAD_DIM
    kernel = functools.partial(_diff_attn_body, blk=blk, lam_init=lam_init)
    return pl.pallas_call(
        kernel,
        grid=(b, n_heads, s // blk),
        in_specs=[pl.BlockSpec((1, blk, hw), lambda bi, h, qi: (bi, qi, h)),
                  pl.BlockSpec((1, s, hw), lambda bi, h, qi: (bi, 0, n_heads + h)),
                  pl.BlockSpec((1, s, hw), lambda bi, h, qi: (bi, 0, 2 * n_heads + h)),
                  pl.BlockSpec((4, HEAD_DIM), lambda bi, h, qi: (0, 0)),
                  pl.BlockSpec((1, hw), lambda bi, h, qi: (0, 0))],
        out_specs=pl.BlockSpec((1, blk, hw), lambda bi, h, qi: (bi, qi, h)),
        out_shape=jax.ShapeDtypeStruct((b, s, n_heads * hw), BF16),
        scratch_shapes=[pltpu.VMEM((2 * blk, hw), BF16), pltpu.VMEM((2 * blk, LANES), F32),
                        pltpu.VMEM((2 * blk, LANES), F32), pltpu.VMEM((2 * blk, hw), F32)],
        compiler_params=_params(3),
        name="diff_attention",
    )(proj, proj, proj, lam_params, subln_g.reshape(1, hw))


def _swa_body(sink_ref, q_ref, kc_ref, kp_ref, vc_ref, vp_ref, o_ref, *, tq, kv_heads, group):
    t = pl.program_id(1)
    w = ATTN_BLOCK
    row = lax.broadcasted_iota(I32, (group * w, 2 * w), 0) & (w - 1)
    col = lax.broadcasted_iota(I32, (group * w, 2 * w), 1)
    band = (col > row) & (col <= row + w)
    band_first = band & ((col >= w) | (t > 0))
    low_half = lax.broadcasted_iota(I32, (w, LANES), 1) < HEAD_DIM
    for j in range(tq // w):
        rows = slice(j * w, (j + 1) * w)
        if j == 0:
            k2 = jnp.concatenate([kp_ref[0], kc_ref[0, rows, :]], axis=0)
            v2 = jnp.concatenate([vp_ref[0], vc_ref[0, rows, :]], axis=0)
            mask = band_first
        else:
            k2 = kc_ref[0, (j - 1) * w:(j + 1) * w, :]
            v2 = vc_ref[0, (j - 1) * w:(j + 1) * w, :]
            mask = band
        outs = []
        for g in range(kv_heads):
            kg = k2[:, g * HEAD_DIM:(g + 1) * HEAD_DIM]
            vg = v2[:, g * HEAD_DIM:(g + 1) * HEAD_DIM]
            kdup = jnp.concatenate([kg, kg], axis=1)
            vdup = jnp.concatenate([vg, vg], axis=1)
            heads = [g * group + i for i in range(group)]
            q_tiles = []
            for h in heads:
                q = q_ref[0, rows, (h // 2) * LANES:(h // 2 + 1) * LANES]
                q = (q.astype(F32) * (HEAD_DIM ** -0.5)).astype(BF16)
                own_half = low_half if h % 2 == 0 else jnp.logical_not(low_half)
                q_tiles.append(jnp.where(own_half, q, jnp.zeros_like(q)))
            sink = jnp.concatenate([jnp.full((w, LANES), sink_ref[h], F32) for h in heads], axis=0)
            s = lax.dot_general(jnp.concatenate(q_tiles, axis=0), kdup, (((1,), (1,)), ((), ())),
                                preferred_element_type=F32)
            s = jnp.where(mask, s, -jnp.inf)
            s0, s1 = s[:, :w], s[:, w:]
            m = jnp.maximum(jnp.max(jnp.maximum(s0, s1), axis=-1, keepdims=True), sink)
            e0 = jnp.exp(s0 - m)
            e1 = jnp.exp(s1 - m)
            denom = jnp.sum(e0 + e1, axis=-1, keepdims=True) + jnp.exp(sink - m)
            e = jnp.concatenate([e0, e1], axis=1).astype(BF16)
            o = jnp.dot(e, vdup, preferred_element_type=F32) / denom
            outs.extend(o[i * w:(i + 1) * w, :] for i in range(group))
        tiles = [jnp.where(low_half, outs[h], outs[h + 1]) for h in range(0, len(outs), 2)]
        o_ref[0, rows, :] = jnp.concatenate(tiles, axis=1).astype(o_ref.dtype)


def sliding_window_attention(proj, sinks, q_col, k_col, v_col, kv_heads, group, tq=512):
    b, s, _ = proj.shape
    tq = _pick(s, tq)
    qw = kv_heads * group * HEAD_DIM
    kw = kv_heads * HEAD_DIM
    sub = tq // ATTN_BLOCK
    kernel = functools.partial(_swa_body, tq=tq, kv_heads=kv_heads, group=group)
    prev = lambda bi, t: (bi, jnp.maximum(t * sub - 1, 0), 0)
    return pl.pallas_call(
        kernel,
        grid=(b, s // tq),
        in_specs=[pl.BlockSpec(memory_space=pltpu.SMEM),
                  pl.BlockSpec((1, tq, qw), lambda bi, t: (bi, t, q_col // qw)),
                  pl.BlockSpec((1, tq, kw), lambda bi, t: (bi, t, k_col // kw)),
                  pl.BlockSpec((1, ATTN_BLOCK, kw), lambda bi, t: prev(bi, t)[:2] + (k_col // kw,)),
                  pl.BlockSpec((1, tq, kw), lambda bi, t: (bi, t, v_col // kw)),
                  pl.BlockSpec((1, ATTN_BLOCK, kw), lambda bi, t: prev(bi, t)[:2] + (v_col // kw,))],
        out_specs=pl.BlockSpec((1, tq, qw), lambda bi, t: (bi, t, 0)),
        out_shape=jax.ShapeDtypeStruct((b, s, qw), BF16),
        compiler_params=_params(2),
        name="swa_attention",
    )(sinks, proj, proj, proj, proj, proj)


def _outproj_body(x_ref, a_ref, b_ref, w_ref, o_ref):
    mix = jnp.concatenate([a_ref[...], b_ref[...]], axis=1)
    o_ref[...] = x_ref[...] + jnp.dot(mix, w_ref[...], preferred_element_type=F32)


def out_projection(x, oa, ob, w, tm=512):
    t, d = x.shape
    tm = _pick(t, tm)
    return pl.pallas_call(
        _outproj_body,
        grid=(t // tm,),
        in_specs=[pl.BlockSpec((tm, d), lambda i: (i, 0)),
                  pl.BlockSpec((tm, oa.shape[1]), lambda i: (i, 0)),
                  pl.BlockSpec((tm, ob.shape[1]), lambda i: (i, 0)),
                  pl.BlockSpec(w.shape, lambda i: (0, 0))],
        out_specs=pl.BlockSpec((tm, d), lambda i: (i, 0)),
        out_shape=jax.ShapeDtypeStruct((t, d), F32),
        compiler_params=_params(1),
        name="out_projection",
    )(x, oa, ob, w)


def _ffn_body(x_ref, g_ref, wg_ref, wu_ref, wd_ref, o_ref, h_ref, acc_ref):
    j = pl.program_id(1)

    @pl.when(j == 0)
    def _():
        h_ref[...] = _rms(x_ref[...], g_ref[...]).astype(BF16)
        acc_ref[...] = jnp.zeros(acc_ref.shape, F32)

    h = h_ref[...]
    a = jnp.dot(h, wg_ref[...], preferred_element_type=F32)
    u = jnp.dot(h, wu_ref[...], preferred_element_type=F32)
    act = (a * jax.nn.sigmoid(a) * u).astype(BF16)
    acc_ref[...] += jnp.dot(act, wd_ref[...], preferred_element_type=F32)

    @pl.when(j == pl.num_programs(1) - 1)
    def _():
        o_ref[...] = x_ref[...] + acc_ref[...]


def ffn_swiglu(x, g, wg, wu, wd, tm=512, tf=1408):
    t, d = x.shape
    f = wg.shape[1]
    tm = _pick(t, tm)
    tf = _pick(f, tf)
    return pl.pallas_call(
        _ffn_body,
        grid=(t // tm, f // tf),
        in_specs=[pl.BlockSpec((tm, d), lambda i, j: (i, 0)),
                  pl.BlockSpec((1, d), lambda i, j: (0, 0)),
                  pl.BlockSpec((d, tf), lambda i, j: (0, j)),
                  pl.BlockSpec((d, tf), lambda i, j: (0, j)),
                  pl.BlockSpec((tf, d), lambda i, j: (j, 0))],
        out_specs=pl.BlockSpec((tm, d), lambda i, j: (i, 0)),
        out_shape=jax.ShapeDtypeStruct((t, d), F32),
        scratch_shapes=[pltpu.VMEM((tm, d), BF16), pltpu.VMEM((tm, d), F32)],
        compiler_params=_params(2),
        name="ffn_swiglu",
    )(x, g.reshape(1, d), wg, wu, wd)


def _mix_ffn_body(x_ref, a_ref, b_ref, wo_ref, g_ref, wg_ref, wu_ref, wd_ref, o_ref, *, ff_chunk):
    mix = jnp.concatenate([a_ref[...], b_ref[...]], axis=1)
    x1 = x_ref[...] + jnp.dot(mix, wo_ref[...], preferred_element_type=F32)
    h = _rms(x1, g_ref[...]).astype(BF16)
    y = x1
    for c in range(wg_ref.shape[1] // ff_chunk):
        cols = slice(c * ff_chunk, (c + 1) * ff_chunk)
        a = jnp.dot(h, wg_ref[:, cols], preferred_element_type=F32)
        u = jnp.dot(h, wu_ref[:, cols], preferred_element_type=F32)
        act = (a * jax.nn.sigmoid(a) * u).astype(BF16)
        y = y + jnp.dot(act, wd_ref[cols, :], preferred_element_type=F32)
    o_ref[...] = y


def mix_ffn(x, oa, ob, wo, g, wg, wu, wd, tm=512, ff_chunk=1408):
    t, d = x.shape
    f = wg.shape[1]
    tm = _pick(t, tm)
    ff_chunk = _pick(f, ff_chunk)
    resident = lambda shape: pl.BlockSpec(shape, lambda i: (0, 0), pipeline_mode=pl.Buffered(1))
    return pl.pallas_call(
        functools.partial(_mix_ffn_body, ff_chunk=ff_chunk),
        grid=(t // tm,),
        in_specs=[pl.BlockSpec((tm, d), lambda i: (i, 0)),
                  pl.BlockSpec((tm, oa.shape[1]), lambda i: (i, 0)),
                  pl.BlockSpec((tm, ob.shape[1]), lambda i: (i, 0)),
                  resident(wo.shape), resident((1, d)),
                  resident(wg.shape), resident(wu.shape), resident(wd.shape)],
        out_specs=pl.BlockSpec((tm, d), lambda i: (i, 0)),
        out_shape=jax.ShapeDtypeStruct((t, d), F32),
        compiler_params=_params(1),
        name="mix_ffn",
    )(x, oa, ob, wo, g.reshape(1, d), wg, wu, wd)


def _conv_body(x_ref, g_ref, w1_ref, b1_ref, wdw_ref, bdw_ref, lng_ref, lnb_ref, w2_ref, b2_ref,
               o_ref, u_ref, v_ref, wb_ref, c_ref, *, tm, rows_per_chunk):
    d = x_ref.shape[-1]
    t = pl.program_id(1)
    sublanes = wb_ref.shape[1]

    @pl.when(t == 0)
    def _():
        u_ref[:, 0:CONV_HIST, :] = jnp.zeros((d // LANES, CONV_HIST, LANES), F32)
        wb_ref[...] = jnp.broadcast_to(wdw_ref[...][:, None, :], wb_ref.shape)

    h = _rms(x_ref[0], g_ref[...]).astype(BF16)
    z = jnp.dot(h, w1_ref[...], preferred_element_type=F32) + b1_ref[...]
    u = z[:, :d] * jax.nn.sigmoid(z[:, d:])
    n_slabs = d // LANES
    for c in range(n_slabs):
        u_ref[c, CONV_HIST:CONV_HIST + tm, :] = u[:, c * LANES:(c + 1) * LANES]

    first_tap = CONV_HIST - (CONV_WIDTH - 1)
    conv_rows = 8 * sublanes

    for c in range(n_slabs):
        lanes = slice(c * LANES, (c + 1) * LANES)
        w = [wb_ref[j, :, lanes] for j in range(CONV_WIDTH)]
        bias = jnp.zeros((sublanes, LANES), F32) + bdw_ref[:, lanes]

        def rows_block(i, carry, c=c, lanes=lanes, w=w, bias=bias):
            r0 = pl.multiple_of(i * conv_rows, conv_rows)
            for r in range(conv_rows // sublanes):
                sums = [bias, None]
                for j in range(CONV_WIDTH):
                    tap = u_ref[c, pl.ds(r0 + (first_tap + j + r * sublanes), sublanes, stride=1), :] * w[j]
                    sums[j % 2] = tap if sums[j % 2] is None else sums[j % 2] + tap
                c_ref[pl.ds(r0 + r * sublanes, sublanes), lanes] = sums[0] + sums[1]
            return carry

        lax.fori_loop(0, tm // conv_rows, rows_block, 0)

    def chunk(i, carry):
        r0 = pl.multiple_of(i * rows_per_chunk, rows_per_chunk)
        acc = c_ref[pl.ds(r0, rows_per_chunk), :]
        mu = jnp.mean(acc, axis=-1, keepdims=True)
        xc = acc - mu
        var = jnp.mean(xc * xc, axis=-1, keepdims=True)
        y = xc * lax.rsqrt(var + LN_EPS) * lng_ref[...] + lnb_ref[...]
        v_ref[pl.ds(r0, rows_per_chunk), :] = (y * jax.nn.sigmoid(y)).astype(BF16)
        return carry

    lax.fori_loop(0, tm // rows_per_chunk, chunk, 0)
    u_ref[:, 0:CONV_HIST, :] = u_ref[:, tm:tm + CONV_HIST, :]
    o_ref[0] = x_ref[0] + jnp.dot(v_ref[...], w2_ref[...], preferred_element_type=F32) + b2_ref[...]


def conformer_conv(x, g, w1, b1, wdw, bdw, lng, lnb, w2, b2, tm=512, rows_per_chunk=128):
    b, s, d = x.shape
    tm = _pick(s, tm)
    kernel = functools.partial(_conv_body, tm=tm, rows_per_chunk=rows_per_chunk)
    vec = lambda n: pl.BlockSpec((1, n), lambda bi, t: (0, 0))
    return pl.pallas_call(
        kernel,
        grid=(b, s // tm),
        in_specs=[pl.BlockSpec((1, tm, d), lambda bi, t: (bi, t, 0)),
                  vec(d),
                  pl.BlockSpec((d, 2 * d), lambda bi, t: (0, 0)),
                  vec(2 * d),
                  pl.BlockSpec((CONV_WIDTH, d), lambda bi, t: (0, 0)),
                  vec(d), vec(d), vec(d),
                  pl.BlockSpec((d, d), lambda bi, t: (0, 0)),
                  vec(d)],
        out_specs=pl.BlockSpec((1, tm, d), lambda bi, t: (bi, t, 0)),
        out_shape=jax.ShapeDtypeStruct((b, s, d), F32),
        scratch_shapes=[pltpu.VMEM((d // LANES, tm + CONV_HIST, LANES), F32), pltpu.VMEM((tm, d), BF16),
                        pltpu.VMEM((CONV_WIDTH, F32_SUBLANES, d), F32), pltpu.VMEM((tm, d), F32)],
        compiler_params=_params(2),
        name="conformer_conv",
    )(x, g.reshape(1, d), w1, b1.reshape(1, 2 * d), wdw, bdw.reshape(1, d), lng.reshape(1, d),
      lnb.reshape(1, d), w2, b2.reshape(1, d))


def _router_body(x_ref, g_ref, wr_ref, h_ref, meta_ref, picks_ref, counts_ref, *, n_experts):
    @pl.when(pl.program_id(0) == 0)
    def _():
        counts_ref[...] = jnp.zeros(counts_ref.shape, F32)

    h = _rms(x_ref[...], g_ref[...])
    _store_slabs(h_ref, 0, h)
    logits = jnp.dot(h.astype(BF16), wr_ref[...], preferred_element_type=F32)
    lane = lax.broadcasted_iota(I32, logits.shape, 1).astype(F32)
    neg = jnp.full_like(logits, -jnp.inf)
    far = jnp.full_like(logits, float(LANES))
    l1 = jnp.where(lane < n_experts, logits, neg)
    m1 = jnp.max(l1, axis=-1, keepdims=True)
    i1 = jnp.min(jnp.where(l1 == m1, lane, far), axis=-1, keepdims=True)
    l2 = jnp.where(lane == i1, neg, l1)
    m2 = jnp.max(l2, axis=-1, keepdims=True)
    i2 = jnp.min(jnp.where(l2 == m2, lane, far), axis=-1, keepdims=True)
    e2 = jnp.exp(m2 - m1)
    w1 = 1.0 / (1.0 + e2)
    w2 = e2 / (1.0 + e2)
    tm = logits.shape[0]
    hot1 = (lane == i1).astype(F32)
    hot2 = (lane == i2).astype(F32)
    both = (hot1 + hot2).astype(BF16)
    earlier = (lax.broadcasted_iota(I32, (tm, tm), 1) < lax.broadcasted_iota(I32, (tm, tm), 0)).astype(BF16)
    before = jnp.dot(earlier, both, preferred_element_type=F32) + counts_ref[...]
    r1 = jnp.sum(hot1 * before, axis=-1, keepdims=True)
    r2 = jnp.sum(hot2 * before, axis=-1, keepdims=True)
    counts_ref[...] += jnp.sum(hot1 + hot2, axis=0, keepdims=True)
    zero = jnp.zeros_like(logits)
    meta = jnp.where(lane == 0, i1,
           jnp.where(lane == 1, i2,
           jnp.where(lane == 2, w1,
           jnp.where(lane == 3, w2,
           jnp.where(lane == 4, r1, jnp.where(lane == 5, r2, zero))))))
    meta_ref[...] = meta
    picks_ref[...] = jnp.transpose(meta)[0:picks_ref.shape[0], :]


def moe_router(x, g, wr, n_experts, tm=512):
    t, d = x.shape
    tm = _pick(t, tm)
    kernel = functools.partial(_router_body, n_experts=n_experts)
    return pl.pallas_call(
        kernel,
        grid=(t // tm,),
        in_specs=[pl.BlockSpec((tm, d), lambda i: (i, 0)),
                  pl.BlockSpec((1, d), lambda i: (0, 0)),
                  pl.BlockSpec((d, LANES), lambda i: (0, 0))],
        out_specs=[pl.BlockSpec((tm * (d // LANES), LANES), lambda i: (i, 0)),
                   pl.BlockSpec((tm, LANES), lambda i: (i, 0)),
                   pl.BlockSpec((F32_SUBLANES, tm), lambda i: (0, i)),
                   pl.BlockSpec((1, LANES), lambda i: (0, 0))],
        out_shape=[jax.ShapeDtypeStruct((t * (d // LANES), LANES), F32),
                   jax.ShapeDtypeStruct((t, LANES), F32),
                   jax.ShapeDtypeStruct((F32_SUBLANES, t), F32),
                   jax.ShapeDtypeStruct((1, LANES), F32)],
        compiler_params=_params(1),
        name="moe_router",
    )(x, g.reshape(1, d), wr)


GATHER_UNROLL = 8


def _store_slabs(ref, first_row, value):
    n, d = value.shape
    n_slabs = d // LANES
    for c in range(n_slabs):
        ref[pl.ds(first_row * n_slabs + c, n, stride=n_slabs), :] = value[:, c * LANES:(c + 1) * LANES]


def _load_slabs(ref, first_row, n, d):
    n_slabs = d // LANES
    return jnp.concatenate([ref[pl.ds(first_row * n_slabs + c, n, stride=n_slabs), :]
                            for c in range(n_slabs)], axis=1)


def _dispatch_body(pad_ref, tn_ref, d_ref, h_hbm, xs_hbm, zero_ref, buf_ref, in_sem, sem, zsem,
                   *, tm, n_slabs, n_experts, n_sub):
    i = pl.program_id(0)
    n = pl.num_programs(0)
    rows = tm * n_slabs
    n_ring = buf_ref.shape[0]

    def stage(tile):
        return pltpu.make_async_copy(h_hbm.at[pl.ds(pl.multiple_of(tile * rows, rows), rows), :],
                                     buf_ref.at[tile % n_ring], in_sem.at[tile % n_ring])

    def tile_copy(slot):
        return pltpu.make_async_copy(buf_ref.at[0], xs_hbm.at[pl.ds(0, rows), :], sem.at[slot])

    @pl.when(i == 0)
    def _():
        stage(0).start()

    @pl.when(i + 1 < n)
    def _():
        stage(i + 1).start()

    stage(i).wait()
    slot = i % 2
    ring = i % n_ring
    for rank in range(TOP_K):
        def body(it, carry, rank=rank):
            base = it * GATHER_UNROLL
            dsts = [pl.multiple_of(d_ref[0, 0, rank * tm + base + k], n_slabs) for k in range(GATHER_UNROLL)]
            for k in range(GATHER_UNROLL):
                src = pl.multiple_of((base + k) * n_slabs, n_slabs)
                pltpu.make_async_copy(buf_ref.at[ring, pl.ds(src, n_slabs), :],
                                      xs_hbm.at[pl.ds(dsts[k], n_slabs), :], sem.at[slot]).start()
            return carry
        lax.fori_loop(0, tm // GATHER_UNROLL, body, 0)

    @pl.when(i == 0)
    def _():
        zero_ref[...] = jnp.zeros(zero_ref.shape, F32)
        zero_row = zero_ref.at[pl.ds(0, n_slabs), :]
        sub_rows = zero_ref.shape[0]

        def unused_sub_tiles(fn):
            def tbody(tile, carry):
                for s in range(n_sub):
                    @pl.when(s >= tn_ref[tile])
                    def _():
                        dst = pl.multiple_of((tile * n_sub + s) * sub_rows, sub_rows)
                        fn(pltpu.make_async_copy(zero_ref, xs_hbm.at[pl.ds(dst, sub_rows), :], zsem))
                return carry
            lax.fori_loop(0, tn_ref.shape[0], tbody, 0)

        def tail_rows(fn):
            for e in range(n_experts):
                def zbody(r, carry, e=e):
                    dst = pl.multiple_of((pad_ref[0, e] + r) * n_slabs, n_slabs)
                    fn(pltpu.make_async_copy(zero_row, xs_hbm.at[pl.ds(dst, n_slabs), :], zsem))
                    return carry
                lax.fori_loop(0, pad_ref[1, e], zbody, 0)

        unused_sub_tiles(lambda cp: cp.start())
        tail_rows(lambda cp: cp.start())
        unused_sub_tiles(lambda cp: cp.wait())
        tail_rows(lambda cp: cp.wait())

    @pl.when(i > 0)
    def _():
        for rank in range(TOP_K):
            tile_copy(1 - slot).wait()

    @pl.when(i == n - 1)
    def _():
        for rank in range(TOP_K):
            tile_copy(slot).wait()


def moe_dispatch(h, dest, pad, tile_count, sub, n_sub, n_slabs):
    n_tok_tiles = dest.shape[0]
    tm = dest.shape[2] // TOP_K
    n_experts = pad.shape[1]
    n_rows = tile_count.shape[0] * n_sub * sub
    grid_spec = pltpu.PrefetchScalarGridSpec(
        num_scalar_prefetch=2,
        grid=(n_tok_tiles,),
        in_specs=[pl.BlockSpec((1, 1, TOP_K * tm), lambda i, pad, tn: (i, 0, 0), memory_space=pltpu.SMEM),
                  pl.BlockSpec(memory_space=pl.ANY)],
        out_specs=pl.BlockSpec(memory_space=pl.ANY),
        scratch_shapes=[pltpu.VMEM((sub * n_slabs, LANES), F32), pltpu.VMEM((3, tm * n_slabs, LANES), F32),
                        pltpu.SemaphoreType.DMA((3,)), pltpu.SemaphoreType.DMA((2,)),
                        pltpu.SemaphoreType.DMA(())],
    )
    return pl.pallas_call(
        functools.partial(_dispatch_body, tm=tm, n_slabs=n_slabs, n_experts=n_experts, n_sub=n_sub),
        grid_spec=grid_spec,
        out_shape=jax.ShapeDtypeStruct((n_rows * n_slabs, LANES), F32),
        compiler_params=_params(1),
        name="moe_dispatch",
    )(pad, tile_count, dest, h)


def _moe_body(te_ref, tn_ref, xs_ref, wg_ref, wu_ref, wd_ref, y_hbm,
              y_ref, xb_ref, wgb_ref, wub_ref, wdb_ref, stage_ref, zero_ref, osem, zsem, *, sub, n_sub):
    i = pl.program_id(0)
    j = pl.program_id(1)
    last = j == pl.num_programs(1) - 1
    count = tn_ref[i]
    d = wg_ref.shape[2]
    slab_rows = stage_ref.shape[1]

    @pl.when((i == 0) & (j == 0))
    def _():
        zero_ref[...] = jnp.zeros(zero_ref.shape, F32)

    for s in range(n_sub):
        rows = pl.ds(s * sub, sub)

        @pl.when((j == 0) & (s < count))
        def _():
            xb_ref[rows, :] = _load_slabs(xs_ref, s * sub, sub, d).astype(BF16)
            y_ref[rows, :] = jnp.zeros((sub, d), F32)

    def expert_rows(s, wg, wu, wd):
        rows = pl.ds(s * sub, sub)
        x = xb_ref[rows, :]
        a = jnp.dot(x, wg, preferred_element_type=F32)
        u = jnp.dot(x, wu, preferred_element_type=F32)
        act = (a * jax.nn.sigmoid(a) * u).astype(BF16)
        y_ref[rows, :] += jnp.dot(act, wd, preferred_element_type=F32)

    def rounded_weights():
        return tuple(r[0, 0].astype(BF16) for r in (wg_ref, wu_ref, wd_ref))

    @pl.when(count == n_sub)
    def _():
        weights = rounded_weights()
        for s in range(n_sub):
            expert_rows(s, *weights)

    for s in range(n_sub - 1):
        @pl.when((s < count) & (count < n_sub))
        def _():
            if s == 0:
                wgb_ref[...], wub_ref[...], wdb_ref[...] = rounded_weights()
            expert_rows(s, wgb_ref[...], wub_ref[...], wdb_ref[...])

    def out_copy(src, s):
        dst = pl.multiple_of((i * n_sub + s) * slab_rows, slab_rows)
        return src, y_hbm.at[pl.ds(dst, slab_rows), :]

    n_tiles = pl.num_programs(0)
    prev_count = tn_ref[jnp.maximum(i - 1, 0)]
    next_count = tn_ref[jnp.minimum(i + 1, n_tiles - 1)]
    final_tile = (i == n_tiles - 1) | (next_count == 0)

    def stage_copy(s):
        return pltpu.make_async_copy(*out_copy(stage_ref.at[s % 2], s), osem.at[s % 2])

    for slot in range(2):
        @pl.when(last & (count > 0) & (i > 0) & (prev_count > slot))
        def _():
            stage_copy(slot).wait()

    for s in range(n_sub):
        @pl.when(last & (s < count))
        def _():
            if s >= 2:
                stage_copy(s - 2).wait()
            _store_slabs(stage_ref.at[s % 2], 0, y_ref[pl.ds(s * sub, sub), :])
            stage_copy(s).start()

        @pl.when(last & (s >= count))
        def _():
            pltpu.make_async_copy(*out_copy(zero_ref, s), zsem).start()

    for s in range(n_sub):
        @pl.when(last & final_tile & (s < count) & (s + 2 >= count))
        def _():
            stage_copy(s).wait()

        @pl.when(last & (s >= count))
        def _():
            pltpu.make_async_copy(*out_copy(zero_ref, s), zsem).wait()


def moe_experts(xs, tile_expert, tile_count, wg, wu, wd, layer, sub, n_sub, tf=512):
    d = wg.shape[2]
    n_slabs = d // LANES
    n_tiles = tile_expert.shape[0]
    tile_rows = n_sub * sub
    f = wg.shape[3]
    tf = _pick(f, tf)
    nf = f // tf

    def w_col(i, j, te, tn):
        return (layer, te[i], 0, jnp.where(tn[i] > 0, j, nf - 1))

    def w_row(i, j, te, tn):
        return (layer, te[i], jnp.where(tn[i] > 0, j, nf - 1), 0)

    grid_spec = pltpu.PrefetchScalarGridSpec(
        num_scalar_prefetch=2,
        grid=(n_tiles, nf),
        in_specs=[pl.BlockSpec((tile_rows * n_slabs, LANES), lambda i, j, te, tn: (i, 0)),
                  pl.BlockSpec((1, 1, d, tf), w_col),
                  pl.BlockSpec((1, 1, d, tf), w_col),
                  pl.BlockSpec((1, 1, tf, d), w_row)],
        out_specs=pl.BlockSpec(memory_space=pl.ANY),
        scratch_shapes=[pltpu.VMEM((tile_rows, d), F32), pltpu.VMEM((tile_rows, d), BF16),
                        pltpu.VMEM((d, tf), BF16), pltpu.VMEM((d, tf), BF16), pltpu.VMEM((tf, d), BF16),
                        pltpu.VMEM((2, sub * n_slabs, LANES), F32), pltpu.VMEM((sub * n_slabs, LANES), F32),
                        pltpu.SemaphoreType.DMA((2,)), pltpu.SemaphoreType.DMA(())],
    )
    return pl.pallas_call(
        functools.partial(_moe_body, sub=sub, n_sub=n_sub),
        grid_spec=grid_spec,
        out_shape=jax.ShapeDtypeStruct((n_tiles * tile_rows * n_slabs, LANES), F32),
        compiler_params=_params(2),
        name="moe_experts",
    )(tile_expert, tile_count, xs, wg, wu, wd)


def _combine_body(d_cur_ref, d_nxt_ref, x_ref, meta_ref, g_ref, y_hbm, o_ref, buf_ref, sem,
                  *, tm, final_norm):
    i = pl.program_id(0)
    n = pl.num_programs(0)
    d = x_ref.shape[1]
    n_slabs = d // LANES

    def gather(d_ref, slot):
        def body(it, carry):
            base = it * GATHER_UNROLL
            srcs = [pl.multiple_of(d_ref[0, 0, base + k], n_slabs) for k in range(GATHER_UNROLL)]
            for k in range(GATHER_UNROLL):
                dst = pl.multiple_of((base + k) * n_slabs, n_slabs)
                pltpu.make_async_copy(y_hbm.at[pl.ds(srcs[k], n_slabs), :],
                                      buf_ref.at[slot, pl.ds(dst, n_slabs), :], sem.at[slot]).start()
            return carry
        lax.fori_loop(0, TOP_K * tm // GATHER_UNROLL, body, 0)

    @pl.when(i == 0)
    def _():
        gather(d_cur_ref, 0)

    @pl.when(i + 1 < n)
    def _():
        gather(d_nxt_ref, (i + 1) % 2)

    slot = i % 2
    pltpu.make_async_copy(y_hbm.at[pl.ds(0, TOP_K * tm * n_slabs), :], buf_ref.at[slot],
                          sem.at[slot]).wait()
    meta = meta_ref[...]
    out = (x_ref[...] + meta[:, 2:3] * _load_slabs(buf_ref.at[slot], 0, tm, d)
           + meta[:, 3:4] * _load_slabs(buf_ref.at[slot], tm, tm, d))
    if final_norm:
        out = _rms(out, g_ref[...])
    o_ref[...] = out


def moe_combine(x, meta, dest, y, g, final_norm, tm=256):
    t, d = x.shape
    n = t // tm
    n_slabs = d // LANES
    smem = lambda fn: pl.BlockSpec((1, 1, TOP_K * tm), fn, memory_space=pltpu.SMEM)
    return pl.pallas_call(
        functools.partial(_combine_body, tm=tm, final_norm=final_norm),
        grid=(n,),
        in_specs=[smem(lambda i: (i, 0, 0)),
                  smem(lambda i: (jnp.minimum(i + 1, n - 1), 0, 0)),
                  pl.BlockSpec((tm, d), lambda i: (i, 0)),
                  pl.BlockSpec((tm, LANES), lambda i: (i, 0)),
                  pl.BlockSpec((1, d), lambda i: (0, 0)),
                  pl.BlockSpec(memory_space=pl.ANY)],
        out_specs=pl.BlockSpec((tm, d), lambda i: (i, 0)),
        out_shape=jax.ShapeDtypeStruct((t, d), F32),
        scratch_shapes=[pltpu.VMEM((2, TOP_K * tm * n_slabs, LANES), F32), pltpu.SemaphoreType.DMA((2,))],
        compiler_params=_params(1),
        name="moe_combine",
    )(dest, dest, x, meta, g.reshape(1, d), y)


def _route(picks, counts, n_experts, sub, n_sub, tm_combine):
    t = picks.shape[1]
    tile_rows = sub * n_sub
    expert = picks[0:TOP_K].astype(I32)
    rank = picks[2 * TOP_K:3 * TOP_K].astype(I32)
    counts = counts[0, :n_experts].astype(I32)
    subs = (counts + sub - 1) // sub
    tiles = (subs + n_sub - 1) // n_sub
    tile_ends = jnp.cumsum(tiles)
    tile_starts = tile_ends - tiles
    dest = tile_starts[expert] * tile_rows + rank
    n_tiles = ((t * TOP_K) // sub + n_experts + n_sub - 1) // n_sub + n_experts
    idx = jnp.arange(n_tiles, dtype=I32)
    tile_expert = jnp.minimum(jnp.searchsorted(tile_ends, idx, side="right"), n_experts - 1).astype(I32)
    local = idx - tile_starts[tile_expert]
    tile_count = jnp.clip(subs[tile_expert] - local * n_sub, 0, n_sub).astype(I32)
    dest_tiles = dest.reshape(TOP_K, t // tm_combine, tm_combine).transpose(1, 0, 2).reshape(
        t // tm_combine, 1, TOP_K * tm_combine)
    pad = jnp.stack([tile_starts * tile_rows + counts, subs * sub - counts]).astype(I32)
    return tile_expert, tile_count, dest_tiles, pad, n_tiles


def moe_layer(x, g, wr, wg, wu, wd, layer, final_g, final_norm, sub=512, n_sub=4, tm_combine=256):
    t, d = x.shape
    n_experts = wr.shape[1]
    tm_combine = min(tm_combine, t)
    wr_pad = jnp.zeros((d, LANES), BF16).at[:, :n_experts].set(wr.astype(BF16))
    h, meta, picks, counts = moe_router(x, g, wr_pad, n_experts)
    n_slabs = d // LANES
    tile_expert, tile_count, dest_tiles, pad, n_tiles = _route(picks, counts, n_experts, sub, n_sub,
                                                               tm_combine)
    slab_dest = dest_tiles * n_slabs
    xs = moe_dispatch(h, slab_dest, pad, tile_count, sub, n_sub, n_slabs)
    y = moe_experts(xs, tile_expert, tile_count, wg, wu, wd, layer, sub, n_sub)
    return moe_combine(x, meta, slab_dest, y, final_g, final_norm, tm_combine)


def attention_mixers(x, norm_g, w_in, lam_params, subln_g, sinks, layer):
    b, s, d = x.shape
    diff_heads = d // (4 * HEAD_DIM)
    swa_q_heads = d // (2 * HEAD_DIM)
    swa_kv_heads = max(1, swa_q_heads // 4)
    group = swa_q_heads // swa_kv_heads
    a_width = diff_heads * 2 * HEAD_DIM
    lam_init = 0.8 - 0.6 * math.exp(-0.3 * layer)
    x2 = x.reshape(b * s, d)
    proj = rms_matmul(x2, norm_g, w_in).reshape(b, s, -1)
    oa = diff_attention(proj, lam_params, subln_g, diff_heads, lam_init)
    q_col = 3 * a_width
    k_col = q_col + swa_q_heads * HEAD_DIM
    v_col = k_col + swa_kv_heads * HEAD_DIM
    ob = sliding_window_attention(proj, sinks, q_col, k_col, v_col, swa_kv_heads, group)
    return oa.reshape(b * s, -1), ob.reshape(b * s, -1)


def attention_layer(x, norm_g, w_in, lam_params, subln_g, sinks, w_out, layer):
    b, s, d = x.shape
    oa, ob = attention_mixers(x, norm_g, w_in, lam_params, subln_g, sinks, layer)
    return out_projection(x.reshape(b * s, d), oa, ob, w_out)


def kernel(x, attn_norm_g, w_in_att, diff_lambda, diff_subln_g, attn_sinks, w_out_att, ffn_norm_g, w_ffn_gate, w_ffn_up, w_ffn_down, conv_norm_g, w_pw1, b_pw1, w_dw, b_dw, conv_ln_g, conv_ln_b, w_pw2, b_pw2, moe_norm_g, w_router, w_exp_gate, w_exp_up, w_exp_down, final_norm_g):
    b, s, d = x.shape
    depth = attn_norm_g.shape[0] + conv_norm_g.shape[0]
    assert depth % 2 == 0, "the final RMSNorm is fused into the last expert layer"
    bf = lambda w: w.astype(BF16)
    for layer in range(depth):
        i = layer // 2
        if layer % 2 == 0:
            oa, ob = attention_mixers(x, attn_norm_g[i], bf(w_in_att[i]), diff_lambda[i], diff_subln_g[i],
                                      attn_sinks[i], layer)
            x2 = mix_ffn(x.reshape(b * s, d), oa, ob, bf(w_out_att[i]), ffn_norm_g[i],
                         bf(w_ffn_gate[i]), bf(w_ffn_up[i]), bf(w_ffn_down[i]))
            x = x2.reshape(b, s, d)
        else:
            x = conformer_conv(x, conv_norm_g[i], bf(w_pw1[i]), b_pw1[i], w_dw[i], b_dw[i],
                               conv_ln_g[i], conv_ln_b[i], bf(w_pw2[i]), b_pw2[i])
            x2 = moe_layer(x.reshape(b * s, d), moe_norm_g[i], w_router[i], w_exp_gate,
                           w_exp_up, w_exp_down, i, final_norm_g,
                           final_norm=(layer == depth - 1))
            x = x2.reshape(b, s, d)
    return x
```

```python
import functools
import math

import jax
import jax.numpy as jnp
from jax import lax
from jax.experimental import pallas as pl
from jax.experimental.pallas import tpu as pltpu

BF16 = jnp.bfloat16
F32 = jnp.float32
I32 = jnp.int32

LOG2_E = 1.4426950408889634
RMS_EPS = 1e-6
LN_EPS = 1e-5
HEAD_DIM = 64
ATTN_BLOCK = 128
CONV_WIDTH = 31
TOP_K = 2
LANES = 128
F32_SUBLANES = 8
CONV_HIST = 32
V7X_VMEM_LIMIT = 56 * 1024 * 1024


def _params(n_axes, flags=None):
    return pltpu.CompilerParams(dimension_semantics=("arbitrary",) * n_axes,
                                vmem_limit_bytes=V7X_VMEM_LIMIT, flags=flags)


def _rms(x, g):
    return x * lax.rsqrt(jnp.mean(x * x, axis=-1, keepdims=True) + RMS_EPS) * g


def _pick(n, pref):
    t = min(n, pref)
    while n % t:
        t -= LANES if t > LANES else 8
    return t


def _rms_matmul_body(x_ref, g_ref, w_ref, o_ref):
    h = _rms(x_ref[...], g_ref[...]).astype(BF16)
    o_ref[...] = jnp.dot(h, w_ref[...], preferred_element_type=F32).astype(o_ref.dtype)


def rms_matmul(x, g, w, tm=512):
    t, d = x.shape
    n = w.shape[1]
    tm = _pick(t, tm)
    return pl.pallas_call(
        _rms_matmul_body,
        grid=(t // tm,),
        in_specs=[pl.BlockSpec((tm, d), lambda i: (i, 0)),
                  pl.BlockSpec((1, d), lambda i: (0, 0)),
                  pl.BlockSpec((d, n), lambda i: (0, 0))],
        out_specs=pl.BlockSpec((tm, n), lambda i: (i, 0)),
        out_shape=jax.ShapeDtypeStruct((t, n), BF16),
        compiler_params=_params(1),
        name="rms_inproj",
    )(x, g.reshape(1, d), w)


def _diff_attn_body(q_ref, k_ref, v_ref, lam_ref, g_ref, o_ref, q_ref2, m_ref, l_ref, acc_ref,
                    *, blk, lam_init):
    qi = pl.program_id(2)
    hw = 2 * HEAD_DIM
    half = blk // 2
    lane = lax.broadcasted_iota(I32, (half, hw), 1)
    q = (q_ref[0].astype(F32) * (HEAD_DIM ** -0.5 * LOG2_E)).astype(BF16)
    zero = jnp.zeros((half, hw), BF16)
    q_ref2[...] = jnp.concatenate(
        [jnp.where(keep, q[r:r + half], zero)
         for r in (0, half) for keep in (lane < HEAD_DIM, lane >= HEAD_DIM)], axis=0)

    m_ref[...] = jnp.full(m_ref.shape, -jnp.inf, F32)
    l_ref[...] = jnp.zeros(l_ref.shape, F32)
    acc_ref[...] = jnp.zeros(acc_ref.shape, F32)

    def block(key_start, n_keys, row0, n_rows, masked):
        rows = pl.ds(row0, n_rows)
        k = k_ref[0, pl.ds(key_start, n_keys), :]
        v = v_ref[0, pl.ds(key_start, n_keys), :]
        s = lax.dot_general(q_ref2[rows, :], k, (((1,), (1,)), ((), ())), preferred_element_type=F32)
        if masked:
            row = lax.broadcasted_iota(I32, (n_rows, n_keys), 0) & (half - 1)
            col = lax.broadcasted_iota(I32, (n_rows, n_keys), 1)
            s = jnp.where(col <= row, s, -jnp.inf)
        tiles = [s[:, c * LANES:(c + 1) * LANES] for c in range(n_keys // LANES)]
        m_prev = m_ref[rows, :]
        m_new = jnp.maximum(m_prev, jnp.max(functools.reduce(jnp.maximum, tiles), axis=-1, keepdims=True))
        alpha = jnp.exp2(m_prev - m_new)
        p_tiles = [jnp.exp2(t - m_new) for t in tiles]
        l_ref[rows, :] = alpha * l_ref[rows, :] + functools.reduce(jnp.add, p_tiles)
        p = jnp.concatenate(p_tiles, axis=1).astype(BF16)
        acc_ref[rows, :] = alpha * acc_ref[rows, :] + jnp.dot(p, v, preferred_element_type=F32)
        m_ref[rows, :] = m_new

    def full_block(ki):
        block(pl.multiple_of(ki * blk, blk), blk, 0, 2 * blk, False)

    def pair(j, carry):
        full_block(2 * j)
        full_block(2 * j + 1)
        return carry

    lax.fori_loop(0, qi // 2, pair, 0)

    @pl.when(qi % 2 == 1)
    def _():
        full_block(qi - 1)

    diag = pl.multiple_of(qi * blk, blk)
    block(diag, half, 0, 2 * half, True)
    block(diag, half, 2 * half, 2 * half, False)
    block(diag + half, half, 2 * half, 2 * half, True)

    lp = lam_ref[...]
    lam = (jnp.exp(jnp.sum(lp[0:1] * lp[1:2], axis=-1, keepdims=True))
           - jnp.exp(jnp.sum(lp[2:3] * lp[3:4], axis=-1, keepdims=True)) + lam_init)
    o = acc_ref[...] / jnp.sum(l_ref[...], axis=-1, keepdims=True)
    o = (jnp.concatenate([o[0:half], o[2 * half:3 * half]], axis=0)
         - lam * jnp.concatenate([o[half:2 * half], o[3 * half:4 * half]], axis=0))
    o = _rms(o, g_ref[...]) * (1.0 - lam_init)
    o_ref[0] = o.astype(o_ref.dtype)


def diff_attention(proj, lam_params, subln_g, n_heads, lam_init, blk=1024):
    b, s, _ = proj.shape
    blk = _pick(s, blk)
    assert blk & (blk - 1) == 0, "the causal mask uses power-of-two half blocks"
    hw = 2 * HEAD_DIM
    kernel = functools.partial(_diff_attn_body, blk=blk, lam_init=lam_init)
    return pl.pallas_call(
        kernel,
        grid=(b, n_heads, s // blk),
        in_specs=[pl.BlockSpec((1, blk, hw), lambda bi, h, qi: (bi, qi, h)),
                  pl.BlockSpec((1, s, hw), lambda bi, h, qi: (bi, 0, n_heads + h)),
                  pl.BlockSpec((1, s, hw), lambda bi, h, qi: (bi, 0, 2 * n_heads + h)),
                  pl.BlockSpec((4, HEAD_DIM), lambda bi, h, qi: (0, 0)),
                  pl.BlockSpec((1, hw), lambda bi, h, qi: (0, 0))],
        out_specs=pl.BlockSpec((1, blk, hw), lambda bi, h, qi: (bi, qi, h)),
        out_shape=jax.ShapeDtypeStruct((b, s, n_heads * hw), BF16),
        scratch_shapes=[pltpu.VMEM((2 * blk, hw), BF16), pltpu.VMEM((2 * blk, LANES), F32),
                        pltpu.VMEM((2 * blk, LANES), F32), pltpu.VMEM((2 * blk, hw), F32)],
        compiler_params=_params(3),
        name="diff_attention",
    )(proj, proj, proj, lam_params, subln_g.reshape(1, hw))


def _swa_body(sink_ref, q_ref, kc_ref, kp_ref, vc_ref, vp_ref, o_ref, *, tq, kv_heads, group):
    t = pl.program_id(1)
    w = ATTN_BLOCK
    row = lax.broadcasted_iota(I32, (group * w, 2 * w), 0) & (w - 1)
    col = lax.broadcasted_iota(I32, (group * w, 2 * w), 1)
    band = (col > row) & (col <= row + w)
    band_first = band & ((col >= w) | (t > 0))
    low_half = lax.broadcasted_iota(I32, (w, LANES), 1) < HEAD_DIM
    for j in range(tq // w):
        rows = slice(j * w, (j + 1) * w)
        if j == 0:
            k2 = jnp.concatenate([kp_ref[0], kc_ref[0, rows, :]], axis=0)
            v2 = jnp.concatenate([vp_ref[0], vc_ref[0, rows, :]], axis=0)
            mask = band_first
        else:
            k2 = kc_ref[0, (j - 1) * w:(j + 1) * w, :]
            v2 = vc_ref[0, (j - 1) * w:(j + 1) * w, :]
            mask = band
        outs = []
        for g in range(kv_heads):
            kg = k2[:, g * HEAD_DIM:(g + 1) * HEAD_DIM]
            vg = v2[:, g * HEAD_DIM:(g + 1) * HEAD_DIM]
            kdup = jnp.concatenate([kg, kg], axis=1)
            vdup = jnp.concatenate([vg, vg], axis=1)
            heads = [g * group + i for i in range(group)]
            q_tiles = []
            for h in heads:
                q = q_ref[0, rows, (h // 2) * LANES:(h // 2 + 1) * LANES]
                q = (q.astype(F32) * (HEAD_DIM ** -0.5)).astype(BF16)
                own_half = low_half if h % 2 == 0 else jnp.logical_not(low_half)
                q_tiles.append(jnp.where(own_half, q, jnp.zeros_like(q)))
            sink = jnp.concatenate([jnp.full((w, LANES), sink_ref[h], F32) for h in heads], axis=0)
            s = lax.dot_general(jnp.concatenate(q_tiles, axis=0), kdup, (((1,), (1,)), ((), ())),
                                preferred_element_type=F32)
            s = jnp.where(mask, s, -jnp.inf)
            s0, s1 = s[:, :w], s[:, w:]
            m = jnp.maximum(jnp.max(jnp.maximum(s0, s1), axis=-1, keepdims=True), sink)
            e0 = jnp.exp(s0 - m)
            e1 = jnp.exp(s1 - m)
            denom = jnp.sum(e0 + e1, axis=-1, keepdims=True) + jnp.exp(sink - m)
            e = jnp.concatenate([e0, e1], axis=1).astype(BF16)
            o = jnp.dot(e, vdup, preferred_element_type=F32) / denom
            outs.extend(o[i * w:(i + 1) * w, :] for i in range(group))
        tiles = [jnp.where(low_half, outs[h], outs[h + 1]) for h in range(0, len(outs), 2)]
        o_ref[0, rows, :] = jnp.concatenate(tiles, axis=1).astype(o_ref.dtype)


def sliding_window_attention(proj, sinks, q_col, k_col, v_col, kv_heads, group, tq=512):
    b, s, _ = proj.shape
    tq = _pick(s, tq)
    qw = kv_heads * group * HEAD_DIM
    kw = kv_heads * HEAD_DIM
    sub = tq // ATTN_BLOCK
    kernel = functools.partial(_swa_body, tq=tq, kv_heads=kv_heads, group=group)
    prev = lambda bi, t: (bi, jnp.maximum(t * sub - 1, 0), 0)
    return pl.pallas_call(
        kernel,
        grid=(b, s // tq),
        in_specs=[pl.BlockSpec(memory_space=pltpu.SMEM),
                  pl.BlockSpec((1, tq, qw), lambda bi, t: (bi, t, q_col // qw)),
                  pl.BlockSpec((1, tq, kw), lambda bi, t: (bi, t, k_col // kw)),
                  pl.BlockSpec((1, ATTN_BLOCK, kw), lambda bi, t: prev(bi, t)[:2] + (k_col // kw,)),
                  pl.BlockSpec((1, tq, kw), lambda bi, t: (bi, t, v_col // kw)),
                  pl.BlockSpec((1, ATTN_BLOCK, kw), lambda bi, t: prev(bi, t)[:2] + (v_col // kw,))],
        out_specs=pl.BlockSpec((1, tq, qw), lambda bi, t: (bi, t, 0)),
        out_shape=jax.ShapeDtypeStruct((b, s, qw), BF16),
        compiler_params=_params(2),
        name="swa_attention",
    )(sinks, proj, proj, proj, proj, proj)


def _outproj_body(x_ref, a_ref, b_ref, w_ref, o_ref):
    mix = jnp.concatenate([a_ref[...], b_ref[...]], axis=1)
    o_ref[...] = x_ref[...] + jnp.dot(mix, w_ref[...], preferred_element_type=F32)


def out_projection(x, oa, ob, w, tm=512):
    t, d = x.shape
    tm = _pick(t, tm)
    return pl.pallas_call(
        _outproj_body,
        grid=(t // tm,),
        in_specs=[pl.BlockSpec((tm, d), lambda i: (i, 0)),
                  pl.BlockSpec((tm, oa.shape[1]), lambda i: (i, 0)),
                  pl.BlockSpec((tm, ob.shape[1]), lambda i: (i, 0)),
                  pl.BlockSpec(w.shape, lambda i: (0, 0))],
        out_specs=pl.BlockSpec((tm, d), lambda i: (i, 0)),
        out_shape=jax.ShapeDtypeStruct((t, d), F32),
        compiler_params=_params(1),
        name="out_projection",
    )(x, oa, ob, w)


def _ffn_body(x_ref, g_ref, wg_ref, wu_ref, wd_ref, o_ref, h_ref, acc_ref):
    j = pl.program_id(1)

    @pl.when(j == 0)
    def _():
        h_ref[...] = _rms(x_ref[...], g_ref[...]).astype(BF16)
        acc_ref[...] = jnp.zeros(acc_ref.shape, F32)

    h = h_ref[...]
    a = jnp.dot(h, wg_ref[...], preferred_element_type=F32)
    u = jnp.dot(h, wu_ref[...], preferred_element_type=F32)
    act = (a * jax.nn.sigmoid(a) * u).astype(BF16)
    acc_ref[...] += jnp.dot(act, wd_ref[...], preferred_element_type=F32)

    @pl.when(j == pl.num_programs(1) - 1)
    def _():
        o_ref[...] = x_ref[...] + acc_ref[...]


def ffn_swiglu(x, g, wg, wu, wd, tm=512, tf=1408):
    t, d = x.shape
    f = wg.shape[1]
    tm = _pick(t, tm)
    tf = _pick(f, tf)
    return pl.pallas_call(
        _ffn_body,
        grid=(t // tm, f // tf),
        in_specs=[pl.BlockSpec((tm, d), lambda i, j: (i, 0)),
                  pl.BlockSpec((1, d), lambda i, j: (0, 0)),
                  pl.BlockSpec((d, tf), lambda i, j: (0, j)),
                  pl.BlockSpec((d, tf), lambda i, j: (0, j)),
                  pl.BlockSpec((tf, d), lambda i, j: (j, 0))],
        out_specs=pl.BlockSpec((tm, d), lambda i, j: (i, 0)),
        out_shape=jax.ShapeDtypeStruct((t, d), F32),
        scratch_shapes=[pltpu.VMEM((tm, d), BF16), pltpu.VMEM((tm, d), F32)],
        compiler_params=_params(2),
        name="ffn_swiglu",
    )(x, g.reshape(1, d), wg, wu, wd)


def _mix_ffn_body(x_ref, a_ref, b_ref, wo_ref, g_ref, wg_ref, wu_ref, wd_ref, o_ref, *, ff_chunk):
    mix = jnp.concatenate([a_ref[...], b_ref[...]], axis=1)
    x1 = x_ref[...] + jnp.dot(mix, wo_ref[...], preferred_element_type=F32)
    h = _rms(x1, g_ref[...]).astype(BF16)
    y = x1
    for c in range(wg_ref.shape[1] // ff_chunk):
        cols = slice(c * ff_chunk, (c + 1) * ff_chunk)
        a = jnp.dot(h, wg_ref[:, cols], preferred_element_type=F32)
        u = jnp.dot(h, wu_ref[:, cols], preferred_element_type=F32)
        act = (a * jax.nn.sigmoid(a) * u).astype(BF16)
        y = y + jnp.dot(act, wd_ref[cols, :], preferred_element_type=F32)
    o_ref[...] = y


def mix_ffn(x, oa, ob, wo, g, wg, wu, wd, tm=512, ff_chunk=1408):
    t, d = x.shape
    f = wg.shape[1]
    tm = _pick(t, tm)
    ff_chunk = _pick(f, ff_chunk)
    resident = lambda shape: pl.BlockSpec(shape, lambda i: (0, 0), pipeline_mode=pl.Buffered(1))
    return pl.pallas_call(
        functools.partial(_mix_ffn_body, ff_chunk=ff_chunk),
        grid=(t // tm,),
        in_specs=[pl.BlockSpec((tm, d), lambda i: (i, 0)),
                  pl.BlockSpec((tm, oa.shape[1]), lambda i: (i, 0)),
                  pl.BlockSpec((tm, ob.shape[1]), lambda i: (i, 0)),
                  resident(wo.shape), resident((1, d)),
                  resident(wg.shape), resident(wu.shape), resident(wd.shape)],
        out_specs=pl.BlockSpec((tm, d), lambda i: (i, 0)),
        out_shape=jax.ShapeDtypeStruct((t, d), F32),
        compiler_params=_params(1),
        name="mix_ffn",
    )(x, oa, ob, wo, g.reshape(1, d), wg, wu, wd)


def _conv_body(x_ref, g_ref, w1_ref, b1_ref, wdw_ref, bdw_ref, lng_ref, lnb_ref, w2_ref, b2_ref,
               o_ref, u_ref, v_ref, wb_ref, c_ref, *, tm, rows_per_chunk):
    d = x_ref.shape[-1]
    t = pl.program_id(1)
    sublanes = wb_ref.shape[1]

    @pl.when(t == 0)
    def _():
        u_ref[:, 0:CONV_HIST, :] = jnp.zeros((d // LANES, CONV_HIST, LANES), F32)
        wb_ref[...] = jnp.broadcast_to(wdw_ref[...][:, None, :], wb_ref.shape)

    h = _rms(x_ref[0], g_ref[...]).astype(BF16)
    z = jnp.dot(h, w1_ref[...], preferred_element_type=F32) + b1_ref[...]
    u = z[:, :d] * jax.nn.sigmoid(z[:, d:])
    n_slabs = d // LANES
    for c in range(n_slabs):
        u_ref[c, CONV_HIST:CONV_HIST + tm, :] = u[:, c * LANES:(c + 1) * LANES]

    first_tap = CONV_HIST - (CONV_WIDTH - 1)
    conv_rows = 8 * sublanes

    for c in range(n_slabs):
        lanes = slice(c * LANES, (c + 1) * LANES)
        w = [wb_ref[j, :, lanes] for j in range(CONV_WIDTH)]
        bias = jnp.zeros((sublanes, LANES), F32) + bdw_ref[:, lanes]

        def rows_block(i, carry, c=c, lanes=lanes, w=w, bias=bias):
            r0 = pl.multiple_of(i * conv_rows, conv_rows)
            for r in range(conv_rows // sublanes):
                sums = [bias, None]
                for j in range(CONV_WIDTH):
                    tap = u_ref[c, pl.ds(r0 + (first_tap + j + r * sublanes), sublanes, stride=1), :] * w[j]
                    sums[j % 2] = tap if sums[j % 2] is None else sums[j % 2] + tap
                c_ref[pl.ds(r0 + r * sublanes, sublanes), lanes] = sums[0] + sums[1]
            return carry

        lax.fori_loop(0, tm // conv_rows, rows_block, 0)

    def chunk(i, carry):
        r0 = pl.multiple_of(i * rows_per_chunk, rows_per_chunk)
        acc = c_ref[pl.ds(r0, rows_per_chunk), :]
        mu = jnp.mean(acc, axis=-1, keepdims=True)
        xc = acc - mu
        var = jnp.mean(xc * xc, axis=-1, keepdims=True)
        y = xc * lax.rsqrt(var + LN_EPS) * lng_ref[...] + lnb_ref[...]
        v_ref[pl.ds(r0, rows_per_chunk), :] = (y * jax.nn.sigmoid(y)).astype(BF16)
        return carry

    lax.fori_loop(0, tm // rows_per_chunk, chunk, 0)
    u_ref[:, 0:CONV_HIST, :] = u_ref[:, tm:tm + CONV_HIST, :]
    o_ref[0] = x_ref[0] + jnp.dot(v_ref[...], w2_ref[...], preferred_element_type=F32) + b2_ref[...]


def conformer_conv(x, g, w1, b1, wdw, bdw, lng, lnb, w2, b2, tm=512, rows_per_chunk=128):
    b, s, d = x.shape
    tm = _pick(s, tm)
    kernel = functools.partial(_conv_body, tm=tm, rows_per_chunk=rows_per_chunk)
    vec = lambda n: pl.BlockSpec((1, n), lambda bi, t: (0, 0))
    return pl.pallas_call(
        kernel,
        grid=(b, s // tm),
        in_specs=[pl.BlockSpec((1, tm, d), lambda bi, t: (bi, t, 0)),
                  vec(d),
                  pl.BlockSpec((d, 2 * d), lambda bi, t: (0, 0)),
                  vec(2 * d),
                  pl.BlockSpec((CONV_WIDTH, d), lambda bi, t: (0, 0)),
                  vec(d), vec(d), vec(d),
                  pl.BlockSpec((d, d), lambda bi, t: (0, 0)),
                  vec(d)],
        out_specs=pl.BlockSpec((1, tm, d), lambda bi, t: (bi, t, 0)),
        out_shape=jax.ShapeDtypeStruct((b, s, d), F32),
        scratch_shapes=[pltpu.VMEM((d // LANES, tm + CONV_HIST, LANES), F32), pltpu.VMEM((tm, d), BF16),
                        pltpu.VMEM((CONV_WIDTH, F32_SUBLANES, d), F32), pltpu.VMEM((tm, d), F32)],
        compiler_params=_params(2),
        name="conformer_conv",
    )(x, g.reshape(1, d), w1, b1.reshape(1, 2 * d), wdw, bdw.reshape(1, d), lng.reshape(1, d),
      lnb.reshape(1, d), w2, b2.reshape(1, d))


def _router_body(x_ref, g_ref, wr_ref, h_ref, meta_ref, picks_ref, counts_ref, *, n_experts):
    @pl.when(pl.program_id(0) == 0)
    def _():
        counts_ref[...] = jnp.zeros(counts_ref.shape, F32)

    h = _rms(x_ref[...], g_ref[...])
    _store_slabs(h_ref, 0, h)
    logits = jnp.dot(h.astype(BF16), wr_ref[...], preferred_element_type=F32)
    lane = lax.broadcasted_iota(I32, logits.shape, 1).astype(F32)
    neg = jnp.full_like(logits, -jnp.inf)
    far = jnp.full_like(logits, float(LANES))
    l1 = jnp.where(lane < n_experts, logits, neg)
    m1 = jnp.max(l1, axis=-1, keepdims=True)
    i1 = jnp.min(jnp.where(l1 == m1, lane, far), axis=-1, keepdims=True)
    l2 = jnp.where(lane == i1, neg, l1)
    m2 = jnp.max(l2, axis=-1, keepdims=True)
    i2 = jnp.min(jnp.where(l2 == m2, lane, far), axis=-1, keepdims=True)
    e2 = jnp.exp(m2 - m1)
    w1 = 1.0 / (1.0 + e2)
    w2 = e2 / (1.0 + e2)
    tm = logits.shape[0]
    hot1 = (lane == i1).astype(F32)
    hot2 = (lane == i2).astype(F32)
    both = (hot1 + hot2).astype(BF16)
    earlier = (lax.broadcasted_iota(I32, (tm, tm), 1) < lax.broadcasted_iota(I32, (tm, tm), 0)).astype(BF16)
    before = jnp.dot(earlier, both, preferred_element_type=F32) + counts_ref[...]
    r1 = jnp.sum(hot1 * before, axis=-1, keepdims=True)
    r2 = jnp.sum(hot2 * before, axis=-1, keepdims=True)
    counts_ref[...] += jnp.sum(hot1 + hot2, axis=0, keepdims=True)
    zero = jnp.zeros_like(logits)
    meta = jnp.where(lane == 0, i1,
           jnp.where(lane == 1, i2,
           jnp.where(lane == 2, w1,
           jnp.where(lane == 3, w2,
           jnp.where(lane == 4, r1, jnp.where(lane == 5, r2, zero))))))
    meta_ref[...] = meta
    picks_ref[...] = jnp.transpose(meta)[0:picks_ref.shape[0], :]


def moe_router(x, g, wr, n_experts, tm=512):
    t, d = x.shape
    tm = _pick(t, tm)
    kernel = functools.partial(_router_body, n_experts=n_experts)
    return pl.pallas_call(
        kernel,
        grid=(t // tm,),
        in_specs=[pl.BlockSpec((tm, d), lambda i: (i, 0)),
                  pl.BlockSpec((1, d), lambda i: (0, 0)),
                  pl.BlockSpec((d, LANES), lambda i: (0, 0))],
        out_specs=[pl.BlockSpec((tm * (d // LANES), LANES), lambda i: (i, 0)),
                   pl.BlockSpec((tm, LANES), lambda i: (i, 0)),
                   pl.BlockSpec((F32_SUBLANES, tm), lambda i: (0, i)),
                   pl.BlockSpec((1, LANES), lambda i: (0, 0))],
        out_shape=[jax.ShapeDtypeStruct((t * (d // LANES), LANES), F32),
                   jax.ShapeDtypeStruct((t, LANES), F32),
                   jax.ShapeDtypeStruct((F32_SUBLANES, t), F32),
                   jax.ShapeDtypeStruct((1, LANES), F32)],
        compiler_params=_params(1),
        name="moe_router",
    )(x, g.reshape(1, d), wr)


GATHER_UNROLL = 8


def _store_slabs(ref, first_row, value):
    n, d = value.shape
    n_slabs = d // LANES
    for c in range(n_slabs):
        ref[pl.ds(first_row * n_slabs + c, n, stride=n_slabs), :] = value[:, c * LANES:(c + 1) * LANES]


def _load_slabs(ref, first_row, n, d):
    n_slabs = d // LANES
    return jnp.concatenate([ref[pl.ds(first_row * n_slabs + c, n, stride=n_slabs), :]
                            for c in range(n_slabs)], axis=1)


def _dispatch_body(pad_ref, tn_ref, d_ref, h_hbm, xs_hbm, zero_ref, buf_ref, in_sem, sem, zsem,
                   *, tm, n_slabs, n_experts, n_sub):
    i = pl.program_id(0)
    n = pl.num_programs(0)
    rows = tm * n_slabs
    n_ring = buf_ref.shape[0]

    def stage(tile):
        return pltpu.make_async_copy(h_hbm.at[pl.ds(pl.multiple_of(tile * rows, rows), rows), :],
                                     buf_ref.at[tile % n_ring], in_sem.at[tile % n_ring])

    def tile_copy(slot):
        return pltpu.make_async_copy(buf_ref.at[0], xs_hbm.at[pl.ds(0, rows), :], sem.at[slot])

    @pl.when(i == 0)
    def _():
        stage(0).start()

    @pl.when(i + 1 < n)
    def _():
        stage(i + 1).start()

    stage(i).wait()
    slot = i % 2
    ring = i % n_ring
    for rank in range(TOP_K):
        def body(it, carry, rank=rank):
            base = it * GATHER_UNROLL
            dsts = [pl.multiple_of(d_ref[0, 0, rank * tm + base + k], n_slabs) for k in range(GATHER_UNROLL)]
            for k in range(GATHER_UNROLL):
                src = pl.multiple_of((base + k) * n_slabs, n_slabs)
                pltpu.make_async_copy(buf_ref.at[ring, pl.ds(src, n_slabs), :],
                                      xs_hbm.at[pl.ds(dsts[k], n_slabs), :], sem.at[slot]).start()
            return carry
        lax.fori_loop(0, tm // GATHER_UNROLL, body, 0)

    @pl.when(i == 0)
    def _():
        zero_ref[...] = jnp.zeros(zero_ref.shape, F32)
        zero_row = zero_ref.at[pl.ds(0, n_slabs), :]
        sub_rows = zero_ref.shape[0]

        def unused_sub_tiles(fn):
            def tbody(tile, carry):
                for s in range(n_sub):
                    @pl.when(s >= tn_ref[tile])
                    def _():
                        dst = pl.multiple_of((tile * n_sub + s) * sub_rows, sub_rows)
                        fn(pltpu.make_async_copy(zero_ref, xs_hbm.at[pl.ds(dst, sub_rows), :], zsem))
                return carry
            lax.fori_loop(0, tn_ref.shape[0], tbody, 0)

        def tail_rows(fn):
            for e in range(n_experts):
                def zbody(r, carry, e=e):
                    dst = pl.multiple_of((pad_ref[0, e] + r) * n_slabs, n_slabs)
                    fn(pltpu.make_async_copy(zero_row, xs_hbm.at[pl.ds(dst, n_slabs), :], zsem))
                    return carry
                lax.fori_loop(0, pad_ref[1, e], zbody, 0)

        unused_sub_tiles(lambda cp: cp.start())
        tail_rows(lambda cp: cp.start())
        unused_sub_tiles(lambda cp: cp.wait())
        tail_rows(lambda cp: cp.wait())

    @pl.when(i > 0)
    def _():
        for rank in range(TOP_K):
            tile_copy(1 - slot).wait()

    @pl.when(i == n - 1)
    def _():
        for rank in range(TOP_K):
            tile_copy(slot).wait()


def moe_dispatch(h, dest, pad, tile_count, sub, n_sub, n_slabs):
    n_tok_tiles = dest.shape[0]
    tm = dest.shape[2] // TOP_K
    n_experts = pad.shape[1]
    n_rows = tile_count.shape[0] * n_sub * sub
    grid_spec = pltpu.PrefetchScalarGridSpec(
        num_scalar_prefetch=2,
        grid=(n_tok_tiles,),
        in_specs=[pl.BlockSpec((1, 1, TOP_K * tm), lambda i, pad, tn: (i, 0, 0), memory_space=pltpu.SMEM),
                  pl.BlockSpec(memory_space=pl.ANY)],
        out_specs=pl.BlockSpec(memory_space=pl.ANY),
        scratch_shapes=[pltpu.VMEM((sub * n_slabs, LANES), F32), pltpu.VMEM((3, tm * n_slabs, LANES), F32),
                        pltpu.SemaphoreType.DMA((3,)), pltpu.SemaphoreType.DMA((2,)),
                        pltpu.SemaphoreType.DMA(())],
    )
    return pl.pallas_call(
        functools.partial(_dispatch_body, tm=tm, n_slabs=n_slabs, n_experts=n_experts, n_sub=n_sub),
        grid_spec=grid_spec,
        out_shape=jax.ShapeDtypeStruct((n_rows * n_slabs, LANES), F32),
        compiler_params=_params(1),
        name="moe_dispatch",
    )(pad, tile_count, dest, h)


def _moe_body(te_ref, tn_ref, xs_ref, wg_ref, wu_ref, wd_ref, y_hbm,
              y_ref, xb_ref, wgb_ref, wub_ref, wdb_ref, stage_ref, zero_ref, osem, zsem, *, sub, n_sub):
    i = pl.program_id(0)
    j = pl.program_id(1)
    last = j == pl.num_programs(1) - 1
    count = tn_ref[i]
    d = wg_ref.shape[2]
    slab_rows = stage_ref.shape[1]

    @pl.when((i == 0) & (j == 0))
    def _():
        zero_ref[...] = jnp.zeros(zero_ref.shape, F32)

    for s in range(n_sub):
        rows = pl.ds(s * sub, sub)

        @pl.when((j == 0) & (s < count))
        def _():
            xb_ref[rows, :] = _load_slabs(xs_ref, s * sub, sub, d).astype(BF16)
            y_ref[rows, :] = jnp.zeros((sub, d), F32)

    def expert_rows(s, wg, wu, wd):
        rows = pl.ds(s * sub, sub)
        x = xb_ref[rows, :]
        a = jnp.dot(x, wg, preferred_element_type=F32)
        u = jnp.dot(x, wu, preferred_element_type=F32)
        act = (a * jax.nn.sigmoid(a) * u).astype(BF16)
        y_ref[rows, :] += jnp.dot(act, wd, preferred_element_type=F32)

    def rounded_weights():
        return tuple(r[0, 0].astype(BF16) for r in (wg_ref, wu_ref, wd_ref))

    @pl.when(count == n_sub)
    def _():
        weights = rounded_weights()
        for s in range(n_sub):
            expert_rows(s, *weights)

    for s in range(n_sub - 1):
        @pl.when((s < count) & (count < n_sub))
        def _():
            if s == 0:
                wgb_ref[...], wub_ref[...], wdb_ref[...] = rounded_weights()
            expert_rows(s, wgb_ref[...], wub_ref[...], wdb_ref[...])

    def out_copy(src, s):
        dst = pl.multiple_of((i * n_sub + s) * slab_rows, slab_rows)
        return src, y_hbm.at[pl.ds(dst, slab_rows), :]

    n_tiles = pl.num_programs(0)
    prev_count = tn_ref[jnp.maximum(i - 1, 0)]
    next_count = tn_ref[jnp.minimum(i + 1, n_tiles - 1)]
    final_tile = (i == n_tiles - 1) | (next_count == 0)

    def stage_copy(s):
        return pltpu.make_async_copy(*out_copy(stage_ref.at[s % 2], s), osem.at[s % 2])

    for slot in range(2):
        @pl.when(last & (count > 0) & (i > 0) & (prev_count > slot))
        def _():
            stage_copy(slot).wait()

    for s in range(n_sub):
        @pl.when(last & (s < count))
        def _():
            if s >= 2:
                stage_copy(s - 2).wait()
            _store_slabs(stage_ref.at[s % 2], 0, y_ref[pl.ds(s * sub, sub), :])
            stage_copy(s).start()

        @pl.when(last & (s >= count))
        def _():
            pltpu.make_async_copy(*out_copy(zero_ref, s), zsem).start()

    for s in range(n_sub):
        @pl.when(last & final_tile & (s < count) & (s + 2 >= count))
        def _():
            stage_copy(s).wait()

        @pl.when(last & (s >= count))
        def _():
            pltpu.make_async_copy(*out_copy(zero_ref, s), zsem).wait()


def moe_experts(xs, tile_expert, tile_count, wg, wu, wd, layer, sub, n_sub, tf=512):
    d = wg.shape[2]
    n_slabs = d // LANES
    n_tiles = tile_expert.shape[0]
    tile_rows = n_sub * sub
    f = wg.shape[3]
    tf = _pick(f, tf)
    nf = f // tf

    def w_col(i, j, te, tn):
        return (layer, te[i], 0, jnp.where(tn[i] > 0, j, nf - 1))

    def w_row(i, j, te, tn):
        return (layer, te[i], jnp.where(tn[i] > 0, j, nf - 1), 0)

    grid_spec = pltpu.PrefetchScalarGridSpec(
        num_scalar_prefetch=2,
        grid=(n_tiles, nf),
        in_specs=[pl.BlockSpec((tile_rows * n_slabs, LANES), lambda i, j, te, tn: (i, 0)),
                  pl.BlockSpec((1, 1, d, tf), w_col),
                  pl.BlockSpec((1, 1, d, tf), w_col),
                  pl.BlockSpec((1, 1, tf, d), w_row)],
        out_specs=pl.BlockSpec(memory_space=pl.ANY),
        scratch_shapes=[pltpu.VMEM((tile_rows, d), F32), pltpu.VMEM((tile_rows, d), BF16),
                        pltpu.VMEM((d, tf), BF16), pltpu.VMEM((d, tf), BF16), pltpu.VMEM((tf, d), BF16),
                        pltpu.VMEM((2, sub * n_slabs, LANES), F32), pltpu.VMEM((sub * n_slabs, LANES), F32),
                        pltpu.SemaphoreType.DMA((2,)), pltpu.SemaphoreType.DMA(())],
    )
    return pl.pallas_call(
        functools.partial(_moe_body, sub=sub, n_sub=n_sub),
        grid_spec=grid_spec,
        out_shape=jax.ShapeDtypeStruct((n_tiles * tile_rows * n_slabs, LANES), F32),
        compiler_params=_params(2),
        name="moe_experts",
    )(tile_expert, tile_count, xs, wg, wu, wd)


def _combine_body(d_cur_ref, d_nxt_ref, x_ref, meta_ref, g_ref, y_hbm, o_ref, buf_ref, sem,
                  *, tm, final_norm):
    i = pl.program_id(0)
    n = pl.num_programs(0)
    d = x_ref.shape[1]
    n_slabs = d // LANES

    def gather(d_ref, slot):
        def body(it, carry):
            base = it * GATHER_UNROLL
            srcs = [pl.multiple_of(d_ref[0, 0, base + k], n_slabs) for k in range(GATHER_UNROLL)]
            for k in range(GATHER_UNROLL):
                dst = pl.multiple_of((base + k) * n_slabs, n_slabs)
                pltpu.make_async_copy(y_hbm.at[pl.ds(srcs[k], n_slabs), :],
                                      buf_ref.at[slot, pl.ds(dst, n_slabs), :], sem.at[slot]).start()
            return carry
        lax.fori_loop(0, TOP_K * tm // GATHER_UNROLL, body, 0)

    @pl.when(i == 0)
    def _():
        gather(d_cur_ref, 0)

    @pl.when(i + 1 < n)
    def _():
        gather(d_nxt_ref, (i + 1) % 2)

    slot = i % 2
    pltpu.make_async_copy(y_hbm.at[pl.ds(0, TOP_K * tm * n_slabs), :], buf_ref.at[slot],
                          sem.at[slot]).wait()
    meta = meta_ref[...]
    out = (x_ref[...] + meta[:, 2:3] * _load_slabs(buf_ref.at[slot], 0, tm, d)
           + meta[:, 3:4] * _load_slabs(buf_ref.at[slot], tm, tm, d))
    if final_norm:
        out = _rms(out, g_ref[...])
    o_ref[...] = out


def moe_combine(x, meta, dest, y, g, final_norm, tm=256):
    t, d = x.shape
    n = t // tm
    n_slabs = d // LANES
    smem = lambda fn: pl.BlockSpec((1, 1, TOP_K * tm), fn, memory_space=pltpu.SMEM)
    return pl.pallas_call(
        functools.partial(_combine_body, tm=tm, final_norm=final_norm),
        grid=(n,),
        in_specs=[smem(lambda i: (i, 0, 0)),
                  smem(lambda i: (jnp.minimum(i + 1, n - 1), 0, 0)),
                  pl.BlockSpec((tm, d), lambda i: (i, 0)),
                  pl.BlockSpec((tm, LANES), lambda i: (i, 0)),
                  pl.BlockSpec((1, d), lambda i: (0, 0)),
                  pl.BlockSpec(memory_space=pl.ANY)],
        out_specs=pl.BlockSpec((tm, d), lambda i: (i, 0)),
        out_shape=jax.ShapeDtypeStruct((t, d), F32),
        scratch_shapes=[pltpu.VMEM((2, TOP_K * tm * n_slabs, LANES), F32), pltpu.SemaphoreType.DMA((2,))],
        compiler_params=_params(1),
        name="moe_combine",
    )(dest, dest, x, meta, g.reshape(1, d), y)


def _route(picks, counts, n_experts, sub, n_sub, tm_combine):
    t = picks.shape[1]
    tile_rows = sub * n_sub
    expert = picks[0:TOP_K].astype(I32)
    rank = picks[2 * TOP_K:3 * TOP_K].astype(I32)
    counts = counts[0, :n_experts].astype(I32)
    subs = (counts + sub - 1) // sub
    tiles = (subs + n_sub - 1) // n_sub
    tile_ends = jnp.cumsum(tiles)
    tile_starts = tile_ends - tiles
    first_row = jnp.zeros_like(expert)
    for e in range(n_experts):
        first_row = jnp.where(expert == e, tile_starts[e] * tile_rows, first_row)
    dest = first_row + rank
    n_tiles = ((t * TOP_K) // sub + n_experts + n_sub - 1) // n_sub + n_experts
    idx = jnp.arange(n_tiles, dtype=I32)
    tile_expert = jnp.minimum(jnp.searchsorted(tile_ends, idx, side="right"), n_experts - 1).astype(I32)
    local = idx - tile_starts[tile_expert]
    tile_count = jnp.clip(subs[tile_expert] - local * n_sub, 0, n_sub).astype(I32)
    dest_tiles = dest.reshape(TOP_K, t // tm_combine, tm_combine).transpose(1, 0, 2).reshape(
        t // tm_combine, 1, TOP_K * tm_combine)
    pad = jnp.stack([tile_starts * tile_rows + counts, subs * sub - counts]).astype(I32)
    return tile_expert, tile_count, dest_tiles, pad, n_tiles


def moe_layer(x, g, wr, wg, wu, wd, layer, final_g, final_norm, sub=512, n_sub=4, tm_combine=256):
    t, d = x.shape
    n_experts = wr.shape[1]
    tm_combine = min(tm_combine, t)
    wr_pad = jnp.zeros((d, LANES), BF16).at[:, :n_experts].set(wr.astype(BF16))
    h, meta, picks, counts = moe_router(x, g, wr_pad, n_experts)
    n_slabs = d // LANES
    tile_expert, tile_count, dest_tiles, pad, n_tiles = _route(picks, counts, n_experts, sub, n_sub,
                                                               tm_combine)
    slab_dest = dest_tiles * n_slabs
    xs = moe_dispatch(h, slab_dest, pad, tile_count, sub, n_sub, n_slabs)
    y = moe_experts(xs, tile_expert, tile_count, wg, wu, wd, layer, sub, n_sub)
    return moe_combine(x, meta, slab_dest, y, final_g, final_norm, tm_combine)


def attention_mixers(x, norm_g, w_in, lam_params, subln_g, sinks, layer):
    b, s, d = x.shape
    diff_heads = d // (4 * HEAD_DIM)
    swa_q_heads = d // (2 * HEAD_DIM)
    swa_kv_heads = max(1, swa_q_heads // 4)
    group = swa_q_heads // swa_kv_heads
    a_width = diff_heads * 2 * HEAD_DIM
    lam_init = 0.8 - 0.6 * math.exp(-0.3 * layer)
    x2 = x.reshape(b * s, d)
    proj = rms_matmul(x2, norm_g, w_in).reshape(b, s, -1)
    oa = diff_attention(proj, lam_params, subln_g, diff_heads, lam_init)
    q_col = 3 * a_width
    k_col = q_col + swa_q_heads * HEAD_DIM
    v_col = k_col + swa_kv_heads * HEAD_DIM
    ob = sliding_window_attention(proj, sinks, q_col, k_col, v_col, swa_kv_heads, group)
    return oa.reshape(b * s, -1), ob.reshape(b * s, -1)


def attention_layer(x, norm_g, w_in, lam_params, subln_g, sinks, w_out, layer):
    b, s, d = x.shape
    oa, ob = attention_mixers(x, norm_g, w_in, lam_params, subln_g, sinks, layer)
    return out_projection(x.reshape(b * s, d), oa, ob, w_out)


def kernel(x, attn_norm_g, w_in_att, diff_lambda, diff_subln_g, attn_sinks, w_out_att, ffn_norm_g, w_ffn_gate, w_ffn_up, w_ffn_down, conv_norm_g, w_pw1, b_pw1, w_dw, b_dw, conv_ln_g, conv_ln_b, w_pw2, b_pw2, moe_norm_g, w_router, w_exp_gate, w_exp_up, w_exp_down, final_norm_g):
    b, s, d = x.shape
    depth = attn_norm_g.shape[0] + conv_norm_g.shape[0]
    assert depth % 2 == 0, "the final RMSNorm is fused into the last expert layer"
    bf = lambda w: w.astype(BF16)
    for layer in range(depth):
        i = layer // 2
        if layer % 2 == 0:
            oa, ob = attention_mixers(x, attn_norm_g[i], bf(w_in_att[i]), diff_lambda[i], diff_subln_g[i],
                                      attn_sinks[i], layer)
            x2 = mix_ffn(x.reshape(b * s, d), oa, ob, bf(w_out_att[i]), ffn_norm_g[i],
                         bf(w_ffn_gate[i]), bf(w_ffn_up[i]), bf(w_ffn_down[i]))
            x = x2.reshape(b, s, d)
        else:
            x = conformer_conv(x, conv_norm_g[i], bf(w_pw1[i]), b_pw1[i], w_dw[i], b_dw[i],
                               conv_ln_g[i], conv_ln_b[i], bf(w_pw2[i]), b_pw2[i])
            x2 = moe_layer(x.reshape(b * s, d), moe_norm_g[i], w_router[i], w_exp_gate,
                           w_exp_up, w_exp_down, i, final_norm_g,
                           final_norm=(layer == depth - 1))
            x = x2.reshape(b, s, d)
    return x
```

```python
import functools
import math

import jax
import jax.numpy as jnp
from jax import lax
from jax.experimental import pallas as pl
from jax.experimental.pallas import tpu as pltpu

BF16 = jnp.bfloat16
F32 = jnp.float32
I32 = jnp.int32

LOG2_E = 1.4426950408889634
RMS_EPS = 1e-6
LN_EPS = 1e-5
HEAD_DIM = 64
ATTN_BLOCK = 128
CONV_WIDTH = 31
TOP_K = 2
LANES = 128
F32_SUBLANES = 8
CONV_HIST = 32
V7X_VMEM_LIMIT = 56 * 1024 * 1024


def _params(n_axes, flags=None):
    return pltpu.CompilerParams(dimension_semantics=("arbitrary",) * n_axes,
                                vmem_limit_bytes=V7X_VMEM_LIMIT, flags=flags)


def _rms(x, g):
    return x * lax.rsqrt(jnp.mean(x * x, axis=-1, keepdims=True) + RMS_EPS) * g


def _pick(n, pref):
    t = min(n, pref)
    while n % t:
        t -= LANES if t > LANES else 8
    return t


def _rms_matmul_body(x_ref, g_ref, w_ref, o_ref):
    h = _rms(x_ref[...], g_ref[...]).astype(BF16)
    o_ref[...] = jnp.dot(h, w_ref[...], preferred_element_type=F32).astype(o_ref.dtype)


def rms_matmul(x, g, w, tm=512):
    t, d = x.shape
    n = w.shape[1]
    tm = _pick(t, tm)
    return pl.pallas_call(
        _rms_matmul_body,
        grid=(t // tm,),
        in_specs=[pl.BlockSpec((tm, d), lambda i: (i, 0)),
                  pl.BlockSpec((1, d), lambda i: (0, 0)),
                  pl.BlockSpec((d, n), lambda i: (0, 0))],
        out_specs=pl.BlockSpec((tm, n), lambda i: (i, 0)),
        out_shape=jax.ShapeDtypeStruct((t, n), BF16),
        compiler_params=_params(1),
        name="rms_inproj",
    )(x, g.reshape(1, d), w)


def _diff_attn_body(q_ref, k_ref, v_ref, lam_ref, g_ref, o_ref, q_ref2, m_ref, l_ref, acc_ref,
                    *, blk, lam_init):
    qi = pl.program_id(2)
    hw = 2 * HEAD_DIM
    half = blk // 2
    lane = lax.broadcasted_iota(I32, (half, hw), 1)
    q = (q_ref[0].astype(F32) * (HEAD_DIM ** -0.5 * LOG2_E)).astype(BF16)
    zero = jnp.zeros((half, hw), BF16)
    q_ref2[...] = jnp.concatenate(
        [jnp.where(keep, q[r:r + half], zero)
         for r in (0, half) for keep in (lane < HEAD_DIM, lane >= HEAD_DIM)], axis=0)

    m_ref[...] = jnp.full(m_ref.shape, -jnp.inf, F32)
    l_ref[...] = jnp.zeros(l_ref.shape, F32)
    acc_ref[...] = jnp.zeros(acc_ref.shape, F32)

    def block(key_start, n_keys, row0, n_rows, masked):
        rows = pl.ds(row0, n_rows)
        k = k_ref[0, pl.ds(key_start, n_keys), :]
        v = v_ref[0, pl.ds(key_start, n_keys), :]
        s = lax.dot_general(q_ref2[rows, :], k, (((1,), (1,)), ((), ())), preferred_element_type=F32)
        if masked:
            row = lax.broadcasted_iota(I32, (n_rows, n_keys), 0) & (half - 1)
            col = lax.broadcasted_iota(I32, (n_rows, n_keys), 1)
            s = jnp.where(col <= row, s, -jnp.inf)
        tiles = [s[:, c * LANES:(c + 1) * LANES] for c in range(n_keys // LANES)]
        m_prev = m_ref[rows, :]
        m_new = jnp.maximum(m_prev, jnp.max(functools.reduce(jnp.maximum, tiles), axis=-1, keepdims=True))
        alpha = jnp.exp2(m_prev - m_new)
        p_tiles = [jnp.exp2(t - m_new) for t in tiles]
        l_ref[rows, :] = alpha * l_ref[rows, :] + functools.reduce(jnp.add, p_tiles)
        p = jnp.concatenate(p_tiles, axis=1).astype(BF16)
        acc_ref[rows, :] = alpha * acc_ref[rows, :] + jnp.dot(p, v, preferred_element_type=F32)
        m_ref[rows, :] = m_new

    def full_block(ki):
        block(pl.multiple_of(ki * blk, blk), blk, 0, 2 * blk, False)

    def pair(j, carry):
        full_block(2 * j)
        full_block(2 * j + 1)
        return carry

    lax.fori_loop(0, qi // 2, pair, 0)

    @pl.when(qi % 2 == 1)
    def _():
        full_block(qi - 1)

    diag = pl.multiple_of(qi * blk, blk)
    block(diag, half, 0, 2 * half, True)
    block(diag, half, 2 * half, 2 * half, False)
    block(diag + half, half, 2 * half, 2 * half, True)

    lp = lam_ref[...]
    lam = (jnp.exp(jnp.sum(lp[0:1] * lp[1:2], axis=-1, keepdims=True))
           - jnp.exp(jnp.sum(lp[2:3] * lp[3:4], axis=-1, keepdims=True)) + lam_init)
    o = acc_ref[...] / jnp.sum(l_ref[...], axis=-1, keepdims=True)
    o = (jnp.concatenate([o[0:half], o[2 * half:3 * half]], axis=0)
         - lam * jnp.concatenate([o[half:2 * half], o[3 * half:4 * half]], axis=0))
    o = _rms(o, g_ref[...]) * (1.0 - lam_init)
    o_ref[0] = o.astype(o_ref.dtype)


def diff_attention(proj, lam_params, subln_g, n_heads, lam_init, blk=1024):
    b, s, _ = proj.shape
    blk = _pick(s, blk)
    assert blk & (blk - 1) == 0, "the causal mask uses power-of-two half blocks"
    hw = 2 * HEAD_DIM
    kernel = functools.partial(_diff_attn_body, blk=blk, lam_init=lam_init)
    return pl.pallas_call(
        kernel,
        grid=(b, n_heads, s // blk),
        in_specs=[pl.BlockSpec((1, blk, hw), lambda bi, h, qi: (bi, qi, h)),
                  pl.BlockSpec((1, s, hw), lambda bi, h, qi: (bi, 0, n_heads + h)),
                  pl.BlockSpec((1, s, hw), lambda bi, h, qi: (bi, 0, 2 * n_heads + h)),
                  pl.BlockSpec((4, HEAD_DIM), lambda bi, h, qi: (0, 0)),
                  pl.BlockSpec((1, hw), lambda bi, h, qi: (0, 0))],
        out_specs=pl.BlockSpec((1, blk, hw), lambda bi, h, qi: (bi, qi, h)),
        out_shape=jax.ShapeDtypeStruct((b, s, n_heads * hw), BF16),
        scratch_shapes=[pltpu.VMEM((2 * blk, hw), BF16), pltpu.VMEM((2 * blk, LANES), F32),
                        pltpu.VMEM((2 * blk, LANES), F32), pltpu.VMEM((2 * blk, hw), F32)],
        compiler_params=_params(3),
        name="diff_attention",
    )(proj, proj, proj, lam_params, subln_g.reshape(1, hw))


def _swa_body(sink_ref, q_ref, kc_ref, kp_ref, vc_ref, vp_ref, o_ref, *, tq, kv_heads, group):
    t = pl.program_id(1)
    w = ATTN_BLOCK
    row = lax.broadcasted_iota(I32, (group * w, 2 * w), 0) & (w - 1)
    col = lax.broadcasted_iota(I32, (group * w, 2 * w), 1)
    band = (col > row) & (col <= row + w)
    band_first = band & ((col >= w) | (t > 0))
    low_half = lax.broadcasted_iota(I32, (w, LANES), 1) < HEAD_DIM
    for j in range(tq // w):
        rows = slice(j * w, (j + 1) * w)
        if j == 0:
            k2 = jnp.concatenate([kp_ref[0], kc_ref[0, rows, :]], axis=0)
            v2 = jnp.concatenate([vp_ref[0], vc_ref[0, rows, :]], axis=0)
            mask = band_first
        else:
            k2 = kc_ref[0, (j - 1) * w:(j + 1) * w, :]
            v2 = vc_ref[0, (j - 1) * w:(j + 1) * w, :]
            mask = band
        outs = []
        for g in range(kv_heads):
            kg = k2[:, g * HEAD_DIM:(g + 1) * HEAD_DIM]
            vg = v2[:, g * HEAD_DIM:(g + 1) * HEAD_DIM]
            kdup = jnp.concatenate([kg, kg], axis=1)
            vdup = jnp.concatenate([vg, vg], axis=1)
            heads = [g * group + i for i in range(group)]
            q_tiles = []
            for h in heads:
                q = q_ref[0, rows, (h // 2) * LANES:(h // 2 + 1) * LANES]
                q = (q.astype(F32) * (HEAD_DIM ** -0.5)).astype(BF16)
                own_half = low_half if h % 2 == 0 else jnp.logical_not(low_half)
                q_tiles.append(jnp.where(own_half, q, jnp.zeros_like(q)))
            sink = jnp.concatenate([jnp.full((w, LANES), sink_ref[h], F32) for h in heads], axis=0)
            s = lax.dot_general(jnp.concatenate(q_tiles, axis=0), kdup, (((1,), (1,)), ((), ())),
                                preferred_element_type=F32)
            s = jnp.where(mask, s, -jnp.inf)
            s0, s1 = s[:, :w], s[:, w:]
            m = jnp.maximum(jnp.max(jnp.maximum(s0, s1), axis=-1, keepdims=True), sink)
            e0 = jnp.exp(s0 - m)
            e1 = jnp.exp(s1 - m)
            denom = jnp.sum(e0 + e1, axis=-1, keepdims=True) + jnp.exp(sink - m)
            e = jnp.concatenate([e0, e1], axis=1).astype(BF16)
            o = jnp.dot(e, vdup, preferred_element_type=F32) / denom
            outs.extend(o[i * w:(i + 1) * w, :] for i in range(group))
        tiles = [jnp.where(low_half, outs[h], outs[h + 1]) for h in range(0, len(outs), 2)]
        o_ref[0, rows, :] = jnp.concatenate(tiles, axis=1).astype(o_ref.dtype)


def sliding_window_attention(proj, sinks, q_col, k_col, v_col, kv_heads, group, tq=512):
    b, s, _ = proj.shape
    tq = _pick(s, tq)
    qw = kv_heads * group * HEAD_DIM
    kw = kv_heads * HEAD_DIM
    sub = tq // ATTN_BLOCK
    kernel = functools.partial(_swa_body, tq=tq, kv_heads=kv_heads, group=group)
    prev = lambda bi, t: (bi, jnp.maximum(t * sub - 1, 0), 0)
    return pl.pallas_call(
        kernel,
        grid=(b, s // tq),
        in_specs=[pl.BlockSpec(memory_space=pltpu.SMEM),
                  pl.BlockSpec((1, tq, qw), lambda bi, t: (bi, t, q_col // qw)),
                  pl.BlockSpec((1, tq, kw), lambda bi, t: (bi, t, k_col // kw)),
                  pl.BlockSpec((1, ATTN_BLOCK, kw), lambda bi, t: prev(bi, t)[:2] + (k_col // kw,)),
                  pl.BlockSpec((1, tq, kw), lambda bi, t: (bi, t, v_col // kw)),
                  pl.BlockSpec((1, ATTN_BLOCK, kw), lambda bi, t: prev(bi, t)[:2] + (v_col // kw,))],
        out_specs=pl.BlockSpec((1, tq, qw), lambda bi, t: (bi, t, 0)),
        out_shape=jax.ShapeDtypeStruct((b, s, qw), BF16),
        compiler_params=_params(2),
        name="swa_attention",
    )(sinks, proj, proj, proj, proj, proj)


def _outproj_body(x_ref, a_ref, b_ref, w_ref, o_ref):
    mix = jnp.concatenate([a_ref[...], b_ref[...]], axis=1)
    o_ref[...] = x_ref[...] + jnp.dot(mix, w_ref[...], preferred_element_type=F32)


def out_projection(x, oa, ob, w, tm=512):
    t, d = x.shape
    tm = _pick(t, tm)
    return pl.pallas_call(
        _outproj_body,
        grid=(t // tm,),
        in_specs=[pl.BlockSpec((tm, d), lambda i: (i, 0)),
                  pl.BlockSpec((tm, oa.shape[1]), lambda i: (i, 0)),
                  pl.BlockSpec((tm, ob.shape[1]), lambda i: (i, 0)),
                  pl.BlockSpec(w.shape, lambda i: (0, 0))],
        out_specs=pl.BlockSpec((tm, d), lambda i: (i, 0)),
        out_shape=jax.ShapeDtypeStruct((t, d), F32),
        compiler_params=_params(1),
        name="out_projection",
    )(x, oa, ob, w)


def _ffn_body(x_ref, g_ref, wg_ref, wu_ref, wd_ref, o_ref, h_ref, acc_ref):
    j = pl.program_id(1)

    @pl.when(j == 0)
    def _():
        h_ref[...] = _rms(x_ref[...], g_ref[...]).astype(BF16)
        acc_ref[...] = jnp.zeros(acc_ref.shape, F32)

    h = h_ref[...]
    a = jnp.dot(h, wg_ref[...], preferred_element_type=F32)
    u = jnp.dot(h, wu_ref[...], preferred_element_type=F32)
    act = (a * jax.nn.sigmoid(a) * u).astype(BF16)
    acc_ref[...] += jnp.dot(act, wd_ref[...], preferred_element_type=F32)

    @pl.when(j == pl.num_programs(1) - 1)
    def _():
        o_ref[...] = x_ref[...] + acc_ref[...]


def ffn_swiglu(x, g, wg, wu, wd, tm=512, tf=1408):
    t, d = x.shape
    f = wg.shape[1]
    tm = _pick(t, tm)
    tf = _pick(f, tf)
    return pl.pallas_call(
        _ffn_body,
        grid=(t // tm, f // tf),
        in_specs=[pl.BlockSpec((tm, d), lambda i, j: (i, 0)),
                  pl.BlockSpec((1, d), lambda i, j: (0, 0)),
                  pl.BlockSpec((d, tf), lambda i, j: (0, j)),
                  pl.BlockSpec((d, tf), lambda i, j: (0, j)),
                  pl.BlockSpec((tf, d), lambda i, j: (j, 0))],
        out_specs=pl.BlockSpec((tm, d), lambda i, j: (i, 0)),
        out_shape=jax.ShapeDtypeStruct((t, d), F32),
        scratch_shapes=[pltpu.VMEM((tm, d), BF16), pltpu.VMEM((tm, d), F32)],
        compiler_params=_params(2),
        name="ffn_swiglu",
    )(x, g.reshape(1, d), wg, wu, wd)


def _mix_ffn_body(x_ref, a_ref, b_ref, wo_ref, g_ref, wg_ref, wu_ref, wd_ref, o_ref, *, ff_chunk):
    mix = jnp.concatenate([a_ref[...], b_ref[...]], axis=1)
    x1 = x_ref[...] + jnp.dot(mix, wo_ref[...], preferred_element_type=F32)
    h = _rms(x1, g_ref[...]).astype(BF16)
    y = x1
    for c in range(wg_ref.shape[1] // ff_chunk):
        cols = slice(c * ff_chunk, (c + 1) * ff_chunk)
        a = jnp.dot(h, wg_ref[:, cols], preferred_element_type=F32)
        u = jnp.dot(h, wu_ref[:, cols], preferred_element_type=F32)
        act = (a * jax.nn.sigmoid(a) * u).astype(BF16)
        y = y + jnp.dot(act, wd_ref[cols, :], preferred_element_type=F32)
    o_ref[...] = y


def mix_ffn(x, oa, ob, wo, g, wg, wu, wd, tm=512, ff_chunk=1408):
    t, d = x.shape
    f = wg.shape[1]
    tm = _pick(t, tm)
    ff_chunk = _pick(f, ff_chunk)
    resident = lambda shape: pl.BlockSpec(shape, lambda i: (0, 0), pipeline_mode=pl.Buffered(1))
    return pl.pallas_call(
        functools.partial(_mix_ffn_body, ff_chunk=ff_chunk),
        grid=(t // tm,),
        in_specs=[pl.BlockSpec((tm, d), lambda i: (i, 0)),
                  pl.BlockSpec((tm, oa.shape[1]), lambda i: (i, 0)),
                  pl.BlockSpec((tm, ob.shape[1]), lambda i: (i, 0)),
                  resident(wo.shape), resident((1, d)),
                  resident(wg.shape), resident(wu.shape), resident(wd.shape)],
        out_specs=pl.BlockSpec((tm, d), lambda i: (i, 0)),
        out_shape=jax.ShapeDtypeStruct((t, d), F32),
        compiler_params=_params(1),
        name="mix_ffn",
    )(x, oa, ob, wo, g.reshape(1, d), wg, wu, wd)


def _conv_body(x_ref, g_ref, w1_ref, b1_ref, wdw_ref, bdw_ref, lng_ref, lnb_ref, w2_ref, b2_ref,
               o_ref, u_ref, v_ref, wb_ref, c_ref, *, tm, rows_per_chunk):
    d = x_ref.shape[-1]
    t = pl.program_id(1)
    sublanes = wb_ref.shape[1]

    @pl.when(t == 0)
    def _():
        u_ref[:, 0:CONV_HIST, :] = jnp.zeros((d // LANES, CONV_HIST, LANES), F32)
        wb_ref[...] = jnp.broadcast_to(wdw_ref[...][:, None, :], wb_ref.shape)

    h = _rms(x_ref[0], g_ref[...]).astype(BF16)
    z = jnp.dot(h, w1_ref[...], preferred_element_type=F32) + b1_ref[...]
    u = z[:, :d] * jax.nn.sigmoid(z[:, d:])
    n_slabs = d // LANES
    for c in range(n_slabs):
        u_ref[c, CONV_HIST:CONV_HIST + tm, :] = u[:, c * LANES:(c + 1) * LANES]

    first_tap = CONV_HIST - (CONV_WIDTH - 1)
    conv_rows = 8 * sublanes

    for c in range(n_slabs):
        lanes = slice(c * LANES, (c + 1) * LANES)
        w = [wb_ref[j, :, lanes] for j in range(CONV_WIDTH)]
        bias = jnp.zeros((sublanes, LANES), F32) + bdw_ref[:, lanes]

        def rows_block(i, carry, c=c, lanes=lanes, w=w, bias=bias):
            r0 = pl.multiple_of(i * conv_rows, conv_rows)
            for r in range(conv_rows // sublanes):
                sums = [bias, None]
                for j in range(CONV_WIDTH):
                    tap = u_ref[c, pl.ds(r0 + (first_tap + j + r * sublanes), sublanes, stride=1), :] * w[j]
                    sums[j % 2] = tap if sums[j % 2] is None else sums[j % 2] + tap
                c_ref[pl.ds(r0 + r * sublanes, sublanes), lanes] = sums[0] + sums[1]
            return carry

        lax.fori_loop(0, tm // conv_rows, rows_block, 0)

    def chunk(i, carry):
        r0 = pl.multiple_of(i * rows_per_chunk, rows_per_chunk)
        acc = c_ref[pl.ds(r0, rows_per_chunk), :]
        mu = jnp.mean(acc, axis=-1, keepdims=True)
        xc = acc - mu
        var = jnp.mean(xc * xc, axis=-1, keepdims=True)
        y = xc * lax.rsqrt(var + LN_EPS) * lng_ref[...] + lnb_ref[...]
        v_ref[pl.ds(r0, rows_per_chunk), :] = (y * jax.nn.sigmoid(y)).astype(BF16)
        return carry

    lax.fori_loop(0, tm // rows_per_chunk, chunk, 0)
    u_ref[:, 0:CONV_HIST, :] = u_ref[:, tm:tm + CONV_HIST, :]
    o_ref[0] = x_ref[0] + jnp.dot(v_ref[...], w2_ref[...], preferred_element_type=F32) + b2_ref[...]


def conformer_conv(x, g, w1, b1, wdw, bdw, lng, lnb, w2, b2, tm=512, rows_per_chunk=128):
    b, s, d = x.shape
    tm = _pick(s, tm)
    kernel = functools.partial(_conv_body, tm=tm, rows_per_chunk=rows_per_chunk)
    vec = lambda n: pl.BlockSpec((1, n), lambda bi, t: (0, 0))
    return pl.pallas_call(
        kernel,
        grid=(b, s // tm),
        in_specs=[pl.BlockSpec((1, tm, d), lambda bi, t: (bi, t, 0)),
                  vec(d),
                  pl.BlockSpec((d, 2 * d), lambda bi, t: (0, 0)),
                  vec(2 * d),
                  pl.BlockSpec((CONV_WIDTH, d), lambda bi, t: (0, 0)),
                  vec(d), vec(d), vec(d),
                  pl.BlockSpec((d, d), lambda bi, t: (0, 0)),
                  vec(d)],
        out_specs=pl.BlockSpec((1, tm, d), lambda bi, t: (bi, t, 0)),
        out_shape=jax.ShapeDtypeStruct((b, s, d), F32),
        scratch_shapes=[pltpu.VMEM((d // LANES, tm + CONV_HIST, LANES), F32), pltpu.VMEM((tm, d), BF16),
                        pltpu.VMEM((CONV_WIDTH, F32_SUBLANES, d), F32), pltpu.VMEM((tm, d), F32)],
        compiler_params=_params(2),
        name="conformer_conv",
    )(x, g.reshape(1, d), w1, b1.reshape(1, 2 * d), wdw, bdw.reshape(1, d), lng.reshape(1, d),
      lnb.reshape(1, d), w2, b2.reshape(1, d))


def _router_body(x_ref, g_ref, wr_ref, h_ref, meta_ref, picks_ref, counts_ref, *, n_experts):
    @pl.when(pl.program_id(0) == 0)
    def _():
        counts_ref[...] = jnp.zeros(counts_ref.shape, F32)

    h = _rms(x_ref[...], g_ref[...])
    _store_slabs(h_ref, 0, h)
    logits = jnp.dot(h.astype(BF16), wr_ref[...], preferred_element_type=F32)
    lane = lax.broadcasted_iota(I32, logits.shape, 1).astype(F32)
    neg = jnp.full_like(logits, -jnp.inf)
    far = jnp.full_like(logits, float(LANES))
    l1 = jnp.where(lane < n_experts, logits, neg)
    m1 = jnp.max(l1, axis=-1, keepdims=True)
    i1 = jnp.min(jnp.where(l1 == m1, lane, far), axis=-1, keepdims=True)
    l2 = jnp.where(lane == i1, neg, l1)
    m2 = jnp.max(l2, axis=-1, keepdims=True)
    i2 = jnp.min(jnp.where(l2 == m2, lane, far), axis=-1, keepdims=True)
    e2 = jnp.exp(m2 - m1)
    w1 = 1.0 / (1.0 + e2)
    w2 = e2 / (1.0 + e2)
    tm = logits.shape[0]
    hot1 = (lane == i1).astype(F32)
    hot2 = (lane == i2).astype(F32)
    both = (hot1 + hot2).astype(BF16)
    earlier = (lax.broadcasted_iota(I32, (tm, tm), 1) < lax.broadcasted_iota(I32, (tm, tm), 0)).astype(BF16)
    before = jnp.dot(earlier, both, preferred_element_type=F32) + counts_ref[...]
    r1 = jnp.sum(hot1 * before, axis=-1, keepdims=True)
    r2 = jnp.sum(hot2 * before, axis=-1, keepdims=True)
    counts_ref[...] += jnp.sum(hot1 + hot2, axis=0, keepdims=True)
    zero = jnp.zeros_like(logits)
    meta = jnp.where(lane == 0, i1,
           jnp.where(lane == 1, i2,
           jnp.where(lane == 2, w1,
           jnp.where(lane == 3, w2,
           jnp.where(lane == 4, r1, jnp.where(lane == 5, r2, zero))))))
    meta_ref[...] = meta
    picks_ref[...] = jnp.transpose(meta)[0:picks_ref.shape[0], :]


def moe_router(x, g, wr, n_experts, tm=512):
    t, d = x.shape
    tm = _pick(t, tm)
    kernel = functools.partial(_router_body, n_experts=n_experts)
    return pl.pallas_call(
        kernel,
        grid=(t // tm,),
        in_specs=[pl.BlockSpec((tm, d), lambda i: (i, 0)),
                  pl.BlockSpec((1, d), lambda i: (0, 0)),
                  pl.BlockSpec((d, LANES), lambda i: (0, 0))],
        out_specs=[pl.BlockSpec((tm * (d // LANES), LANES), lambda i: (i, 0)),
                   pl.BlockSpec((tm, LANES), lambda i: (i, 0)),
                   pl.BlockSpec((F32_SUBLANES, tm), lambda i: (0, i)),
                   pl.BlockSpec((1, LANES), lambda i: (0, 0))],
        out_shape=[jax.ShapeDtypeStruct((t * (d // LANES), LANES), F32),
                   jax.ShapeDtypeStruct((t, LANES), F32),
                   jax.ShapeDtypeStruct((F32_SUBLANES, t), F32),
                   jax.ShapeDtypeStruct((1, LANES), F32)],
        compiler_params=_params(1),
        name="moe_router",
    )(x, g.reshape(1, d), wr)


GATHER_UNROLL = 8
N_DMA_PRIORITIES = 2


def _store_slabs(ref, first_row, value):
    n, d = value.shape
    n_slabs = d // LANES
    for c in range(n_slabs):
        ref[pl.ds(first_row * n_slabs + c, n, stride=n_slabs), :] = value[:, c * LANES:(c + 1) * LANES]


def _load_slabs(ref, first_row, n, d):
    n_slabs = d // LANES
    return jnp.concatenate([ref[pl.ds(first_row * n_slabs + c, n, stride=n_slabs), :]
                            for c in range(n_slabs)], axis=1)


def _dispatch_body(pad_ref, tn_ref, d_ref, h_hbm, xs_hbm, zero_ref, buf_ref, in_sem, sem, zsem,
                   *, tm, n_slabs, n_experts, n_sub):
    i = pl.program_id(0)
    n = pl.num_programs(0)
    rows = tm * n_slabs
    n_ring = buf_ref.shape[0]

    def stage(tile):
        return pltpu.make_async_copy(h_hbm.at[pl.ds(pl.multiple_of(tile * rows, rows), rows), :],
                                     buf_ref.at[tile % n_ring], in_sem.at[tile % n_ring])

    def tile_copy(slot):
        return pltpu.make_async_copy(buf_ref.at[0], xs_hbm.at[pl.ds(0, rows), :], sem.at[slot])

    @pl.when(i == 0)
    def _():
        stage(0).start()

    @pl.when(i + 1 < n)
    def _():
        stage(i + 1).start()

    stage(i).wait()
    slot = i % 2
    ring = i % n_ring
    for rank in range(TOP_K):
        def body(it, carry, rank=rank):
            base = it * GATHER_UNROLL
            dsts = [pl.multiple_of(d_ref[0, 0, rank * tm + base + k], n_slabs) for k in range(GATHER_UNROLL)]
            for k in range(GATHER_UNROLL):
                src = pl.multiple_of((base + k) * n_slabs, n_slabs)
                pltpu.make_async_copy(buf_ref.at[ring, pl.ds(src, n_slabs), :],
                                      xs_hbm.at[pl.ds(dsts[k], n_slabs), :], sem.at[slot]
                                      ).start(priority=k % N_DMA_PRIORITIES)
            return carry
        lax.fori_loop(0, tm // GATHER_UNROLL, body, 0)

    @pl.when(i == 0)
    def _():
        zero_ref[...] = jnp.zeros(zero_ref.shape, F32)
        zero_row = zero_ref.at[pl.ds(0, n_slabs), :]
        sub_rows = zero_ref.shape[0]

        def unused_sub_tiles(fn):
            def tbody(tile, carry):
                for s in range(n_sub):
                    @pl.when(s >= tn_ref[tile])
                    def _():
                        dst = pl.multiple_of((tile * n_sub + s) * sub_rows, sub_rows)
                        fn(pltpu.make_async_copy(zero_ref, xs_hbm.at[pl.ds(dst, sub_rows), :], zsem))
                return carry
            lax.fori_loop(0, tn_ref.shape[0], tbody, 0)

        def tail_rows(fn):
            for e in range(n_experts):
                def zbody(r, carry, e=e):
                    dst = pl.multiple_of((pad_ref[0, e] + r) * n_slabs, n_slabs)
                    fn(pltpu.make_async_copy(zero_row, xs_hbm.at[pl.ds(dst, n_slabs), :], zsem))
                    return carry
                lax.fori_loop(0, pad_ref[1, e], zbody, 0)

        unused_sub_tiles(lambda cp: cp.start())
        tail_rows(lambda cp: cp.start())
        unused_sub_tiles(lambda cp: cp.wait())
        tail_rows(lambda cp: cp.wait())

    @pl.when(i > 0)
    def _():
        for rank in range(TOP_K):
            tile_copy(1 - slot).wait()

    @pl.when(i == n - 1)
    def _():
        for rank in range(TOP_K):
            tile_copy(slot).wait()


def moe_dispatch(h, dest, pad, tile_count, sub, n_sub, n_slabs):
    n_tok_tiles = dest.shape[0]
    tm = dest.shape[2] // TOP_K
    n_experts = pad.shape[1]
    n_rows = tile_count.shape[0] * n_sub * sub
    grid_spec = pltpu.PrefetchScalarGridSpec(
        num_scalar_prefetch=2,
        grid=(n_tok_tiles,),
        in_specs=[pl.BlockSpec((1, 1, TOP_K * tm), lambda i, pad, tn: (i, 0, 0), memory_space=pltpu.SMEM),
                  pl.BlockSpec(memory_space=pl.ANY)],
        out_specs=pl.BlockSpec(memory_space=pl.ANY),
        scratch_shapes=[pltpu.VMEM((sub * n_slabs, LANES), F32), pltpu.VMEM((3, tm * n_slabs, LANES), F32),
                        pltpu.SemaphoreType.DMA((3,)), pltpu.SemaphoreType.DMA((2,)),
                        pltpu.SemaphoreType.DMA(())],
    )
    return pl.pallas_call(
        functools.partial(_dispatch_body, tm=tm, n_slabs=n_slabs, n_experts=n_experts, n_sub=n_sub),
        grid_spec=grid_spec,
        out_shape=jax.ShapeDtypeStruct((n_rows * n_slabs, LANES), F32),
        compiler_params=_params(1),
        name="moe_dispatch",
    )(pad, tile_count, dest, h)


def _moe_body(te_ref, tn_ref, xs_ref, wg_ref, wu_ref, wd_ref, y_hbm,
              y_ref, xb_ref, wgb_ref, wub_ref, wdb_ref, stage_ref, zero_ref, osem, zsem, *, sub, n_sub):
    i = pl.program_id(0)
    j = pl.program_id(1)
    last = j == pl.num_programs(1) - 1
    count = tn_ref[i]
    d = wg_ref.shape[2]
    slab_rows = stage_ref.shape[1]

    @pl.when((i == 0) & (j == 0))
    def _():
        zero_ref[...] = jnp.zeros(zero_ref.shape, F32)

    for s in range(n_sub):
        rows = pl.ds(s * sub, sub)

        @pl.when((j == 0) & (s < count))
        def _():
            xb_ref[rows, :] = _load_slabs(xs_ref, s * sub, sub, d).astype(BF16)
            y_ref[rows, :] = jnp.zeros((sub, d), F32)

    def expert_rows(s, wg, wu, wd):
        rows = pl.ds(s * sub, sub)
        x = xb_ref[rows, :]
        a = jnp.dot(x, wg, preferred_element_type=F32)
        u = jnp.dot(x, wu, preferred_element_type=F32)
        act = (a * jax.nn.sigmoid(a) * u).astype(BF16)
        y_ref[rows, :] += jnp.dot(act, wd, preferred_element_type=F32)

    def rounded_weights():
        return tuple(r[0, 0].astype(BF16) for r in (wg_ref, wu_ref, wd_ref))

    @pl.when(count == n_sub)
    def _():
        weights = rounded_weights()
        for s in range(n_sub):
            expert_rows(s, *weights)

    for s in range(n_sub - 1):
        @pl.when((s < count) & (count < n_sub))
        def _():
            if s == 0:
                wgb_ref[...], wub_ref[...], wdb_ref[...] = rounded_weights()
            expert_rows(s, wgb_ref[...], wub_ref[...], wdb_ref[...])

    def out_copy(src, s):
        dst = pl.multiple_of((i * n_sub + s) * slab_rows, slab_rows)
        return src, y_hbm.at[pl.ds(dst, slab_rows), :]

    n_tiles = pl.num_programs(0)
    prev_count = tn_ref[jnp.maximum(i - 1, 0)]
    next_count = tn_ref[jnp.minimum(i + 1, n_tiles - 1)]
    final_tile = (i == n_tiles - 1) | (next_count == 0)

    def stage_copy(s):
        return pltpu.make_async_copy(*out_copy(stage_ref.at[s % 2], s), osem.at[s % 2])

    for slot in range(2):
        @pl.when(last & (count > 0) & (i > 0) & (prev_count > slot))
        def _():
            stage_copy(slot).wait()

    for s in range(n_sub):
        @pl.when(last & (s < count))
        def _():
            if s >= 2:
                stage_copy(s - 2).wait()
            _store_slabs(stage_ref.at[s % 2], 0, y_ref[pl.ds(s * sub, sub), :])
            stage_copy(s).start()

        @pl.when(last & (s >= count))
        def _():
            pltpu.make_async_copy(*out_copy(zero_ref, s), zsem).start()

    for s in range(n_sub):
        @pl.when(last & final_tile & (s < count) & (s + 2 >= count))
        def _():
            stage_copy(s).wait()

        @pl.when(last & (s >= count))
        def _():
            pltpu.make_async_copy(*out_copy(zero_ref, s), zsem).wait()


def moe_experts(xs, tile_expert, tile_count, wg, wu, wd, layer, sub, n_sub, tf=512):
    d = wg.shape[2]
    n_slabs = d // LANES
    n_tiles = tile_expert.shape[0]
    tile_rows = n_sub * sub
    f = wg.shape[3]
    tf = _pick(f, tf)
    nf = f // tf

    def w_col(i, j, te, tn):
        return (layer, te[i], 0, jnp.where(tn[i] > 0, j, nf - 1))

    def w_row(i, j, te, tn):
        return (layer, te[i], jnp.where(tn[i] > 0, j, nf - 1), 0)

    grid_spec = pltpu.PrefetchScalarGridSpec(
        num_scalar_prefetch=2,
        grid=(n_tiles, nf),
        in_specs=[pl.BlockSpec((tile_rows * n_slabs, LANES), lambda i, j, te, tn: (i, 0)),
                  pl.BlockSpec((1, 1, d, tf), w_col),
                  pl.BlockSpec((1, 1, d, tf), w_col),
                  pl.BlockSpec((1, 1, tf, d), w_row)],
        out_specs=pl.BlockSpec(memory_space=pl.ANY),
        scratch_shapes=[pltpu.VMEM((tile_rows, d), F32), pltpu.VMEM((tile_rows, d), BF16),
                        pltpu.VMEM((d, tf), BF16), pltpu.VMEM((d, tf), BF16), pltpu.VMEM((tf, d), BF16),
                        pltpu.VMEM((2, sub * n_slabs, LANES), F32), pltpu.VMEM((sub * n_slabs, LANES), F32),
                        pltpu.SemaphoreType.DMA((2,)), pltpu.SemaphoreType.DMA(())],
    )
    return pl.pallas_call(
        functools.partial(_moe_body, sub=sub, n_sub=n_sub),
        grid_spec=grid_spec,
        out_shape=jax.ShapeDtypeStruct((n_tiles * tile_rows * n_slabs, LANES), F32),
        compiler_params=_params(2),
        name="moe_experts",
    )(tile_expert, tile_count, xs, wg, wu, wd)


def _combine_body(d_cur_ref, d_nxt_ref, x_ref, meta_ref, g_ref, y_hbm, o_ref, buf_ref, sem,
                  *, tm, final_norm):
    i = pl.program_id(0)
    n = pl.num_programs(0)
    d = x_ref.shape[1]
    n_slabs = d // LANES

    def gather(d_ref, slot):
        def body(it, carry):
            base = it * GATHER_UNROLL
            srcs = [pl.multiple_of(d_ref[0, 0, base + k], n_slabs) for k in range(GATHER_UNROLL)]
            for k in range(GATHER_UNROLL):
                dst = pl.multiple_of((base + k) * n_slabs, n_slabs)
                pltpu.make_async_copy(y_hbm.at[pl.ds(srcs[k], n_slabs), :],
                                      buf_ref.at[slot, pl.ds(dst, n_slabs), :], sem.at[slot]
                                      ).start(priority=k % N_DMA_PRIORITIES)
            return carry
        lax.fori_loop(0, TOP_K * tm // GATHER_UNROLL, body, 0)

    @pl.when(i == 0)
    def _():
        gather(d_cur_ref, 0)

    @pl.when(i + 1 < n)
    def _():
        gather(d_nxt_ref, (i + 1) % 2)

    slot = i % 2
    pltpu.make_async_copy(y_hbm.at[pl.ds(0, TOP_K * tm * n_slabs), :], buf_ref.at[slot],
                          sem.at[slot]).wait()
    meta = meta_ref[...]
    out = (x_ref[...] + meta[:, 2:3] * _load_slabs(buf_ref.at[slot], 0, tm, d)
           + meta[:, 3:4] * _load_slabs(buf_ref.at[slot], tm, tm, d))
    if final_norm:
        out = _rms(out, g_ref[...])
    o_ref[...] = out


def moe_combine(x, meta, dest, y, g, final_norm, tm=256):
    t, d = x.shape
    n = t // tm
    n_slabs = d // LANES
    smem = lambda fn: pl.BlockSpec((1, 1, TOP_K * tm), fn, memory_space=pltpu.SMEM)
    return pl.pallas_call(
        functools.partial(_combine_body, tm=tm, final_norm=final_norm),
        grid=(n,),
        in_specs=[smem(lambda i: (i, 0, 0)),
                  smem(lambda i: (jnp.minimum(i + 1, n - 1), 0, 0)),
                  pl.BlockSpec((tm, d), lambda i: (i, 0)),
                  pl.BlockSpec((tm, LANES), lambda i: (i, 0)),
                  pl.BlockSpec((1, d), lambda i: (0, 0)),
                  pl.BlockSpec(memory_space=pl.ANY)],
        out_specs=pl.BlockSpec((tm, d), lambda i: (i, 0)),
        out_shape=jax.ShapeDtypeStruct((t, d), F32),
        scratch_shapes=[pltpu.VMEM((2, TOP_K * tm * n_slabs, LANES), F32), pltpu.SemaphoreType.DMA((2,))],
        compiler_params=_params(1),
        name="moe_combine",
    )(dest, dest, x, meta, g.reshape(1, d), y)


def _route(picks, counts, n_experts, sub, n_sub, tm_combine):
    t = picks.shape[1]
    tile_rows = sub * n_sub
    expert = picks[0:TOP_K].astype(I32)
    rank = picks[2 * TOP_K:3 * TOP_K].astype(I32)
    counts = counts[0, :n_experts].astype(I32)
    subs = (counts + sub - 1) // sub
    tiles = (subs + n_sub - 1) // n_sub
    tile_ends = jnp.cumsum(tiles)
    tile_starts = tile_ends - tiles
    first_row = jnp.zeros_like(expert)
    for e in range(n_experts):
        first_row = jnp.where(expert == e, tile_starts[e] * tile_rows, first_row)
    dest = first_row + rank
    n_tiles = ((t * TOP_K) // sub + n_experts * n_sub) // n_sub
    idx = jnp.arange(n_tiles, dtype=I32)
    tile_expert = jnp.minimum(jnp.searchsorted(tile_ends, idx, side="right"), n_experts - 1).astype(I32)
    local = idx - tile_starts[tile_expert]
    tile_count = jnp.clip(subs[tile_expert] - local * n_sub, 0, n_sub).astype(I32)
    dest_tiles = dest.reshape(TOP_K, t // tm_combine, tm_combine).transpose(1, 0, 2).reshape(
        t // tm_combine, 1, TOP_K * tm_combine)
    pad = jnp.stack([tile_starts * tile_rows + counts, subs * sub - counts]).astype(I32)
    return tile_expert, tile_count, dest_tiles, pad, n_tiles


def moe_layer(x, g, wr, wg, wu, wd, layer, final_g, final_norm, sub=512, n_sub=4, tm_combine=256):
    t, d = x.shape
    n_experts = wr.shape[1]
    tm_combine = min(tm_combine, t)
    wr_pad = jnp.zeros((d, LANES), BF16).at[:, :n_experts].set(wr.astype(BF16))
    h, meta, picks, counts = moe_router(x, g, wr_pad, n_experts)
    n_slabs = d // LANES
    tile_expert, tile_count, dest_tiles, pad, n_tiles = _route(picks, counts, n_experts, sub, n_sub,
                                                               tm_combine)
    slab_dest = dest_tiles * n_slabs
    xs = moe_dispatch(h, slab_dest, pad, tile_count, sub, n_sub, n_slabs)
    y = moe_experts(xs, tile_expert, tile_count, wg, wu, wd, layer, sub, n_sub)
    return moe_combine(x, meta, slab_dest, y, final_g, final_norm, tm_combine)


def attention_mixers(x, norm_g, w_in, lam_params, subln_g, sinks, layer):
    b, s, d = x.shape
    diff_heads = d // (4 * HEAD_DIM)
    swa_q_heads = d // (2 * HEAD_DIM)
    swa_kv_heads = max(1, swa_q_heads // 4)
    group = swa_q_heads // swa_kv_heads
    a_width = diff_heads * 2 * HEAD_DIM
    lam_init = 0.8 - 0.6 * math.exp(-0.3 * layer)
    x2 = x.reshape(b * s, d)
    proj = rms_matmul(x2, norm_g, w_in).reshape(b, s, -1)
    oa = diff_attention(proj, lam_params, subln_g, diff_heads, lam_init)
    q_col = 3 * a_width
    k_col = q_col + swa_q_heads * HEAD_DIM
    v_col = k_col + swa_kv_heads * HEAD_DIM
    ob = sliding_window_attention(proj, sinks, q_col, k_col, v_col, swa_kv_heads, group)
    return oa.reshape(b * s, -1), ob.reshape(b * s, -1)


def attention_layer(x, norm_g, w_in, lam_params, subln_g, sinks, w_out, layer):
    b, s, d = x.shape
    oa, ob = attention_mixers(x, norm_g, w_in, lam_params, subln_g, sinks, layer)
    return out_projection(x.reshape(b * s, d), oa, ob, w_out)


def kernel(x, attn_norm_g, w_in_att, diff_lambda, diff_subln_g, attn_sinks, w_out_att, ffn_norm_g, w_ffn_gate, w_ffn_up, w_ffn_down, conv_norm_g, w_pw1, b_pw1, w_dw, b_dw, conv_ln_g, conv_ln_b, w_pw2, b_pw2, moe_norm_g, w_router, w_exp_gate, w_exp_up, w_exp_down, final_norm_g):
    b, s, d = x.shape
    depth = attn_norm_g.shape[0] + conv_norm_g.shape[0]
    assert depth % 2 == 0, "the final RMSNorm is fused into the last expert layer"
    bf = lambda w: w.astype(BF16)
    for layer in range(depth):
        i = layer // 2
        if layer % 2 == 0:
            oa, ob = attention_mixers(x, attn_norm_g[i], bf(w_in_att[i]), diff_lambda[i], diff_subln_g[i],
                                      attn_sinks[i], layer)
            x2 = mix_ffn(x.reshape(b * s, d), oa, ob, bf(w_out_att[i]), ffn_norm_g[i],
                         bf(w_ffn_gate[i]), bf(w_ffn_up[i]), bf(w_ffn_down[i]))
            x = x2.reshape(b, s, d)
        else:
            x = conformer_conv(x, conv_norm_g[i], bf(w_pw1[i]), b_pw1[i], w_dw[i], b_dw[i],
                               conv_ln_g[i], conv_ln_b[i], bf(w_pw2[i]), b_pw2[i])
            x2 = moe_layer(x.reshape(b * s, d), moe_norm_g[i], w_router[i], w_exp_gate,
                           w_exp_up, w_exp_down, i, final_norm_g,
                           final_norm=(layer == depth - 1))
            x = x2.reshape(b, s, d)
    return x
```

```python
import functools
import math

import jax
import jax.numpy as jnp
from jax import lax
from jax.experimental import pallas as pl
from jax.experimental.pallas import tpu as pltpu

BF16 = jnp.bfloat16
F32 = jnp.float32
I32 = jnp.int32

LOG2_E = 1.4426950408889634
RMS_EPS = 1e-6
LN_EPS = 1e-5
HEAD_DIM = 64
ATTN_BLOCK = 128
CONV_WIDTH = 31
TOP_K = 2
LANES = 128
F32_SUBLANES = 8
CONV_HIST = 32
V7X_VMEM_LIMIT = 56 * 1024 * 1024


def _params(n_axes):
    return pltpu.CompilerParams(dimension_semantics=("arbitrary",) * n_axes,
                                vmem_limit_bytes=V7X_VMEM_LIMIT)


def _rms(x, g):
    return x * lax.rsqrt(jnp.mean(x * x, axis=-1, keepdims=True) + RMS_EPS) * g


def _pick(n, pref):
    t = min(n, pref)
    while n % t:
        t -= LANES if t > LANES else 8
    return t


def _rms_matmul_body(x_ref, g_ref, w_ref, o_ref):
    h = _rms(x_ref[...], g_ref[...]).astype(BF16)
    o_ref[...] = jnp.dot(h, w_ref[...], preferred_element_type=F32).astype(o_ref.dtype)


def rms_matmul(x, g, w, tm=512):
    t, d = x.shape
    n = w.shape[1]
    tm = _pick(t, tm)
    return pl.pallas_call(
        _rms_matmul_body,
        grid=(t // tm,),
        in_specs=[pl.BlockSpec((tm, d), lambda i: (i, 0)),
                  pl.BlockSpec((1, d), lambda i: (0, 0)),
                  pl.BlockSpec((d, n), lambda i: (0, 0))],
        out_specs=pl.BlockSpec((tm, n), lambda i: (i, 0)),
        out_shape=jax.ShapeDtypeStruct((t, n), BF16),
        compiler_params=_params(1),
        name="rms_inproj",
    )(x, g.reshape(1, d), w)


def _diff_attn_body(q_ref, k_ref, v_ref, lam_ref, g_ref, o_ref, q_ref2, m_ref, l_ref, acc_ref,
                    *, blk, lam_init):
    qi = pl.program_id(2)
    hw = 2 * HEAD_DIM
    half = blk // 2
    lane = lax.broadcasted_iota(I32, (half, hw), 1)
    q = (q_ref[0].astype(F32) * (HEAD_DIM ** -0.5 * LOG2_E)).astype(BF16)
    zero = jnp.zeros((half, hw), BF16)
    q_ref2[...] = jnp.concatenate(
        [jnp.where(keep, q[r:r + half], zero)
         for r in (0, half) for keep in (lane < HEAD_DIM, lane >= HEAD_DIM)], axis=0)

    m_ref[...] = jnp.full(m_ref.shape, -jnp.inf, F32)
    l_ref[...] = jnp.zeros(l_ref.shape, F32)
    acc_ref[...] = jnp.zeros(acc_ref.shape, F32)

    def block(key_start, n_keys, row0, n_rows, masked):
        rows = pl.ds(row0, n_rows)
        k = k_ref[0, pl.ds(key_start, n_keys), :]
        v = v_ref[0, pl.ds(key_start, n_keys), :]
        s = lax.dot_general(q_ref2[rows, :], k, (((1,), (1,)), ((), ())), preferred_element_type=F32)
        if masked:
            row = lax.broadcasted_iota(I32, (n_rows, n_keys), 0) & (half - 1)
            col = lax.broadcasted_iota(I32, (n_rows, n_keys), 1)
            s = jnp.where(col <= row, s, -jnp.inf)
        tiles = [s[:, c * LANES:(c + 1) * LANES] for c in range(n_keys // LANES)]
        m_prev = m_ref[rows, :]
        m_new = jnp.maximum(m_prev, jnp.max(functools.reduce(jnp.maximum, tiles), axis=-1, keepdims=True))
        alpha = jnp.exp2(m_prev - m_new)
        p_tiles = [jnp.exp2(t - m_new) for t in tiles]
        l_ref[rows, :] = alpha * l_ref[rows, :] + functools.reduce(jnp.add, p_tiles)
        p = jnp.concatenate(p_tiles, axis=1).astype(BF16)
        acc_ref[rows, :] = alpha * acc_ref[rows, :] + jnp.dot(p, v, preferred_element_type=F32)
        m_ref[rows, :] = m_new

    def full_block(ki):
        block(pl.multiple_of(ki * blk, blk), blk, 0, 2 * blk, False)

    def pair(j, carry):
        full_block(2 * j)
        full_block(2 * j + 1)
        return carry

    lax.fori_loop(0, qi // 2, pair, 0)

    @pl.when(qi % 2 == 1)
    def _():
        full_block(qi - 1)

    diag = pl.multiple_of(qi * blk, blk)
    block(diag, half, 0, 2 * half, True)
    block(diag, half, 2 * half, 2 * half, False)
    block(diag + half, half, 2 * half, 2 * half, True)

    lp = lam_ref[...]
    lam = (jnp.exp(jnp.sum(lp[0:1] * lp[1:2], axis=-1, keepdims=True))
           - jnp.exp(jnp.sum(lp[2:3] * lp[3:4], axis=-1, keepdims=True)) + lam_init)
    o = acc_ref[...] / jnp.sum(l_ref[...], axis=-1, keepdims=True)
    o = (jnp.concatenate([o[0:half], o[2 * half:3 * half]], axis=0)
         - lam * jnp.concatenate([o[half:2 * half], o[3 * half:4 * half]], axis=0))
    o = _rms(o, g_ref[...]) * (1.0 - lam_init)
    o_ref[0] = o.astype(o_ref.dtype)


def diff_attention(proj, lam_params, subln_g, n_heads, lam_init, blk=1024):
    b, s, _ = proj.shape
    blk = _pick(s, blk)
    assert blk & (blk - 1) == 0, "the causal mask uses power-of-two half blocks"
    hw = 2 * HEAD_DIM
    kernel = functools.partial(_diff_attn_body, blk=blk, lam_init=lam_init)
    return pl.pallas_call(
        kernel,
        grid=(b, n_heads, s // blk),
        in_specs=[pl.BlockSpec((1, blk, hw), lambda bi, h, qi: (bi, qi, h)),
                  pl.BlockSpec((1, s, hw), lambda bi, h, qi: (bi, 0, n_heads + h)),
                  pl.BlockSpec((1, s, hw), lambda bi, h, qi: (bi, 0, 2 * n_heads + h)),
                  pl.BlockSpec((4, HEAD_DIM), lambda bi, h, qi: (0, 0)),
                  pl.BlockSpec((1, hw), lambda bi, h, qi: (0, 0))],
        out_specs=pl.BlockSpec((1, blk, hw), lambda bi, h, qi: (bi, qi, h)),
        out_shape=jax.ShapeDtypeStruct((b, s, n_heads * hw), BF16),
        scratch_shapes=[pltpu.VMEM((2 * blk, hw), BF16), pltpu.VMEM((2 * blk, LANES), F32),
                        pltpu.VMEM((2 * blk, LANES), F32), pltpu.VMEM((2 * blk, hw), F32)],
        compiler_params=_params(3),
        name="diff_attention",
    )(proj, proj, proj, lam_params, subln_g.reshape(1, hw))


def _swa_body(sink_ref, q_ref, kc_ref, kp_ref, vc_ref, vp_ref, o_ref, *, tq, kv_heads, group):
    t = pl.program_id(1)
    w = ATTN_BLOCK
    row = lax.broadcasted_iota(I32, (group * w, 2 * w), 0) & (w - 1)
    col = lax.broadcasted_iota(I32, (group * w, 2 * w), 1)
    band = (col > row) & (col <= row + w)
    band_first = band & ((col >= w) | (t > 0))
    low_half = lax.broadcasted_iota(I32, (w, LANES), 1) < HEAD_DIM
    for j in range(tq // w):
        rows = slice(j * w, (j + 1) * w)
        if j == 0:
            k2 = jnp.concatenate([kp_ref[0], kc_ref[0, rows, :]], axis=0)
            v2 = jnp.concatenate([vp_ref[0], vc_ref[0, rows, :]], axis=0)
            mask = band_first
        else:
            k2 = kc_ref[0, (j - 1) * w:(j + 1) * w, :]
            v2 = vc_ref[0, (j - 1) * w:(j + 1) * w, :]
            mask = band
        outs = []
        for g in range(kv_heads):
            kg = k2[:, g * HEAD_DIM:(g + 1) * HEAD_DIM]
            vg = v2[:, g * HEAD_DIM:(g + 1) * HEAD_DIM]
            kdup = jnp.concatenate([kg, kg], axis=1)
            vdup = jnp.concatenate([vg, vg], axis=1)
            heads = [g * group + i for i in range(group)]
            q_tiles = []
            for h in heads:
                q = q_ref[0, rows, (h // 2) * LANES:(h // 2 + 1) * LANES]
                q = (q.astype(F32) * (HEAD_DIM ** -0.5 * LOG2_E)).astype(BF16)
                own_half = low_half if h % 2 == 0 else jnp.logical_not(low_half)
                q_tiles.append(jnp.where(own_half, q, jnp.zeros_like(q)))
            sink = jnp.concatenate([jnp.full((w, LANES), sink_ref[h] * LOG2_E, F32) for h in heads], axis=0)
            s = lax.dot_general(jnp.concatenate(q_tiles, axis=0), kdup, (((1,), (1,)), ((), ())),
                                preferred_element_type=F32)
            s = jnp.where(mask, s, -jnp.inf)
            s0, s1 = s[:, :w], s[:, w:]
            m = jnp.maximum(jnp.max(jnp.maximum(s0, s1), axis=-1, keepdims=True), sink)
            e0 = jnp.exp2(s0 - m)
            e1 = jnp.exp2(s1 - m)
            denom = jnp.sum(e0 + e1, axis=-1, keepdims=True) + jnp.exp2(sink - m)
            e = jnp.concatenate([e0, e1], axis=1).astype(BF16)
            o = jnp.dot(e, vdup, preferred_element_type=F32) / denom
            outs.extend(o[i * w:(i + 1) * w, :] for i in range(group))
        tiles = [jnp.where(low_half, outs[h], outs[h + 1]) for h in range(0, len(outs), 2)]
        o_ref[0, rows, :] = jnp.concatenate(tiles, axis=1).astype(o_ref.dtype)


def sliding_window_attention(proj, sinks, q_col, k_col, v_col, kv_heads, group, tq=512):
    b, s, _ = proj.shape
    tq = _pick(s, tq)
    qw = kv_heads * group * HEAD_DIM
    kw = kv_heads * HEAD_DIM
    sub = tq // ATTN_BLOCK
    kernel = functools.partial(_swa_body, tq=tq, kv_heads=kv_heads, group=group)
    prev = lambda bi, t: (bi, jnp.maximum(t * sub - 1, 0), 0)
    return pl.pallas_call(
        kernel,
        grid=(b, s // tq),
        in_specs=[pl.BlockSpec(memory_space=pltpu.SMEM),
                  pl.BlockSpec((1, tq, qw), lambda bi, t: (bi, t, q_col // qw)),
                  pl.BlockSpec((1, tq, kw), lambda bi, t: (bi, t, k_col // kw)),
                  pl.BlockSpec((1, ATTN_BLOCK, kw), lambda bi, t: prev(bi, t)[:2] + (k_col // kw,)),
                  pl.BlockSpec((1, tq, kw), lambda bi, t: (bi, t, v_col // kw)),
                  pl.BlockSpec((1, ATTN_BLOCK, kw), lambda bi, t: prev(bi, t)[:2] + (v_col // kw,))],
        out_specs=pl.BlockSpec((1, tq, qw), lambda bi, t: (bi, t, 0)),
        out_shape=jax.ShapeDtypeStruct((b, s, qw), BF16),
        compiler_params=_params(2),
        name="swa_attention",
    )(sinks, proj, proj, proj, proj, proj)


def _mix_ffn_body(x_ref, a_ref, b_ref, wo_ref, g_ref, wg_ref, wu_ref, wd_ref, o_ref, *, ff_chunk):
    mix = jnp.concatenate([a_ref[...], b_ref[...]], axis=1)
    x1 = x_ref[...] + jnp.dot(mix, wo_ref[...], preferred_element_type=F32)
    h = _rms(x1, g_ref[...]).astype(BF16)
    y = x1
    for c in range(wg_ref.shape[1] // ff_chunk):
        cols = slice(c * ff_chunk, (c + 1) * ff_chunk)
        a = jnp.dot(h, wg_ref[:, cols], preferred_element_type=F32)
        u = jnp.dot(h, wu_ref[:, cols], preferred_element_type=F32)
        act = (a * jax.nn.sigmoid(a) * u).astype(BF16)
        y = y + jnp.dot(act, wd_ref[cols, :], preferred_element_type=F32)
    o_ref[...] = y


def mix_ffn(x, oa, ob, wo, g, wg, wu, wd, tm=512, ff_chunk=1408):
    t, d = x.shape
    f = wg.shape[1]
    tm = _pick(t, tm)
    ff_chunk = _pick(f, ff_chunk)
    resident = lambda shape: pl.BlockSpec(shape, lambda i: (0, 0), pipeline_mode=pl.Buffered(1))
    return pl.pallas_call(
        functools.partial(_mix_ffn_body, ff_chunk=ff_chunk),
        grid=(t // tm,),
        in_specs=[pl.BlockSpec((tm, d), lambda i: (i, 0)),
                  pl.BlockSpec((tm, oa.shape[1]), lambda i: (i, 0)),
                  pl.BlockSpec((tm, ob.shape[1]), lambda i: (i, 0)),
                  resident(wo.shape), resident((1, d)),
                  resident(wg.shape), resident(wu.shape), resident(wd.shape)],
        out_specs=pl.BlockSpec((tm, d), lambda i: (i, 0)),
        out_shape=jax.ShapeDtypeStruct((t, d), F32),
        compiler_params=_params(1),
        name="mix_ffn",
    )(x, oa, ob, wo, g.reshape(1, d), wg, wu, wd)


def _conv_body(x_ref, g_ref, w1_ref, b1_ref, wdw_ref, bdw_ref, lng_ref, lnb_ref, w2_ref, b2_ref,
               o_ref, u_ref, v_ref, wb_ref, c_ref, *, tm, rows_per_chunk):
    d = x_ref.shape[-1]
    t = pl.program_id(1)
    sublanes = wb_ref.shape[1]

    @pl.when(t == 0)
    def _():
        u_ref[:, 0:CONV_HIST, :] = jnp.zeros((d // LANES, CONV_HIST, LANES), F32)
        wb_ref[...] = jnp.broadcast_to(wdw_ref[...][:, None, :], wb_ref.shape)

    h = _rms(x_ref[0], g_ref[...]).astype(BF16)
    z = jnp.dot(h, w1_ref[...], preferred_element_type=F32) + b1_ref[...]
    u = z[:, :d] * jax.nn.sigmoid(z[:, d:])
    n_slabs = d // LANES
    for c in range(n_slabs):
        u_ref[c, CONV_HIST:CONV_HIST + tm, :] = u[:, c * LANES:(c + 1) * LANES]

    first_tap = CONV_HIST - (CONV_WIDTH - 1)
    conv_rows = 8 * sublanes

    for c in range(n_slabs):
        lanes = slice(c * LANES, (c + 1) * LANES)
        w = [wb_ref[j, :, lanes] for j in range(CONV_WIDTH)]
        bias = jnp.zeros((sublanes, LANES), F32) + bdw_ref[:, lanes]

        def rows_block(i, carry, c=c, lanes=lanes, w=w, bias=bias):
            r0 = pl.multiple_of(i * conv_rows, conv_rows)
            for r in range(conv_rows // sublanes):
                sums = [bias, None]
                for j in range(CONV_WIDTH):
                    tap = u_ref[c, pl.ds(r0 + (first_tap + j + r * sublanes), sublanes, stride=1), :] * w[j]
                    sums[j % 2] = tap if sums[j % 2] is None else sums[j % 2] + tap
                c_ref[pl.ds(r0 + r * sublanes, sublanes), lanes] = sums[0] + sums[1]
            return carry

        lax.fori_loop(0, tm // conv_rows, rows_block, 0)

    def chunk(i, carry):
        r0 = pl.multiple_of(i * rows_per_chunk, rows_per_chunk)
        acc = c_ref[pl.ds(r0, rows_per_chunk), :]
        mu = jnp.mean(acc, axis=-1, keepdims=True)
        xc = acc - mu
        var = jnp.mean(xc * xc, axis=-1, keepdims=True)
        y = xc * lax.rsqrt(var + LN_EPS) * lng_ref[...] + lnb_ref[...]
        v_ref[pl.ds(r0, rows_per_chunk), :] = (y * jax.nn.sigmoid(y)).astype(BF16)
        return carry

    lax.fori_loop(0, tm // rows_per_chunk, chunk, 0)
    u_ref[:, 0:CONV_HIST, :] = u_ref[:, tm:tm + CONV_HIST, :]
    o_ref[0] = x_ref[0] + jnp.dot(v_ref[...], w2_ref[...], preferred_element_type=F32) + b2_ref[...]


def conformer_conv(x, g, w1, b1, wdw, bdw, lng, lnb, w2, b2, tm=512, rows_per_chunk=128):
    b, s, d = x.shape
    tm = _pick(s, tm)
    kernel = functools.partial(_conv_body, tm=tm, rows_per_chunk=rows_per_chunk)
    vec = lambda n: pl.BlockSpec((1, n), lambda bi, t: (0, 0))
    return pl.pallas_call(
        kernel,
        grid=(b, s // tm),
        in_specs=[pl.BlockSpec((1, tm, d), lambda bi, t: (bi, t, 0)),
                  vec(d),
                  pl.BlockSpec((d, 2 * d), lambda bi, t: (0, 0)),
                  vec(2 * d),
                  pl.BlockSpec((CONV_WIDTH, d), lambda bi, t: (0, 0)),
                  vec(d), vec(d), vec(d),
                  pl.BlockSpec((d, d), lambda bi, t: (0, 0)),
                  vec(d)],
        out_specs=pl.BlockSpec((1, tm, d), lambda bi, t: (bi, t, 0)),
        out_shape=jax.ShapeDtypeStruct((b, s, d), F32),
        scratch_shapes=[pltpu.VMEM((d // LANES, tm + CONV_HIST, LANES), F32), pltpu.VMEM((tm, d), BF16),
                        pltpu.VMEM((CONV_WIDTH, F32_SUBLANES, d), F32), pltpu.VMEM((tm, d), F32)],
        compiler_params=_params(2),
        name="conformer_conv",
    )(x, g.reshape(1, d), w1, b1.reshape(1, 2 * d), wdw, bdw.reshape(1, d), lng.reshape(1, d),
      lnb.reshape(1, d), w2, b2.reshape(1, d))


def _router_body(x_ref, g_ref, wr_ref, h_ref, meta_ref, picks_ref, counts_ref, *, n_experts):
    @pl.when(pl.program_id(0) == 0)
    def _():
        counts_ref[...] = jnp.zeros(counts_ref.shape, F32)

    h = _rms(x_ref[...], g_ref[...])
    _store_slabs(h_ref, 0, h)
    logits = jnp.dot(h.astype(BF16), wr_ref[...], preferred_element_type=F32)
    lane = lax.broadcasted_iota(I32, logits.shape, 1).astype(F32)
    neg = jnp.full_like(logits, -jnp.inf)
    far = jnp.full_like(logits, float(LANES))
    l1 = jnp.where(lane < n_experts, logits, neg)
    m1 = jnp.max(l1, axis=-1, keepdims=True)
    i1 = jnp.min(jnp.where(l1 == m1, lane, far), axis=-1, keepdims=True)
    l2 = jnp.where(lane == i1, neg, l1)
    m2 = jnp.max(l2, axis=-1, keepdims=True)
    i2 = jnp.min(jnp.where(l2 == m2, lane, far), axis=-1, keepdims=True)
    e2 = jnp.exp(m2 - m1)
    w1 = 1.0 / (1.0 + e2)
    w2 = e2 / (1.0 + e2)
    tm = logits.shape[0]
    hot1 = (lane == i1).astype(F32)
    hot2 = (lane == i2).astype(F32)
    both = (hot1 + hot2).astype(BF16)
    earlier = (lax.broadcasted_iota(I32, (tm, tm), 1) < lax.broadcasted_iota(I32, (tm, tm), 0)).astype(BF16)
    before = jnp.dot(earlier, both, preferred_element_type=F32) + counts_ref[...]
    r1 = jnp.sum(hot1 * before, axis=-1, keepdims=True)
    r2 = jnp.sum(hot2 * before, axis=-1, keepdims=True)
    counts_ref[...] += jnp.sum(hot1 + hot2, axis=0, keepdims=True)
    zero = jnp.zeros_like(logits)
    meta = jnp.where(lane == 0, i1,
           jnp.where(lane == 1, i2,
           jnp.where(lane == 2, w1,
           jnp.where(lane == 3, w2,
           jnp.where(lane == 4, r1, jnp.where(lane == 5, r2, zero))))))
    meta_ref[...] = meta
    picks_ref[...] = jnp.transpose(meta)[0:picks_ref.shape[0], :]


def moe_router(x, g, wr, n_experts, tm=512):
    t, d = x.shape
    tm = _pick(t, tm)
    kernel = functools.partial(_router_body, n_experts=n_experts)
    return pl.pallas_call(
        kernel,
        grid=(t // tm,),
        in_specs=[pl.BlockSpec((tm, d), lambda i: (i, 0)),
                  pl.BlockSpec((1, d), lambda i: (0, 0)),
                  pl.BlockSpec((d, LANES), lambda i: (0, 0))],
        out_specs=[pl.BlockSpec((tm * (d // LANES), LANES), lambda i: (i, 0)),
                   pl.BlockSpec((tm, LANES), lambda i: (i, 0)),
                   pl.BlockSpec((F32_SUBLANES, tm), lambda i: (0, i)),
                   pl.BlockSpec((1, LANES), lambda i: (0, 0))],
        out_shape=[jax.ShapeDtypeStruct((t * (d // LANES), LANES), F32),
                   jax.ShapeDtypeStruct((t, LANES), F32),
                   jax.ShapeDtypeStruct((F32_SUBLANES, t), F32),
                   jax.ShapeDtypeStruct((1, LANES), F32)],
        compiler_params=_params(1),
        name="moe_router",
    )(x, g.reshape(1, d), wr)


GATHER_UNROLL = 8
N_DMA_PRIORITIES = 2


def _store_slabs(ref, first_row, value):
    n, d = value.shape
    n_slabs = d // LANES
    for c in range(n_slabs):
        ref[pl.ds(first_row * n_slabs + c, n, stride=n_slabs), :] = value[:, c * LANES:(c + 1) * LANES]


def _load_slabs(ref, first_row, n, d):
    n_slabs = d // LANES
    return jnp.concatenate([ref[pl.ds(first_row * n_slabs + c, n, stride=n_slabs), :]
                            for c in range(n_slabs)], axis=1)


def _dispatch_body(pad_ref, tn_ref, d_ref, h_hbm, xs_hbm, zero_ref, buf_ref, in_sem, sem, zsem,
                   *, tm, n_slabs, n_experts, n_sub):
    i = pl.program_id(0)
    n = pl.num_programs(0)
    rows = tm * n_slabs
    n_ring = buf_ref.shape[0]

    def stage(tile):
        return pltpu.make_async_copy(h_hbm.at[pl.ds(pl.multiple_of(tile * rows, rows), rows), :],
                                     buf_ref.at[tile % n_ring], in_sem.at[tile % n_ring])

    def tile_copy(slot):
        return pltpu.make_async_copy(buf_ref.at[0], xs_hbm.at[pl.ds(0, rows), :], sem.at[slot])

    @pl.when(i == 0)
    def _():
        stage(0).start()

    @pl.when(i + 1 < n)
    def _():
        stage(i + 1).start()

    stage(i).wait()
    slot = i % 2
    ring = i % n_ring
    for rank in range(TOP_K):
        def body(it, carry, rank=rank):
            base = it * GATHER_UNROLL
            dsts = [pl.multiple_of(d_ref[0, 0, rank * tm + base + k], n_slabs) for k in range(GATHER_UNROLL)]
            for k in range(GATHER_UNROLL):
                src = pl.multiple_of((base + k) * n_slabs, n_slabs)
                pltpu.make_async_copy(buf_ref.at[ring, pl.ds(src, n_slabs), :],
                                      xs_hbm.at[pl.ds(dsts[k], n_slabs), :], sem.at[slot]
                                      ).start(priority=k % N_DMA_PRIORITIES)
            return carry
        lax.fori_loop(0, tm // GATHER_UNROLL, body, 0)

    @pl.when(i == 0)
    def _():
        zero_ref[...] = jnp.zeros(zero_ref.shape, F32)
        zero_row = zero_ref.at[pl.ds(0, n_slabs), :]
        sub_rows = zero_ref.shape[0]

        def unused_sub_tiles(fn):
            def tbody(tile, carry):
                for s in range(n_sub):
                    @pl.when(s >= tn_ref[tile])
                    def _():
                        dst = pl.multiple_of((tile * n_sub + s) * sub_rows, sub_rows)
                        fn(pltpu.make_async_copy(zero_ref, xs_hbm.at[pl.ds(dst, sub_rows), :], zsem))
                return carry
            lax.fori_loop(0, tn_ref.shape[0], tbody, 0)

        def tail_rows(fn):
            for e in range(n_experts):
                def zbody(r, carry, e=e):
                    dst = pl.multiple_of((pad_ref[0, e] + r) * n_slabs, n_slabs)
                    fn(pltpu.make_async_copy(zero_row, xs_hbm.at[pl.ds(dst, n_slabs), :], zsem))
                    return carry
                lax.fori_loop(0, pad_ref[1, e], zbody, 0)

        unused_sub_tiles(lambda cp: cp.start())
        tail_rows(lambda cp: cp.start())
        unused_sub_tiles(lambda cp: cp.wait())
        tail_rows(lambda cp: cp.wait())

    @pl.when(i > 0)
    def _():
        for rank in range(TOP_K):
            tile_copy(1 - slot).wait()

    @pl.when(i == n - 1)
    def _():
        for rank in range(TOP_K):
            tile_copy(slot).wait()


def moe_dispatch(h, dest, pad, tile_count, sub, n_sub, n_slabs):
    n_tok_tiles = dest.shape[0]
    tm = dest.shape[2] // TOP_K
    n_experts = pad.shape[1]
    n_rows = tile_count.shape[0] * n_sub * sub
    grid_spec = pltpu.PrefetchScalarGridSpec(
        num_scalar_prefetch=2,
        grid=(n_tok_tiles,),
        in_specs=[pl.BlockSpec((1, 1, TOP_K * tm), lambda i, pad, tn: (i, 0, 0), memory_space=pltpu.SMEM),
                  pl.BlockSpec(memory_space=pl.ANY)],
        out_specs=pl.BlockSpec(memory_space=pl.ANY),
        scratch_shapes=[pltpu.VMEM((sub * n_slabs, LANES), F32), pltpu.VMEM((3, tm * n_slabs, LANES), F32),
                        pltpu.SemaphoreType.DMA((3,)), pltpu.SemaphoreType.DMA((2,)),
                        pltpu.SemaphoreType.DMA(())],
    )
    return pl.pallas_call(
        functools.partial(_dispatch_body, tm=tm, n_slabs=n_slabs, n_experts=n_experts, n_sub=n_sub),
        grid_spec=grid_spec,
        out_shape=jax.ShapeDtypeStruct((n_rows * n_slabs, LANES), F32),
        compiler_params=_params(1),
        name="moe_dispatch",
    )(pad, tile_count, dest, h)


def _moe_body(te_ref, tn_ref, xs_ref, wg_ref, wu_ref, wd_ref, y_hbm,
              y_ref, xb_ref, wgb_ref, wub_ref, wdb_ref, stage_ref, zero_ref, osem, zsem, *, sub, n_sub):
    i = pl.program_id(0)
    j = pl.program_id(1)
    last = j == pl.num_programs(1) - 1
    count = tn_ref[i]
    d = wg_ref.shape[2]
    slab_rows = stage_ref.shape[1]

    @pl.when((i == 0) & (j == 0))
    def _():
        zero_ref[...] = jnp.zeros(zero_ref.shape, F32)

    for s in range(n_sub):
        rows = pl.ds(s * sub, sub)

        @pl.when((j == 0) & (s < count))
        def _():
            xb_ref[rows, :] = _load_slabs(xs_ref, s * sub, sub, d).astype(BF16)
            y_ref[rows, :] = jnp.zeros((sub, d), F32)

    def expert_rows(s, wg, wu, wd):
        rows = pl.ds(s * sub, sub)
        x = xb_ref[rows, :]
        a = jnp.dot(x, wg, preferred_element_type=F32)
        u = jnp.dot(x, wu, preferred_element_type=F32)
        act = (a * jax.nn.sigmoid(a) * u).astype(BF16)
        y_ref[rows, :] += jnp.dot(act, wd, preferred_element_type=F32)

    def rounded_weights():
        return tuple(r[0, 0].astype(BF16) for r in (wg_ref, wu_ref, wd_ref))

    @pl.when(count == n_sub)
    def _():
        weights = rounded_weights()
        for s in range(n_sub):
            expert_rows(s, *weights)

    for s in range(n_sub - 1):
        @pl.when((s < count) & (count < n_sub))
        def _():
            if s == 0:
                wgb_ref[...], wub_ref[...], wdb_ref[...] = rounded_weights()
            expert_rows(s, wgb_ref[...], wub_ref[...], wdb_ref[...])

    def out_copy(src, s):
        dst = pl.multiple_of((i * n_sub + s) * slab_rows, slab_rows)
        return src, y_hbm.at[pl.ds(dst, slab_rows), :]

    n_tiles = pl.num_programs(0)
    prev_count = tn_ref[jnp.maximum(i - 1, 0)]
    next_count = tn_ref[jnp.minimum(i + 1, n_tiles - 1)]
    final_tile = (i == n_tiles - 1) | (next_count == 0)

    def stage_copy(s):
        return pltpu.make_async_copy(*out_copy(stage_ref.at[s % 2], s), osem.at[s % 2])

    for slot in range(2):
        @pl.when(last & (count > 0) & (i > 0) & (prev_count > slot))
        def _():
            stage_copy(slot).wait()

    for s in range(n_sub):
        @pl.when(last & (s < count))
        def _():
            if s >= 2:
                stage_copy(s - 2).wait()
            _store_slabs(stage_ref.at[s % 2], 0, y_ref[pl.ds(s * sub, sub), :])
            stage_copy(s).start()

        @pl.when(last & (s >= count))
        def _():
            pltpu.make_async_copy(*out_copy(zero_ref, s), zsem).start()

    for s in range(n_sub):
        @pl.when(last & final_tile & (s < count) & (s + 2 >= count))
        def _():
            stage_copy(s).wait()

        @pl.when(last & (s >= count))
        def _():
            pltpu.make_async_copy(*out_copy(zero_ref, s), zsem).wait()


def moe_experts(xs, tile_expert, tile_count, wg, wu, wd, layer, sub, n_sub, tf=512):
    d = wg.shape[2]
    n_slabs = d // LANES
    n_tiles = tile_expert.shape[0]
    tile_rows = n_sub * sub
    f = wg.shape[3]
    tf = _pick(f, tf)
    nf = f // tf

    def w_col(i, j, te, tn):
        return (layer, te[i], 0, jnp.where(tn[i] > 0, j, nf - 1))

    def w_row(i, j, te, tn):
        return (layer, te[i], jnp.where(tn[i] > 0, j, nf - 1), 0)

    grid_spec = pltpu.PrefetchScalarGridSpec(
        num_scalar_prefetch=2,
        grid=(n_tiles, nf),
        in_specs=[pl.BlockSpec((tile_rows * n_slabs, LANES), lambda i, j, te, tn: (i, 0)),
                  pl.BlockSpec((1, 1, d, tf), w_col),
                  pl.BlockSpec((1, 1, d, tf), w_col),
                  pl.BlockSpec((1, 1, tf, d), w_row)],
        out_specs=pl.BlockSpec(memory_space=pl.ANY),
        scratch_shapes=[pltpu.VMEM((tile_rows, d), F32), pltpu.VMEM((tile_rows, d), BF16),
                        pltpu.VMEM((d, tf), BF16), pltpu.VMEM((d, tf), BF16), pltpu.VMEM((tf, d), BF16),
                        pltpu.VMEM((2, sub * n_slabs, LANES), F32), pltpu.VMEM((sub * n_slabs, LANES), F32),
                        pltpu.SemaphoreType.DMA((2,)), pltpu.SemaphoreType.DMA(())],
    )
    return pl.pallas_call(
        functools.partial(_moe_body, sub=sub, n_sub=n_sub),
        grid_spec=grid_spec,
        out_shape=jax.ShapeDtypeStruct((n_tiles * tile_rows * n_slabs, LANES), F32),
        compiler_params=_params(2),
        name="moe_experts",
    )(tile_expert, tile_count, xs, wg, wu, wd)


def _combine_body(d_cur_ref, d_nxt_ref, x_ref, meta_ref, g_ref, y_hbm, o_ref, buf_ref, sem,
                  *, tm, final_norm):
    i = pl.program_id(0)
    n = pl.num_programs(0)
    d = x_ref.shape[1]
    n_slabs = d // LANES

    def gather(d_ref, slot):
        def body(it, carry):
            base = it * GATHER_UNROLL
            srcs = [pl.multiple_of(d_ref[0, 0, base + k], n_slabs) for k in range(GATHER_UNROLL)]
            for k in range(GATHER_UNROLL):
                dst = pl.multiple_of((base + k) * n_slabs, n_slabs)
                pltpu.make_async_copy(y_hbm.at[pl.ds(srcs[k], n_slabs), :],
                                      buf_ref.at[slot, pl.ds(dst, n_slabs), :], sem.at[slot]
                                      ).start(priority=k % N_DMA_PRIORITIES)
            return carry
        lax.fori_loop(0, TOP_K * tm // GATHER_UNROLL, body, 0)

    @pl.when(i == 0)
    def _():
        gather(d_cur_ref, 0)

    @pl.when(i + 1 < n)
    def _():
        gather(d_nxt_ref, (i + 1) % 2)

    slot = i % 2
    pltpu.make_async_copy(y_hbm.at[pl.ds(0, TOP_K * tm * n_slabs), :], buf_ref.at[slot],
                          sem.at[slot]).wait()
    meta = meta_ref[...]
    out = (x_ref[...] + meta[:, 2:3] * _load_slabs(buf_ref.at[slot], 0, tm, d)
           + meta[:, 3:4] * _load_slabs(buf_ref.at[slot], tm, tm, d))
    if final_norm:
        out = _rms(out, g_ref[...])
    o_ref[...] = out


def moe_combine(x, meta, dest, y, g, final_norm, tm=256):
    t, d = x.shape
    n = t // tm
    n_slabs = d // LANES
    smem = lambda fn: pl.BlockSpec((1, 1, TOP_K * tm), fn, memory_space=pltpu.SMEM)
    return pl.pallas_call(
        functools.partial(_combine_body, tm=tm, final_norm=final_norm),
        grid=(n,),
        in_specs=[smem(lambda i: (i, 0, 0)),
                  smem(lambda i: (jnp.minimum(i + 1, n - 1), 0, 0)),
                  pl.BlockSpec((tm, d), lambda i: (i, 0)),
                  pl.BlockSpec((tm, LANES), lambda i: (i, 0)),
                  pl.BlockSpec((1, d), lambda i: (0, 0)),
                  pl.BlockSpec(memory_space=pl.ANY)],
        out_specs=pl.BlockSpec((tm, d), lambda i: (i, 0)),
        out_shape=jax.ShapeDtypeStruct((t, d), F32),
        scratch_shapes=[pltpu.VMEM((2, TOP_K * tm * n_slabs, LANES), F32), pltpu.SemaphoreType.DMA((2,))],
        compiler_params=_params(1),
        name="moe_combine",
    )(dest, dest, x, meta, g.reshape(1, d), y)


def _route(picks, counts, n_experts, sub, n_sub, tm_combine):
    t = picks.shape[1]
    tile_rows = sub * n_sub
    expert = picks[0:TOP_K].astype(I32)
    rank = picks[2 * TOP_K:3 * TOP_K].astype(I32)
    counts = counts[0, :n_experts].astype(I32)
    subs = (counts + sub - 1) // sub
    tiles = (subs + n_sub - 1) // n_sub
    tile_ends = jnp.cumsum(tiles)
    tile_starts = tile_ends - tiles
    first_row = jnp.zeros_like(expert)
    for e in range(n_experts):
        first_row = jnp.where(expert == e, tile_starts[e] * tile_rows, first_row)
    dest = first_row + rank
    n_tiles = ((t * TOP_K) // sub + n_experts * n_sub) // n_sub
    idx = jnp.arange(n_tiles, dtype=I32)
    tile_expert = jnp.minimum(jnp.searchsorted(tile_ends, idx, side="right"), n_experts - 1).astype(I32)
    local = idx - tile_starts[tile_expert]
    tile_count = jnp.clip(subs[tile_expert] - local * n_sub, 0, n_sub).astype(I32)
    dest_tiles = dest.reshape(TOP_K, t // tm_combine, tm_combine).transpose(1, 0, 2).reshape(
        t // tm_combine, 1, TOP_K * tm_combine)
    pad = jnp.stack([tile_starts * tile_rows + counts, subs * sub - counts]).astype(I32)
    return tile_expert, tile_count, dest_tiles, pad, n_tiles


def moe_layer(x, g, wr, wg, wu, wd, layer, final_g, final_norm, sub=512, n_sub=4, tm_combine=512):
    t, d = x.shape
    n_experts = wr.shape[1]
    tm_combine = min(tm_combine, t)
    wr_pad = jnp.zeros((d, LANES), BF16).at[:, :n_experts].set(wr.astype(BF16))
    h, meta, picks, counts = moe_router(x, g, wr_pad, n_experts)
    n_slabs = d // LANES
    tile_expert, tile_count, dest_tiles, pad, n_tiles = _route(picks, counts, n_experts, sub, n_sub,
                                                               tm_combine)
    slab_dest = dest_tiles * n_slabs
    xs = moe_dispatch(h, slab_dest, pad, tile_count, sub, n_sub, n_slabs)
    y = moe_experts(xs, tile_expert, tile_count, wg, wu, wd, layer, sub, n_sub)
    return moe_combine(x, meta, slab_dest, y, final_g, final_norm, tm_combine)


def attention_mixers(x, norm_g, w_in, lam_params, subln_g, sinks, layer):
    b, s, d = x.shape
    diff_heads = d // (4 * HEAD_DIM)
    swa_q_heads = d // (2 * HEAD_DIM)
    swa_kv_heads = max(1, swa_q_heads // 4)
    group = swa_q_heads // swa_kv_heads
    a_width = diff_heads * 2 * HEAD_DIM
    lam_init = 0.8 - 0.6 * math.exp(-0.3 * layer)
    x2 = x.reshape(b * s, d)
    proj = rms_matmul(x2, norm_g, w_in).reshape(b, s, -1)
    oa = diff_attention(proj, lam_params, subln_g, diff_heads, lam_init)
    q_col = 3 * a_width
    k_col = q_col + swa_q_heads * HEAD_DIM
    v_col = k_col + swa_kv_heads * HEAD_DIM
    ob = sliding_window_attention(proj, sinks, q_col, k_col, v_col, swa_kv_heads, group)
    return oa.reshape(b * s, -1), ob.reshape(b * s, -1)


def kernel(x, attn_norm_g, w_in_att, diff_lambda, diff_subln_g, attn_sinks, w_out_att, ffn_norm_g, w_ffn_gate, w_ffn_up, w_ffn_down, conv_norm_g, w_pw1, b_pw1, w_dw, b_dw, conv_ln_g, conv_ln_b, w_pw2, b_pw2, moe_norm_g, w_router, w_exp_gate, w_exp_up, w_exp_down, final_norm_g):
    b, s, d = x.shape
    depth = attn_norm_g.shape[0] + conv_norm_g.shape[0]
    assert depth % 2 == 0, "the final RMSNorm is fused into the last expert layer"
    bf = lambda w: w.astype(BF16)
    for layer in range(depth):
        i = layer // 2
        if layer % 2 == 0:
            oa, ob = attention_mixers(x, attn_norm_g[i], bf(w_in_att[i]), diff_lambda[i], diff_subln_g[i],
                                      attn_sinks[i], layer)
            x2 = mix_ffn(x.reshape(b * s, d), oa, ob, bf(w_out_att[i]), ffn_norm_g[i],
                         bf(w_ffn_gate[i]), bf(w_ffn_up[i]), bf(w_ffn_down[i]))
            x = x2.reshape(b, s, d)
        else:
            x = conformer_conv(x, conv_norm_g[i], bf(w_pw1[i]), b_pw1[i], w_dw[i], b_dw[i],
                               conv_ln_g[i], conv_ln_b[i], bf(w_pw2[i]), b_pw2[i])
            x2 = moe_layer(x.reshape(b * s, d), moe_norm_g[i], w_router[i], w_exp_gate,
                           w_exp_up, w_exp_down, i, final_norm_g,
                           final_norm=(layer == depth - 1))
            x = x2.reshape(b, s, d)
    return x
```

```python
import functools
import math

import jax
import jax.numpy as jnp
from jax import lax
from jax.experimental import pallas as pl
from jax.experimental.pallas import tpu as pltpu

BF16 = jnp.bfloat16
F32 = jnp.float32
I32 = jnp.int32

LOG2_E = 1.4426950408889634
RMS_EPS = 1e-6
LN_EPS = 1e-5
HEAD_DIM = 64
ATTN_BLOCK = 128
CONV_WIDTH = 31
TOP_K = 2
LANES = 128
F32_SUBLANES = 8
CONV_HIST = 32
V7X_VMEM_LIMIT = 56 * 1024 * 1024


def _params(n_axes):
    return pltpu.CompilerParams(dimension_semantics=("arbitrary",) * n_axes,
                                vmem_limit_bytes=V7X_VMEM_LIMIT)


def _rms(x, g):
    return x * lax.rsqrt(jnp.mean(x * x, axis=-1, keepdims=True) + RMS_EPS) * g


def _pick(n, pref):
    t = min(n, pref)
    while n % t:
        t -= LANES if t > LANES else 8
    return t


def _rms_matmul_body(x_ref, g_ref, w_ref, o_ref):
    h = _rms(x_ref[...], g_ref[...]).astype(BF16)
    o_ref[...] = jnp.dot(h, w_ref[...], preferred_element_type=F32).astype(o_ref.dtype)


def rms_matmul(x, g, w, tm=1024):
    t, d = x.shape
    n = w.shape[1]
    tm = _pick(t, tm)
    return pl.pallas_call(
        _rms_matmul_body,
        grid=(t // tm,),
        in_specs=[pl.BlockSpec((tm, d), lambda i: (i, 0)),
                  pl.BlockSpec((1, d), lambda i: (0, 0)),
                  pl.BlockSpec((d, n), lambda i: (0, 0))],
        out_specs=pl.BlockSpec((tm, n), lambda i: (i, 0)),
        out_shape=jax.ShapeDtypeStruct((t, n), BF16),
        compiler_params=_params(1),
        name="rms_inproj",
    )(x, g.reshape(1, d), w)


def _diff_attn_body(q_ref, k_ref, v_ref, lam_ref, g_ref, o_ref, q_ref2, m_ref, l_ref, acc_ref,
                    *, blk, lam_init):
    qi = pl.program_id(2)
    hw = 2 * HEAD_DIM
    half = blk // 2
    lane = lax.broadcasted_iota(I32, (half, hw), 1)
    q = (q_ref[0].astype(F32) * (HEAD_DIM ** -0.5 * LOG2_E)).astype(BF16)
    zero = jnp.zeros((half, hw), BF16)
    q_ref2[...] = jnp.concatenate(
        [jnp.where(keep, q[r:r + half], zero)
         for r in (0, half) for keep in (lane < HEAD_DIM, lane >= HEAD_DIM)], axis=0)

    m_ref[...] = jnp.full(m_ref.shape, -jnp.inf, F32)
    l_ref[...] = jnp.zeros(l_ref.shape, F32)
    acc_ref[...] = jnp.zeros(acc_ref.shape, F32)

    def block(key_start, n_keys, row0, n_rows, masked):
        rows = pl.ds(row0, n_rows)
        k = k_ref[0, pl.ds(key_start, n_keys), :]
        v = v_ref[0, pl.ds(key_start, n_keys), :]
        s = lax.dot_general(q_ref2[rows, :], k, (((1,), (1,)), ((), ())), preferred_element_type=F32)
        if masked:
            row = lax.broadcasted_iota(I32, (n_rows, n_keys), 0) & (half - 1)
            col = lax.broadcasted_iota(I32, (n_rows, n_keys), 1)
            s = jnp.where(col <= row, s, -jnp.inf)
        tiles = [s[:, c * LANES:(c + 1) * LANES] for c in range(n_keys // LANES)]
        m_prev = m_ref[rows, :]
        m_new = jnp.maximum(m_prev, jnp.max(functools.reduce(jnp.maximum, tiles), axis=-1, keepdims=True))
        alpha = jnp.exp2(m_prev - m_new)
        p_tiles = [jnp.exp2(t - m_new) for t in tiles]
        l_ref[rows, :] = alpha * l_ref[rows, :] + functools.reduce(jnp.add, p_tiles)
        p = jnp.concatenate(p_tiles, axis=1).astype(BF16)
        acc_ref[rows, :] = alpha * acc_ref[rows, :] + jnp.dot(p, v, preferred_element_type=F32)
        m_ref[rows, :] = m_new

    def full_block(ki):
        block(pl.multiple_of(ki * blk, blk), blk, 0, 2 * blk, False)

    def pair(j, carry):
        full_block(2 * j)
        full_block(2 * j + 1)
        return carry

    lax.fori_loop(0, qi // 2, pair, 0)

    def diagonal_block():
        diag = pl.multiple_of(qi * blk, blk)
        block(diag, half, 0, 2 * half, True)
        block(diag, half, 2 * half, 2 * half, False)
        block(diag + half, half, 2 * half, 2 * half, True)

    @pl.when(qi % 2 == 1)
    def _():
        full_block(qi - 1)
        diagonal_block()

    @pl.when(qi % 2 == 0)
    def _():
        diagonal_block()

    lp = lam_ref[...]
    lam = (jnp.exp(jnp.sum(lp[0:1] * lp[1:2], axis=-1, keepdims=True))
           - jnp.exp(jnp.sum(lp[2:3] * lp[3:4], axis=-1, keepdims=True)) + lam_init)
    o = acc_ref[...] / jnp.sum(l_ref[...], axis=-1, keepdims=True)
    o = (jnp.concatenate([o[0:half], o[2 * half:3 * half]], axis=0)
         - lam * jnp.concatenate([o[half:2 * half], o[3 * half:4 * half]], axis=0))
    o = _rms(o, g_ref[...]) * (1.0 - lam_init)
    o_ref[0] = o.astype(o_ref.dtype)


def diff_attention(proj, lam_params, subln_g, n_heads, lam_init, blk=1024):
    b, s, _ = proj.shape
    blk = _pick(s, blk)
    assert blk & (blk - 1) == 0, "the causal mask uses power-of-two half blocks"
    hw = 2 * HEAD_DIM
    kernel = functools.partial(_diff_attn_body, blk=blk, lam_init=lam_init)
    return pl.pallas_call(
        kernel,
        grid=(b, n_heads, s // blk),
        in_specs=[pl.BlockSpec((1, blk, hw), lambda bi, h, qi: (bi, qi, h)),
                  pl.BlockSpec((1, s, hw), lambda bi, h, qi: (bi, 0, n_heads + h)),
                  pl.BlockSpec((1, s, hw), lambda bi, h, qi: (bi, 0, 2 * n_heads + h)),
                  pl.BlockSpec((4, HEAD_DIM), lambda bi, h, qi: (0, 0)),
                  pl.BlockSpec((1, hw), lambda bi, h, qi: (0, 0))],
        out_specs=pl.BlockSpec((1, blk, hw), lambda bi, h, qi: (bi, qi, h)),
        out_shape=jax.ShapeDtypeStruct((b, s, n_heads * hw), BF16),
        scratch_shapes=[pltpu.VMEM((2 * blk, hw), BF16), pltpu.VMEM((2 * blk, LANES), F32),
                        pltpu.VMEM((2 * blk, LANES), F32), pltpu.VMEM((2 * blk, hw), F32)],
        compiler_params=_params(3),
        name="diff_attention",
    )(proj, proj, proj, lam_params, subln_g.reshape(1, hw))


def _swa_body(sink_ref, q_ref, kc_ref, kp_ref, vc_ref, vp_ref, o_ref, *, tq, kv_heads, group):
    t = pl.program_id(1)
    w = ATTN_BLOCK
    row = lax.broadcasted_iota(I32, (group * w, 2 * w), 0) & (w - 1)
    col = lax.broadcasted_iota(I32, (group * w, 2 * w), 1)
    band = (col > row) & (col <= row + w)
    band_first = band & ((col >= w) | (t > 0))
    low_half = lax.broadcasted_iota(I32, (w, LANES), 1) < HEAD_DIM
    for j in range(tq // w):
        rows = slice(j * w, (j + 1) * w)
        if j == 0:
            k2 = jnp.concatenate([kp_ref[0], kc_ref[0, rows, :]], axis=0)
            v2 = jnp.concatenate([vp_ref[0], vc_ref[0, rows, :]], axis=0)
            mask = band_first
        else:
            k2 = kc_ref[0, (j - 1) * w:(j + 1) * w, :]
            v2 = vc_ref[0, (j - 1) * w:(j + 1) * w, :]
            mask = band
        outs = []
        for g in range(kv_heads):
            kg = k2[:, g * HEAD_DIM:(g + 1) * HEAD_DIM]
            vg = v2[:, g * HEAD_DIM:(g + 1) * HEAD_DIM]
            kdup = jnp.concatenate([kg, kg], axis=1)
            vdup = jnp.concatenate([vg, vg], axis=1)
            heads = [g * group + i for i in range(group)]
            q_tiles = []
            for h in heads:
                q = q_ref[0, rows, (h // 2) * LANES:(h // 2 + 1) * LANES]
                q = (q.astype(F32) * (HEAD_DIM ** -0.5 * LOG2_E)).astype(BF16)
                own_half = low_half if h % 2 == 0 else jnp.logical_not(low_half)
                q_tiles.append(jnp.where(own_half, q, jnp.zeros_like(q)))
            sink = jnp.concatenate([jnp.full((w, LANES), sink_ref[h] * LOG2_E, F32) for h in heads], axis=0)
            s = lax.dot_general(jnp.concatenate(q_tiles, axis=0), kdup, (((1,), (1,)), ((), ())),
                                preferred_element_type=F32)
            s = jnp.where(mask, s, -jnp.inf)
            s0, s1 = s[:, :w], s[:, w:]
            m = jnp.maximum(jnp.max(jnp.maximum(s0, s1), axis=-1, keepdims=True), sink)
            e0 = jnp.exp2(s0 - m)
            e1 = jnp.exp2(s1 - m)
            denom = jnp.sum(e0 + e1, axis=-1, keepdims=True) + jnp.exp2(sink - m)
            e = jnp.concatenate([e0, e1], axis=1).astype(BF16)
            o = jnp.dot(e, vdup, preferred_element_type=F32) / denom
            outs.extend(o[i * w:(i + 1) * w, :] for i in range(group))
        tiles = [jnp.where(low_half, outs[h], outs[h + 1]) for h in range(0, len(outs), 2)]
        o_ref[0, rows, :] = jnp.concatenate(tiles, axis=1).astype(o_ref.dtype)


def sliding_window_attention(proj, sinks, q_col, k_col, v_col, kv_heads, group, tq=512):
    b, s, _ = proj.shape
    tq = _pick(s, tq)
    qw = kv_heads * group * HEAD_DIM
    kw = kv_heads * HEAD_DIM
    sub = tq // ATTN_BLOCK
    kernel = functools.partial(_swa_body, tq=tq, kv_heads=kv_heads, group=group)
    prev = lambda bi, t: (bi, jnp.maximum(t * sub - 1, 0), 0)
    return pl.pallas_call(
        kernel,
        grid=(b, s // tq),
        in_specs=[pl.BlockSpec(memory_space=pltpu.SMEM),
                  pl.BlockSpec((1, tq, qw), lambda bi, t: (bi, t, q_col // qw)),
                  pl.BlockSpec((1, tq, kw), lambda bi, t: (bi, t, k_col // kw)),
                  pl.BlockSpec((1, ATTN_BLOCK, kw), lambda bi, t: prev(bi, t)[:2] + (k_col // kw,)),
                  pl.BlockSpec((1, tq, kw), lambda bi, t: (bi, t, v_col // kw)),
                  pl.BlockSpec((1, ATTN_BLOCK, kw), lambda bi, t: prev(bi, t)[:2] + (v_col // kw,))],
        out_specs=pl.BlockSpec((1, tq, qw), lambda bi, t: (bi, t, 0)),
        out_shape=jax.ShapeDtypeStruct((b, s, qw), BF16),
        compiler_params=_params(2),
        name="swa_attention",
    )(sinks, proj, proj, proj, proj, proj)


def _mix_ffn_body(x_ref, a_ref, b_ref, wo_ref, g_ref, wg_ref, wu_ref, wd_ref, o_ref, *, ff_chunk):
    mix = jnp.concatenate([a_ref[...], b_ref[...]], axis=1)
    x1 = x_ref[...] + jnp.dot(mix, wo_ref[...], preferred_element_type=F32)
    h = _rms(x1, g_ref[...]).astype(BF16)
    y = x1
    for c in range(wg_ref.shape[1] // ff_chunk):
        cols = slice(c * ff_chunk, (c + 1) * ff_chunk)
        a = jnp.dot(h, wg_ref[:, cols], preferred_element_type=F32)
        u = jnp.dot(h, wu_ref[:, cols], preferred_element_type=F32)
        act = (a * jax.nn.sigmoid(a) * u).astype(BF16)
        y = y + jnp.dot(act, wd_ref[cols, :], preferred_element_type=F32)
    o_ref[...] = y


def mix_ffn(x, oa, ob, wo, g, wg, wu, wd, tm=512, ff_chunk=1408):
    t, d = x.shape
    f = wg.shape[1]
    tm = _pick(t, tm)
    ff_chunk = _pick(f, ff_chunk)
    resident = lambda shape: pl.BlockSpec(shape, lambda i: (0, 0), pipeline_mode=pl.Buffered(1))
    return pl.pallas_call(
        functools.partial(_mix_ffn_body, ff_chunk=ff_chunk),
        grid=(t // tm,),
        in_specs=[pl.BlockSpec((tm, d), lambda i: (i, 0)),
                  pl.BlockSpec((tm, oa.shape[1]), lambda i: (i, 0)),
                  pl.BlockSpec((tm, ob.shape[1]), lambda i: (i, 0)),
                  resident(wo.shape), resident((1, d)),
                  resident(wg.shape), resident(wu.shape), resident(wd.shape)],
        out_specs=pl.BlockSpec((tm, d), lambda i: (i, 0)),
        out_shape=jax.ShapeDtypeStruct((t, d), F32),
        compiler_params=_params(1),
        name="mix_ffn",
    )(x, oa, ob, wo, g.reshape(1, d), wg, wu, wd)


def _conv_body(x_ref, g_ref, w1_ref, b1_ref, wdw_ref, bdw_ref, lng_ref, lnb_ref, w2_ref, b2_ref,
               o_ref, u_ref, v_ref, wb_ref, c_ref, *, tm, rows_per_chunk):
    d = x_ref.shape[-1]
    t = pl.program_id(1)
    sublanes = wb_ref.shape[1]

    @pl.when(t == 0)
    def _():
        u_ref[:, 0:CONV_HIST, :] = jnp.zeros((d // LANES, CONV_HIST, LANES), F32)
        wb_ref[...] = jnp.broadcast_to(wdw_ref[...][:, None, :], wb_ref.shape)

    h = _rms(x_ref[0], g_ref[...]).astype(BF16)
    z = jnp.dot(h, w1_ref[...], preferred_element_type=F32) + b1_ref[...]
    u = z[:, :d] * jax.nn.sigmoid(z[:, d:])
    n_slabs = d // LANES
    for c in range(n_slabs):
        u_ref[c, CONV_HIST:CONV_HIST + tm, :] = u[:, c * LANES:(c + 1) * LANES]

    first_tap = CONV_HIST - (CONV_WIDTH - 1)
    conv_rows = 8 * sublanes

    for c in range(n_slabs):
        lanes = slice(c * LANES, (c + 1) * LANES)
        w = [wb_ref[j, :, lanes] for j in range(CONV_WIDTH)]
        bias = jnp.zeros((sublanes, LANES), F32) + bdw_ref[:, lanes]

        def rows_block(i, carry, c=c, lanes=lanes, w=w, bias=bias):
            r0 = pl.multiple_of(i * conv_rows, conv_rows)
            for r in range(conv_rows // sublanes):
                sums = [bias, None]
                for j in range(CONV_WIDTH):
                    tap = u_ref[c, pl.ds(r0 + (first_tap + j + r * sublanes), sublanes, stride=1), :] * w[j]
                    sums[j % 2] = tap if sums[j % 2] is None else sums[j % 2] + tap
                c_ref[pl.ds(r0 + r * sublanes, sublanes), lanes] = sums[0] + sums[1]
            return carry

        lax.fori_loop(0, tm // conv_rows, rows_block, 0)

    def chunk(i, carry):
        r0 = pl.multiple_of(i * rows_per_chunk, rows_per_chunk)
        acc = c_ref[pl.ds(r0, rows_per_chunk), :]
        mu = jnp.mean(acc, axis=-1, keepdims=True)
        xc = acc - mu
        var = jnp.mean(xc * xc, axis=-1, keepdims=True)
        y = xc * lax.rsqrt(var + LN_EPS) * lng_ref[...] + lnb_ref[...]
        v_ref[pl.ds(r0, rows_per_chunk), :] = (y * jax.nn.sigmoid(y)).astype(BF16)
        return carry

    lax.fori_loop(0, tm // rows_per_chunk, chunk, 0)
    u_ref[:, 0:CONV_HIST, :] = u_ref[:, tm:tm + CONV_HIST, :]
    o_ref[0] = x_ref[0] + jnp.dot(v_ref[...], w2_ref[...], preferred_element_type=F32) + b2_ref[...]


def conformer_conv(x, g, w1, b1, wdw, bdw, lng, lnb, w2, b2, tm=512, rows_per_chunk=256):
    b, s, d = x.shape
    tm = _pick(s, tm)
    kernel = functools.partial(_conv_body, tm=tm, rows_per_chunk=rows_per_chunk)
    vec = lambda n: pl.BlockSpec((1, n), lambda bi, t: (0, 0))
    return pl.pallas_call(
        kernel,
        grid=(b, s // tm),
        in_specs=[pl.BlockSpec((1, tm, d), lambda bi, t: (bi, t, 0)),
                  vec(d),
                  pl.BlockSpec((d, 2 * d), lambda bi, t: (0, 0)),
                  vec(2 * d),
                  pl.BlockSpec((CONV_WIDTH, d), lambda bi, t: (0, 0)),
                  vec(d), vec(d), vec(d),
                  pl.BlockSpec((d, d), lambda bi, t: (0, 0)),
                  vec(d)],
        out_specs=pl.BlockSpec((1, tm, d), lambda bi, t: (bi, t, 0)),
        out_shape=jax.ShapeDtypeStruct((b, s, d), F32),
        scratch_shapes=[pltpu.VMEM((d // LANES, tm + CONV_HIST, LANES), F32), pltpu.VMEM((tm, d), BF16),
                        pltpu.VMEM((CONV_WIDTH, F32_SUBLANES, d), F32), pltpu.VMEM((tm, d), F32)],
        compiler_params=_params(2),
        name="conformer_conv",
    )(x, g.reshape(1, d), w1, b1.reshape(1, 2 * d), wdw, bdw.reshape(1, d), lng.reshape(1, d),
      lnb.reshape(1, d), w2, b2.reshape(1, d))


def _router_body(x_ref, g_ref, wr_ref, h_ref, meta_ref, picks_ref, counts_ref, *, n_experts):
    @pl.when(pl.program_id(0) == 0)
    def _():
        counts_ref[...] = jnp.zeros(counts_ref.shape, F32)

    h = _rms(x_ref[...], g_ref[...])
    _store_slabs(h_ref, 0, h)
    logits = jnp.dot(h.astype(BF16), wr_ref[...], preferred_element_type=F32)
    lane = lax.broadcasted_iota(I32, logits.shape, 1).astype(F32)
    neg = jnp.full_like(logits, -jnp.inf)
    far = jnp.full_like(logits, float(LANES))
    l1 = jnp.where(lane < n_experts, logits, neg)
    m1 = jnp.max(l1, axis=-1, keepdims=True)
    i1 = jnp.min(jnp.where(l1 == m1, lane, far), axis=-1, keepdims=True)
    l2 = jnp.where(lane == i1, neg, l1)
    m2 = jnp.max(l2, axis=-1, keepdims=True)
    i2 = jnp.min(jnp.where(l2 == m2, lane, far), axis=-1, keepdims=True)
    e2 = jnp.exp(m2 - m1)
    w1 = 1.0 / (1.0 + e2)
    w2 = e2 / (1.0 + e2)
    tm = logits.shape[0]
    hot1 = (lane == i1).astype(F32)
    hot2 = (lane == i2).astype(F32)
    both = (hot1 + hot2).astype(BF16)
    earlier = (lax.broadcasted_iota(I32, (tm, tm), 1) < lax.broadcasted_iota(I32, (tm, tm), 0)).astype(BF16)
    before = jnp.dot(earlier, both, preferred_element_type=F32) + counts_ref[...]
    r1 = jnp.sum(hot1 * before, axis=-1, keepdims=True)
    r2 = jnp.sum(hot2 * before, axis=-1, keepdims=True)
    counts_ref[...] += jnp.sum(hot1 + hot2, axis=0, keepdims=True)
    zero = jnp.zeros_like(logits)
    meta = jnp.where(lane == 0, i1,
           jnp.where(lane == 1, i2,
           jnp.where(lane == 2, w1,
           jnp.where(lane == 3, w2,
           jnp.where(lane == 4, r1, jnp.where(lane == 5, r2, zero))))))
    meta_ref[...] = meta
    picks_ref[...] = jnp.transpose(meta)[0:picks_ref.shape[0], :]


def moe_router(x, g, wr, n_experts, tm=512):
    t, d = x.shape
    tm = _pick(t, tm)
    kernel = functools.partial(_router_body, n_experts=n_experts)
    return pl.pallas_call(
        kernel,
        grid=(t // tm,),
        in_specs=[pl.BlockSpec((tm, d), lambda i: (i, 0)),
                  pl.BlockSpec((1, d), lambda i: (0, 0)),
                  pl.BlockSpec((d, LANES), lambda i: (0, 0))],
        out_specs=[pl.BlockSpec((tm * (d // LANES), LANES), lambda i: (i, 0)),
                   pl.BlockSpec((tm, LANES), lambda i: (i, 0)),
                   pl.BlockSpec((F32_SUBLANES, tm), lambda i: (0, i)),
                   pl.BlockSpec((1, LANES), lambda i: (0, 0))],
        out_shape=[jax.ShapeDtypeStruct((t * (d // LANES), LANES), F32),
                   jax.ShapeDtypeStruct((t, LANES), F32),
                   jax.ShapeDtypeStruct((F32_SUBLANES, t), F32),
                   jax.ShapeDtypeStruct((1, LANES), F32)],
        compiler_params=_params(1),
        name="moe_router",
    )(x, g.reshape(1, d), wr)


GATHER_UNROLL = 8
N_DMA_PRIORITIES = 2


def _store_slabs(ref, first_row, value):
    n, d = value.shape
    n_slabs = d // LANES
    for c in range(n_slabs):
        ref[pl.ds(first_row * n_slabs + c, n, stride=n_slabs), :] = value[:, c * LANES:(c + 1) * LANES]


def _load_slabs(ref, first_row, n, d):
    n_slabs = d // LANES
    return jnp.concatenate([ref[pl.ds(first_row * n_slabs + c, n, stride=n_slabs), :]
                            for c in range(n_slabs)], axis=1)


def _dispatch_body(pad_ref, tn_ref, d_ref, h_hbm, xs_hbm, zero_ref, buf_ref, in_sem, sem, zsem,
                   *, tm, n_slabs, n_experts, n_sub):
    i = pl.program_id(0)
    n = pl.num_programs(0)
    rows = tm * n_slabs
    n_ring = buf_ref.shape[0]

    def stage(tile):
        return pltpu.make_async_copy(h_hbm.at[pl.ds(pl.multiple_of(tile * rows, rows), rows), :],
                                     buf_ref.at[tile % n_ring], in_sem.at[tile % n_ring])

    def tile_copy(slot):
        return pltpu.make_async_copy(buf_ref.at[0], xs_hbm.at[pl.ds(0, rows), :], sem.at[slot])

    @pl.when(i == 0)
    def _():
        stage(0).start()

    @pl.when(i + 1 < n)
    def _():
        stage(i + 1).start()

    stage(i).wait()
    slot = i % 2
    ring = i % n_ring
    for rank in range(TOP_K):
        def body(it, carry, rank=rank):
            base = it * GATHER_UNROLL
            dsts = [pl.multiple_of(d_ref[0, 0, rank * tm + base + k], n_slabs) for k in range(GATHER_UNROLL)]
            for k in range(GATHER_UNROLL):
                src = pl.multiple_of((base + k) * n_slabs, n_slabs)
                pltpu.make_async_copy(buf_ref.at[ring, pl.ds(src, n_slabs), :],
                                      xs_hbm.at[pl.ds(dsts[k], n_slabs), :], sem.at[slot]
                                      ).start(priority=k % N_DMA_PRIORITIES)
            return carry
        lax.fori_loop(0, tm // GATHER_UNROLL, body, 0)

    @pl.when(i == 0)
    def _():
        zero_ref[...] = jnp.zeros(zero_ref.shape, F32)
        zero_row = zero_ref.at[pl.ds(0, n_slabs), :]
        sub_rows = zero_ref.shape[0]

        def unused_sub_tiles(fn):
            def tbody(tile, carry):
                for s in range(n_sub):
                    @pl.when(s >= tn_ref[tile])
                    def _():
                        dst = pl.multiple_of((tile * n_sub + s) * sub_rows, sub_rows)
                        fn(pltpu.make_async_copy(zero_ref, xs_hbm.at[pl.ds(dst, sub_rows), :], zsem))
                return carry
            lax.fori_loop(0, tn_ref.shape[0], tbody, 0)

        def tail_rows(fn):
            for e in range(n_experts):
                def zbody(r, carry, e=e):
                    dst = pl.multiple_of((pad_ref[0, e] + r) * n_slabs, n_slabs)
                    fn(pltpu.make_async_copy(zero_row, xs_hbm.at[pl.ds(dst, n_slabs), :], zsem))
                    return carry
                lax.fori_loop(0, pad_ref[1, e], zbody, 0)

        unused_sub_tiles(lambda cp: cp.start())
        tail_rows(lambda cp: cp.start())
        unused_sub_tiles(lambda cp: cp.wait())
        tail_rows(lambda cp: cp.wait())

    @pl.when(i > 0)
    def _():
        for rank in range(TOP_K):
            tile_copy(1 - slot).wait()

    @pl.when(i == n - 1)
    def _():
        for rank in range(TOP_K):
            tile_copy(slot).wait()


def moe_dispatch(h, dest, pad, tile_count, sub, n_sub, n_slabs):
    n_tok_tiles = dest.shape[0]
    tm = dest.shape[2] // TOP_K
    n_experts = pad.shape[1]
    n_rows = tile_count.shape[0] * n_sub * sub
    grid_spec = pltpu.PrefetchScalarGridSpec(
        num_scalar_prefetch=2,
        grid=(n_tok_tiles,),
        in_specs=[pl.BlockSpec((1, 1, TOP_K * tm), lambda i, pad, tn: (i, 0, 0), memory_space=pltpu.SMEM),
                  pl.BlockSpec(memory_space=pl.ANY)],
        out_specs=pl.BlockSpec(memory_space=pl.ANY),
        scratch_shapes=[pltpu.VMEM((sub * n_slabs, LANES), F32), pltpu.VMEM((3, tm * n_slabs, LANES), F32),
                        pltpu.SemaphoreType.DMA((3,)), pltpu.SemaphoreType.DMA((2,)),
                        pltpu.SemaphoreType.DMA(())],
    )
    return pl.pallas_call(
        functools.partial(_dispatch_body, tm=tm, n_slabs=n_slabs, n_experts=n_experts, n_sub=n_sub),
        grid_spec=grid_spec,
        out_shape=jax.ShapeDtypeStruct((n_rows * n_slabs, LANES), F32),
        compiler_params=_params(1),
        name="moe_dispatch",
    )(pad, tile_count, dest, h)


def _moe_body(te_ref, tn_ref, xs_ref, wg_ref, wu_ref, wd_ref, y_hbm,
              y_ref, xb_ref, wgb_ref, wub_ref, wdb_ref, stage_ref, zero_ref, osem, zsem, *, sub, n_sub):
    i = pl.program_id(0)
    j = pl.program_id(1)
    last = j == pl.num_programs(1) - 1
    count = tn_ref[i]
    d = wg_ref.shape[2]
    slab_rows = stage_ref.shape[1]

    @pl.when((i == 0) & (j == 0))
    def _():
        zero_ref[...] = jnp.zeros(zero_ref.shape, F32)

    for s in range(n_sub):
        rows = pl.ds(s * sub, sub)

        @pl.when((j == 0) & (s < count))
        def _():
            xb_ref[rows, :] = _load_slabs(xs_ref, s * sub, sub, d).astype(BF16)
            y_ref[rows, :] = jnp.zeros((sub, d), F32)

    def expert_rows(s, wg, wu, wd):
        rows = pl.ds(s * sub, sub)
        x = xb_ref[rows, :]
        a = jnp.dot(x, wg, preferred_element_type=F32)
        u = jnp.dot(x, wu, preferred_element_type=F32)
        act = (a * jax.nn.sigmoid(a) * u).astype(BF16)
        y_ref[rows, :] += jnp.dot(act, wd, preferred_element_type=F32)

    def rounded_weights():
        return tuple(r[0, 0].astype(BF16) for r in (wg_ref, wu_ref, wd_ref))

    @pl.when(count == n_sub)
    def _():
        weights = rounded_weights()
        for s in range(n_sub):
            expert_rows(s, *weights)

    for s in range(n_sub - 1):
        @pl.when((s < count) & (count < n_sub))
        def _():
            if s == 0:
                wgb_ref[...], wub_ref[...], wdb_ref[...] = rounded_weights()
            expert_rows(s, wgb_ref[...], wub_ref[...], wdb_ref[...])

    def out_copy(src, s):
        dst = pl.multiple_of((i * n_sub + s) * slab_rows, slab_rows)
        return src, y_hbm.at[pl.ds(dst, slab_rows), :]

    n_tiles = pl.num_programs(0)
    prev_count = tn_ref[jnp.maximum(i - 1, 0)]
    next_count = tn_ref[jnp.minimum(i + 1, n_tiles - 1)]
    final_tile = (i == n_tiles - 1) | (next_count == 0)

    def stage_copy(s):
        return pltpu.make_async_copy(*out_copy(stage_ref.at[s % 2], s), osem.at[s % 2])

    for slot in range(2):
        @pl.when(last & (count > 0) & (i > 0) & (prev_count > slot))
        def _():
            stage_copy(slot).wait()

    for s in range(n_sub):
        @pl.when(last & (s < count))
        def _():
            if s >= 2:
                stage_copy(s - 2).wait()
            _store_slabs(stage_ref.at[s % 2], 0, y_ref[pl.ds(s * sub, sub), :])
            stage_copy(s).start()

        @pl.when(last & (s >= count))
        def _():
            pltpu.make_async_copy(*out_copy(zero_ref, s), zsem).start()

    for s in range(n_sub):
        @pl.when(last & final_tile & (s < count) & (s + 2 >= count))
        def _():
            stage_copy(s).wait()

        @pl.when(last & (s >= count))
        def _():
            pltpu.make_async_copy(*out_copy(zero_ref, s), zsem).wait()


def moe_experts(xs, tile_expert, tile_count, wg, wu, wd, layer, sub, n_sub, tf=512):
    d = wg.shape[2]
    n_slabs = d // LANES
    n_tiles = tile_expert.shape[0]
    tile_rows = n_sub * sub
    f = wg.shape[3]
    tf = _pick(f, tf)
    nf = f // tf

    def w_col(i, j, te, tn):
        return (layer, te[i], 0, jnp.where(tn[i] > 0, j, nf - 1))

    def w_row(i, j, te, tn):
        return (layer, te[i], jnp.where(tn[i] > 0, j, nf - 1), 0)

    grid_spec = pltpu.PrefetchScalarGridSpec(
        num_scalar_prefetch=2,
        grid=(n_tiles, nf),
        in_specs=[pl.BlockSpec((tile_rows * n_slabs, LANES), lambda i, j, te, tn: (i, 0)),
                  pl.BlockSpec((1, 1, d, tf), w_col),
                  pl.BlockSpec((1, 1, d, tf), w_col),
                  pl.BlockSpec((1, 1, tf, d), w_row)],
        out_specs=pl.BlockSpec(memory_space=pl.ANY),
        scratch_shapes=[pltpu.VMEM((tile_rows, d), F32), pltpu.VMEM((tile_rows, d), BF16),
                        pltpu.VMEM((d, tf), BF16), pltpu.VMEM((d, tf), BF16), pltpu.VMEM((tf, d), BF16),
                        pltpu.VMEM((2, sub * n_slabs, LANES), F32), pltpu.VMEM((sub * n_slabs, LANES), F32),
                        pltpu.SemaphoreType.DMA((2,)), pltpu.SemaphoreType.DMA(())],
    )
    return pl.pallas_call(
        functools.partial(_moe_body, sub=sub, n_sub=n_sub),
        grid_spec=grid_spec,
        out_shape=jax.ShapeDtypeStruct((n_tiles * tile_rows * n_slabs, LANES), F32),
        compiler_params=_params(2),
        name="moe_experts",
    )(tile_expert, tile_count, xs, wg, wu, wd)


def _combine_body(d_cur_ref, d_nxt_ref, x_ref, meta_ref, g_ref, y_hbm, o_ref, buf_ref, sem,
                  *, tm, final_norm):
    i = pl.program_id(0)
    n = pl.num_programs(0)
    d = x_ref.shape[1]
    n_slabs = d // LANES

    def gather(d_ref, slot):
        def body(it, carry):
            base = it * GATHER_UNROLL
            srcs = [pl.multiple_of(d_ref[0, 0, base + k], n_slabs) for k in range(GATHER_UNROLL)]
            for k in range(GATHER_UNROLL):
                dst = pl.multiple_of((base + k) * n_slabs, n_slabs)
                pltpu.make_async_copy(y_hbm.at[pl.ds(srcs[k], n_slabs), :],
                                      buf_ref.at[slot, pl.ds(dst, n_slabs), :], sem.at[slot]
                                      ).start(priority=k % N_DMA_PRIORITIES)
            return carry
        lax.fori_loop(0, TOP_K * tm // GATHER_UNROLL, body, 0)

    @pl.when(i == 0)
    def _():
        gather(d_cur_ref, 0)

    @pl.when(i + 1 < n)
    def _():
        gather(d_nxt_ref, (i + 1) % 2)

    slot = i % 2
    pltpu.make_async_copy(y_hbm.at[pl.ds(0, TOP_K * tm * n_slabs), :], buf_ref.at[slot],
                          sem.at[slot]).wait()
    meta = meta_ref[...]
    out = (x_ref[...] + meta[:, 2:3] * _load_slabs(buf_ref.at[slot], 0, tm, d)
           + meta[:, 3:4] * _load_slabs(buf_ref.at[slot], tm, tm, d))
    if final_norm:
        out = _rms(out, g_ref[...])
    o_ref[...] = out


def moe_combine(x, meta, dest, y, g, final_norm, tm=256):
    t, d = x.shape
    n = t // tm
    n_slabs = d // LANES
    smem = lambda fn: pl.BlockSpec((1, 1, TOP_K * tm), fn, memory_space=pltpu.SMEM)
    return pl.pallas_call(
        functools.partial(_combine_body, tm=tm, final_norm=final_norm),
        grid=(n,),
        in_specs=[smem(lambda i: (i, 0, 0)),
                  smem(lambda i: (jnp.minimum(i + 1, n - 1), 0, 0)),
                  pl.BlockSpec((tm, d), lambda i: (i, 0)),
                  pl.BlockSpec((tm, LANES), lambda i: (i, 0)),
                  pl.BlockSpec((1, d), lambda i: (0, 0)),
                  pl.BlockSpec(memory_space=pl.ANY)],
        out_specs=pl.BlockSpec((tm, d), lambda i: (i, 0)),
        out_shape=jax.ShapeDtypeStruct((t, d), F32),
        scratch_shapes=[pltpu.VMEM((2, TOP_K * tm * n_slabs, LANES), F32), pltpu.SemaphoreType.DMA((2,))],
        compiler_params=_params(1),
        name="moe_combine",
    )(dest, dest, x, meta, g.reshape(1, d), y)


def _route(picks, counts, n_experts, sub, n_sub, tm_combine):
    t = picks.shape[1]
    tile_rows = sub * n_sub
    expert = picks[0:TOP_K].astype(I32)
    rank = picks[2 * TOP_K:3 * TOP_K].astype(I32)
    counts = counts[0, :n_experts].astype(I32)
    subs = (counts + sub - 1) // sub
    tiles = (subs + n_sub - 1) // n_sub
    tile_ends = jnp.cumsum(tiles)
    tile_starts = tile_ends - tiles
    first_row = jnp.zeros_like(expert)
    for e in range(n_experts):
        first_row = jnp.where(expert == e, tile_starts[e] * tile_rows, first_row)
    dest = first_row + rank
    n_tiles = ((t * TOP_K) // sub + n_experts * n_sub) // n_sub
    idx = jnp.arange(n_tiles, dtype=I32)
    tile_expert = jnp.minimum(jnp.searchsorted(tile_ends, idx, side="right"), n_experts - 1).astype(I32)
    local = idx - tile_starts[tile_expert]
    tile_count = jnp.clip(subs[tile_expert] - local * n_sub, 0, n_sub).astype(I32)
    dest_tiles = dest.reshape(TOP_K, t // tm_combine, tm_combine).transpose(1, 0, 2).reshape(
        t // tm_combine, 1, TOP_K * tm_combine)
    pad = jnp.stack([tile_starts * tile_rows + counts, subs * sub - counts]).astype(I32)
    return tile_expert, tile_count, dest_tiles, pad, n_tiles


def moe_layer(x, g, wr, wg, wu, wd, layer, final_g, final_norm, sub=512, n_sub=4, tm_combine=512):
    t, d = x.shape
    n_experts = wr.shape[1]
    tm_combine = min(tm_combine, t)
    wr_pad = jnp.zeros((d, LANES), BF16).at[:, :n_experts].set(wr.astype(BF16))
    h, meta, picks, counts = moe_router(x, g, wr_pad, n_experts)
    n_slabs = d // LANES
    tile_expert, tile_count, dest_tiles, pad, n_tiles = _route(picks, counts, n_experts, sub, n_sub,
                                                               tm_combine)
    slab_dest = dest_tiles * n_slabs
    xs = moe_dispatch(h, slab_dest, pad, tile_count, sub, n_sub, n_slabs)
    y = moe_experts(xs, tile_expert, tile_count, wg, wu, wd, layer, sub, n_sub)
    return moe_combine(x, meta, slab_dest, y, final_g, final_norm, tm_combine)


def attention_mixers(x, norm_g, w_in, lam_params, subln_g, sinks, layer):
    b, s, d = x.shape
    diff_heads = d // (4 * HEAD_DIM)
    swa_q_heads = d // (2 * HEAD_DIM)
    swa_kv_heads = max(1, swa_q_heads // 4)
    group = swa_q_heads // swa_kv_heads
    a_width = diff_heads * 2 * HEAD_DIM
    lam_init = 0.8 - 0.6 * math.exp(-0.3 * layer)
    x2 = x.reshape(b * s, d)
    proj = rms_matmul(x2, norm_g, w_in).reshape(b, s, -1)
    oa = diff_attention(proj, lam_params, subln_g, diff_heads, lam_init)
    q_col = 3 * a_width
    k_col = q_col + swa_q_heads * HEAD_DIM
    v_col = k_col + swa_kv_heads * HEAD_DIM
    ob = sliding_window_attention(proj, sinks, q_col, k_col, v_col, swa_kv_heads, group)
    return oa.reshape(b * s, -1), ob.reshape(b * s, -1)


def kernel(x, attn_norm_g, w_in_att, diff_lambda, diff_subln_g, attn_sinks, w_out_att, ffn_norm_g, w_ffn_gate, w_ffn_up, w_ffn_down, conv_norm_g, w_pw1, b_pw1, w_dw, b_dw, conv_ln_g, conv_ln_b, w_pw2, b_pw2, moe_norm_g, w_router, w_exp_gate, w_exp_up, w_exp_down, final_norm_g):
    b, s, d = x.shape
    depth = attn_norm_g.shape[0] + conv_norm_g.shape[0]
    assert depth % 2 == 0, "the final RMSNorm is fused into the last expert layer"
    bf = lambda w: w.astype(BF16)
    for layer in range(depth):
        i = layer // 2
        if layer % 2 == 0:
            oa, ob = attention_mixers(x, attn_norm_g[i], bf(w_in_att[i]), diff_lambda[i], diff_subln_g[i],
                                      attn_sinks[i], layer)
            x2 = mix_ffn(x.reshape(b * s, d), oa, ob, bf(w_out_att[i]), ffn_norm_g[i],
                         bf(w_ffn_gate[i]), bf(w_ffn_up[i]), bf(w_ffn_down[i]))
            x = x2.reshape(b, s, d)
        else:
            x = conformer_conv(x, conv_norm_g[i], bf(w_pw1[i]), b_pw1[i], w_dw[i], b_dw[i],
                               conv_ln_g[i], conv_ln_b[i], bf(w_pw2[i]), b_pw2[i])
            x2 = moe_layer(x.reshape(b * s, d), moe_norm_g[i], w_router[i], w_exp_gate,
                           w_exp_up, w_exp_down, i, final_norm_g,
                           final_norm=(layer == depth - 1))
            x = x2.reshape(b, s, d)
    return x
```

```python
import functools
import math

import jax
import jax.numpy as jnp
from jax import lax
from jax.experimental import pallas as pl
from jax.experimental.pallas import tpu as pltpu

BF16 = jnp.bfloat16
F32 = jnp.float32
I32 = jnp.int32

LOG2_E = 1.4426950408889634
RMS_EPS = 1e-6
LN_EPS = 1e-5
HEAD_DIM = 64
ATTN_BLOCK = 128
CONV_WIDTH = 31
TOP_K = 2
LANES = 128
F32_SUBLANES = 8
CONV_HIST = 32
V7X_VMEM_LIMIT = 56 * 1024 * 1024


def _params(n_axes):
    return pltpu.CompilerParams(dimension_semantics=("arbitrary",) * n_axes,
                                vmem_limit_bytes=V7X_VMEM_LIMIT)


def _rms(x, g):
    return x * lax.rsqrt(jnp.mean(x * x, axis=-1, keepdims=True) + RMS_EPS) * g


def _pick(n, pref):
    t = min(n, pref)
    while n % t:
        t -= LANES if t > LANES else 8
    return t


def _rms_matmul_body(x_ref, g_ref, w_ref, o_ref):
    h = _rms(x_ref[...], g_ref[...]).astype(BF16)
    o_ref[...] = jnp.dot(h, w_ref[...], preferred_element_type=F32).astype(o_ref.dtype)


def rms_matmul(x, g, w, tm=1024):
    t, d = x.shape
    n = w.shape[1]
    tm = _pick(t, tm)
    return pl.pallas_call(
        _rms_matmul_body,
        grid=(t // tm,),
        in_specs=[pl.BlockSpec((tm, d), lambda i: (i, 0)),
                  pl.BlockSpec((1, d), lambda i: (0, 0)),
                  pl.BlockSpec((d, n), lambda i: (0, 0))],
        out_specs=pl.BlockSpec((tm, n), lambda i: (i, 0)),
        out_shape=jax.ShapeDtypeStruct((t, n), BF16),
        compiler_params=_params(1),
        name="rms_inproj",
    )(x, g.reshape(1, d), w)


def _diff_attn_body(q_ref, k_ref, v_ref, lam_ref, g_ref, o_ref, q_ref2, m_ref, l_ref, acc_ref,
                    *, blk, lam_init):
    qi = pl.program_id(2)
    hw = 2 * HEAD_DIM
    half = blk // 2
    lane = lax.broadcasted_iota(I32, (half, hw), 1)
    q = (q_ref[0].astype(F32) * (HEAD_DIM ** -0.5 * LOG2_E)).astype(BF16)
    zero = jnp.zeros((half, hw), BF16)
    q_ref2[...] = jnp.concatenate(
        [jnp.where(keep, q[r:r + half], zero)
         for r in (0, half) for keep in (lane < HEAD_DIM, lane >= HEAD_DIM)], axis=0)

    m_ref[...] = jnp.full(m_ref.shape, -jnp.inf, F32)
    l_ref[...] = jnp.zeros(l_ref.shape, F32)
    acc_ref[...] = jnp.zeros(acc_ref.shape, F32)

    def block(key_start, n_keys, row0, n_rows, masked):
        rows = pl.ds(row0, n_rows)
        k = k_ref[0, pl.ds(key_start, n_keys), :]
        v = v_ref[0, pl.ds(key_start, n_keys), :]
        s = lax.dot_general(q_ref2[rows, :], k, (((1,), (1,)), ((), ())), preferred_element_type=F32)
        if masked:
            row = lax.broadcasted_iota(I32, (n_rows, n_keys), 0) & (half - 1)
            col = lax.broadcasted_iota(I32, (n_rows, n_keys), 1)
            s = jnp.where(col <= row, s, -jnp.inf)
        tiles = [s[:, c * LANES:(c + 1) * LANES] for c in range(n_keys // LANES)]
        m_prev = m_ref[rows, :]
        m_new = jnp.maximum(m_prev, jnp.max(functools.reduce(jnp.maximum, tiles), axis=-1, keepdims=True))
        alpha = jnp.exp2(m_prev - m_new)
        p_tiles = [jnp.exp2(t - m_new) for t in tiles]
        l_ref[rows, :] = alpha * l_ref[rows, :] + functools.reduce(jnp.add, p_tiles)
        p = jnp.concatenate(p_tiles, axis=1).astype(BF16)
        acc_ref[rows, :] = alpha * acc_ref[rows, :] + jnp.dot(p, v, preferred_element_type=F32)
        m_ref[rows, :] = m_new

    def full_block(ki):
        block(pl.multiple_of(ki * blk, blk), blk, 0, 2 * blk, False)

    def pair(j, carry):
        full_block(2 * j)
        full_block(2 * j + 1)
        return carry

    lax.fori_loop(0, qi // 2, pair, 0)

    @pl.when(qi % 2 == 1)
    def _():
        full_block(qi - 1)

    diag = pl.multiple_of(qi * blk, blk)
    block(diag, half, 0, 2 * half, True)
    block(diag, half, 2 * half, 2 * half, False)
    block(diag + half, half, 2 * half, 2 * half, True)

    lp = lam_ref[...]
    lam = (jnp.exp(jnp.sum(lp[0:1] * lp[1:2], axis=-1, keepdims=True))
           - jnp.exp(jnp.sum(lp[2:3] * lp[3:4], axis=-1, keepdims=True)) + lam_init)
    o = acc_ref[...] / jnp.sum(l_ref[...], axis=-1, keepdims=True)
    o = (jnp.concatenate([o[0:half], o[2 * half:3 * half]], axis=0)
         - lam * jnp.concatenate([o[half:2 * half], o[3 * half:4 * half]], axis=0))
    o = _rms(o, g_ref[...]) * (1.0 - lam_init)
    o_ref[0] = o.astype(o_ref.dtype)


def diff_attention(proj, lam_params, subln_g, n_heads, lam_init, blk=1024):
    b, s, _ = proj.shape
    blk = _pick(s, blk)
    assert blk & (blk - 1) == 0, "the causal mask uses power-of-two half blocks"
    hw = 2 * HEAD_DIM
    kernel = functools.partial(_diff_attn_body, blk=blk, lam_init=lam_init)
    return pl.pallas_call(
        kernel,
        grid=(b, n_heads, s // blk),
        in_specs=[pl.BlockSpec((1, blk, hw), lambda bi, h, qi: (bi, qi, h)),
                  pl.BlockSpec((1, s, hw), lambda bi, h, qi: (bi, 0, n_heads + h)),
                  pl.BlockSpec((1, s, hw), lambda bi, h, qi: (bi, 0, 2 * n_heads + h)),
                  pl.BlockSpec((4, HEAD_DIM), lambda bi, h, qi: (0, 0)),
                  pl.BlockSpec((1, hw), lambda bi, h, qi: (0, 0))],
        out_specs=pl.BlockSpec((1, blk, hw), lambda bi, h, qi: (bi, qi, h)),
        out_shape=jax.ShapeDtypeStruct((b, s, n_heads * hw), BF16),
        scratch_shapes=[pltpu.VMEM((2 * blk, hw), BF16), pltpu.VMEM((2 * blk, LANES), F32),
                        pltpu.VMEM((2 * blk, LANES), F32), pltpu.VMEM((2 * blk, hw), F32)],
        compiler_params=_params(3),
        name="diff_attention",
    )(proj, proj, proj, lam_params, subln_g.reshape(1, hw))


def _swa_body(sink_ref, q_ref, kc_ref, kp_ref, vc_ref, vp_ref, o_ref, *, tq, kv_heads, group):
    t = pl.program_id(1)
    w = ATTN_BLOCK
    row = lax.broadcasted_iota(I32, (group * w, 2 * w), 0) & (w - 1)
    col = lax.broadcasted_iota(I32, (group * w, 2 * w), 1)
    band = (col > row) & (col <= row + w)
    band_first = band & ((col >= w) | (t > 0))
    low_half = lax.broadcasted_iota(I32, (w, LANES), 1) < HEAD_DIM
    for j in range(tq // w):
        rows = slice(j * w, (j + 1) * w)
        if j == 0:
            k2 = jnp.concatenate([kp_ref[0], kc_ref[0, rows, :]], axis=0)
            v2 = jnp.concatenate([vp_ref[0], vc_ref[0, rows, :]], axis=0)
            mask = band_first
        else:
            k2 = kc_ref[0, (j - 1) * w:(j + 1) * w, :]
            v2 = vc_ref[0, (j - 1) * w:(j + 1) * w, :]
            mask = band
        outs = []
        for g in range(kv_heads):
            kg = k2[:, g * HEAD_DIM:(g + 1) * HEAD_DIM]
            vg = v2[:, g * HEAD_DIM:(g + 1) * HEAD_DIM]
            kdup = jnp.concatenate([kg, kg], axis=1)
            vdup = jnp.concatenate([vg, vg], axis=1)
            heads = [g * group + i for i in range(group)]
            q_tiles = []
            for h in heads:
                q = q_ref[0, rows, (h // 2) * LANES:(h // 2 + 1) * LANES]
                q = (q.astype(F32) * (HEAD_DIM ** -0.5 * LOG2_E)).astype(BF16)
                own_half = low_half if h % 2 == 0 else jnp.logical_not(low_half)
                q_tiles.append(jnp.where(own_half, q, jnp.zeros_like(q)))
            sink = jnp.concatenate([jnp.full((w, LANES), sink_ref[h] * LOG2_E, F32) for h in heads], axis=0)
            s = lax.dot_general(jnp.concatenate(q_tiles, axis=0), kdup, (((1,), (1,)), ((), ())),
                                preferred_element_type=F32)
            s = jnp.where(mask, s, -jnp.inf)
            s0, s1 = s[:, :w], s[:, w:]
            m = jnp.maximum(jnp.max(jnp.maximum(s0, s1), axis=-1, keepdims=True), sink)
            e0 = jnp.exp2(s0 - m)
            e1 = jnp.exp2(s1 - m)
            denom = jnp.sum(e0 + e1, axis=-1, keepdims=True) + jnp.exp2(sink - m)
            e = jnp.concatenate([e0, e1], axis=1).astype(BF16)
            o = jnp.dot(e, vdup, preferred_element_type=F32) / denom
            outs.extend(o[i * w:(i + 1) * w, :] for i in range(group))
        tiles = [jnp.where(low_half, outs[h], outs[h + 1]) for h in range(0, len(outs), 2)]
        o_ref[0, rows, :] = jnp.concatenate(tiles, axis=1).astype(o_ref.dtype)


def sliding_window_attention(proj, sinks, q_col, k_col, v_col, kv_heads, group, tq=1024):
    b, s, _ = proj.shape
    tq = _pick(s, tq)
    qw = kv_heads * group * HEAD_DIM
    kw = kv_heads * HEAD_DIM
    sub = tq // ATTN_BLOCK
    kernel = functools.partial(_swa_body, tq=tq, kv_heads=kv_heads, group=group)
    prev = lambda bi, t: (bi, jnp.maximum(t * sub - 1, 0), 0)
    return pl.pallas_call(
        kernel,
        grid=(b, s // tq),
        in_specs=[pl.BlockSpec(memory_space=pltpu.SMEM),
                  pl.BlockSpec((1, tq, qw), lambda bi, t: (bi, t, q_col // qw)),
                  pl.BlockSpec((1, tq, kw), lambda bi, t: (bi, t, k_col // kw)),
                  pl.BlockSpec((1, ATTN_BLOCK, kw), lambda bi, t: prev(bi, t)[:2] + (k_col // kw,)),
                  pl.BlockSpec((1, tq, kw), lambda bi, t: (bi, t, v_col // kw)),
                  pl.BlockSpec((1, ATTN_BLOCK, kw), lambda bi, t: prev(bi, t)[:2] + (v_col // kw,))],
        out_specs=pl.BlockSpec((1, tq, qw), lambda bi, t: (bi, t, 0)),
        out_shape=jax.ShapeDtypeStruct((b, s, qw), BF16),
        compiler_params=_params(2),
        name="swa_attention",
    )(sinks, proj, proj, proj, proj, proj)


def _mix_ffn_body(x_ref, a_ref, b_ref, wo_ref, g_ref, wg_ref, wu_ref, wd_ref, o_ref, *, ff_chunk):
    mix = jnp.concatenate([a_ref[...], b_ref[...]], axis=1)
    x1 = x_ref[...] + jnp.dot(mix, wo_ref[...], preferred_element_type=F32)
    h = _rms(x1, g_ref[...]).astype(BF16)
    y = x1
    for c in range(wg_ref.shape[1] // ff_chunk):
        cols = slice(c * ff_chunk, (c + 1) * ff_chunk)
        a = jnp.dot(h, wg_ref[:, cols], preferred_element_type=F32)
        u = jnp.dot(h, wu_ref[:, cols], preferred_element_type=F32)
        act = (a * jax.nn.sigmoid(a) * u).astype(BF16)
        y = y + jnp.dot(act, wd_ref[cols, :], preferred_element_type=F32)
    o_ref[...] = y


def mix_ffn(x, oa, ob, wo, g, wg, wu, wd, tm=512, ff_chunk=1408):
    t, d = x.shape
    f = wg.shape[1]
    tm = _pick(t, tm)
    ff_chunk = _pick(f, ff_chunk)
    resident = lambda shape: pl.BlockSpec(shape, lambda i: (0, 0), pipeline_mode=pl.Buffered(1))
    return pl.pallas_call(
        functools.partial(_mix_ffn_body, ff_chunk=ff_chunk),
        grid=(t // tm,),
        in_specs=[pl.BlockSpec((tm, d), lambda i: (i, 0)),
                  pl.BlockSpec((tm, oa.shape[1]), lambda i: (i, 0)),
                  pl.BlockSpec((tm, ob.shape[1]), lambda i: (i, 0)),
                  resident(wo.shape), resident((1, d)),
                  resident(wg.shape), resident(wu.shape), resident(wd.shape)],
        out_specs=pl.BlockSpec((tm, d), lambda i: (i, 0)),
        out_shape=jax.ShapeDtypeStruct((t, d), F32),
        compiler_params=_params(1),
        name="mix_ffn",
    )(x, oa, ob, wo, g.reshape(1, d), wg, wu, wd)


def _conv_body(x_ref, g_ref, w1_ref, b1_ref, wdw_ref, bdw_ref, lng_ref, lnb_ref, w2_ref, b2_ref,
               o_ref, u_ref, v_ref, wb_ref, c_ref, *, tm, rows_per_chunk):
    d = x_ref.shape[-1]
    t = pl.program_id(1)
    sublanes = wb_ref.shape[1]

    @pl.when(t == 0)
    def _():
        u_ref[:, 0:CONV_HIST, :] = jnp.zeros((d // LANES, CONV_HIST, LANES), F32)
        wb_ref[...] = jnp.broadcast_to(wdw_ref[...][:, None, :], wb_ref.shape)

    h = _rms(x_ref[0], g_ref[...]).astype(BF16)
    z = jnp.dot(h, w1_ref[...], preferred_element_type=F32) + b1_ref[...]
    u = z[:, :d] * jax.nn.sigmoid(z[:, d:])
    n_slabs = d // LANES
    for c in range(n_slabs):
        u_ref[c, CONV_HIST:CONV_HIST + tm, :] = u[:, c * LANES:(c + 1) * LANES]

    first_tap = CONV_HIST - (CONV_WIDTH - 1)
    conv_rows = 8 * sublanes

    for c in range(n_slabs):
        lanes = slice(c * LANES, (c + 1) * LANES)
        w = [wb_ref[j, :, lanes] for j in range(CONV_WIDTH)]
        bias = jnp.zeros((sublanes, LANES), F32) + bdw_ref[:, lanes]

        def rows_block(i, carry, c=c, lanes=lanes, w=w, bias=bias):
            r0 = pl.multiple_of(i * conv_rows, conv_rows)
            for r in range(conv_rows // sublanes):
                sums = [bias, None]
                for j in range(CONV_WIDTH):
                    tap = u_ref[c, pl.ds(r0 + (first_tap + j + r * sublanes), sublanes, stride=1), :] * w[j]
                    sums[j % 2] = tap if sums[j % 2] is None else sums[j % 2] + tap
                c_ref[pl.ds(r0 + r * sublanes, sublanes), lanes] = sums[0] + sums[1]
            return carry

        lax.fori_loop(0, tm // conv_rows, rows_block, 0)

    def chunk(i, carry):
        r0 = pl.multiple_of(i * rows_per_chunk, rows_per_chunk)
        acc = c_ref[pl.ds(r0, rows_per_chunk), :]
        mu = jnp.mean(acc, axis=-1, keepdims=True)
        xc = acc - mu
        var = jnp.mean(xc * xc, axis=-1, keepdims=True)
        y = xc * lax.rsqrt(var + LN_EPS) * lng_ref[...] + lnb_ref[...]
        v_ref[pl.ds(r0, rows_per_chunk), :] = (y * jax.nn.sigmoid(y)).astype(BF16)
        return carry

    lax.fori_loop(0, tm // rows_per_chunk, chunk, 0)
    u_ref[:, 0:CONV_HIST, :] = u_ref[:, tm:tm + CONV_HIST, :]
    o_ref[0] = x_ref[0] + jnp.dot(v_ref[...], w2_ref[...], preferred_element_type=F32) + b2_ref[...]


def conformer_conv(x, g, w1, b1, wdw, bdw, lng, lnb, w2, b2, tm=512, rows_per_chunk=256):
    b, s, d = x.shape
    tm = _pick(s, tm)
    kernel = functools.partial(_conv_body, tm=tm, rows_per_chunk=rows_per_chunk)
    vec = lambda n: pl.BlockSpec((1, n), lambda bi, t: (0, 0))
    return pl.pallas_call(
        kernel,
        grid=(b, s // tm),
        in_specs=[pl.BlockSpec((1, tm, d), lambda bi, t: (bi, t, 0)),
                  vec(d),
                  pl.BlockSpec((d, 2 * d), lambda bi, t: (0, 0)),
                  vec(2 * d),
                  pl.BlockSpec((CONV_WIDTH, d), lambda bi, t: (0, 0)),
                  vec(d), vec(d), vec(d),
                  pl.BlockSpec((d, d), lambda bi, t: (0, 0)),
                  vec(d)],
        out_specs=pl.BlockSpec((1, tm, d), lambda bi, t: (bi, t, 0)),
        out_shape=jax.ShapeDtypeStruct((b, s, d), F32),
        scratch_shapes=[pltpu.VMEM((d // LANES, tm + CONV_HIST, LANES), F32), pltpu.VMEM((tm, d), BF16),
                        pltpu.VMEM((CONV_WIDTH, F32_SUBLANES, d), F32), pltpu.VMEM((tm, d), F32)],
        compiler_params=_params(2),
        name="conformer_conv",
    )(x, g.reshape(1, d), w1, b1.reshape(1, 2 * d), wdw, bdw.reshape(1, d), lng.reshape(1, d),
      lnb.reshape(1, d), w2, b2.reshape(1, d))


def _router_body(x_ref, g_ref, wr_ref, h_ref, meta_ref, picks_ref, counts_ref, *, n_experts):
    @pl.when(pl.program_id(0) == 0)
    def _():
        counts_ref[...] = jnp.zeros(counts_ref.shape, F32)

    h = _rms(x_ref[...], g_ref[...])
    _store_slabs(h_ref, 0, h)
    logits = jnp.dot(h.astype(BF16), wr_ref[...], preferred_element_type=F32)
    lane = lax.broadcasted_iota(I32, logits.shape, 1).astype(F32)
    neg = jnp.full_like(logits, -jnp.inf)
    far = jnp.full_like(logits, float(LANES))
    l1 = jnp.where(lane < n_experts, logits, neg)
    m1 = jnp.max(l1, axis=-1, keepdims=True)
    i1 = jnp.min(jnp.where(l1 == m1, lane, far), axis=-1, keepdims=True)
    l2 = jnp.where(lane == i1, neg, l1)
    m2 = jnp.max(l2, axis=-1, keepdims=True)
    i2 = jnp.min(jnp.where(l2 == m2, lane, far), axis=-1, keepdims=True)
    e2 = jnp.exp(m2 - m1)
    w1 = 1.0 / (1.0 + e2)
    w2 = e2 / (1.0 + e2)
    tm = logits.shape[0]
    hot1 = (lane == i1).astype(F32)
    hot2 = (lane == i2).astype(F32)
    both = (hot1 + hot2).astype(BF16)
    earlier = (lax.broadcasted_iota(I32, (tm, tm), 1) < lax.broadcasted_iota(I32, (tm, tm), 0)).astype(BF16)
    before = jnp.dot(earlier, both, preferred_element_type=F32) + counts_ref[...]
    r1 = jnp.sum(hot1 * before, axis=-1, keepdims=True)
    r2 = jnp.sum(hot2 * before, axis=-1, keepdims=True)
    counts_ref[...] += jnp.sum(hot1 + hot2, axis=0, keepdims=True)
    zero = jnp.zeros_like(logits)
    meta = jnp.where(lane == 0, i1,
           jnp.where(lane == 1, i2,
           jnp.where(lane == 2, w1,
           jnp.where(lane == 3, w2,
           jnp.where(lane == 4, r1, jnp.where(lane == 5, r2, zero))))))
    meta_ref[...] = meta
    picks_ref[...] = jnp.transpose(meta)[0:picks_ref.shape[0], :]


def moe_router(x, g, wr, n_experts, tm=1024):
    t, d = x.shape
    tm = _pick(t, tm)
    kernel = functools.partial(_router_body, n_experts=n_experts)
    return pl.pallas_call(
        kernel,
        grid=(t // tm,),
        in_specs=[pl.BlockSpec((tm, d), lambda i: (i, 0)),
                  pl.BlockSpec((1, d), lambda i: (0, 0)),
                  pl.BlockSpec((d, LANES), lambda i: (0, 0))],
        out_specs=[pl.BlockSpec((tm * (d // LANES), LANES), lambda i: (i, 0)),
                   pl.BlockSpec((tm, LANES), lambda i: (i, 0)),
                   pl.BlockSpec((F32_SUBLANES, tm), lambda i: (0, i)),
                   pl.BlockSpec((1, LANES), lambda i: (0, 0))],
        out_shape=[jax.ShapeDtypeStruct((t * (d // LANES), LANES), F32),
                   jax.ShapeDtypeStruct((t, LANES), F32),
                   jax.ShapeDtypeStruct((F32_SUBLANES, t), F32),
                   jax.ShapeDtypeStruct((1, LANES), F32)],
        compiler_params=_params(1),
        name="moe_router",
    )(x, g.reshape(1, d), wr)


GATHER_UNROLL = 8
N_DMA_PRIORITIES = 2


def _store_slabs(ref, first_row, value):
    n, d = value.shape
    n_slabs = d // LANES
    for c in range(n_slabs):
        ref[pl.ds(first_row * n_slabs + c, n, stride=n_slabs), :] = value[:, c * LANES:(c + 1) * LANES]


def _load_slabs(ref, first_row, n, d):
    n_slabs = d // LANES
    return jnp.concatenate([ref[pl.ds(first_row * n_slabs + c, n, stride=n_slabs), :]
                            for c in range(n_slabs)], axis=1)


def _dispatch_body(pad_ref, tn_ref, d_ref, h_hbm, xs_hbm, zero_ref, buf_ref, in_sem, sem, zsem,
                   *, tm, n_slabs, n_experts, n_sub):
    i = pl.program_id(0)
    n = pl.num_programs(0)
    rows = tm * n_slabs
    n_ring = buf_ref.shape[0]

    def stage(tile):
        return pltpu.make_async_copy(h_hbm.at[pl.ds(pl.multiple_of(tile * rows, rows), rows), :],
                                     buf_ref.at[tile % n_ring], in_sem.at[tile % n_ring])

    def tile_copy(slot):
        return pltpu.make_async_copy(buf_ref.at[0], xs_hbm.at[pl.ds(0, rows), :], sem.at[slot])

    @pl.when(i == 0)
    def _():
        stage(0).start()

    @pl.when(i + 1 < n)
    def _():
        stage(i + 1).start()

    stage(i).wait()
    slot = i % 2
    ring = i % n_ring
    for rank in range(TOP_K):
        def body(it, carry, rank=rank):
            base = it * GATHER_UNROLL
            dsts = [pl.multiple_of(d_ref[0, 0, rank * tm + base + k], n_slabs) for k in range(GATHER_UNROLL)]
            for k in range(GATHER_UNROLL):
                src = pl.multiple_of((base + k) * n_slabs, n_slabs)
                pltpu.make_async_copy(buf_ref.at[ring, pl.ds(src, n_slabs), :],
                                      xs_hbm.at[pl.ds(dsts[k], n_slabs), :], sem.at[slot]
                                      ).start(priority=k % N_DMA_PRIORITIES)
            return carry
        lax.fori_loop(0, tm // GATHER_UNROLL, body, 0)

    @pl.when(i == 0)
    def _():
        zero_ref[...] = jnp.zeros(zero_ref.shape, F32)
        zero_row = zero_ref.at[pl.ds(0, n_slabs), :]
        sub_rows = zero_ref.shape[0]

        def unused_sub_tiles(fn):
            def tbody(tile, carry):
                for s in range(n_sub):
                    @pl.when(s >= tn_ref[tile])
                    def _():
                        dst = pl.multiple_of((tile * n_sub + s) * sub_rows, sub_rows)
                        fn(pltpu.make_async_copy(zero_ref, xs_hbm.at[pl.ds(dst, sub_rows), :], zsem))
                return carry
            lax.fori_loop(0, tn_ref.shape[0], tbody, 0)

        def tail_rows(fn):
            for e in range(n_experts):
                def zbody(r, carry, e=e):
                    dst = pl.multiple_of((pad_ref[0, e] + r) * n_slabs, n_slabs)
                    fn(pltpu.make_async_copy(zero_row, xs_hbm.at[pl.ds(dst, n_slabs), :], zsem))
                    return carry
                lax.fori_loop(0, pad_ref[1, e], zbody, 0)

        unused_sub_tiles(lambda cp: cp.start())
        tail_rows(lambda cp: cp.start())
        unused_sub_tiles(lambda cp: cp.wait())
        tail_rows(lambda cp: cp.wait())

    @pl.when(i > 0)
    def _():
        for rank in range(TOP_K):
            tile_copy(1 - slot).wait()

    @pl.when(i == n - 1)
    def _():
        for rank in range(TOP_K):
            tile_copy(slot).wait()


def moe_dispatch(h, dest, pad, tile_count, sub, n_sub, n_slabs):
    n_tok_tiles = dest.shape[0]
    tm = dest.shape[2] // TOP_K
    n_experts = pad.shape[1]
    n_rows = tile_count.shape[0] * n_sub * sub
    grid_spec = pltpu.PrefetchScalarGridSpec(
        num_scalar_prefetch=2,
        grid=(n_tok_tiles,),
        in_specs=[pl.BlockSpec((1, 1, TOP_K * tm), lambda i, pad, tn: (i, 0, 0), memory_space=pltpu.SMEM),
                  pl.BlockSpec(memory_space=pl.ANY)],
        out_specs=pl.BlockSpec(memory_space=pl.ANY),
        scratch_shapes=[pltpu.VMEM((sub * n_slabs, LANES), F32), pltpu.VMEM((3, tm * n_slabs, LANES), F32),
                        pltpu.SemaphoreType.DMA((3,)), pltpu.SemaphoreType.DMA((2,)),
                        pltpu.SemaphoreType.DMA(())],
    )
    return pl.pallas_call(
        functools.partial(_dispatch_body, tm=tm, n_slabs=n_slabs, n_experts=n_experts, n_sub=n_sub),
        grid_spec=grid_spec,
        out_shape=jax.ShapeDtypeStruct((n_rows * n_slabs, LANES), F32),
        compiler_params=_params(1),
        name="moe_dispatch",
    )(pad, tile_count, dest, h)


def _moe_body(te_ref, tn_ref, xs_ref, wg_ref, wu_ref, wd_ref, y_hbm,
              y_ref, xb_ref, wgb_ref, wub_ref, wdb_ref, stage_ref, zero_ref, osem, zsem, *, sub, n_sub):
    i = pl.program_id(0)
    j = pl.program_id(1)
    last = j == pl.num_programs(1) - 1
    count = tn_ref[i]
    d = wg_ref.shape[2]
    slab_rows = stage_ref.shape[1]

    @pl.when((i == 0) & (j == 0))
    def _():
        zero_ref[...] = jnp.zeros(zero_ref.shape, F32)

    for s in range(n_sub):
        rows = pl.ds(s * sub, sub)

        @pl.when((j == 0) & (s < count))
        def _():
            xb_ref[rows, :] = _load_slabs(xs_ref, s * sub, sub, d).astype(BF16)
            y_ref[rows, :] = jnp.zeros((sub, d), F32)

    def expert_rows(s, wg, wu, wd):
        rows = pl.ds(s * sub, sub)
        x = xb_ref[rows, :]
        a = jnp.dot(x, wg, preferred_element_type=F32)
        u = jnp.dot(x, wu, preferred_element_type=F32)
        act = (a * jax.nn.sigmoid(a) * u).astype(BF16)
        y_ref[rows, :] += jnp.dot(act, wd, preferred_element_type=F32)

    def rounded_weights():
        return tuple(r[0, 0].astype(BF16) for r in (wg_ref, wu_ref, wd_ref))

    @pl.when(count == n_sub)
    def _():
        weights = rounded_weights()
        for s in range(n_sub):
            expert_rows(s, *weights)

    for s in range(n_sub - 1):
        @pl.when((s < count) & (count < n_sub))
        def _():
            if s == 0:
                wgb_ref[...], wub_ref[...], wdb_ref[...] = rounded_weights()
            expert_rows(s, wgb_ref[...], wub_ref[...], wdb_ref[...])

    def out_copy(src, s):
        dst = pl.multiple_of((i * n_sub + s) * slab_rows, slab_rows)
        return src, y_hbm.at[pl.ds(dst, slab_rows), :]

    n_tiles = pl.num_programs(0)
    prev_count = tn_ref[jnp.maximum(i - 1, 0)]
    next_count = tn_ref[jnp.minimum(i + 1, n_tiles - 1)]
    final_tile = (i == n_tiles - 1) | (next_count == 0)

    def stage_copy(s):
        return pltpu.make_async_copy(*out_copy(stage_ref.at[s % 2], s), osem.at[s % 2])

    for slot in range(2):
        @pl.when(last & (count > 0) & (i > 0) & (prev_count > slot))
        def _():
            stage_copy(slot).wait()

    for s in range(n_sub):
        @pl.when(last & (s < count))
        def _():
            if s >= 2:
                stage_copy(s - 2).wait()
            _store_slabs(stage_ref.at[s % 2], 0, y_ref[pl.ds(s * sub, sub), :])
            stage_copy(s).start()

        @pl.when(last & (s >= count))
        def _():
            pltpu.make_async_copy(*out_copy(zero_ref, s), zsem).start()

    for s in range(n_sub):
        @pl.when(last & final_tile & (s < count) & (s + 2 >= count))
        def _():
            stage_copy(s).wait()

        @pl.when(last & (s >= count))
        def _():
            pltpu.make_async_copy(*out_copy(zero_ref, s), zsem).wait()


def moe_experts(xs, tile_expert, tile_count, wg, wu, wd, layer, sub, n_sub, tf=512):
    d = wg.shape[2]
    n_slabs = d // LANES
    n_tiles = tile_expert.shape[0]
    tile_rows = n_sub * sub
    f = wg.shape[3]
    tf = _pick(f, tf)
    nf = f // tf

    def w_col(i, j, te, tn):
        return (layer, te[i], 0, jnp.where(tn[i] > 0, j, nf - 1))

    def w_row(i, j, te, tn):
        return (layer, te[i], jnp.where(tn[i] > 0, j, nf - 1), 0)

    grid_spec = pltpu.PrefetchScalarGridSpec(
        num_scalar_prefetch=2,
        grid=(n_tiles, nf),
        in_specs=[pl.BlockSpec((tile_rows * n_slabs, LANES), lambda i, j, te, tn: (i, 0)),
                  pl.BlockSpec((1, 1, d, tf), w_col),
                  pl.BlockSpec((1, 1, d, tf), w_col),
                  pl.BlockSpec((1, 1, tf, d), w_row)],
        out_specs=pl.BlockSpec(memory_space=pl.ANY),
        scratch_shapes=[pltpu.VMEM((tile_rows, d), F32), pltpu.VMEM((tile_rows, d), BF16),
                        pltpu.VMEM((d, tf), BF16), pltpu.VMEM((d, tf), BF16), pltpu.VMEM((tf, d), BF16),
                        pltpu.VMEM((2, sub * n_slabs, LANES), F32), pltpu.VMEM((sub * n_slabs, LANES), F32),
                        pltpu.SemaphoreType.DMA((2,)), pltpu.SemaphoreType.DMA(())],
    )
    return pl.pallas_call(
        functools.partial(_moe_body, sub=sub, n_sub=n_sub),
        grid_spec=grid_spec,
        out_shape=jax.ShapeDtypeStruct((n_tiles * tile_rows * n_slabs, LANES), F32),
        compiler_params=_params(2),
        name="moe_experts",
    )(tile_expert, tile_count, xs, wg, wu, wd)


def _combine_body(d_cur_ref, d_nxt_ref, x_ref, meta_ref, g_ref, y_hbm, o_ref, buf_ref, sem,
                  *, tm, final_norm):
    i = pl.program_id(0)
    n = pl.num_programs(0)
    d = x_ref.shape[1]
    n_slabs = d // LANES

    def gather(d_ref, slot):
        def body(it, carry):
            base = it * GATHER_UNROLL
            srcs = [pl.multiple_of(d_ref[0, 0, base + k], n_slabs) for k in range(GATHER_UNROLL)]
            for k in range(GATHER_UNROLL):
                dst = pl.multiple_of((base + k) * n_slabs, n_slabs)
                pltpu.make_async_copy(y_hbm.at[pl.ds(srcs[k], n_slabs), :],
                                      buf_ref.at[slot, pl.ds(dst, n_slabs), :], sem.at[slot]
                                      ).start(priority=k % N_DMA_PRIORITIES)
            return carry
        lax.fori_loop(0, TOP_K * tm // GATHER_UNROLL, body, 0)

    @pl.when(i == 0)
    def _():
        gather(d_cur_ref, 0)

    @pl.when(i + 1 < n)
    def _():
        gather(d_nxt_ref, (i + 1) % 2)

    slot = i % 2
    pltpu.make_async_copy(y_hbm.at[pl.ds(0, TOP_K * tm * n_slabs), :], buf_ref.at[slot],
                          sem.at[slot]).wait()
    meta = meta_ref[...]
    out = (x_ref[...] + meta[:, 2:3] * _load_slabs(buf_ref.at[slot], 0, tm, d)
           + meta[:, 3:4] * _load_slabs(buf_ref.at[slot], tm, tm, d))
    if final_norm:
        out = _rms(out, g_ref[...])
    o_ref[...] = out


def moe_combine(x, meta, dest, y, g, final_norm, tm=256):
    t, d = x.shape
    n = t // tm
    n_slabs = d // LANES
    smem = lambda fn: pl.BlockSpec((1, 1, TOP_K * tm), fn, memory_space=pltpu.SMEM)
    return pl.pallas_call(
        functools.partial(_combine_body, tm=tm, final_norm=final_norm),
        grid=(n,),
        in_specs=[smem(lambda i: (i, 0, 0)),
                  smem(lambda i: (jnp.minimum(i + 1, n - 1), 0, 0)),
                  pl.BlockSpec((tm, d), lambda i: (i, 0)),
                  pl.BlockSpec((tm, LANES), lambda i: (i, 0)),
                  pl.BlockSpec((1, d), lambda i: (0, 0)),
                  pl.BlockSpec(memory_space=pl.ANY)],
        out_specs=pl.BlockSpec((tm, d), lambda i: (i, 0)),
        out_shape=jax.ShapeDtypeStruct((t, d), F32),
        scratch_shapes=[pltpu.VMEM((2, TOP_K * tm * n_slabs, LANES), F32), pltpu.SemaphoreType.DMA((2,))],
        compiler_params=_params(1),
        name="moe_combine",
    )(dest, dest, x, meta, g.reshape(1, d), y)


def _route(picks, counts, n_experts, sub, n_sub, tm_combine):
    t = picks.shape[1]
    tile_rows = sub * n_sub
    expert = picks[0:TOP_K].astype(I32)
    rank = picks[2 * TOP_K:3 * TOP_K].astype(I32)
    counts = counts[0, :n_experts].astype(I32)
    subs = (counts + sub - 1) // sub
    tiles = (subs + n_sub - 1) // n_sub
    tile_ends = jnp.cumsum(tiles)
    tile_starts = tile_ends - tiles
    first_row = jnp.zeros_like(expert)
    for e in range(n_experts):
        first_row = jnp.where(expert == e, tile_starts[e] * tile_rows, first_row)
    dest = first_row + rank
    n_tiles = ((t * TOP_K) // sub + n_experts * n_sub) // n_sub
    idx = jnp.arange(n_tiles, dtype=I32)
    tile_expert = jnp.minimum(jnp.searchsorted(tile_ends, idx, side="right"), n_experts - 1).astype(I32)
    local = idx - tile_starts[tile_expert]
    tile_count = jnp.clip(subs[tile_expert] - local * n_sub, 0, n_sub).astype(I32)
    dest_tiles = dest.reshape(TOP_K, t // tm_combine, tm_combine).transpose(1, 0, 2).reshape(
        t // tm_combine, 1, TOP_K * tm_combine)
    pad = jnp.stack([tile_starts * tile_rows + counts, subs * sub - counts]).astype(I32)
    return tile_expert, tile_count, dest_tiles, pad, n_tiles


def moe_layer(x, g, wr, wg, wu, wd, layer, final_g, final_norm, sub=512, n_sub=4, tm_combine=1024):
    t, d = x.shape
    n_experts = wr.shape[1]
    tm_combine = min(tm_combine, t)
    wr_pad = jnp.zeros((d, LANES), BF16).at[:, :n_experts].set(wr.astype(BF16))
    h, meta, picks, counts = moe_router(x, g, wr_pad, n_experts)
    n_slabs = d // LANES
    tile_expert, tile_count, dest_tiles, pad, n_tiles = _route(picks, counts, n_experts, sub, n_sub,
                                                               tm_combine)
    slab_dest = dest_tiles * n_slabs
    xs = moe_dispatch(h, slab_dest, pad, tile_count, sub, n_sub, n_slabs)
    y = moe_experts(xs, tile_expert, tile_count, wg, wu, wd, layer, sub, n_sub)
    return moe_combine(x, meta, slab_dest, y, final_g, final_norm, tm_combine)


def attention_mixers(x, norm_g, w_in, lam_params, subln_g, sinks, layer):
    b, s, d = x.shape
    diff_heads = d // (4 * HEAD_DIM)
    swa_q_heads = d // (2 * HEAD_DIM)
    swa_kv_heads = max(1, swa_q_heads // 4)
    group = swa_q_heads // swa_kv_heads
    a_width = diff_heads * 2 * HEAD_DIM
    lam_init = 0.8 - 0.6 * math.exp(-0.3 * layer)
    x2 = x.reshape(b * s, d)
    proj = rms_matmul(x2, norm_g, w_in).reshape(b, s, -1)
    oa = diff_attention(proj, lam_params, subln_g, diff_heads, lam_init)
    q_col = 3 * a_width
    k_col = q_col + swa_q_heads * HEAD_DIM
    v_col = k_col + swa_kv_heads * HEAD_DIM
    ob = sliding_window_attention(proj, sinks, q_col, k_col, v_col, swa_kv_heads, group)
    return oa.reshape(b * s, -1), ob.reshape(b * s, -1)


def kernel(x, attn_norm_g, w_in_att, diff_lambda, diff_subln_g, attn_sinks, w_out_att, ffn_norm_g, w_ffn_gate, w_ffn_up, w_ffn_down, conv_norm_g, w_pw1, b_pw1, w_dw, b_dw, conv_ln_g, conv_ln_b, w_pw2, b_pw2, moe_norm_g, w_router, w_exp_gate, w_exp_up, w_exp_down, final_norm_g):
    b, s, d = x.shape
    depth = attn_norm_g.shape[0] + conv_norm_g.shape[0]
    assert depth % 2 == 0, "the final RMSNorm is fused into the last expert layer"
    bf = lambda w: w.astype(BF16)
    for layer in range(depth):
        i = layer // 2
        if layer % 2 == 0:
            oa, ob = attention_mixers(x, attn_norm_g[i], bf(w_in_att[i]), diff_lambda[i], diff_subln_g[i],
                                      attn_sinks[i], layer)
            x2 = mix_ffn(x.reshape(b * s, d), oa, ob, bf(w_out_att[i]), ffn_norm_g[i],
                         bf(w_ffn_gate[i]), bf(w_ffn_up[i]), bf(w_ffn_down[i]))
            x = x2.reshape(b, s, d)
        else:
            x = conformer_conv(x, conv_norm_g[i], bf(w_pw1[i]), b_pw1[i], w_dw[i], b_dw[i],
                               conv_ln_g[i], conv_ln_b[i], bf(w_pw2[i]), b_pw2[i])
            x2 = moe_layer(x.reshape(b * s, d), moe_norm_g[i], w_router[i], w_exp_gate,
                           w_exp_up, w_exp_down, i, final_norm_g,
                           final_norm=(layer == depth - 1))
            x = x2.reshape(b, s, d)
    return x
```

```python
import functools
import math

import jax
import jax.numpy as jnp
from jax import lax
from jax.experimental import pallas as pl
from jax.experimental.pallas import tpu as pltpu

BF16 = jnp.bfloat16
F32 = jnp.float32
I32 = jnp.int32

LOG2_E = 1.4426950408889634
RMS_EPS = 1e-6
LN_EPS = 1e-5
HEAD_DIM = 64
ATTN_BLOCK = 128
CONV_WIDTH = 31
TOP_K = 2
LANES = 128
F32_SUBLANES = 8
CONV_HIST = 32
V7X_VMEM_LIMIT = 56 * 1024 * 1024


def _params(n_axes):
    return pltpu.CompilerParams(dimension_semantics=("arbitrary",) * n_axes,
                                vmem_limit_bytes=V7X_VMEM_LIMIT)


def _rms(x, g):
    return x * lax.rsqrt(jnp.mean(x * x, axis=-1, keepdims=True) + RMS_EPS) * g


def _pick(n, pref):
    t = min(n, pref)
    while n % t:
        t -= LANES if t > LANES else 8
    return t


def _rms_matmul_body(x_ref, g_ref, w_ref, o_ref):
    h = _rms(x_ref[...], g_ref[...]).astype(BF16)
    o_ref[...] = jnp.dot(h, w_ref[...], preferred_element_type=F32).astype(o_ref.dtype)


def rms_matmul(x, g, w, tm=1024):
    t, d = x.shape
    n = w.shape[1]
    tm = _pick(t, tm)
    return pl.pallas_call(
        _rms_matmul_body,
        grid=(t // tm,),
        in_specs=[pl.BlockSpec((tm, d), lambda i: (i, 0)),
                  pl.BlockSpec((1, d), lambda i: (0, 0)),
                  pl.BlockSpec((d, n), lambda i: (0, 0))],
        out_specs=pl.BlockSpec((tm, n), lambda i: (i, 0)),
        out_shape=jax.ShapeDtypeStruct((t, n), BF16),
        compiler_params=_params(1),
        name="rms_inproj",
    )(x, g.reshape(1, d), w)


def _diff_attn_body(q_ref, k_ref, v_ref, lam_ref, g_ref, o_ref, q_ref2, m_ref, l_ref, acc_ref,
                    *, blk, lam_init):
    qi = pl.program_id(2)
    hw = 2 * HEAD_DIM
    half = blk // 2
    lane = lax.broadcasted_iota(I32, (half, hw), 1)
    q = (q_ref[0].astype(F32) * (HEAD_DIM ** -0.5 * LOG2_E)).astype(BF16)
    zero = jnp.zeros((half, hw), BF16)
    q_ref2[...] = jnp.concatenate(
        [jnp.where(keep, q[r:r + half], zero)
         for r in (0, half) for keep in (lane < HEAD_DIM, lane >= HEAD_DIM)], axis=0)

    m_ref[...] = jnp.full(m_ref.shape, -jnp.inf, F32)
    l_ref[...] = jnp.zeros(l_ref.shape, F32)
    acc_ref[...] = jnp.zeros(acc_ref.shape, F32)

    def block(key_start, n_keys, row0, n_rows, masked):
        rows = pl.ds(row0, n_rows)
        k = k_ref[0, pl.ds(key_start, n_keys), :]
        v = v_ref[0, pl.ds(key_start, n_keys), :]
        s = lax.dot_general(q_ref2[rows, :], k, (((1,), (1,)), ((), ())), preferred_element_type=F32)
        if masked:
            row = lax.broadcasted_iota(I32, (n_rows, n_keys), 0) & (half - 1)
            col = lax.broadcasted_iota(I32, (n_rows, n_keys), 1)
            s = jnp.where(col <= row, s, -jnp.inf)
        tiles = [s[:, c * LANES:(c + 1) * LANES] for c in range(n_keys // LANES)]
        m_prev = m_ref[rows, :]
        m_new = jnp.maximum(m_prev, jnp.max(functools.reduce(jnp.maximum, tiles), axis=-1, keepdims=True))
        alpha = jnp.exp2(m_prev - m_new)
        p_tiles = [jnp.exp2(t - m_new) for t in tiles]
        l_ref[rows, :] = alpha * l_ref[rows, :] + functools.reduce(jnp.add, p_tiles)
        p = jnp.concatenate(p_tiles, axis=1).astype(BF16)
        acc_ref[rows, :] = alpha * acc_ref[rows, :] + jnp.dot(p, v, preferred_element_type=F32)
        m_ref[rows, :] = m_new

    def full_block(ki):
        block(pl.multiple_of(ki * blk, blk), blk, 0, 2 * blk, False)

    def pair(j, carry):
        full_block(2 * j)
        full_block(2 * j + 1)
        return carry

    lax.fori_loop(0, qi // 2, pair, 0)

    @pl.when(qi % 2 == 1)
    def _():
        full_block(qi - 1)

    diag = pl.multiple_of(qi * blk, blk)
    block(diag, half, 0, 2 * half, True)
    block(diag, half, 2 * half, 2 * half, False)
    block(diag + half, half, 2 * half, 2 * half, True)

    lp = lam_ref[...]
    lam = (jnp.exp(jnp.sum(lp[0:1] * lp[1:2], axis=-1, keepdims=True))
           - jnp.exp(jnp.sum(lp[2:3] * lp[3:4], axis=-1, keepdims=True)) + lam_init)
    o = acc_ref[...] / jnp.sum(l_ref[...], axis=-1, keepdims=True)
    o = (jnp.concatenate([o[0:half], o[2 * half:3 * half]], axis=0)
         - lam * jnp.concatenate([o[half:2 * half], o[3 * half:4 * half]], axis=0))
    o = _rms(o, g_ref[...]) * (1.0 - lam_init)
    o_ref[0] = o.astype(o_ref.dtype)


def diff_attention(proj, lam_params, subln_g, n_heads, lam_init, blk=1024):
    b, s, _ = proj.shape
    blk = _pick(s, blk)
    assert blk & (blk - 1) == 0, "the causal mask uses power-of-two half blocks"
    hw = 2 * HEAD_DIM
    kernel = functools.partial(_diff_attn_body, blk=blk, lam_init=lam_init)
    return pl.pallas_call(
        kernel,
        grid=(b, n_heads, s // blk),
        in_specs=[pl.BlockSpec((1, blk, hw), lambda bi, h, qi: (bi, qi, h)),
                  pl.BlockSpec((1, s, hw), lambda bi, h, qi: (bi, 0, n_heads + h)),
                  pl.BlockSpec((1, s, hw), lambda bi, h, qi: (bi, 0, 2 * n_heads + h)),
                  pl.BlockSpec((4, HEAD_DIM), lambda bi, h, qi: (0, 0)),
                  pl.BlockSpec((1, hw), lambda bi, h, qi: (0, 0))],
        out_specs=pl.BlockSpec((1, blk, hw), lambda bi, h, qi: (bi, qi, h)),
        out_shape=jax.ShapeDtypeStruct((b, s, n_heads * hw), BF16),
        scratch_shapes=[pltpu.VMEM((2 * blk, hw), BF16), pltpu.VMEM((2 * blk, LANES), F32),
                        pltpu.VMEM((2 * blk, LANES), F32), pltpu.VMEM((2 * blk, hw), F32)],
        compiler_params=_params(3),
        name="diff_attention",
    )(proj, proj, proj, lam_params, subln_g.reshape(1, hw))


def _swa_body(sink_ref, q_ref, kc_ref, kp_ref, vc_ref, vp_ref, o_ref, *, tq, kv_heads, group):
    t = pl.program_id(1)
    w = ATTN_BLOCK
    row = lax.broadcasted_iota(I32, (group * w, 2 * w), 0) & (w - 1)
    col = lax.broadcasted_iota(I32, (group * w, 2 * w), 1)
    band = (col > row) & (col <= row + w)
    band_first = band & ((col >= w) | (t > 0))
    low_half = lax.broadcasted_iota(I32, (w, LANES), 1) < HEAD_DIM
    for j in range(tq // w):
        rows = slice(j * w, (j + 1) * w)
        if j == 0:
            k2 = jnp.concatenate([kp_ref[0], kc_ref[0, rows, :]], axis=0)
            v2 = jnp.concatenate([vp_ref[0], vc_ref[0, rows, :]], axis=0)
            mask = band_first
        else:
            k2 = kc_ref[0, (j - 1) * w:(j + 1) * w, :]
            v2 = vc_ref[0, (j - 1) * w:(j + 1) * w, :]
            mask = band
        outs = []
        for g in range(kv_heads):
            kg = k2[:, g * HEAD_DIM:(g + 1) * HEAD_DIM]
            vg = v2[:, g * HEAD_DIM:(g + 1) * HEAD_DIM]
            kdup = jnp.concatenate([kg, kg], axis=1)
            vdup = jnp.concatenate([vg, vg], axis=1)
            heads = [g * group + i for i in range(group)]
            q_tiles = []
            for h in heads:
                q = q_ref[0, rows, (h // 2) * LANES:(h // 2 + 1) * LANES]
                q = (q.astype(F32) * (HEAD_DIM ** -0.5 * LOG2_E)).astype(BF16)
                own_half = low_half if h % 2 == 0 else jnp.logical_not(low_half)
                q_tiles.append(jnp.where(own_half, q, jnp.zeros_like(q)))
            sink = jnp.concatenate([jnp.full((w, LANES), sink_ref[h] * LOG2_E, F32) for h in heads], axis=0)
            s = lax.dot_general(jnp.concatenate(q_tiles, axis=0), kdup, (((1,), (1,)), ((), ())),
                                preferred_element_type=F32)
            s = jnp.where(mask, s, -jnp.inf)
            s0, s1 = s[:, :w], s[:, w:]
            m = jnp.maximum(jnp.max(jnp.maximum(s0, s1), axis=-1, keepdims=True), sink)
            e0 = jnp.exp2(s0 - m)
            e1 = jnp.exp2(s1 - m)
            denom = jnp.sum(e0 + e1, axis=-1, keepdims=True) + jnp.exp2(sink - m)
            e = jnp.concatenate([e0, e1], axis=1).astype(BF16)
            o = jnp.dot(e, vdup, preferred_element_type=F32) / denom
            outs.extend(o[i * w:(i + 1) * w, :] for i in range(group))
        tiles = [jnp.where(low_half, outs[h], outs[h + 1]) for h in range(0, len(outs), 2)]
        o_ref[0, rows, :] = jnp.concatenate(tiles, axis=1).astype(o_ref.dtype)


def sliding_window_attention(proj, sinks, q_col, k_col, v_col, kv_heads, group, tq=1024):
    b, s, _ = proj.shape
    tq = _pick(s, tq)
    qw = kv_heads * group * HEAD_DIM
    kw = kv_heads * HEAD_DIM
    sub = tq // ATTN_BLOCK
    kernel = functools.partial(_swa_body, tq=tq, kv_heads=kv_heads, group=group)
    prev = lambda bi, t: (bi, jnp.maximum(t * sub - 1, 0), 0)
    return pl.pallas_call(
        kernel,
        grid=(b, s // tq),
        in_specs=[pl.BlockSpec(memory_space=pltpu.SMEM),
                  pl.BlockSpec((1, tq, qw), lambda bi, t: (bi, t, q_col // qw)),
                  pl.BlockSpec((1, tq, kw), lambda bi, t: (bi, t, k_col // kw)),
                  pl.BlockSpec((1, ATTN_BLOCK, kw), lambda bi, t: prev(bi, t)[:2] + (k_col // kw,)),
                  pl.BlockSpec((1, tq, kw), lambda bi, t: (bi, t, v_col // kw)),
                  pl.BlockSpec((1, ATTN_BLOCK, kw), lambda bi, t: prev(bi, t)[:2] + (v_col // kw,))],
        out_specs=pl.BlockSpec((1, tq, qw), lambda bi, t: (bi, t, 0)),
        out_shape=jax.ShapeDtypeStruct((b, s, qw), BF16),
        compiler_params=_params(2),
        name="swa_attention",
    )(sinks, proj, proj, proj, proj, proj)


def _mix_ffn_body(x_ref, a_ref, b_ref, wo_ref, g_ref, wg_ref, wu_ref, wd_ref, o_ref, *, ff_chunk):
    mix = jnp.concatenate([a_ref[...], b_ref[...]], axis=1)
    x1 = x_ref[...] + jnp.dot(mix, wo_ref[...], preferred_element_type=F32)
    h = _rms(x1, g_ref[...]).astype(BF16)
    y = x1
    for c in range(wg_ref.shape[1] // ff_chunk):
        cols = slice(c * ff_chunk, (c + 1) * ff_chunk)
        a = jnp.dot(h, wg_ref[:, cols], preferred_element_type=F32)
        u = jnp.dot(h, wu_ref[:, cols], preferred_element_type=F32)
        act = (a * jax.nn.sigmoid(a) * u).astype(BF16)
        y = y + jnp.dot(act, wd_ref[cols, :], preferred_element_type=F32)
    o_ref[...] = y


def mix_ffn(x, oa, ob, wo, g, wg, wu, wd, tm=512, ff_chunk=1408):
    t, d = x.shape
    f = wg.shape[1]
    tm = _pick(t, tm)
    ff_chunk = _pick(f, ff_chunk)
    resident = lambda shape: pl.BlockSpec(shape, lambda i: (0, 0), pipeline_mode=pl.Buffered(1))
    return pl.pallas_call(
        functools.partial(_mix_ffn_body, ff_chunk=ff_chunk),
        grid=(t // tm,),
        in_specs=[pl.BlockSpec((tm, d), lambda i: (i, 0)),
                  pl.BlockSpec((tm, oa.shape[1]), lambda i: (i, 0)),
                  pl.BlockSpec((tm, ob.shape[1]), lambda i: (i, 0)),
                  resident(wo.shape), resident((1, d)),
                  resident(wg.shape), resident(wu.shape), resident(wd.shape)],
        out_specs=pl.BlockSpec((tm, d), lambda i: (i, 0)),
        out_shape=jax.ShapeDtypeStruct((t, d), F32),
        compiler_params=_params(1),
        name="mix_ffn",
    )(x, oa, ob, wo, g.reshape(1, d), wg, wu, wd)


def _conv_body(x_ref, g_ref, w1_ref, b1_ref, wdw_ref, bdw_ref, lng_ref, lnb_ref, w2_ref, b2_ref,
               o_ref, u_ref, v_ref, wb_ref, c_ref, *, tm, rows_per_chunk):
    d = x_ref.shape[-1]
    t = pl.program_id(1)
    sublanes = wb_ref.shape[1]

    @pl.when(t == 0)
    def _():
        u_ref[:, 0:CONV_HIST, :] = jnp.zeros((d // LANES, CONV_HIST, LANES), F32)
        wb_ref[...] = jnp.broadcast_to(wdw_ref[...][:, None, :], wb_ref.shape)

    h = _rms(x_ref[0], g_ref[...]).astype(BF16)
    z = jnp.dot(h, w1_ref[...], preferred_element_type=F32) + b1_ref[...]
    u = z[:, :d] * jax.nn.sigmoid(z[:, d:])
    n_slabs = d // LANES
    for c in range(n_slabs):
        u_ref[c, CONV_HIST:CONV_HIST + tm, :] = u[:, c * LANES:(c + 1) * LANES]

    first_tap = CONV_HIST - (CONV_WIDTH - 1)
    conv_rows = 8 * sublanes

    for c in range(n_slabs):
        lanes = slice(c * LANES, (c + 1) * LANES)
        w = [wb_ref[j, :, lanes] for j in range(CONV_WIDTH)]
        bias = jnp.zeros((sublanes, LANES), F32) + bdw_ref[:, lanes]

        def rows_block(i, carry, c=c, lanes=lanes, w=w, bias=bias):
            r0 = pl.multiple_of(i * conv_rows, conv_rows)
            for r in range(conv_rows // sublanes):
                sums = [bias, None]
                for j in range(CONV_WIDTH):
                    tap = u_ref[c, pl.ds(r0 + (first_tap + j + r * sublanes), sublanes, stride=1), :] * w[j]
                    sums[j % 2] = tap if sums[j % 2] is None else sums[j % 2] + tap
                c_ref[pl.ds(r0 + r * sublanes, sublanes), lanes] = sums[0] + sums[1]
            return carry

        lax.fori_loop(0, tm // conv_rows, rows_block, 0)

    def chunk(i, carry):
        r0 = pl.multiple_of(i * rows_per_chunk, rows_per_chunk)
        acc = c_ref[pl.ds(r0, rows_per_chunk), :]
        mu = jnp.mean(acc, axis=-1, keepdims=True)
        xc = acc - mu
        var = jnp.mean(xc * xc, axis=-1, keepdims=True)
        y = xc * lax.rsqrt(var + LN_EPS) * lng_ref[...] + lnb_ref[...]
        v_ref[pl.ds(r0, rows_per_chunk), :] = (y * jax.nn.sigmoid(y)).astype(BF16)
        return carry

    lax.fori_loop(0, tm // rows_per_chunk, chunk, 0)
    u_ref[:, 0:CONV_HIST, :] = u_ref[:, tm:tm + CONV_HIST, :]
    o_ref[0] = x_ref[0] + jnp.dot(v_ref[...], w2_ref[...], preferred_element_type=F32) + b2_ref[...]


def conformer_conv(x, g, w1, b1, wdw, bdw, lng, lnb, w2, b2, tm=1024, rows_per_chunk=256):
    b, s, d = x.shape
    tm = _pick(s, tm)
    kernel = functools.partial(_conv_body, tm=tm, rows_per_chunk=rows_per_chunk)
    vec = lambda n: pl.BlockSpec((1, n), lambda bi, t: (0, 0))
    return pl.pallas_call(
        kernel,
        grid=(b, s // tm),
        in_specs=[pl.BlockSpec((1, tm, d), lambda bi, t: (bi, t, 0)),
                  vec(d),
                  pl.BlockSpec((d, 2 * d), lambda bi, t: (0, 0)),
                  vec(2 * d),
                  pl.BlockSpec((CONV_WIDTH, d), lambda bi, t: (0, 0)),
                  vec(d), vec(d), vec(d),
                  pl.BlockSpec((d, d), lambda bi, t: (0, 0)),
                  vec(d)],
        out_specs=pl.BlockSpec((1, tm, d), lambda bi, t: (bi, t, 0)),
        out_shape=jax.ShapeDtypeStruct((b, s, d), F32),
        scratch_shapes=[pltpu.VMEM((d // LANES, tm + CONV_HIST, LANES), F32), pltpu.VMEM((tm, d), BF16),
                        pltpu.VMEM((CONV_WIDTH, F32_SUBLANES, d), F32), pltpu.VMEM((tm, d), F32)],
        compiler_params=_params(2),
        name="conformer_conv",
    )(x, g.reshape(1, d), w1, b1.reshape(1, 2 * d), wdw, bdw.reshape(1, d), lng.reshape(1, d),
      lnb.reshape(1, d), w2, b2.reshape(1, d))


def _router_body(x_ref, g_ref, wr_ref, h_ref, meta_ref, picks_ref, counts_ref, *, n_experts):
    @pl.when(pl.program_id(0) == 0)
    def _():
        counts_ref[...] = jnp.zeros(counts_ref.shape, F32)

    h = _rms(x_ref[...], g_ref[...])
    _store_slabs(h_ref, 0, h)
    logits = jnp.dot(h.astype(BF16), wr_ref[...], preferred_element_type=F32)
    lane = lax.broadcasted_iota(I32, logits.shape, 1).astype(F32)
    neg = jnp.full_like(logits, -jnp.inf)
    far = jnp.full_like(logits, float(LANES))
    l1 = jnp.where(lane < n_experts, logits, neg)
    m1 = jnp.max(l1, axis=-1, keepdims=True)
    i1 = jnp.min(jnp.where(l1 == m1, lane, far), axis=-1, keepdims=True)
    l2 = jnp.where(lane == i1, neg, l1)
    m2 = jnp.max(l2, axis=-1, keepdims=True)
    i2 = jnp.min(jnp.where(l2 == m2, lane, far), axis=-1, keepdims=True)
    e2 = jnp.exp(m2 - m1)
    w1 = 1.0 / (1.0 + e2)
    w2 = e2 / (1.0 + e2)
    tm = logits.shape[0]
    hot1 = (lane == i1).astype(F32)
    hot2 = (lane == i2).astype(F32)
    both = (hot1 + hot2).astype(BF16)
    earlier = (lax.broadcasted_iota(I32, (tm, tm), 1) < lax.broadcasted_iota(I32, (tm, tm), 0)).astype(BF16)
    before = jnp.dot(earlier, both, preferred_element_type=F32) + counts_ref[...]
    r1 = jnp.sum(hot1 * before, axis=-1, keepdims=True)
    r2 = jnp.sum(hot2 * before, axis=-1, keepdims=True)
    counts_ref[...] += jnp.sum(hot1 + hot2, axis=0, keepdims=True)
    zero = jnp.zeros_like(logits)
    meta = jnp.where(lane == 0, i1,
           jnp.where(lane == 1, i2,
           jnp.where(lane == 2, w1,
           jnp.where(lane == 3, w2,
           jnp.where(lane == 4, r1, jnp.where(lane == 5, r2, zero))))))
    meta_ref[...] = meta
    picks_ref[...] = jnp.transpose(meta)[0:picks_ref.shape[0], :]


def moe_router(x, g, wr, n_experts, tm=1024):
    t, d = x.shape
    tm = _pick(t, tm)
    kernel = functools.partial(_router_body, n_experts=n_experts)
    return pl.pallas_call(
        kernel,
        grid=(t // tm,),
        in_specs=[pl.BlockSpec((tm, d), lambda i: (i, 0)),
                  pl.BlockSpec((1, d), lambda i: (0, 0)),
                  pl.BlockSpec((d, LANES), lambda i: (0, 0))],
        out_specs=[pl.BlockSpec((tm * (d // LANES), LANES), lambda i: (i, 0)),
                   pl.BlockSpec((tm, LANES), lambda i: (i, 0)),
                   pl.BlockSpec((F32_SUBLANES, tm), lambda i: (0, i)),
                   pl.BlockSpec((1, LANES), lambda i: (0, 0))],
        out_shape=[jax.ShapeDtypeStruct((t * (d // LANES), LANES), F32),
                   jax.ShapeDtypeStruct((t, LANES), F32),
                   jax.ShapeDtypeStruct((F32_SUBLANES, t), F32),
                   jax.ShapeDtypeStruct((1, LANES), F32)],
        compiler_params=_params(1),
        name="moe_router",
    )(x, g.reshape(1, d), wr)


GATHER_UNROLL = 8
N_DMA_PRIORITIES = 2


def _store_slabs(ref, first_row, value):
    n, d = value.shape
    n_slabs = d // LANES
    for c in range(n_slabs):
        ref[pl.ds(first_row * n_slabs + c, n, stride=n_slabs), :] = value[:, c * LANES:(c + 1) * LANES]


def _load_slabs(ref, first_row, n, d):
    n_slabs = d // LANES
    return jnp.concatenate([ref[pl.ds(first_row * n_slabs + c, n, stride=n_slabs), :]
                            for c in range(n_slabs)], axis=1)


def _dispatch_body(pad_ref, tn_ref, d_ref, h_hbm, xs_hbm, zero_ref, buf_ref, in_sem, sem, zsem,
                   *, tm, n_slabs, n_experts, n_sub):
    i = pl.program_id(0)
    n = pl.num_programs(0)
    rows = tm * n_slabs
    n_ring = buf_ref.shape[0]

    def stage(tile):
        return pltpu.make_async_copy(h_hbm.at[pl.ds(pl.multiple_of(tile * rows, rows), rows), :],
                                     buf_ref.at[tile % n_ring], in_sem.at[tile % n_ring])

    def tile_copy(slot):
        return pltpu.make_async_copy(buf_ref.at[0], xs_hbm.at[pl.ds(0, rows), :], sem.at[slot])

    @pl.when(i == 0)
    def _():
        stage(0).start()

    @pl.when(i + 1 < n)
    def _():
        stage(i + 1).start()

    stage(i).wait()
    slot = i % 2
    ring = i % n_ring
    for rank in range(TOP_K):
        def body(it, carry, rank=rank):
            base = it * GATHER_UNROLL
            dsts = [pl.multiple_of(d_ref[0, 0, rank * tm + base + k], n_slabs) for k in range(GATHER_UNROLL)]
            for k in range(GATHER_UNROLL):
                src = pl.multiple_of((base + k) * n_slabs, n_slabs)
                pltpu.make_async_copy(buf_ref.at[ring, pl.ds(src, n_slabs), :],
                                      xs_hbm.at[pl.ds(dsts[k], n_slabs), :], sem.at[slot]
                                      ).start(priority=k % N_DMA_PRIORITIES)
            return carry
        lax.fori_loop(0, tm // GATHER_UNROLL, body, 0)

    @pl.when(i == 0)
    def _():
        zero_ref[...] = jnp.zeros(zero_ref.shape, F32)
        zero_row = zero_ref.at[pl.ds(0, n_slabs), :]
        sub_rows = zero_ref.shape[0]

        def unused_sub_tiles(fn):
            def tbody(tile, carry):
                for s in range(n_sub):
                    @pl.when(s >= tn_ref[tile])
                    def _():
                        dst = pl.multiple_of((tile * n_sub + s) * sub_rows, sub_rows)
                        fn(pltpu.make_async_copy(zero_ref, xs_hbm.at[pl.ds(dst, sub_rows), :], zsem))
                return carry
            lax.fori_loop(0, tn_ref.shape[0], tbody, 0)

        def tail_rows(fn):
            for e in range(n_experts):
                def zbody(r, carry, e=e):
                    dst = pl.multiple_of((pad_ref[0, e] + r) * n_slabs, n_slabs)
                    fn(pltpu.make_async_copy(zero_row, xs_hbm.at[pl.ds(dst, n_slabs), :], zsem))
                    return carry
                lax.fori_loop(0, pad_ref[1, e], zbody, 0)

        unused_sub_tiles(lambda cp: cp.start())
        tail_rows(lambda cp: cp.start())
        unused_sub_tiles(lambda cp: cp.wait())
        tail_rows(lambda cp: cp.wait())

    @pl.when(i > 0)
    def _():
        for rank in range(TOP_K):
            tile_copy(1 - slot).wait()

    @pl.when(i == n - 1)
    def _():
        for rank in range(TOP_K):
            tile_copy(slot).wait()


def moe_dispatch(h, dest, pad, tile_count, sub, n_sub, n_slabs):
    n_tok_tiles = dest.shape[0]
    tm = dest.shape[2] // TOP_K
    n_experts = pad.shape[1]
    n_rows = tile_count.shape[0] * n_sub * sub
    grid_spec = pltpu.PrefetchScalarGridSpec(
        num_scalar_prefetch=2,
        grid=(n_tok_tiles,),
        in_specs=[pl.BlockSpec((1, 1, TOP_K * tm), lambda i, pad, tn: (i, 0, 0), memory_space=pltpu.SMEM),
                  pl.BlockSpec(memory_space=pl.ANY)],
        out_specs=pl.BlockSpec(memory_space=pl.ANY),
        scratch_shapes=[pltpu.VMEM((sub * n_slabs, LANES), F32), pltpu.VMEM((3, tm * n_slabs, LANES), F32),
                        pltpu.SemaphoreType.DMA((3,)), pltpu.SemaphoreType.DMA((2,)),
                        pltpu.SemaphoreType.DMA(())],
    )
    return pl.pallas_call(
        functools.partial(_dispatch_body, tm=tm, n_slabs=n_slabs, n_experts=n_experts, n_sub=n_sub),
        grid_spec=grid_spec,
        out_shape=jax.ShapeDtypeStruct((n_rows * n_slabs, LANES), F32),
        compiler_params=_params(1),
        name="moe_dispatch",
    )(pad, tile_count, dest, h)


def _moe_body(te_ref, tn_ref, xs_ref, wg_ref, wu_ref, wd_ref, y_hbm,
              y_ref, xb_ref, wgb_ref, wub_ref, wdb_ref, stage_ref, zero_ref, osem, zsem, *, sub, n_sub):
    i = pl.program_id(0)
    j = pl.program_id(1)
    last = j == pl.num_programs(1) - 1
    count = tn_ref[i]
    d = wg_ref.shape[2]
    slab_rows = stage_ref.shape[1]

    @pl.when((i == 0) & (j == 0))
    def _():
        zero_ref[...] = jnp.zeros(zero_ref.shape, F32)

    for s in range(n_sub):
        rows = pl.ds(s * sub, sub)

        @pl.when((j == 0) & (s < count))
        def _():
            xb_ref[rows, :] = _load_slabs(xs_ref, s * sub, sub, d).astype(BF16)
            y_ref[rows, :] = jnp.zeros((sub, d), F32)

    def expert_rows(s, wg, wu, wd):
        rows = pl.ds(s * sub, sub)
        x = xb_ref[rows, :]
        a = jnp.dot(x, wg, preferred_element_type=F32)
        u = jnp.dot(x, wu, preferred_element_type=F32)
        act = (a * jax.nn.sigmoid(a) * u).astype(BF16)
        y_ref[rows, :] += jnp.dot(act, wd, preferred_element_type=F32)

    def rounded_weights():
        return tuple(r[0, 0].astype(BF16) for r in (wg_ref, wu_ref, wd_ref))

    @pl.when(count == n_sub)
    def _():
        weights = rounded_weights()
        for s in range(n_sub):
            expert_rows(s, *weights)

    for s in range(n_sub - 1):
        @pl.when((s < count) & (count < n_sub))
        def _():
            if s == 0:
                wgb_ref[...], wub_ref[...], wdb_ref[...] = rounded_weights()
            expert_rows(s, wgb_ref[...], wub_ref[...], wdb_ref[...])

    def out_copy(src, s):
        dst = pl.multiple_of((i * n_sub + s) * slab_rows, slab_rows)
        return src, y_hbm.at[pl.ds(dst, slab_rows), :]

    n_tiles = pl.num_programs(0)
    prev_count = tn_ref[jnp.maximum(i - 1, 0)]
    next_count = tn_ref[jnp.minimum(i + 1, n_tiles - 1)]
    final_tile = (i == n_tiles - 1) | (next_count == 0)

    def stage_copy(s):
        return pltpu.make_async_copy(*out_copy(stage_ref.at[s % 2], s), osem.at[s % 2])

    for slot in range(2):
        @pl.when(last & (count > 0) & (i > 0) & (prev_count > slot))
        def _():
            stage_copy(slot).wait()

    for s in range(n_sub):
        @pl.when(last & (s < count))
        def _():
            if s >= 2:
                stage_copy(s - 2).wait()
            _store_slabs(stage_ref.at[s % 2], 0, y_ref[pl.ds(s * sub, sub), :])
            stage_copy(s).start()

        @pl.when(last & (s >= count))
        def _():
            pltpu.make_async_copy(*out_copy(zero_ref, s), zsem).start()

    for s in range(n_sub):
        @pl.when(last & final_tile & (s < count) & (s + 2 >= count))
        def _():
            stage_copy(s).wait()

        @pl.when(last & (s >= count))
        def _():
            pltpu.make_async_copy(*out_copy(zero_ref, s), zsem).wait()


def moe_experts(xs, tile_expert, tile_count, wg, wu, wd, layer, sub, n_sub, tf=512):
    d = wg.shape[2]
    n_slabs = d // LANES
    n_tiles = tile_expert.shape[0]
    tile_rows = n_sub * sub
    f = wg.shape[3]
    tf = _pick(f, tf)
    nf = f // tf

    def w_col(i, j, te, tn):
        return (layer, te[i], 0, jnp.where(tn[i] > 0, j, nf - 1))

    def w_row(i, j, te, tn):
        return (layer, te[i], jnp.where(tn[i] > 0, j, nf - 1), 0)

    grid_spec = pltpu.PrefetchScalarGridSpec(
        num_scalar_prefetch=2,
        grid=(n_tiles, nf),
        in_specs=[pl.BlockSpec((tile_rows * n_slabs, LANES), lambda i, j, te, tn: (i, 0)),
                  pl.BlockSpec((1, 1, d, tf), w_col),
                  pl.BlockSpec((1, 1, d, tf), w_col),
                  pl.BlockSpec((1, 1, tf, d), w_row)],
        out_specs=pl.BlockSpec(memory_space=pl.ANY),
        scratch_shapes=[pltpu.VMEM((tile_rows, d), F32), pltpu.VMEM((tile_rows, d), BF16),
                        pltpu.VMEM((d, tf), BF16), pltpu.VMEM((d, tf), BF16), pltpu.VMEM((tf, d), BF16),
                        pltpu.VMEM((2, sub * n_slabs, LANES), F32), pltpu.VMEM((sub * n_slabs, LANES), F32),
                        pltpu.SemaphoreType.DMA((2,)), pltpu.SemaphoreType.DMA(())],
    )
    return pl.pallas_call(
        functools.partial(_moe_body, sub=sub, n_sub=n_sub),
        grid_spec=grid_spec,
        out_shape=jax.ShapeDtypeStruct((n_tiles * tile_rows * n_slabs, LANES), F32),
        compiler_params=_params(2),
        name="moe_experts",
    )(tile_expert, tile_count, xs, wg, wu, wd)


def _combine_body(d_cur_ref, d_nxt_ref, x_ref, meta_ref, g_ref, y_hbm, o_ref, buf_ref, sem,
                  *, tm, final_norm):
    i = pl.program_id(0)
    n = pl.num_programs(0)
    d = x_ref.shape[1]
    n_slabs = d // LANES

    def gather(d_ref, slot):
        def body(it, carry):
            base = it * GATHER_UNROLL
            srcs = [pl.multiple_of(d_ref[0, 0, base + k], n_slabs) for k in range(GATHER_UNROLL)]
            for k in range(GATHER_UNROLL):
                dst = pl.multiple_of((base + k) * n_slabs, n_slabs)
                pltpu.make_async_copy(y_hbm.at[pl.ds(srcs[k], n_slabs), :],
                                      buf_ref.at[slot, pl.ds(dst, n_slabs), :], sem.at[slot]
                                      ).start(priority=k % N_DMA_PRIORITIES)
            return carry
        lax.fori_loop(0, TOP_K * tm // GATHER_UNROLL, body, 0)

    @pl.when(i == 0)
    def _():
        gather(d_cur_ref, 0)

    @pl.when(i + 1 < n)
    def _():
        gather(d_nxt_ref, (i + 1) % 2)

    slot = i % 2
    pltpu.make_async_copy(y_hbm.at[pl.ds(0, TOP_K * tm * n_slabs), :], buf_ref.at[slot],
                          sem.at[slot]).wait()
    meta = meta_ref[...]
    out = (x_ref[...] + meta[:, 2:3] * _load_slabs(buf_ref.at[slot], 0, tm, d)
           + meta[:, 3:4] * _load_slabs(buf_ref.at[slot], tm, tm, d))
    if final_norm:
        out = _rms(out, g_ref[...])
    o_ref[...] = out


def moe_combine(x, meta, dest, y, g, final_norm, tm=256):
    t, d = x.shape
    n = t // tm
    n_slabs = d // LANES
    smem = lambda fn: pl.BlockSpec((1, 1, TOP_K * tm), fn, memory_space=pltpu.SMEM)
    return pl.pallas_call(
        functools.partial(_combine_body, tm=tm, final_norm=final_norm),
        grid=(n,),
        in_specs=[smem(lambda i: (i, 0, 0)),
                  smem(lambda i: (jnp.minimum(i + 1, n - 1), 0, 0)),
                  pl.BlockSpec((tm, d), lambda i: (i, 0)),
                  pl.BlockSpec((tm, LANES), lambda i: (i, 0)),
                  pl.BlockSpec((1, d), lambda i: (0, 0)),
                  pl.BlockSpec(memory_space=pl.ANY)],
        out_specs=pl.BlockSpec((tm, d), lambda i: (i, 0)),
        out_shape=jax.ShapeDtypeStruct((t, d), F32),
        scratch_shapes=[pltpu.VMEM((2, TOP_K * tm * n_slabs, LANES), F32), pltpu.SemaphoreType.DMA((2,))],
        compiler_params=_params(1),
        name="moe_combine",
    )(dest, dest, x, meta, g.reshape(1, d), y)


def _route(picks, counts, n_experts, sub, n_sub, tm_combine):
    t = picks.shape[1]
    tile_rows = sub * n_sub
    expert = picks[0:TOP_K].astype(I32)
    rank = picks[2 * TOP_K:3 * TOP_K].astype(I32)
    counts = counts[0, :n_experts].astype(I32)
    subs = (counts + sub - 1) // sub
    tiles = (subs + n_sub - 1) // n_sub
    tile_ends = jnp.cumsum(tiles)
    tile_starts = tile_ends - tiles
    first_row = jnp.zeros_like(expert)
    for e in range(n_experts):
        first_row = jnp.where(expert == e, tile_starts[e] * tile_rows, first_row)
    dest = first_row + rank
    n_tiles = ((t * TOP_K) // sub + n_experts * n_sub) // n_sub
    idx = jnp.arange(n_tiles, dtype=I32)
    tile_expert = jnp.minimum(jnp.searchsorted(tile_ends, idx, side="right"), n_experts - 1).astype(I32)
    local = idx - tile_starts[tile_expert]
    tile_count = jnp.clip(subs[tile_expert] - local * n_sub, 0, n_sub).astype(I32)
    dest_tiles = dest.reshape(TOP_K, t // tm_combine, tm_combine).transpose(1, 0, 2).reshape(
        t // tm_combine, 1, TOP_K * tm_combine)
    pad = jnp.stack([tile_starts * tile_rows + counts, subs * sub - counts]).astype(I32)
    return tile_expert, tile_count, dest_tiles, pad, n_tiles


def moe_layer(x, g, wr, wg, wu, wd, layer, final_g, final_norm, sub=512, n_sub=4, tm_combine=512):
    t, d = x.shape
    n_experts = wr.shape[1]
    tm_combine = min(tm_combine, t)
    wr_pad = jnp.zeros((d, LANES), BF16).at[:, :n_experts].set(wr.astype(BF16))
    h, meta, picks, counts = moe_router(x, g, wr_pad, n_experts)
    n_slabs = d // LANES
    tile_expert, tile_count, dest_tiles, pad, n_tiles = _route(picks, counts, n_experts, sub, n_sub,
                                                               tm_combine)
    slab_dest = dest_tiles * n_slabs
    xs = moe_dispatch(h, slab_dest, pad, tile_count, sub, n_sub, n_slabs)
    y = moe_experts(xs, tile_expert, tile_count, wg, wu, wd, layer, sub, n_sub)
    return moe_combine(x, meta, slab_dest, y, final_g, final_norm, tm_combine)


def attention_mixers(x, norm_g, w_in, lam_params, subln_g, sinks, layer):
    b, s, d = x.shape
    diff_heads = d // (4 * HEAD_DIM)
    swa_q_heads = d // (2 * HEAD_DIM)
    swa_kv_heads = max(1, swa_q_heads // 4)
    group = swa_q_heads // swa_kv_heads
    a_width = diff_heads * 2 * HEAD_DIM
    lam_init = 0.8 - 0.6 * math.exp(-0.3 * layer)
    x2 = x.reshape(b * s, d)
    proj = rms_matmul(x2, norm_g, w_in).reshape(b, s, -1)
    oa = diff_attention(proj, lam_params, subln_g, diff_heads, lam_init)
    q_col = 3 * a_width
    k_col = q_col + swa_q_heads * HEAD_DIM
    v_col = k_col + swa_kv_heads * HEAD_DIM
    ob = sliding_window_attention(proj, sinks, q_col, k_col, v_col, swa_kv_heads, group)
    return oa.reshape(b * s, -1), ob.reshape(b * s, -1)


def kernel(x, attn_norm_g, w_in_att, diff_lambda, diff_subln_g, attn_sinks, w_out_att, ffn_norm_g, w_ffn_gate, w_ffn_up, w_ffn_down, conv_norm_g, w_pw1, b_pw1, w_dw, b_dw, conv_ln_g, conv_ln_b, w_pw2, b_pw2, moe_norm_g, w_router, w_exp_gate, w_exp_up, w_exp_down, final_norm_g):
    b, s, d = x.shape
    depth = attn_norm_g.shape[0] + conv_norm_g.shape[0]
    assert depth % 2 == 0, "the final RMSNorm is fused into the last expert layer"
    bf = lambda w: w.astype(BF16)
    for layer in range(depth):
        i = layer // 2
        if layer % 2 == 0:
            oa, ob = attention_mixers(x, attn_norm_g[i], bf(w_in_att[i]), diff_lambda[i], diff_subln_g[i],
                                      attn_sinks[i], layer)
            x2 = mix_ffn(x.reshape(b * s, d), oa, ob, bf(w_out_att[i]), ffn_norm_g[i],
                         bf(w_ffn_gate[i]), bf(w_ffn_up[i]), bf(w_ffn_down[i]))
            x = x2.reshape(b, s, d)
        else:
            x = conformer_conv(x, conv_norm_g[i], bf(w_pw1[i]), b_pw1[i], w_dw[i], b_dw[i],
                               conv_ln_g[i], conv_ln_b[i], bf(w_pw2[i]), b_pw2[i])
            x2 = moe_layer(x.reshape(b * s, d), moe_norm_g[i], w_router[i], w_exp_gate,
                           w_exp_up, w_exp_down, i, final_norm_g,
                           final_norm=(layer == depth - 1))
            x = x2.reshape(b, s, d)
    return x
```

```python
import functools
import math

import jax
import jax.numpy as jnp
from jax import lax
from jax.experimental import pallas as pl
from jax.experimental.pallas import tpu as pltpu

BF16 = jnp.bfloat16
F32 = jnp.float32
I32 = jnp.int32

LOG2_E = 1.4426950408889634
RMS_EPS = 1e-6
LN_EPS = 1e-5
HEAD_DIM = 64
ATTN_BLOCK = 128
CONV_WIDTH = 31
TOP_K = 2
LANES = 128
F32_SUBLANES = 8
CONV_HIST = 32
V7X_VMEM_LIMIT = 56 * 1024 * 1024


def _params(n_axes):
    return pltpu.CompilerParams(dimension_semantics=("arbitrary",) * n_axes,
                                vmem_limit_bytes=V7X_VMEM_LIMIT)


def _rms(x, g):
    return x * lax.rsqrt(jnp.mean(x * x, axis=-1, keepdims=True) + RMS_EPS) * g


def _pick(n, pref):
    t = min(n, pref)
    while n % t:
        t -= LANES if t > LANES else 8
    return t


def _rms_matmul_body(x_ref, g_ref, w_ref, o_ref):
    h = _rms(x_ref[...], g_ref[...]).astype(BF16)
    o_ref[...] = jnp.dot(h, w_ref[...], preferred_element_type=F32).astype(o_ref.dtype)


def rms_matmul(x, g, w, tm=1024):
    t, d = x.shape
    n = w.shape[1]
    tm = _pick(t, tm)
    return pl.pallas_call(
        _rms_matmul_body,
        grid=(t // tm,),
        in_specs=[pl.BlockSpec((tm, d), lambda i: (i, 0)),
                  pl.BlockSpec((1, d), lambda i: (0, 0)),
                  pl.BlockSpec((d, n), lambda i: (0, 0))],
        out_specs=pl.BlockSpec((tm, n), lambda i: (i, 0)),
        out_shape=jax.ShapeDtypeStruct((t, n), BF16),
        compiler_params=_params(1),
        name="rms_inproj",
    )(x, g.reshape(1, d), w)


def _diff_attn_body(q_ref, k_ref, v_ref, lam_ref, g_ref, o_ref, q_ref2, m_ref, l_ref, acc_ref,
                    *, blk, lam_init):
    qi = pl.program_id(2)
    hw = 2 * HEAD_DIM
    half = blk // 2
    lane = lax.broadcasted_iota(I32, (half, hw), 1)
    q = (q_ref[0].astype(F32) * (HEAD_DIM ** -0.5 * LOG2_E)).astype(BF16)
    zero = jnp.zeros((half, hw), BF16)
    q_ref2[...] = jnp.concatenate(
        [jnp.where(keep, q[r:r + half], zero)
         for r in (0, half) for keep in (lane < HEAD_DIM, lane >= HEAD_DIM)], axis=0)

    m_ref[...] = jnp.full(m_ref.shape, -jnp.inf, F32)
    l_ref[...] = jnp.zeros(l_ref.shape, F32)
    acc_ref[...] = jnp.zeros(acc_ref.shape, F32)

    def block(key_start, n_keys, row0, n_rows, masked):
        rows = pl.ds(row0, n_rows)
        k = k_ref[0, pl.ds(key_start, n_keys), :]
        v = v_ref[0, pl.ds(key_start, n_keys), :]
        s = lax.dot_general(q_ref2[rows, :], k, (((1,), (1,)), ((), ())), preferred_element_type=F32)
        if masked:
            row = lax.broadcasted_iota(I32, (n_rows, n_keys), 0) & (half - 1)
            col = lax.broadcasted_iota(I32, (n_rows, n_keys), 1)
            s = jnp.where(col <= row, s, -jnp.inf)
        tiles = [s[:, c * LANES:(c + 1) * LANES] for c in range(n_keys // LANES)]
        m_prev = m_ref[rows, :]
        m_new = jnp.maximum(m_prev, jnp.max(functools.reduce(jnp.maximum, tiles), axis=-1, keepdims=True))
        alpha = jnp.exp2(m_prev - m_new)
        p_tiles = [jnp.exp2(t - m_new) for t in tiles]
        l_ref[rows, :] = alpha * l_ref[rows, :] + functools.reduce(jnp.add, p_tiles)
        p = jnp.concatenate(p_tiles, axis=1).astype(BF16)
        acc_ref[rows, :] = alpha * acc_ref[rows, :] + jnp.dot(p, v, preferred_element_type=F32)
        m_ref[rows, :] = m_new

    def full_block(ki):
        block(pl.multiple_of(ki * blk, blk), blk, 0, 2 * blk, False)

    def pair(j, carry):
        full_block(2 * j)
        full_block(2 * j + 1)
        return carry

    lax.fori_loop(0, qi // 2, pair, 0)

    @pl.when(qi % 2 == 1)
    def _():
        full_block(qi - 1)

    diag = pl.multiple_of(qi * blk, blk)
    block(diag, half, 0, 2 * half, True)
    block(diag, half, 2 * half, 2 * half, False)
    block(diag + half, half, 2 * half, 2 * half, True)

    lp = lam_ref[...]
    lam = (jnp.exp(jnp.sum(lp[0:1] * lp[1:2], axis=-1, keepdims=True))
           - jnp.exp(jnp.sum(lp[2:3] * lp[3:4], axis=-1, keepdims=True)) + lam_init)
    o = acc_ref[...] / jnp.sum(l_ref[...], axis=-1, keepdims=True)
    o = (jnp.concatenate([o[0:half], o[2 * half:3 * half]], axis=0)
         - lam * jnp.concatenate([o[half:2 * half], o[3 * half:4 * half]], axis=0))
    o = _rms(o, g_ref[...]) * (1.0 - lam_init)
    o_ref[0] = o.astype(o_ref.dtype)


def diff_attention(proj, lam_params, subln_g, n_heads, lam_init, blk=1024):
    b, s, _ = proj.shape
    blk = _pick(s, blk)
    assert blk & (blk - 1) == 0, "the causal mask uses power-of-two half blocks"
    hw = 2 * HEAD_DIM
    kernel = functools.partial(_diff_attn_body, blk=blk, lam_init=lam_init)
    return pl.pallas_call(
        kernel,
        grid=(b, n_heads, s // blk),
        in_specs=[pl.BlockSpec((1, blk, hw), lambda bi, h, qi: (bi, qi, h)),
                  pl.BlockSpec((1, s, hw), lambda bi, h, qi: (bi, 0, n_heads + h)),
                  pl.BlockSpec((1, s, hw), lambda bi, h, qi: (bi, 0, 2 * n_heads + h)),
                  pl.BlockSpec((4, HEAD_DIM), lambda bi, h, qi: (0, 0)),
                  pl.BlockSpec((1, hw), lambda bi, h, qi: (0, 0))],
        out_specs=pl.BlockSpec((1, blk, hw), lambda bi, h, qi: (bi, qi, h)),
        out_shape=jax.ShapeDtypeStruct((b, s, n_heads * hw), BF16),
        scratch_shapes=[pltpu.VMEM((2 * blk, hw), BF16), pltpu.VMEM((2 * blk, LANES), F32),
                        pltpu.VMEM((2 * blk, LANES), F32), pltpu.VMEM((2 * blk, hw), F32)],
        compiler_params=_params(3),
        name="diff_attention",
    )(proj, proj, proj, lam_params, subln_g.reshape(1, hw))


def _swa_body(sink_ref, q_ref, kc_ref, kp_ref, vc_ref, vp_ref, o_ref, *, tq, kv_heads, group):
    t = pl.program_id(1)
    w = ATTN_BLOCK
    row = lax.broadcasted_iota(I32, (group * w, 2 * w), 0) & (w - 1)
    col = lax.broadcasted_iota(I32, (group * w, 2 * w), 1)
    band = (col > row) & (col <= row + w)
    band_first = band & ((col >= w) | (t > 0))
    low_half = lax.broadcasted_iota(I32, (w, LANES), 1) < HEAD_DIM
    for j in range(tq // w):
        rows = slice(j * w, (j + 1) * w)
        if j == 0:
            k2 = jnp.concatenate([kp_ref[0], kc_ref[0, rows, :]], axis=0)
            v2 = jnp.concatenate([vp_ref[0], vc_ref[0, rows, :]], axis=0)
            mask = band_first
        else:
            k2 = kc_ref[0, (j - 1) * w:(j + 1) * w, :]
            v2 = vc_ref[0, (j - 1) * w:(j + 1) * w, :]
            mask = band
        outs = []
        for g in range(kv_heads):
            kg = k2[:, g * HEAD_DIM:(g + 1) * HEAD_DIM]
            vg = v2[:, g * HEAD_DIM:(g + 1) * HEAD_DIM]
            kdup = jnp.concatenate([kg, kg], axis=1)
            vdup = jnp.concatenate([vg, vg], axis=1)
            heads = [g * group + i for i in range(group)]
            q_tiles = []
            for h in heads:
                q = q_ref[0, rows, (h // 2) * LANES:(h // 2 + 1) * LANES]
                q = (q.astype(F32) * (HEAD_DIM ** -0.5 * LOG2_E)).astype(BF16)
                own_half = low_half if h % 2 == 0 else jnp.logical_not(low_half)
                q_tiles.append(jnp.where(own_half, q, jnp.zeros_like(q)))
            sink = jnp.concatenate([jnp.full((w, LANES), sink_ref[h] * LOG2_E, F32) for h in heads], axis=0)
            s = lax.dot_general(jnp.concatenate(q_tiles, axis=0), kdup, (((1,), (1,)), ((), ())),
                                preferred_element_type=F32)
            s = jnp.where(mask, s, -jnp.inf)
            s0, s1 = s[:, :w], s[:, w:]
            m = jnp.maximum(jnp.max(jnp.maximum(s0, s1), axis=-1, keepdims=True), sink)
            e0 = jnp.exp2(s0 - m)
            e1 = jnp.exp2(s1 - m)
            denom = jnp.sum(e0 + e1, axis=-1, keepdims=True) + jnp.exp2(sink - m)
            e = jnp.concatenate([e0, e1], axis=1).astype(BF16)
            o = jnp.dot(e, vdup, preferred_element_type=F32) / denom
            outs.extend(o[i * w:(i + 1) * w, :] for i in range(group))
        tiles = [jnp.where(low_half, outs[h], outs[h + 1]) for h in range(0, len(outs), 2)]
        o_ref[0, rows, :] = jnp.concatenate(tiles, axis=1).astype(o_ref.dtype)


def sliding_window_attention(proj, sinks, q_col, k_col, v_col, kv_heads, group, tq=1024):
    b, s, _ = proj.shape
    tq = _pick(s, tq)
    qw = kv_heads * group * HEAD_DIM
    kw = kv_heads * HEAD_DIM
    sub = tq // ATTN_BLOCK
    kernel = functools.partial(_swa_body, tq=tq, kv_heads=kv_heads, group=group)
    prev = lambda bi, t: (bi, jnp.maximum(t * sub - 1, 0), 0)
    return pl.pallas_call(
        kernel,
        grid=(b, s // tq),
        in_specs=[pl.BlockSpec(memory_space=pltpu.SMEM),
                  pl.BlockSpec((1, tq, qw), lambda bi, t: (bi, t, q_col // qw)),
                  pl.BlockSpec((1, tq, kw), lambda bi, t: (bi, t, k_col // kw)),
                  pl.BlockSpec((1, ATTN_BLOCK, kw), lambda bi, t: prev(bi, t)[:2] + (k_col // kw,)),
                  pl.BlockSpec((1, tq, kw), lambda bi, t: (bi, t, v_col // kw)),
                  pl.BlockSpec((1, ATTN_BLOCK, kw), lambda bi, t: prev(bi, t)[:2] + (v_col // kw,))],
        out_specs=pl.BlockSpec((1, tq, qw), lambda bi, t: (bi, t, 0)),
        out_shape=jax.ShapeDtypeStruct((b, s, qw), BF16),
        compiler_params=_params(2),
        name="swa_attention",
    )(sinks, proj, proj, proj, proj, proj)


def _mix_ffn_body(x_ref, a_ref, b_ref, wo_ref, g_ref, wg_ref, wu_ref, wd_ref, o_ref, *, ff_chunk):
    mix = jnp.concatenate([a_ref[...], b_ref[...]], axis=1)
    x1 = x_ref[...] + jnp.dot(mix, wo_ref[...], preferred_element_type=F32)
    h = _rms(x1, g_ref[...]).astype(BF16)
    y = x1
    for c in range(wg_ref.shape[1] // ff_chunk):
        cols = slice(c * ff_chunk, (c + 1) * ff_chunk)
        a = jnp.dot(h, wg_ref[:, cols], preferred_element_type=F32)
        u = jnp.dot(h, wu_ref[:, cols], preferred_element_type=F32)
        act = (a * jax.nn.sigmoid(a) * u).astype(BF16)
        y = y + jnp.dot(act, wd_ref[cols, :], preferred_element_type=F32)
    o_ref[...] = y


def mix_ffn(x, oa, ob, wo, g, wg, wu, wd, tm=512, ff_chunk=2816):
    t, d = x.shape
    f = wg.shape[1]
    tm = _pick(t, tm)
    ff_chunk = _pick(f, ff_chunk)
    resident = lambda shape: pl.BlockSpec(shape, lambda i: (0, 0), pipeline_mode=pl.Buffered(1))
    return pl.pallas_call(
        functools.partial(_mix_ffn_body, ff_chunk=ff_chunk),
        grid=(t // tm,),
        in_specs=[pl.BlockSpec((tm, d), lambda i: (i, 0)),
                  pl.BlockSpec((tm, oa.shape[1]), lambda i: (i, 0)),
                  pl.BlockSpec((tm, ob.shape[1]), lambda i: (i, 0)),
                  resident(wo.shape), resident((1, d)),
                  resident(wg.shape), resident(wu.shape), resident(wd.shape)],
        out_specs=pl.BlockSpec((tm, d), lambda i: (i, 0)),
        out_shape=jax.ShapeDtypeStruct((t, d), F32),
        compiler_params=_params(1),
        name="mix_ffn",
    )(x, oa, ob, wo, g.reshape(1, d), wg, wu, wd)


def _conv_body(x_ref, g_ref, w1_ref, b1_ref, wdw_ref, bdw_ref, lng_ref, lnb_ref, w2_ref, b2_ref,
               o_ref, u_ref, v_ref, wb_ref, c_ref, *, tm, rows_per_chunk):
    d = x_ref.shape[-1]
    t = pl.program_id(1)
    sublanes = wb_ref.shape[1]

    @pl.when(t == 0)
    def _():
        u_ref[:, 0:CONV_HIST, :] = jnp.zeros((d // LANES, CONV_HIST, LANES), F32)
        wb_ref[...] = jnp.broadcast_to(wdw_ref[...][:, None, :], wb_ref.shape)

    h = _rms(x_ref[0], g_ref[...]).astype(BF16)
    z = jnp.dot(h, w1_ref[...], preferred_element_type=F32) + b1_ref[...]
    u = z[:, :d] * jax.nn.sigmoid(z[:, d:])
    n_slabs = d // LANES
    for c in range(n_slabs):
        u_ref[c, CONV_HIST:CONV_HIST + tm, :] = u[:, c * LANES:(c + 1) * LANES]

    first_tap = CONV_HIST - (CONV_WIDTH - 1)
    conv_rows = 8 * sublanes

    for c in range(n_slabs):
        lanes = slice(c * LANES, (c + 1) * LANES)
        w = [wb_ref[j, :, lanes] for j in range(CONV_WIDTH)]
        bias = jnp.zeros((sublanes, LANES), F32) + bdw_ref[:, lanes]

        def rows_block(i, carry, c=c, lanes=lanes, w=w, bias=bias):
            r0 = pl.multiple_of(i * conv_rows, conv_rows)
            for r in range(conv_rows // sublanes):
                sums = [bias, None]
                for j in range(CONV_WIDTH):
                    tap = u_ref[c, pl.ds(r0 + (first_tap + j + r * sublanes), sublanes, stride=1), :] * w[j]
                    sums[j % 2] = tap if sums[j % 2] is None else sums[j % 2] + tap
                c_ref[pl.ds(r0 + r * sublanes, sublanes), lanes] = sums[0] + sums[1]
            return carry

        lax.fori_loop(0, tm // conv_rows, rows_block, 0)

    def chunk(i, carry):
        r0 = pl.multiple_of(i * rows_per_chunk, rows_per_chunk)
        acc = c_ref[pl.ds(r0, rows_per_chunk), :]
        mu = jnp.mean(acc, axis=-1, keepdims=True)
        xc = acc - mu
        var = jnp.mean(xc * xc, axis=-1, keepdims=True)
        y = xc * lax.rsqrt(var + LN_EPS) * lng_ref[...] + lnb_ref[...]
        v_ref[pl.ds(r0, rows_per_chunk), :] = (y * jax.nn.sigmoid(y)).astype(BF16)
        return carry

    lax.fori_loop(0, tm // rows_per_chunk, chunk, 0)
    u_ref[:, 0:CONV_HIST, :] = u_ref[:, tm:tm + CONV_HIST, :]
    o_ref[0] = x_ref[0] + jnp.dot(v_ref[...], w2_ref[...], preferred_element_type=F32) + b2_ref[...]


def conformer_conv(x, g, w1, b1, wdw, bdw, lng, lnb, w2, b2, tm=1024, rows_per_chunk=256):
    b, s, d = x.shape
    tm = _pick(s, tm)
    kernel = functools.partial(_conv_body, tm=tm, rows_per_chunk=rows_per_chunk)
    vec = lambda n: pl.BlockSpec((1, n), lambda bi, t: (0, 0))
    return pl.pallas_call(
        kernel,
        grid=(b, s // tm),
        in_specs=[pl.BlockSpec((1, tm, d), lambda bi, t: (bi, t, 0)),
                  vec(d),
                  pl.BlockSpec((d, 2 * d), lambda bi, t: (0, 0)),
                  vec(2 * d),
                  pl.BlockSpec((CONV_WIDTH, d), lambda bi, t: (0, 0)),
                  vec(d), vec(d), vec(d),
                  pl.BlockSpec((d, d), lambda bi, t: (0, 0)),
                  vec(d)],
        out_specs=pl.BlockSpec((1, tm, d), lambda bi, t: (bi, t, 0)),
        out_shape=jax.ShapeDtypeStruct((b, s, d), F32),
        scratch_shapes=[pltpu.VMEM((d // LANES, tm + CONV_HIST, LANES), F32), pltpu.VMEM((tm, d), BF16),
                        pltpu.VMEM((CONV_WIDTH, F32_SUBLANES, d), F32), pltpu.VMEM((tm, d), F32)],
        compiler_params=_params(2),
        name="conformer_conv",
    )(x, g.reshape(1, d), w1, b1.reshape(1, 2 * d), wdw, bdw.reshape(1, d), lng.reshape(1, d),
      lnb.reshape(1, d), w2, b2.reshape(1, d))


def _router_body(x_ref, g_ref, wr_ref, h_ref, meta_ref, picks_ref, counts_ref, *, n_experts):
    @pl.when(pl.program_id(0) == 0)
    def _():
        counts_ref[...] = jnp.zeros(counts_ref.shape, F32)

    h = _rms(x_ref[...], g_ref[...])
    _store_slabs(h_ref, 0, h)
    logits = jnp.dot(h.astype(BF16), wr_ref[...], preferred_element_type=F32)
    lane = lax.broadcasted_iota(I32, logits.shape, 1).astype(F32)
    neg = jnp.full_like(logits, -jnp.inf)
    far = jnp.full_like(logits, float(LANES))
    l1 = jnp.where(lane < n_experts, logits, neg)
    m1 = jnp.max(l1, axis=-1, keepdims=True)
    i1 = jnp.min(jnp.where(l1 == m1, lane, far), axis=-1, keepdims=True)
    l2 = jnp.where(lane == i1, neg, l1)
    m2 = jnp.max(l2, axis=-1, keepdims=True)
    i2 = jnp.min(jnp.where(l2 == m2, lane, far), axis=-1, keepdims=True)
    e2 = jnp.exp(m2 - m1)
    w1 = 1.0 / (1.0 + e2)
    w2 = e2 / (1.0 + e2)
    tm = logits.shape[0]
    hot1 = (lane == i1).astype(F32)
    hot2 = (lane == i2).astype(F32)
    both = (hot1 + hot2).astype(BF16)
    earlier = (lax.broadcasted_iota(I32, (tm, tm), 1) < lax.broadcasted_iota(I32, (tm, tm), 0)).astype(BF16)
    before = jnp.dot(earlier, both, preferred_element_type=F32) + counts_ref[...]
    r1 = jnp.sum(hot1 * before, axis=-1, keepdims=True)
    r2 = jnp.sum(hot2 * before, axis=-1, keepdims=True)
    counts_ref[...] += jnp.sum(hot1 + hot2, axis=0, keepdims=True)
    zero = jnp.zeros_like(logits)
    meta = jnp.where(lane == 0, i1,
           jnp.where(lane == 1, i2,
           jnp.where(lane == 2, w1,
           jnp.where(lane == 3, w2,
           jnp.where(lane == 4, r1, jnp.where(lane == 5, r2, zero))))))
    meta_ref[...] = meta
    picks_ref[...] = jnp.transpose(meta)[0:picks_ref.shape[0], :]


def moe_router(x, g, wr, n_experts, tm=1024):
    t, d = x.shape
    tm = _pick(t, tm)
    kernel = functools.partial(_router_body, n_experts=n_experts)
    return pl.pallas_call(
        kernel,
        grid=(t // tm,),
        in_specs=[pl.BlockSpec((tm, d), lambda i: (i, 0)),
                  pl.BlockSpec((1, d), lambda i: (0, 0)),
                  pl.BlockSpec((d, LANES), lambda i: (0, 0))],
        out_specs=[pl.BlockSpec((tm * (d // LANES), LANES), lambda i: (i, 0)),
                   pl.BlockSpec((tm, LANES), lambda i: (i, 0)),
                   pl.BlockSpec((F32_SUBLANES, tm), lambda i: (0, i)),
                   pl.BlockSpec((1, LANES), lambda i: (0, 0))],
        out_shape=[jax.ShapeDtypeStruct((t * (d // LANES), LANES), F32),
                   jax.ShapeDtypeStruct((t, LANES), F32),
                   jax.ShapeDtypeStruct((F32_SUBLANES, t), F32),
                   jax.ShapeDtypeStruct((1, LANES), F32)],
        compiler_params=_params(1),
        name="moe_router",
    )(x, g.reshape(1, d), wr)


GATHER_UNROLL = 8
N_DMA_PRIORITIES = 2


def _store_slabs(ref, first_row, value):
    n, d = value.shape
    n_slabs = d // LANES
    for c in range(n_slabs):
        ref[pl.ds(first_row * n_slabs + c, n, stride=n_slabs), :] = value[:, c * LANES:(c + 1) * LANES]


def _load_slabs(ref, first_row, n, d):
    n_slabs = d // LANES
    return jnp.concatenate([ref[pl.ds(first_row * n_slabs + c, n, stride=n_slabs), :]
                            for c in range(n_slabs)], axis=1)


def _dispatch_body(pad_ref, tn_ref, d_ref, h_hbm, xs_hbm, zero_ref, buf_ref, in_sem, sem, zsem,
                   *, tm, n_slabs, n_experts, n_sub):
    i = pl.program_id(0)
    n = pl.num_programs(0)
    rows = tm * n_slabs
    n_ring = buf_ref.shape[0]

    def stage(tile):
        return pltpu.make_async_copy(h_hbm.at[pl.ds(pl.multiple_of(tile * rows, rows), rows), :],
                                     buf_ref.at[tile % n_ring], in_sem.at[tile % n_ring])

    def tile_copy(slot):
        return pltpu.make_async_copy(buf_ref.at[0], xs_hbm.at[pl.ds(0, rows), :], sem.at[slot])

    @pl.when(i == 0)
    def _():
        stage(0).start()

    @pl.when(i + 1 < n)
    def _():
        stage(i + 1).start()

    stage(i).wait()
    slot = i % 2
    ring = i % n_ring
    for rank in range(TOP_K):
        def body(it, carry, rank=rank):
            base = it * GATHER_UNROLL
            dsts = [pl.multiple_of(d_ref[0, 0, rank * tm + base + k], n_slabs) for k in range(GATHER_UNROLL)]
            for k in range(GATHER_UNROLL):
                src = pl.multiple_of((base + k) * n_slabs, n_slabs)
                pltpu.make_async_copy(buf_ref.at[ring, pl.ds(src, n_slabs), :],
                                      xs_hbm.at[pl.ds(dsts[k], n_slabs), :], sem.at[slot]
                                      ).start(priority=k % N_DMA_PRIORITIES)
            return carry
        lax.fori_loop(0, tm // GATHER_UNROLL, body, 0)

    @pl.when(i == 0)
    def _():
        zero_ref[...] = jnp.zeros(zero_ref.shape, F32)
        zero_row = zero_ref.at[pl.ds(0, n_slabs), :]
        sub_rows = zero_ref.shape[0]

        def unused_sub_tiles(fn):
            def tbody(tile, carry):
                for s in range(n_sub):
                    @pl.when(s >= tn_ref[tile])
                    def _():
                        dst = pl.multiple_of((tile * n_sub + s) * sub_rows, sub_rows)
                        fn(pltpu.make_async_copy(zero_ref, xs_hbm.at[pl.ds(dst, sub_rows), :], zsem))
                return carry
            lax.fori_loop(0, tn_ref.shape[0], tbody, 0)

        def tail_rows(fn):
            for e in range(n_experts):
                def zbody(r, carry, e=e):
                    dst = pl.multiple_of((pad_ref[0, e] + r) * n_slabs, n_slabs)
                    fn(pltpu.make_async_copy(zero_row, xs_hbm.at[pl.ds(dst, n_slabs), :], zsem))
                    return carry
                lax.fori_loop(0, pad_ref[1, e], zbody, 0)

        unused_sub_tiles(lambda cp: cp.start())
        tail_rows(lambda cp: cp.start())
        unused_sub_tiles(lambda cp: cp.wait())
        tail_rows(lambda cp: cp.wait())

    @pl.when(i > 0)
    def _():
        for rank in range(TOP_K):
            tile_copy(1 - slot).wait()

    @pl.when(i == n - 1)
    def _():
        for rank in range(TOP_K):
            tile_copy(slot).wait()


def moe_dispatch(h, dest, pad, tile_count, sub, n_sub, n_slabs):
    n_tok_tiles = dest.shape[0]
    tm = dest.shape[2] // TOP_K
    n_experts = pad.shape[1]
    n_rows = tile_count.shape[0] * n_sub * sub
    grid_spec = pltpu.PrefetchScalarGridSpec(
        num_scalar_prefetch=2,
        grid=(n_tok_tiles,),
        in_specs=[pl.BlockSpec((1, 1, TOP_K * tm), lambda i, pad, tn: (i, 0, 0), memory_space=pltpu.SMEM),
                  pl.BlockSpec(memory_space=pl.ANY)],
        out_specs=pl.BlockSpec(memory_space=pl.ANY),
        scratch_shapes=[pltpu.VMEM((sub * n_slabs, LANES), F32), pltpu.VMEM((3, tm * n_slabs, LANES), F32),
                        pltpu.SemaphoreType.DMA((3,)), pltpu.SemaphoreType.DMA((2,)),
                        pltpu.SemaphoreType.DMA(())],
    )
    return pl.pallas_call(
        functools.partial(_dispatch_body, tm=tm, n_slabs=n_slabs, n_experts=n_experts, n_sub=n_sub),
        grid_spec=grid_spec,
        out_shape=jax.ShapeDtypeStruct((n_rows * n_slabs, LANES), F32),
        compiler_params=_params(1),
        name="moe_dispatch",
    )(pad, tile_count, dest, h)


def _moe_body(te_ref, tn_ref, xs_ref, wg_ref, wu_ref, wd_ref, y_hbm,
              y_ref, xb_ref, wgb_ref, wub_ref, wdb_ref, stage_ref, zero_ref, osem, zsem, *, sub, n_sub):
    i = pl.program_id(0)
    j = pl.program_id(1)
    last = j == pl.num_programs(1) - 1
    count = tn_ref[i]
    d = wg_ref.shape[2]
    slab_rows = stage_ref.shape[1]

    @pl.when((i == 0) & (j == 0))
    def _():
        zero_ref[...] = jnp.zeros(zero_ref.shape, F32)

    for s in range(n_sub):
        rows = pl.ds(s * sub, sub)

        @pl.when((j == 0) & (s < count))
        def _():
            xb_ref[rows, :] = _load_slabs(xs_ref, s * sub, sub, d).astype(BF16)
            y_ref[rows, :] = jnp.zeros((sub, d), F32)

    def expert_rows(s, wg, wu, wd):
        rows = pl.ds(s * sub, sub)
        x = xb_ref[rows, :]
        a = jnp.dot(x, wg, preferred_element_type=F32)
        u = jnp.dot(x, wu, preferred_element_type=F32)
        act = (a * jax.nn.sigmoid(a) * u).astype(BF16)
        y_ref[rows, :] += jnp.dot(act, wd, preferred_element_type=F32)

    def rounded_weights():
        return tuple(r[0, 0].astype(BF16) for r in (wg_ref, wu_ref, wd_ref))

    @pl.when(count == n_sub)
    def _():
        weights = rounded_weights()
        for s in range(n_sub):
            expert_rows(s, *weights)

    for s in range(n_sub - 1):
        @pl.when((s < count) & (count < n_sub))
        def _():
            if s == 0:
                wgb_ref[...], wub_ref[...], wdb_ref[...] = rounded_weights()
            expert_rows(s, wgb_ref[...], wub_ref[...], wdb_ref[...])

    def out_copy(src, s):
        dst = pl.multiple_of((i * n_sub + s) * slab_rows, slab_rows)
        return src, y_hbm.at[pl.ds(dst, slab_rows), :]

    n_tiles = pl.num_programs(0)
    prev_count = tn_ref[jnp.maximum(i - 1, 0)]
    next_count = tn_ref[jnp.minimum(i + 1, n_tiles - 1)]
    final_tile = (i == n_tiles - 1) | (next_count == 0)

    def stage_copy(s):
        return pltpu.make_async_copy(*out_copy(stage_ref.at[s % 2], s), osem.at[s % 2])

    for slot in range(2):
        @pl.when(last & (count > 0) & (i > 0) & (prev_count > slot))
        def _():
            stage_copy(slot).wait()

    for s in range(n_sub):
        @pl.when(last & (s < count))
        def _():
            if s >= 2:
                stage_copy(s - 2).wait()
            _store_slabs(stage_ref.at[s % 2], 0, y_ref[pl.ds(s * sub, sub), :])
            stage_copy(s).start()

        @pl.when(last & (s >= count))
        def _():
            pltpu.make_async_copy(*out_copy(zero_ref, s), zsem).start()

    for s in range(n_sub):
        @pl.when(last & final_tile & (s < count) & (s + 2 >= count))
        def _():
            stage_copy(s).wait()

        @pl.when(last & (s >= count))
        def _():
            pltpu.make_async_copy(*out_copy(zero_ref, s), zsem).wait()


def moe_experts(xs, tile_expert, tile_count, wg, wu, wd, layer, sub, n_sub, tf=512):
    d = wg.shape[2]
    n_slabs = d // LANES
    n_tiles = tile_expert.shape[0]
    tile_rows = n_sub * sub
    f = wg.shape[3]
    tf = _pick(f, tf)
    nf = f // tf

    def w_col(i, j, te, tn):
        return (layer, te[i], 0, jnp.where(tn[i] > 0, j, nf - 1))

    def w_row(i, j, te, tn):
        return (layer, te[i], jnp.where(tn[i] > 0, j, nf - 1), 0)

    grid_spec = pltpu.PrefetchScalarGridSpec(
        num_scalar_prefetch=2,
        grid=(n_tiles, nf),
        in_specs=[pl.BlockSpec((tile_rows * n_slabs, LANES), lambda i, j, te, tn: (i, 0)),
                  pl.BlockSpec((1, 1, d, tf), w_col),
                  pl.BlockSpec((1, 1, d, tf), w_col),
                  pl.BlockSpec((1, 1, tf, d), w_row)],
        out_specs=pl.BlockSpec(memory_space=pl.ANY),
        scratch_shapes=[pltpu.VMEM((tile_rows, d), F32), pltpu.VMEM((tile_rows, d), BF16),
                        pltpu.VMEM((d, tf), BF16), pltpu.VMEM((d, tf), BF16), pltpu.VMEM((tf, d), BF16),
                        pltpu.VMEM((2, sub * n_slabs, LANES), F32), pltpu.VMEM((sub * n_slabs, LANES), F32),
                        pltpu.SemaphoreType.DMA((2,)), pltpu.SemaphoreType.DMA(())],
    )
    return pl.pallas_call(
        functools.partial(_moe_body, sub=sub, n_sub=n_sub),
        grid_spec=grid_spec,
        out_shape=jax.ShapeDtypeStruct((n_tiles * tile_rows * n_slabs, LANES), F32),
        compiler_params=_params(2),
        name="moe_experts",
    )(tile_expert, tile_count, xs, wg, wu, wd)


def _combine_body(d_cur_ref, d_nxt_ref, x_ref, meta_ref, g_ref, y_hbm, o_ref, buf_ref, sem,
                  *, tm, final_norm):
    i = pl.program_id(0)
    n = pl.num_programs(0)
    d = x_ref.shape[1]
    n_slabs = d // LANES

    def gather(d_ref, slot):
        def body(it, carry):
            base = it * GATHER_UNROLL
            srcs = [pl.multiple_of(d_ref[0, 0, base + k], n_slabs) for k in range(GATHER_UNROLL)]
            for k in range(GATHER_UNROLL):
                dst = pl.multiple_of((base + k) * n_slabs, n_slabs)
                pltpu.make_async_copy(y_hbm.at[pl.ds(srcs[k], n_slabs), :],
                                      buf_ref.at[slot, pl.ds(dst, n_slabs), :], sem.at[slot]
                                      ).start(priority=k % N_DMA_PRIORITIES)
            return carry
        lax.fori_loop(0, TOP_K * tm // GATHER_UNROLL, body, 0)

    @pl.when(i == 0)
    def _():
        gather(d_cur_ref, 0)

    @pl.when(i + 1 < n)
    def _():
        gather(d_nxt_ref, (i + 1) % 2)

    slot = i % 2
    pltpu.make_async_copy(y_hbm.at[pl.ds(0, TOP_K * tm * n_slabs), :], buf_ref.at[slot],
                          sem.at[slot]).wait()
    meta = meta_ref[...]
    out = (x_ref[...] + meta[:, 2:3] * _load_slabs(buf_ref.at[slot], 0, tm, d)
           + meta[:, 3:4] * _load_slabs(buf_ref.at[slot], tm, tm, d))
    if final_norm:
        out = _rms(out, g_ref[...])
    o_ref[...] = out


def moe_combine(x, meta, dest, y, g, final_norm, tm=256):
    t, d = x.shape
    n = t // tm
    n_slabs = d // LANES
    smem = lambda fn: pl.BlockSpec((1, 1, TOP_K * tm), fn, memory_space=pltpu.SMEM)
    return pl.pallas_call(
        functools.partial(_combine_body, tm=tm, final_norm=final_norm),
        grid=(n,),
        in_specs=[smem(lambda i: (i, 0, 0)),
                  smem(lambda i: (jnp.minimum(i + 1, n - 1), 0, 0)),
                  pl.BlockSpec((tm, d), lambda i: (i, 0)),
                  pl.BlockSpec((tm, LANES), lambda i: (i, 0)),
                  pl.BlockSpec((1, d), lambda i: (0, 0)),
                  pl.BlockSpec(memory_space=pl.ANY)],
        out_specs=pl.BlockSpec((tm, d), lambda i: (i, 0)),
        out_shape=jax.ShapeDtypeStruct((t, d), F32),
        scratch_shapes=[pltpu.VMEM((2, TOP_K * tm * n_slabs, LANES), F32), pltpu.SemaphoreType.DMA((2,))],
        compiler_params=_params(1),
        name="moe_combine",
    )(dest, dest, x, meta, g.reshape(1, d), y)


def _route(picks, counts, n_experts, sub, n_sub, tm_combine):
    t = picks.shape[1]
    tile_rows = sub * n_sub
    expert = picks[0:TOP_K].astype(I32)
    rank = picks[2 * TOP_K:3 * TOP_K].astype(I32)
    counts = counts[0, :n_experts].astype(I32)
    subs = (counts + sub - 1) // sub
    tiles = (subs + n_sub - 1) // n_sub
    tile_ends = jnp.cumsum(tiles)
    tile_starts = tile_ends - tiles
    first_row = jnp.zeros_like(expert)
    for e in range(n_experts):
        first_row = jnp.where(expert == e, tile_starts[e] * tile_rows, first_row)
    dest = first_row + rank
    n_tiles = ((t * TOP_K) // sub + n_experts * n_sub) // n_sub
    idx = jnp.arange(n_tiles, dtype=I32)
    tile_expert = jnp.minimum(jnp.searchsorted(tile_ends, idx, side="right"), n_experts - 1).astype(I32)
    local = idx - tile_starts[tile_expert]
    tile_count = jnp.clip(subs[tile_expert] - local * n_sub, 0, n_sub).astype(I32)
    dest_tiles = dest.reshape(TOP_K, t // tm_combine, tm_combine).transpose(1, 0, 2).reshape(
        t // tm_combine, 1, TOP_K * tm_combine)
    pad = jnp.stack([tile_starts * tile_rows + counts, subs * sub - counts]).astype(I32)
    return tile_expert, tile_count, dest_tiles, pad, n_tiles


def moe_layer(x, g, wr, wg, wu, wd, layer, final_g, final_norm, sub=512, n_sub=4, tm_combine=512):
    t, d = x.shape
    n_experts = wr.shape[1]
    tm_combine = min(tm_combine, t)
    wr_pad = jnp.zeros((d, LANES), BF16).at[:, :n_experts].set(wr.astype(BF16))
    h, meta, picks, counts = moe_router(x, g, wr_pad, n_experts)
    n_slabs = d // LANES
    tile_expert, tile_count, dest_tiles, pad, n_tiles = _route(picks, counts, n_experts, sub, n_sub,
                                                               tm_combine)
    slab_dest = dest_tiles * n_slabs
    xs = moe_dispatch(h, slab_dest, pad, tile_count, sub, n_sub, n_slabs)
    y = moe_experts(xs, tile_expert, tile_count, wg, wu, wd, layer, sub, n_sub)
    return moe_combine(x, meta, slab_dest, y, final_g, final_norm, tm_combine)


def attention_mixers(x, norm_g, w_in, lam_params, subln_g, sinks, layer):
    b, s, d = x.shape
    diff_heads = d // (4 * HEAD_DIM)
    swa_q_heads = d // (2 * HEAD_DIM)
    swa_kv_heads = max(1, swa_q_heads // 4)
    group = swa_q_heads // swa_kv_heads
    a_width = diff_heads * 2 * HEAD_DIM
    lam_init = 0.8 - 0.6 * math.exp(-0.3 * layer)
    x2 = x.reshape(b * s, d)
    proj = rms_matmul(x2, norm_g, w_in).reshape(b, s, -1)
    oa = diff_attention(proj, lam_params, subln_g, diff_heads, lam_init)
    q_col = 3 * a_width
    k_col = q_col + swa_q_heads * HEAD_DIM
    v_col = k_col + swa_kv_heads * HEAD_DIM
    ob = sliding_window_attention(proj, sinks, q_col, k_col, v_col, swa_kv_heads, group)
    return oa.reshape(b * s, -1), ob.reshape(b * s, -1)


def kernel(x, attn_norm_g, w_in_att, diff_lambda, diff_subln_g, attn_sinks, w_out_att, ffn_norm_g, w_ffn_gate, w_ffn_up, w_ffn_down, conv_norm_g, w_pw1, b_pw1, w_dw, b_dw, conv_ln_g, conv_ln_b, w_pw2, b_pw2, moe_norm_g, w_router, w_exp_gate, w_exp_up, w_exp_down, final_norm_g):
    b, s, d = x.shape
    depth = attn_norm_g.shape[0] + conv_norm_g.shape[0]
    assert depth % 2 == 0, "the final RMSNorm is fused into the last expert layer"
    bf = lambda w: w.astype(BF16)
    for layer in range(depth):
        i = layer // 2
        if layer % 2 == 0:
            oa, ob = attention_mixers(x, attn_norm_g[i], bf(w_in_att[i]), diff_lambda[i], diff_subln_g[i],
                                      attn_sinks[i], layer)
            x2 = mix_ffn(x.reshape(b * s, d), oa, ob, bf(w_out_att[i]), ffn_norm_g[i],
                         bf(w_ffn_gate[i]), bf(w_ffn_up[i]), bf(w_ffn_down[i]))
            x = x2.reshape(b, s, d)
        else:
            x = conformer_conv(x, conv_norm_g[i], bf(w_pw1[i]), b_pw1[i], w_dw[i], b_dw[i],
                               conv_ln_g[i], conv_ln_b[i], bf(w_pw2[i]), b_pw2[i])
            x2 = moe_layer(x.reshape(b * s, d), moe_norm_g[i], w_router[i], w_exp_gate,
                           w_exp_up, w_exp_down, i, final_norm_g,
                           final_norm=(layer == depth - 1))
            x = x2.reshape(b, s, d)
    return x
```

```python
import functools
import math

import jax
import jax.numpy as jnp
from jax import lax
from jax.experimental import pallas as pl
from jax.experimental.pallas import tpu as pltpu

BF16 = jnp.bfloat16
F32 = jnp.float32
I32 = jnp.int32

LOG2_E = 1.4426950408889634
RMS_EPS = 1e-6
LN_EPS = 1e-5
HEAD_DIM = 64
ATTN_BLOCK = 128
CONV_WIDTH = 31
TOP_K = 2
LANES = 128
F32_SUBLANES = 8
CONV_HIST = 32
V7X_VMEM_LIMIT = 56 * 1024 * 1024


def _params(n_axes):
    return pltpu.CompilerParams(dimension_semantics=("arbitrary",) * n_axes,
                                vmem_limit_bytes=V7X_VMEM_LIMIT)


def _rms(x, g):
    return x * lax.rsqrt(jnp.mean(x * x, axis=-1, keepdims=True) + RMS_EPS) * g


def _pick(n, pref):
    t = min(n, pref)
    while n % t:
        t -= LANES if t > LANES else 8
    return t


def _rms_matmul_body(x_ref, g_ref, w_ref, o_ref):
    h = _rms(x_ref[...], g_ref[...]).astype(BF16)
    o_ref[...] = jnp.dot(h, w_ref[...], preferred_element_type=F32).astype(o_ref.dtype)


def rms_matmul(x, g, w, tm=1024):
    t, d = x.shape
    n = w.shape[1]
    tm = _pick(t, tm)
    return pl.pallas_call(
        _rms_matmul_body,
        grid=(t // tm,),
        in_specs=[pl.BlockSpec((tm, d), lambda i: (i, 0)),
                  pl.BlockSpec((1, d), lambda i: (0, 0)),
                  pl.BlockSpec((d, n), lambda i: (0, 0))],
        out_specs=pl.BlockSpec((tm, n), lambda i: (i, 0)),
        out_shape=jax.ShapeDtypeStruct((t, n), BF16),
        compiler_params=_params(1),
        name="rms_inproj",
    )(x, g.reshape(1, d), w)


def _diff_attn_body(q_ref, k_ref, v_ref, lam_ref, g_ref, o_ref, q_ref2, m_ref, l_ref, acc_ref,
                    *, blk, lam_init):
    qi = pl.program_id(2)
    hw = 2 * HEAD_DIM
    half = blk // 2
    lane = lax.broadcasted_iota(I32, (half, hw), 1)
    q = (q_ref[0].astype(F32) * (HEAD_DIM ** -0.5 * LOG2_E)).astype(BF16)
    zero = jnp.zeros((half, hw), BF16)
    q_ref2[...] = jnp.concatenate(
        [jnp.where(keep, q[r:r + half], zero)
         for r in (0, half) for keep in (lane < HEAD_DIM, lane >= HEAD_DIM)], axis=0)

    m_ref[...] = jnp.full(m_ref.shape, -jnp.inf, F32)
    l_ref[...] = jnp.zeros(l_ref.shape, F32)
    acc_ref[...] = jnp.zeros(acc_ref.shape, F32)

    def block(key_start, n_keys, row0, n_rows, masked):
        rows = pl.ds(row0, n_rows)
        k = k_ref[0, pl.ds(key_start, n_keys), :]
        v = v_ref[0, pl.ds(key_start, n_keys), :]
        s = lax.dot_general(q_ref2[rows, :], k, (((1,), (1,)), ((), ())), preferred_element_type=F32)
        if masked:
            row = lax.broadcasted_iota(I32, (n_rows, n_keys), 0) & (half - 1)
            col = lax.broadcasted_iota(I32, (n_rows, n_keys), 1)
            s = jnp.where(col <= row, s, -jnp.inf)
        tiles = [s[:, c * LANES:(c + 1) * LANES] for c in range(n_keys // LANES)]
        m_prev = m_ref[rows, :]
        m_new = jnp.maximum(m_prev, jnp.max(functools.reduce(jnp.maximum, tiles), axis=-1, keepdims=True))
        alpha = jnp.exp2(m_prev - m_new)
        p_tiles = [jnp.exp2(t - m_new) for t in tiles]
        l_ref[rows, :] = alpha * l_ref[rows, :] + functools.reduce(jnp.add, p_tiles)
        p = jnp.concatenate(p_tiles, axis=1).astype(BF16)
        acc_ref[rows, :] = alpha * acc_ref[rows, :] + jnp.dot(p, v, preferred_element_type=F32)
        m_ref[rows, :] = m_new

    def full_block(ki):
        block(pl.multiple_of(ki * blk, blk), blk, 0, 2 * blk, False)

    def pair(j, carry):
        full_block(2 * j)
        full_block(2 * j + 1)
        return carry

    lax.fori_loop(0, qi // 2, pair, 0)

    @pl.when(qi % 2 == 1)
    def _():
        full_block(qi - 1)

    diag = pl.multiple_of(qi * blk, blk)
    block(diag, half, 0, 2 * half, True)
    block(diag, half, 2 * half, 2 * half, False)
    block(diag + half, half, 2 * half, 2 * half, True)

    lp = lam_ref[...]
    lam = (jnp.exp(jnp.sum(lp[0:1] * lp[1:2], axis=-1, keepdims=True))
           - jnp.exp(jnp.sum(lp[2:3] * lp[3:4], axis=-1, keepdims=True)) + lam_init)
    o = acc_ref[...] / jnp.sum(l_ref[...], axis=-1, keepdims=True)
    o = (jnp.concatenate([o[0:half], o[2 * half:3 * half]], axis=0)
         - lam * jnp.concatenate([o[half:2 * half], o[3 * half:4 * half]], axis=0))
    o = _rms(o, g_ref[...]) * (1.0 - lam_init)
    o_ref[0] = o.astype(o_ref.dtype)


def diff_attention(proj, lam_params, subln_g, n_heads, lam_init, blk=1024):
    b, s, _ = proj.shape
    blk = _pick(s, blk)
    assert blk & (blk - 1) == 0, "the causal mask uses power-of-two half blocks"
    hw = 2 * HEAD_DIM
    kernel = functools.partial(_diff_attn_body, blk=blk, lam_init=lam_init)
    return pl.pallas_call(
        kernel,
        grid=(b, n_heads, s // blk),
        in_specs=[pl.BlockSpec((1, blk, hw), lambda bi, h, qi: (bi, qi, h)),
                  pl.BlockSpec((1, s, hw), lambda bi, h, qi: (bi, 0, n_heads + h)),
                  pl.BlockSpec((1, s, hw), lambda bi, h, qi: (bi, 0, 2 * n_heads + h)),
                  pl.BlockSpec((4, HEAD_DIM), lambda bi, h, qi: (0, 0)),
                  pl.BlockSpec((1, hw), lambda bi, h, qi: (0, 0))],
        out_specs=pl.BlockSpec((1, blk, hw), lambda bi, h, qi: (bi, qi, h)),
        out_shape=jax.ShapeDtypeStruct((b, s, n_heads * hw), BF16),
        scratch_shapes=[pltpu.VMEM((2 * blk, hw), BF16), pltpu.VMEM((2 * blk, LANES), F32),
                        pltpu.VMEM((2 * blk, LANES), F32), pltpu.VMEM((2 * blk, hw), F32)],
        compiler_params=_params(3),
        name="diff_attention",
    )(proj, proj, proj, lam_params, subln_g.reshape(1, hw))


def _swa_body(sink_ref, q_ref, kc_ref, kp_ref, vc_ref, vp_ref, o_ref, *, tq, kv_heads, group):
    t = pl.program_id(1)
    w = ATTN_BLOCK
    row = lax.broadcasted_iota(I32, (group * w, 2 * w), 0) & (w - 1)
    col = lax.broadcasted_iota(I32, (group * w, 2 * w), 1)
    band = (col > row) & (col <= row + w)
    band_first = band & ((col >= w) | (t > 0))
    low_half = lax.broadcasted_iota(I32, (w, LANES), 1) < HEAD_DIM
    for j in range(tq // w):
        rows = slice(j * w, (j + 1) * w)
        if j == 0:
            k2 = jnp.concatenate([kp_ref[0], kc_ref[0, rows, :]], axis=0)
            v2 = jnp.concatenate([vp_ref[0], vc_ref[0, rows, :]], axis=0)
            mask = band_first
        else:
            k2 = kc_ref[0, (j - 1) * w:(j + 1) * w, :]
            v2 = vc_ref[0, (j - 1) * w:(j + 1) * w, :]
            mask = band
        outs = []
        for g in range(kv_heads):
            kg = k2[:, g * HEAD_DIM:(g + 1) * HEAD_DIM]
            vg = v2[:, g * HEAD_DIM:(g + 1) * HEAD_DIM]
            kdup = jnp.concatenate([kg, kg], axis=1)
            vdup = jnp.concatenate([vg, vg], axis=1)
            heads = [g * group + i for i in range(group)]
            q_tiles = []
            for h in heads:
                q = q_ref[0, rows, (h // 2) * LANES:(h // 2 + 1) * LANES]
                q = (q.astype(F32) * (HEAD_DIM ** -0.5 * LOG2_E)).astype(BF16)
                own_half = low_half if h % 2 == 0 else jnp.logical_not(low_half)
                q_tiles.append(jnp.where(own_half, q, jnp.zeros_like(q)))
            sink = jnp.concatenate([jnp.full((w, LANES), sink_ref[h] * LOG2_E, F32) for h in heads], axis=0)
            s = lax.dot_general(jnp.concatenate(q_tiles, axis=0), kdup, (((1,), (1,)), ((), ())),
                                preferred_element_type=F32)
            s = jnp.where(mask, s, -jnp.inf)
            s0, s1 = s[:, :w], s[:, w:]
            m = jnp.maximum(jnp.max(jnp.maximum(s0, s1), axis=-1, keepdims=True), sink)
            e0 = jnp.exp2(s0 - m)
            e1 = jnp.exp2(s1 - m)
            denom = jnp.sum(e0 + e1, axis=-1, keepdims=True) + jnp.exp2(sink - m)
            e = jnp.concatenate([e0, e1], axis=1).astype(BF16)
            o = jnp.dot(e, vdup, preferred_element_type=F32) / denom
            outs.extend(o[i * w:(i + 1) * w, :] for i in range(group))
        tiles = [jnp.where(low_half, outs[h], outs[h + 1]) for h in range(0, len(outs), 2)]
        o_ref[0, rows, :] = jnp.concatenate(tiles, axis=1).astype(o_ref.dtype)


def sliding_window_attention(proj, sinks, q_col, k_col, v_col, kv_heads, group, tq=1024):
    b, s, _ = proj.shape
    tq = _pick(s, tq)
    qw = kv_heads * group * HEAD_DIM
    kw = kv_heads * HEAD_DIM
    sub = tq // ATTN_BLOCK
    kernel = functools.partial(_swa_body, tq=tq, kv_heads=kv_heads, group=group)
    prev = lambda bi, t: (bi, jnp.maximum(t * sub - 1, 0), 0)
    return pl.pallas_call(
        kernel,
        grid=(b, s // tq),
        in_specs=[pl.BlockSpec(memory_space=pltpu.SMEM),
                  pl.BlockSpec((1, tq, qw), lambda bi, t: (bi, t, q_col // qw)),
                  pl.BlockSpec((1, tq, kw), lambda bi, t: (bi, t, k_col // kw)),
                  pl.BlockSpec((1, ATTN_BLOCK, kw), lambda bi, t: prev(bi, t)[:2] + (k_col // kw,)),
                  pl.BlockSpec((1, tq, kw), lambda bi, t: (bi, t, v_col // kw)),
                  pl.BlockSpec((1, ATTN_BLOCK, kw), lambda bi, t: prev(bi, t)[:2] + (v_col // kw,))],
        out_specs=pl.BlockSpec((1, tq, qw), lambda bi, t: (bi, t, 0)),
        out_shape=jax.ShapeDtypeStruct((b, s, qw), BF16),
        compiler_params=_params(2),
        name="swa_attention",
    )(sinks, proj, proj, proj, proj, proj)


def _mix_ffn_body(x_ref, a_ref, b_ref, wo_ref, g_ref, wg_ref, wu_ref, wd_ref, o_ref, *, ff_chunk):
    mix = jnp.concatenate([a_ref[...], b_ref[...]], axis=1)
    x1 = x_ref[...] + jnp.dot(mix, wo_ref[...], preferred_element_type=F32)
    h = _rms(x1, g_ref[...]).astype(BF16)
    y = x1
    for c in range(wg_ref.shape[1] // ff_chunk):
        cols = slice(c * ff_chunk, (c + 1) * ff_chunk)
        a = jnp.dot(h, wg_ref[:, cols], preferred_element_type=F32)
        u = jnp.dot(h, wu_ref[:, cols], preferred_element_type=F32)
        act = (a * jax.nn.sigmoid(a) * u).astype(BF16)
        y = y + jnp.dot(act, wd_ref[cols, :], preferred_element_type=F32)
    o_ref[...] = y


def mix_ffn(x, oa, ob, wo, g, wg, wu, wd, tm=512, ff_chunk=2816):
    t, d = x.shape
    f = wg.shape[1]
    tm = _pick(t, tm)
    ff_chunk = _pick(f, ff_chunk)
    resident = lambda shape: pl.BlockSpec(shape, lambda i: (0, 0), pipeline_mode=pl.Buffered(1))
    return pl.pallas_call(
        functools.partial(_mix_ffn_body, ff_chunk=ff_chunk),
        grid=(t // tm,),
        in_specs=[pl.BlockSpec((tm, d), lambda i: (i, 0)),
                  pl.BlockSpec((tm, oa.shape[1]), lambda i: (i, 0)),
                  pl.BlockSpec((tm, ob.shape[1]), lambda i: (i, 0)),
                  resident(wo.shape), resident((1, d)),
                  resident(wg.shape), resident(wu.shape), resident(wd.shape)],
        out_specs=pl.BlockSpec((tm, d), lambda i: (i, 0)),
        out_shape=jax.ShapeDtypeStruct((t, d), F32),
        compiler_params=_params(1),
        name="mix_ffn",
    )(x, oa, ob, wo, g.reshape(1, d), wg, wu, wd)


def _conv_body(x_ref, g_ref, w1_ref, b1_ref, wdw_ref, bdw_ref, lng_ref, lnb_ref, w2_ref, b2_ref,
               o_ref, u_ref, v_ref, wb_ref, c_ref, *, tm, rows_per_chunk):
    d = x_ref.shape[-1]
    t = pl.program_id(1)
    sublanes = wb_ref.shape[1]

    @pl.when(t == 0)
    def _():
        u_ref[:, 0:CONV_HIST, :] = jnp.zeros((d // LANES, CONV_HIST, LANES), F32)
        wb_ref[...] = jnp.broadcast_to(wdw_ref[...][:, None, :], wb_ref.shape)

    h = _rms(x_ref[0], g_ref[...]).astype(BF16)
    z = jnp.dot(h, w1_ref[...], preferred_element_type=F32) + b1_ref[...]
    u = z[:, :d] * jax.nn.sigmoid(z[:, d:])
    n_slabs = d // LANES
    for c in range(n_slabs):
        u_ref[c, CONV_HIST:CONV_HIST + tm, :] = u[:, c * LANES:(c + 1) * LANES]

    first_tap = CONV_HIST - (CONV_WIDTH - 1)
    conv_rows = 8 * sublanes

    for c in range(n_slabs):
        lanes = slice(c * LANES, (c + 1) * LANES)
        w = [wb_ref[j, :, lanes] for j in range(CONV_WIDTH)]
        bias = jnp.zeros((sublanes, LANES), F32) + bdw_ref[:, lanes]

        def rows_block(i, carry, c=c, lanes=lanes, w=w, bias=bias):
            r0 = pl.multiple_of(i * conv_rows, conv_rows)
            for r in range(conv_rows // sublanes):
                sums = [bias, None]
                for j in range(CONV_WIDTH):
                    tap = u_ref[c, pl.ds(r0 + (first_tap + j + r * sublanes), sublanes, stride=1), :] * w[j]
                    sums[j % 2] = tap if sums[j % 2] is None else sums[j % 2] + tap
                c_ref[pl.ds(r0 + r * sublanes, sublanes), lanes] = sums[0] + sums[1]
            return carry

        lax.fori_loop(0, tm // conv_rows, rows_block, 0)

    def chunk(i, carry):
        r0 = pl.multiple_of(i * rows_per_chunk, rows_per_chunk)
        acc = c_ref[pl.ds(r0, rows_per_chunk), :]
        mu = jnp.mean(acc, axis=-1, keepdims=True)
        xc = acc - mu
        var = jnp.mean(xc * xc, axis=-1, keepdims=True)
        y = xc * lax.rsqrt(var + LN_EPS) * lng_ref[...] + lnb_ref[...]
        v_ref[pl.ds(r0, rows_per_chunk), :] = (y * jax.nn.sigmoid(y)).astype(BF16)
        return carry

    lax.fori_loop(0, tm // rows_per_chunk, chunk, 0)
    u_ref[:, 0:CONV_HIST, :] = u_ref[:, tm:tm + CONV_HIST, :]
    o_ref[0] = x_ref[0] + jnp.dot(v_ref[...], w2_ref[...], preferred_element_type=F32) + b2_ref[...]


def conformer_conv(x, g, w1, b1, wdw, bdw, lng, lnb, w2, b2, tm=1024, rows_per_chunk=256):
    b, s, d = x.shape
    tm = _pick(s, tm)
    kernel = functools.partial(_conv_body, tm=tm, rows_per_chunk=rows_per_chunk)
    vec = lambda n: pl.BlockSpec((1, n), lambda bi, t: (0, 0))
    return pl.pallas_call(
        kernel,
        grid=(b, s // tm),
        in_specs=[pl.BlockSpec((1, tm, d), lambda bi, t: (bi, t, 0)),
                  vec(d),
                  pl.BlockSpec((d, 2 * d), lambda bi, t: (0, 0)),
                  vec(2 * d),
                  pl.BlockSpec((CONV_WIDTH, d), lambda bi, t: (0, 0)),
                  vec(d), vec(d), vec(d),
                  pl.BlockSpec((d, d), lambda bi, t: (0, 0)),
                  vec(d)],
        out_specs=pl.BlockSpec((1, tm, d), lambda bi, t: (bi, t, 0)),
        out_shape=jax.ShapeDtypeStruct((b, s, d), F32),
        scratch_shapes=[pltpu.VMEM((d // LANES, tm + CONV_HIST, LANES), F32), pltpu.VMEM((tm, d), BF16),
                        pltpu.VMEM((CONV_WIDTH, F32_SUBLANES, d), F32), pltpu.VMEM((tm, d), F32)],
        compiler_params=_params(2),
        name="conformer_conv",
    )(x, g.reshape(1, d), w1, b1.reshape(1, 2 * d), wdw, bdw.reshape(1, d), lng.reshape(1, d),
      lnb.reshape(1, d), w2, b2.reshape(1, d))


def _router_body(x_ref, g_ref, wr_ref, h_ref, meta_ref, picks_ref, counts_ref, *, n_experts):
    @pl.when(pl.program_id(0) == 0)
    def _():
        counts_ref[...] = jnp.zeros(counts_ref.shape, F32)

    h = _rms(x_ref[...], g_ref[...])
    _store_slabs(h_ref, 0, h)
    logits = jnp.dot(h.astype(BF16), wr_ref[...], preferred_element_type=F32)
    lane = lax.broadcasted_iota(I32, logits.shape, 1).astype(F32)
    neg = jnp.full_like(logits, -jnp.inf)
    far = jnp.full_like(logits, float(LANES))
    l1 = jnp.where(lane < n_experts, logits, neg)
    m1 = jnp.max(l1, axis=-1, keepdims=True)
    i1 = jnp.min(jnp.where(l1 == m1, lane, far), axis=-1, keepdims=True)
    l2 = jnp.where(lane == i1, neg, l1)
    m2 = jnp.max(l2, axis=-1, keepdims=True)
    i2 = jnp.min(jnp.where(l2 == m2, lane, far), axis=-1, keepdims=True)
    e2 = jnp.exp(m2 - m1)
    w1 = 1.0 / (1.0 + e2)
    w2 = e2 / (1.0 + e2)
    tm = logits.shape[0]
    hot1 = (lane == i1).astype(F32)
    hot2 = (lane == i2).astype(F32)
    both = (hot1 + hot2).astype(BF16)
    earlier = (lax.broadcasted_iota(I32, (tm, tm), 1) < lax.broadcasted_iota(I32, (tm, tm), 0)).astype(BF16)
    before = jnp.dot(earlier, both, preferred_element_type=F32) + counts_ref[...]
    r1 = jnp.sum(hot1 * before, axis=-1, keepdims=True)
    r2 = jnp.sum(hot2 * before, axis=-1, keepdims=True)
    counts_ref[...] += jnp.sum(hot1 + hot2, axis=0, keepdims=True)
    zero = jnp.zeros_like(logits)
    meta = jnp.where(lane == 0, i1,
           jnp.where(lane == 1, i2,
           jnp.where(lane == 2, w1,
           jnp.where(lane == 3, w2,
           jnp.where(lane == 4, r1, jnp.where(lane == 5, r2, zero))))))
    meta_ref[...] = meta
    picks_ref[...] = jnp.transpose(meta)[0:picks_ref.shape[0], :]


def moe_router(x, g, wr, n_experts, tm=1024):
    t, d = x.shape
    tm = _pick(t, tm)
    kernel = functools.partial(_router_body, n_experts=n_experts)
    return pl.pallas_call(
        kernel,
        grid=(t // tm,),
        in_specs=[pl.BlockSpec((tm, d), lambda i: (i, 0)),
                  pl.BlockSpec((1, d), lambda i: (0, 0)),
                  pl.BlockSpec((d, LANES), lambda i: (0, 0))],
        out_specs=[pl.BlockSpec((tm * (d // LANES), LANES), lambda i: (i, 0)),
                   pl.BlockSpec((tm, LANES), lambda i: (i, 0)),
                   pl.BlockSpec((F32_SUBLANES, tm), lambda i: (0, i)),
                   pl.BlockSpec((1, LANES), lambda i: (0, 0))],
        out_shape=[jax.ShapeDtypeStruct((t * (d // LANES), LANES), F32),
                   jax.ShapeDtypeStruct((t, LANES), F32),
                   jax.ShapeDtypeStruct((F32_SUBLANES, t), F32),
                   jax.ShapeDtypeStruct((1, LANES), F32)],
        compiler_params=_params(1),
        name="moe_router",
    )(x, g.reshape(1, d), wr)


GATHER_UNROLL = 8
N_DMA_PRIORITIES = 2


def _store_slabs(ref, first_row, value):
    n, d = value.shape
    n_slabs = d // LANES
    for c in range(n_slabs):
        ref[pl.ds(first_row * n_slabs + c, n, stride=n_slabs), :] = value[:, c * LANES:(c + 1) * LANES]


def _load_slabs(ref, first_row, n, d):
    n_slabs = d // LANES
    return jnp.concatenate([ref[pl.ds(first_row * n_slabs + c, n, stride=n_slabs), :]
                            for c in range(n_slabs)], axis=1)


def _dispatch_body(pad_ref, tn_ref, d_ref, h_hbm, xs_hbm, zero_ref, buf_ref, in_sem, sem, zsem,
                   *, tm, n_slabs, n_experts, n_sub):
    i = pl.program_id(0)
    n = pl.num_programs(0)
    rows = tm * n_slabs
    n_ring = buf_ref.shape[0]

    def stage(tile):
        return pltpu.make_async_copy(h_hbm.at[pl.ds(pl.multiple_of(tile * rows, rows), rows), :],
                                     buf_ref.at[tile % n_ring], in_sem.at[tile % n_ring])

    def tile_copy(slot):
        return pltpu.make_async_copy(buf_ref.at[0], xs_hbm.at[pl.ds(0, rows), :], sem.at[slot])

    @pl.when(i == 0)
    def _():
        stage(0).start()

    @pl.when(i + 1 < n)
    def _():
        stage(i + 1).start()

    stage(i).wait()
    slot = i % 2
    ring = i % n_ring
    for rank in range(TOP_K):
        def body(it, carry, rank=rank):
            base = it * GATHER_UNROLL
            dsts = [pl.multiple_of(d_ref[0, 0, rank * tm + base + k], n_slabs) for k in range(GATHER_UNROLL)]
            for k in range(GATHER_UNROLL):
                src = pl.multiple_of((base + k) * n_slabs, n_slabs)
                pltpu.make_async_copy(buf_ref.at[ring, pl.ds(src, n_slabs), :],
                                      xs_hbm.at[pl.ds(dsts[k], n_slabs), :], sem.at[slot]
                                      ).start(priority=k % N_DMA_PRIORITIES)
            return carry
        lax.fori_loop(0, tm // GATHER_UNROLL, body, 0)

    @pl.when(i == 0)
    def _():
        zero_ref[...] = jnp.zeros(zero_ref.shape, F32)
        sub_rows = zero_ref.shape[0]

        def unused_sub_tiles(fn):
            def tbody(tile, carry):
                for s in range(n_sub):
                    @pl.when(s >= tn_ref[tile])
                    def _():
                        dst = pl.multiple_of((tile * n_sub + s) * sub_rows, sub_rows)
                        fn(pltpu.make_async_copy(zero_ref, xs_hbm.at[pl.ds(dst, sub_rows), :], zsem))
                return carry
            lax.fori_loop(0, tn_ref.shape[0], tbody, 0)

        def tail_rows(fn):
            for e in range(n_experts):
                first, count = pad_ref[0, e], pad_ref[1, e]
                bit = sub_rows // n_slabs // 2
                while bit:
                    @pl.when((count & bit) != 0)
                    def _(bit=bit):
                        done = count & ~(2 * bit - 1)
                        dst = pl.multiple_of((first + done) * n_slabs, n_slabs)
                        fn(pltpu.make_async_copy(zero_ref.at[pl.ds(0, bit * n_slabs), :],
                                                 xs_hbm.at[pl.ds(dst, bit * n_slabs), :], zsem))
                    bit //= 2

        unused_sub_tiles(lambda cp: cp.start())
        tail_rows(lambda cp: cp.start())
        unused_sub_tiles(lambda cp: cp.wait())
        tail_rows(lambda cp: cp.wait())

    @pl.when(i > 0)
    def _():
        for rank in range(TOP_K):
            tile_copy(1 - slot).wait()

    @pl.when(i == n - 1)
    def _():
        for rank in range(TOP_K):
            tile_copy(slot).wait()


def moe_dispatch(h, dest, pad, tile_count, sub, n_sub, n_slabs):
    n_tok_tiles = dest.shape[0]
    tm = dest.shape[2] // TOP_K
    n_experts = pad.shape[1]
    n_rows = tile_count.shape[0] * n_sub * sub
    grid_spec = pltpu.PrefetchScalarGridSpec(
        num_scalar_prefetch=2,
        grid=(n_tok_tiles,),
        in_specs=[pl.BlockSpec((1, 1, TOP_K * tm), lambda i, pad, tn: (i, 0, 0), memory_space=pltpu.SMEM),
                  pl.BlockSpec(memory_space=pl.ANY)],
        out_specs=pl.BlockSpec(memory_space=pl.ANY),
        scratch_shapes=[pltpu.VMEM((sub * n_slabs, LANES), F32), pltpu.VMEM((3, tm * n_slabs, LANES), F32),
                        pltpu.SemaphoreType.DMA((3,)), pltpu.SemaphoreType.DMA((2,)),
                        pltpu.SemaphoreType.DMA(())],
    )
    return pl.pallas_call(
        functools.partial(_dispatch_body, tm=tm, n_slabs=n_slabs, n_experts=n_experts, n_sub=n_sub),
        grid_spec=grid_spec,
        out_shape=jax.ShapeDtypeStruct((n_rows * n_slabs, LANES), F32),
        compiler_params=_params(1),
        name="moe_dispatch",
    )(pad, tile_count, dest, h)


def _moe_body(te_ref, tn_ref, xs_ref, wg_ref, wu_ref, wd_ref, y_hbm,
              y_ref, xb_ref, wgb_ref, wub_ref, wdb_ref, stage_ref, zero_ref, osem, zsem, *, sub, n_sub):
    i = pl.program_id(0)
    j = pl.program_id(1)
    last = j == pl.num_programs(1) - 1
    count = tn_ref[i]
    d = wg_ref.shape[2]
    slab_rows = stage_ref.shape[1]

    @pl.when((i == 0) & (j == 0))
    def _():
        zero_ref[...] = jnp.zeros(zero_ref.shape, F32)

    for s in range(n_sub):
        rows = pl.ds(s * sub, sub)

        @pl.when((j == 0) & (s < count))
        def _():
            xb_ref[rows, :] = _load_slabs(xs_ref, s * sub, sub, d).astype(BF16)
            y_ref[rows, :] = jnp.zeros((sub, d), F32)

    def expert_rows(s, wg, wu, wd):
        rows = pl.ds(s * sub, sub)
        x = xb_ref[rows, :]
        a = jnp.dot(x, wg, preferred_element_type=F32)
        u = jnp.dot(x, wu, preferred_element_type=F32)
        act = (a * jax.nn.sigmoid(a) * u).astype(BF16)
        y_ref[rows, :] += jnp.dot(act, wd, preferred_element_type=F32)

    def rounded_weights():
        return tuple(r[0, 0].astype(BF16) for r in (wg_ref, wu_ref, wd_ref))

    @pl.when(count == n_sub)
    def _():
        weights = rounded_weights()
        for s in range(n_sub):
            expert_rows(s, *weights)

    for s in range(n_sub - 1):
        @pl.when((s < count) & (count < n_sub))
        def _():
            if s == 0:
                wgb_ref[...], wub_ref[...], wdb_ref[...] = rounded_weights()
            expert_rows(s, wgb_ref[...], wub_ref[...], wdb_ref[...])

    def out_copy(src, s):
        dst = pl.multiple_of((i * n_sub + s) * slab_rows, slab_rows)
        return src, y_hbm.at[pl.ds(dst, slab_rows), :]

    n_tiles = pl.num_programs(0)
    prev_count = tn_ref[jnp.maximum(i - 1, 0)]
    next_count = tn_ref[jnp.minimum(i + 1, n_tiles - 1)]
    final_tile = (i == n_tiles - 1) | (next_count == 0)

    def stage_copy(s):
        return pltpu.make_async_copy(*out_copy(stage_ref.at[s % 2], s), osem.at[s % 2])

    for slot in range(2):
        @pl.when(last & (count > 0) & (i > 0) & (prev_count > slot))
        def _():
            stage_copy(slot).wait()

    for s in range(n_sub):
        @pl.when(last & (s < count))
        def _():
            if s >= 2:
                stage_copy(s - 2).wait()
            _store_slabs(stage_ref.at[s % 2], 0, y_ref[pl.ds(s * sub, sub), :])
            stage_copy(s).start()

        @pl.when(last & (s >= count))
        def _():
            pltpu.make_async_copy(*out_copy(zero_ref, s), zsem).start()

    for s in range(n_sub):
        @pl.when(last & final_tile & (s < count) & (s + 2 >= count))
        def _():
            stage_copy(s).wait()

        @pl.when(last & (s >= count))
        def _():
            pltpu.make_async_copy(*out_copy(zero_ref, s), zsem).wait()


def moe_experts(xs, tile_expert, tile_count, wg, wu, wd, layer, sub, n_sub, tf=512):
    d = wg.shape[2]
    n_slabs = d // LANES
    n_tiles = tile_expert.shape[0]
    tile_rows = n_sub * sub
    f = wg.shape[3]
    tf = _pick(f, tf)
    nf = f // tf

    def w_col(i, j, te, tn):
        return (layer, te[i], 0, jnp.where(tn[i] > 0, j, nf - 1))

    def w_row(i, j, te, tn):
        return (layer, te[i], jnp.where(tn[i] > 0, j, nf - 1), 0)

    grid_spec = pltpu.PrefetchScalarGridSpec(
        num_scalar_prefetch=2,
        grid=(n_tiles, nf),
        in_specs=[pl.BlockSpec((tile_rows * n_slabs, LANES), lambda i, j, te, tn: (i, 0)),
                  pl.BlockSpec((1, 1, d, tf), w_col),
                  pl.BlockSpec((1, 1, d, tf), w_col),
                  pl.BlockSpec((1, 1, tf, d), w_row)],
        out_specs=pl.BlockSpec(memory_space=pl.ANY),
        scratch_shapes=[pltpu.VMEM((tile_rows, d), F32), pltpu.VMEM((tile_rows, d), BF16),
                        pltpu.VMEM((d, tf), BF16), pltpu.VMEM((d, tf), BF16), pltpu.VMEM((tf, d), BF16),
                        pltpu.VMEM((2, sub * n_slabs, LANES), F32), pltpu.VMEM((sub * n_slabs, LANES), F32),
                        pltpu.SemaphoreType.DMA((2,)), pltpu.SemaphoreType.DMA(())],
    )
    return pl.pallas_call(
        functools.partial(_moe_body, sub=sub, n_sub=n_sub),
        grid_spec=grid_spec,
        out_shape=jax.ShapeDtypeStruct((n_tiles * tile_rows * n_slabs, LANES), F32),
        compiler_params=_params(2),
        name="moe_experts",
    )(tile_expert, tile_count, xs, wg, wu, wd)


def _combine_body(d_cur_ref, d_nxt_ref, x_ref, meta_ref, g_ref, y_hbm, o_ref, buf_ref, sem,
                  *, tm, final_norm):
    i = pl.program_id(0)
    n = pl.num_programs(0)
    d = x_ref.shape[1]
    n_slabs = d // LANES

    def gather(d_ref, slot):
        def body(it, carry):
            base = it * GATHER_UNROLL
            srcs = [pl.multiple_of(d_ref[0, 0, base + k], n_slabs) for k in range(GATHER_UNROLL)]
            for k in range(GATHER_UNROLL):
                dst = pl.multiple_of((base + k) * n_slabs, n_slabs)
                pltpu.make_async_copy(y_hbm.at[pl.ds(srcs[k], n_slabs), :],
                                      buf_ref.at[slot, pl.ds(dst, n_slabs), :], sem.at[slot]
                                      ).start(priority=k % N_DMA_PRIORITIES)
            return carry
        lax.fori_loop(0, TOP_K * tm // GATHER_UNROLL, body, 0)

    @pl.when(i == 0)
    def _():
        gather(d_cur_ref, 0)

    @pl.when(i + 1 < n)
    def _():
        gather(d_nxt_ref, (i + 1) % 2)

    slot = i % 2
    pltpu.make_async_copy(y_hbm.at[pl.ds(0, TOP_K * tm * n_slabs), :], buf_ref.at[slot],
                          sem.at[slot]).wait()
    meta = meta_ref[...]
    out = (x_ref[...] + meta[:, 2:3] * _load_slabs(buf_ref.at[slot], 0, tm, d)
           + meta[:, 3:4] * _load_slabs(buf_ref.at[slot], tm, tm, d))
    if final_norm:
        out = _rms(out, g_ref[...])
    o_ref[...] = out


def moe_combine(x, meta, dest, y, g, final_norm, tm=256):
    t, d = x.shape
    n = t // tm
    n_slabs = d // LANES
    smem = lambda fn: pl.BlockSpec((1, 1, TOP_K * tm), fn, memory_space=pltpu.SMEM)
    return pl.pallas_call(
        functools.partial(_combine_body, tm=tm, final_norm=final_norm),
        grid=(n,),
        in_specs=[smem(lambda i: (i, 0, 0)),
                  smem(lambda i: (jnp.minimum(i + 1, n - 1), 0, 0)),
                  pl.BlockSpec((tm, d), lambda i: (i, 0)),
                  pl.BlockSpec((tm, LANES), lambda i: (i, 0)),
                  pl.BlockSpec((1, d), lambda i: (0, 0)),
                  pl.BlockSpec(memory_space=pl.ANY)],
        out_specs=pl.BlockSpec((tm, d), lambda i: (i, 0)),
        out_shape=jax.ShapeDtypeStruct((t, d), F32),
        scratch_shapes=[pltpu.VMEM((2, TOP_K * tm * n_slabs, LANES), F32), pltpu.SemaphoreType.DMA((2,))],
        compiler_params=_params(1),
        name="moe_combine",
    )(dest, dest, x, meta, g.reshape(1, d), y)


def _route(picks, counts, n_experts, sub, n_sub, tm_combine):
    t = picks.shape[1]
    tile_rows = sub * n_sub
    expert = picks[0:TOP_K].astype(I32)
    rank = picks[2 * TOP_K:3 * TOP_K].astype(I32)
    counts = counts[0, :n_experts].astype(I32)
    subs = (counts + sub - 1) // sub
    tiles = (subs + n_sub - 1) // n_sub
    tile_ends = jnp.cumsum(tiles)
    tile_starts = tile_ends - tiles
    first_row = jnp.zeros_like(expert)
    for e in range(n_experts):
        first_row = jnp.where(expert == e, tile_starts[e] * tile_rows, first_row)
    dest = first_row + rank
    n_tiles = ((t * TOP_K) // sub + n_experts * n_sub) // n_sub
    idx = jnp.arange(n_tiles, dtype=I32)
    tile_expert = jnp.minimum(jnp.searchsorted(tile_ends, idx, side="right"), n_experts - 1).astype(I32)
    local = idx - tile_starts[tile_expert]
    tile_count = jnp.clip(subs[tile_expert] - local * n_sub, 0, n_sub).astype(I32)
    dest_tiles = dest.reshape(TOP_K, t // tm_combine, tm_combine).transpose(1, 0, 2).reshape(
        t // tm_combine, 1, TOP_K * tm_combine)
    pad = jnp.stack([tile_starts * tile_rows + counts, subs * sub - counts]).astype(I32)
    return tile_expert, tile_count, dest_tiles, pad, n_tiles


def moe_layer(x, g, wr, wg, wu, wd, layer, final_g, final_norm, sub=512, n_sub=4, tm_combine=512):
    t, d = x.shape
    n_experts = wr.shape[1]
    tm_combine = min(tm_combine, t)
    wr_pad = jnp.zeros((d, LANES), BF16).at[:, :n_experts].set(wr.astype(BF16))
    h, meta, picks, counts = moe_router(x, g, wr_pad, n_experts)
    n_slabs = d // LANES
    tile_expert, tile_count, dest_tiles, pad, n_tiles = _route(picks, counts, n_experts, sub, n_sub,
                                                               tm_combine)
    slab_dest = dest_tiles * n_slabs
    xs = moe_dispatch(h, slab_dest, pad, tile_count, sub, n_sub, n_slabs)
    y = moe_experts(xs, tile_expert, tile_count, wg, wu, wd, layer, sub, n_sub)
    return moe_combine(x, meta, slab_dest, y, final_g, final_norm, tm_combine)


def attention_mixers(x, norm_g, w_in, lam_params, subln_g, sinks, layer):
    b, s, d = x.shape
    diff_heads = d // (4 * HEAD_DIM)
    swa_q_heads = d // (2 * HEAD_DIM)
    swa_kv_heads = max(1, swa_q_heads // 4)
    group = swa_q_heads // swa_kv_heads
    a_width = diff_heads * 2 * HEAD_DIM
    lam_init = 0.8 - 0.6 * math.exp(-0.3 * layer)
    x2 = x.reshape(b * s, d)
    proj = rms_matmul(x2, norm_g, w_in).reshape(b, s, -1)
    oa = diff_attention(proj, lam_params, subln_g, diff_heads, lam_init)
    q_col = 3 * a_width
    k_col = q_col + swa_q_heads * HEAD_DIM
    v_col = k_col + swa_kv_heads * HEAD_DIM
    ob = sliding_window_attention(proj, sinks, q_col, k_col, v_col, swa_kv_heads, group)
    return oa.reshape(b * s, -1), ob.reshape(b * s, -1)


def kernel(x, attn_norm_g, w_in_att, diff_lambda, diff_subln_g, attn_sinks, w_out_att, ffn_norm_g, w_ffn_gate, w_ffn_up, w_ffn_down, conv_norm_g, w_pw1, b_pw1, w_dw, b_dw, conv_ln_g, conv_ln_b, w_pw2, b_pw2, moe_norm_g, w_router, w_exp_gate, w_exp_up, w_exp_down, final_norm_g):
    b, s, d = x.shape
    depth = attn_norm_g.shape[0] + conv_norm_g.shape[0]
    assert depth % 2 == 0, "the final RMSNorm is fused into the last expert layer"
    bf = lambda w: w.astype(BF16)
    for layer in range(depth):
        i = layer // 2
        if layer % 2 == 0:
            oa, ob = attention_mixers(x, attn_norm_g[i], bf(w_in_att[i]), diff_lambda[i], diff_subln_g[i],
                                      attn_sinks[i], layer)
            x2 = mix_ffn(x.reshape(b * s, d), oa, ob, bf(w_out_att[i]), ffn_norm_g[i],
                         bf(w_ffn_gate[i]), bf(w_ffn_up[i]), bf(w_ffn_down[i]))
            x = x2.reshape(b, s, d)
        else:
            x = conformer_conv(x, conv_norm_g[i], bf(w_pw1[i]), b_pw1[i], w_dw[i], b_dw[i],
                               conv_ln_g[i], conv_ln_b[i], bf(w_pw2[i]), b_pw2[i])
            x2 = moe_layer(x.reshape(b * s, d), moe_norm_g[i], w_router[i], w_exp_gate,
                           w_exp_up, w_exp_down, i, final_norm_g,
                           final_norm=(layer == depth - 1))
            x = x2.reshape(b, s, d)
    return x
```
